```python
import jax, jax.numpy as jnp
from jax import lax

D_MODEL = 2048
BATCH = 4
SEQ = 2048
DEPTH = 2

BLOCK = 128
ROPE_THETA = 10000.0
NORM_EPS = 1e-6
N_BRANCH = 4

SB_HEADS = 8
SB_DIM = 128
SB_W = SB_HEADS * SB_DIM

RW_HEADS = 16
RW_DIM = 64
RW_W = RW_HEADS * RW_DIM
RW_LORA_W = 96
RW_LORA_A = 96
RW_LORA_G = 64
RW_SIZES = (RW_W, RW_W, RW_W, RW_LORA_W, RW_LORA_A, RW_LORA_G)
RW_IN = 3 * RW_W + RW_LORA_W + RW_LORA_A + RW_LORA_G
RW_GN_EPS = 64e-5

DSA_HEADS = 8
DSA_DIM = 128
DSA_W = DSA_HEADS * DSA_DIM
DSA_KV_RANK = 256
IDX_HEADS = 16
IDX_DIM = 64
IDX_TOPK_MAX = 256

SW_HEADS = 16
SW_KV_HEADS = 2
SW_DIM = 64
SW_W = SW_HEADS * SW_DIM
WINDOW = 128

BRANCH_W = 1024

IN_SIZES = (SB_W, SB_W, SB_W,
            RW_IN,
            DSA_W, DSA_KV_RANK, IDX_HEADS * IDX_DIM, IDX_DIM, IDX_HEADS,
            SW_W, SW_KV_HEADS * SW_DIM, SW_KV_HEADS * SW_DIM,
            N_BRANCH * D_MODEL)
N_IN = 3 * SB_W + RW_IN + DSA_W + DSA_KV_RANK + IDX_HEADS * IDX_DIM + IDX_DIM + IDX_HEADS + SW_W + 2 * SW_KV_HEADS * SW_DIM + N_BRANCH * D_MODEL

FFN_DENSE = 7168
N_EXPERTS = 8
TOP_K = 2
FFN_EXPERT = 7168
PLE_DIM = 256
N_DENSE_LAYERS = (DEPTH + 1) // 2
N_MOE_LAYERS = DEPTH // 2

kernel_name = 'hybrid_gated_sb_rwkv7_dsa_swa_moe'


def rms_norm(x, gain, eps=NORM_EPS):
    xf = x.astype(jnp.float32)
    xf = xf * lax.rsqrt(jnp.mean(xf * xf, axis=-1, keepdims=True) + eps)
    return (xf * gain.astype(jnp.float32)).astype(x.dtype)


def split_cols(u, sizes):
    parts, start = [], 0
    for size in sizes:
        parts.append(u[..., start:start + size])
        start += size
    return parts


def split_heads(u, n_heads):
    return u.reshape(u.shape[0], u.shape[1], n_heads, u.shape[-1] // n_heads)


def rope(u, positions):
    half = u.shape[-1] // 2
    inv_freq = ROPE_THETA ** (-jnp.arange(half, dtype=jnp.float32) / half)
    ang = positions.astype(jnp.float32)[..., None] * inv_freq
    cos = jnp.cos(ang)[:, :, None, :]
    sin = jnp.sin(ang)[:, :, None, :]
    uf = u.astype(jnp.float32)
    u1, u2 = uf[..., :half], uf[..., half:]
    return jnp.concatenate([u1 * cos - u2 * sin, u2 * cos + u1 * sin], axis=-1).astype(u.dtype)


def to_blocks(u):
    b, s = u.shape[:2]
    return u.reshape((b, s // BLOCK, BLOCK) + u.shape[2:]).swapaxes(0, 1)


def from_blocks(u):
    nb, b = u.shape[:2]
    return u.swapaxes(0, 1).reshape((b, nb * BLOCK) + u.shape[3:])


def token_shift(u):
    return jnp.pad(u, ((0, 0), (1, 0), (0, 0)))[:, :-1]


def stick_breaking_attention(q, k, v):
    b, s, _, dh = q.shape
    scale = dh ** -0.5
    key_pos = jnp.arange(s)

    def one_block(args):
        q_b, start = args
        z = jnp.einsum('bqhd,bshd->bhqs', q_b, k).astype(jnp.float32) * scale
        q_pos = start + jnp.arange(BLOCK)
        before = key_pos[None, :] < q_pos[:, None]
        log_keep = jnp.where(before, jax.nn.log_sigmoid(-z), 0.0)
        log_between = lax.cumsum(log_keep, axis=3, reverse=True) - log_keep
        w = jnp.where(before, jnp.exp(jax.nn.log_sigmoid(z) + log_between), 0.0)
        return jnp.einsum('bhqs,bshd->bqhd', w.astype(v.dtype), v)

    starts = jnp.arange(s // BLOCK) * BLOCK
    out = lax.map(one_block, (to_blocks(q), starts))
    return from_blocks(out).reshape(b, s, SB_W)


def rwkv7_time_mix(u, mu, w0, w2, a0, a2, g2, k_k, k_a, r_k, ln_g, ln_b):
    b, s, _ = u.shape
    f32 = jnp.float32
    u = u + mu * (token_shift(u) - u)
    r, k, v, w_lo, a_lo, g_lo = split_cols(u, RW_SIZES)
    w_log = -jax.nn.softplus(-(w0 + jnp.tanh(w_lo) @ w2)) - 0.5
    decay = jnp.exp(-jnp.exp(w_log.astype(f32)))
    a = jax.nn.sigmoid(a0 + a_lo @ a2)
    g = jax.nn.sigmoid(g_lo) @ g2
    kk = split_heads((k * k_k).astype(f32), RW_HEADS)
    kk = kk / jnp.maximum(jnp.sqrt(jnp.sum(kk * kk, axis=-1, keepdims=True)), 1e-12)
    k = k * (1.0 + (a - 1.0) * k_a)
    r_h, k_h, v_h, w_h, a_h = [split_heads(t.astype(f32), RW_HEADS) for t in (r, k, v, decay, a)]

    def step(state, inp):
        r_t, w_t, k_t, v_t, kk_t, a_t = inp
        s_kk = jnp.einsum('bhvk,bhk->bhv', state, kk_t)
        state = (state * w_t[:, :, None, :]
                 - s_kk[..., None] * (kk_t * a_t)[:, :, None, :]
                 + v_t[..., None] * k_t[:, :, None, :])
        return state, jnp.einsum('bhvk,bhk->bhv', state, r_t)

    state0 = jnp.zeros((b, RW_HEADS, RW_DIM, RW_DIM), f32)
    xs = tuple(t.swapaxes(0, 1) for t in (r_h, w_h, k_h, v_h, kk, a_h))
    _, y = lax.scan(step, state0, xs)
    y = y.swapaxes(0, 1)
    mean = jnp.mean(y, axis=-1, keepdims=True)
    var = jnp.mean(jnp.square(y - mean), axis=-1, keepdims=True)
    y = ((y - mean) * lax.rsqrt(var + RW_GN_EPS) * ln_g.astype(f32).reshape(RW_HEADS, RW_DIM)
         + ln_b.astype(f32).reshape(RW_HEADS, RW_DIM))
    bonus = jnp.sum(r_h * k_h * r_k.astype(f32), axis=-1, keepdims=True) * v_h
    return (y + bonus).reshape(b, s, RW_W).astype(u.dtype) * g


def dsa_attention(q, c_kv, q_idx, k_idx, w_idx, positions, q_gain, k_gain, kv_gain, w_uk, w_uv):
    b, s, _ = q.shape
    f32 = jnp.float32
    q = rope(rms_norm(split_heads(q, DSA_HEADS), q_gain), positions)
    c = rms_norm(c_kv, kv_gain)
    k = rope(rms_norm(c @ w_uk, k_gain)[:, :, None, :], positions)[:, :, 0]
    v = c @ w_uv
    q_idx = rope(split_heads(q_idx, IDX_HEADS), positions)
    k_idx = rope(k_idx[:, :, None, :], positions)[:, :, 0]
    w_idx = w_idx * (IDX_HEADS ** -0.5 * IDX_DIM ** -0.5)
    top_k = min(IDX_TOPK_MAX, s // 4)
    key_pos = jnp.arange(s)
    gather = jax.vmap(lambda table, idx: table[idx])

    def one_block(args):
        q_b, qi_b, wi_b, start = args
        q_pos = start + jnp.arange(BLOCK)
        causal = key_pos[None, :] <= q_pos[:, None]
        score = jax.nn.relu(jnp.einsum('bqhd,bsd->bqhs', qi_b, k_idx))
        score = jnp.einsum('bqhs,bqh->bqs', score, wi_b).astype(f32)
        score = jnp.where(causal[None], score, -jnp.inf)
        _, sel = lax.top_k(score, top_k)
        valid = sel <= q_pos[None, :, None]
        k_sel = gather(k, sel)
        v_sel = gather(v, sel)
        logits = jnp.einsum('bqhd,bqkd->bqhk', q_b, k_sel).astype(f32) * DSA_DIM ** -0.5
        logits = jnp.where(valid[:, :, None, :], logits, -jnp.inf)
        prob = jax.nn.softmax(logits, axis=-1)
        return jnp.einsum('bqhk,bqkd->bqhd', prob.astype(v.dtype), v_sel)

    starts = jnp.arange(s // BLOCK) * BLOCK
    out = lax.map(one_block, (to_blocks(q), to_blocks(q_idx), to_blocks(w_idx), starts))
    return from_blocks(out).reshape(b, s, DSA_W)


def sliding_window_attention(q, k, v, positions, q_gain, k_gain, sinks):
    b, s = q.shape[:2]
    nb = s // BLOCK
    grp = SW_HEADS // SW_KV_HEADS
    f32 = jnp.float32
    q = rope(rms_norm(q, q_gain), positions)
    k = rope(rms_norm(k, k_gain), positions)
    q_b = q.reshape(b, nb, BLOCK, SW_KV_HEADS, grp, SW_DIM)
    k_b = k.reshape(b, nb, BLOCK, SW_KV_HEADS, SW_DIM)
    v_b = v.reshape(b, nb, BLOCK, SW_KV_HEADS, SW_DIM)
    prev = lambda t: jnp.pad(t, ((0, 0), (1, 0), (0, 0), (0, 0), (0, 0)))[:, :-1]
    k_cat = jnp.concatenate([prev(k_b), k_b], axis=2)
    v_cat = jnp.concatenate([prev(v_b), v_b], axis=2)
    q_off = jnp.arange(BLOCK)
    k_off = jnp.arange(2 * BLOCK) - BLOCK
    dist = q_off[:, None] - k_off[None, :]
    band = (dist >= 0) & (dist < WINDOW)
    in_seq = (jnp.arange(nb)[:, None] * BLOCK + k_off[None, :]) >= 0
    mask = band[None] & in_seq[:, None, :]
    logits = jnp.einsum('bnqkgd,bnskd->bnkgqs', q_b, k_cat).astype(f32) * SW_DIM ** -0.5
    logits = jnp.where(mask[None, :, None, None], logits, -jnp.inf)
    sink = jnp.broadcast_to(sinks.astype(f32).reshape(SW_KV_HEADS, grp)[None, None, :, :, None, None],
                            logits.shape[:-1] + (1,))
    prob = jax.nn.softmax(jnp.concatenate([logits, sink], axis=-1), axis=-1)[..., :-1]
    out = jnp.einsum('bnkgqs,bnskd->bnqkgd', prob.astype(v.dtype), v_cat)
    return out.reshape(b, s, SW_W)


def swiglu(h, w_gate, w_up, w_down):
    return (jax.nn.silu(h @ w_gate) * (h @ w_up)) @ w_down


def moe_swiglu(h, router, w_gate, w_up, w_down):
    logits = (h @ router).astype(jnp.float32)
    top_val, top_idx = lax.top_k(logits, TOP_K)
    top_w = jax.nn.softmax(top_val, axis=-1)
    combine = jnp.sum(jax.nn.one_hot(top_idx, N_EXPERTS, dtype=jnp.float32) * top_w[..., None], axis=-2)
    combine = combine.astype(h.dtype)
    out = jnp.zeros_like(h)
    for e in range(N_EXPERTS):
        out = out + combine[..., e:e + 1] * swiglu(h, w_gate[e], w_up[e], w_down[e])
    return out


def setup_inputs(seed: int = 0) -> dict:
    key = jax.random.key(seed)
    ks = iter(jax.random.split(key, 48))
    f32 = jnp.float32
    L = DEPTH

    def normal(shape, scale):
        return jax.random.normal(next(ks), shape, f32) * scale

    def gain(shape):
        return 1.0 + normal(shape, 0.05)

    return {
        'x': normal((BATCH, SEQ, D_MODEL), 1.0),
        'p': normal((L, BATCH, SEQ, PLE_DIM), 1.0),
        'positions': (jax.random.randint(next(ks), (BATCH, 1), 0, 1024) + jnp.arange(SEQ)[None, :]).astype(jnp.int32),
        'w_in': normal((L, D_MODEL, N_IN), D_MODEL ** -0.5),
        'mix_norm': gain((L, D_MODEL)),
        'ffn_norm': gain((L, D_MODEL)),
        'ple_norm': gain((L, D_MODEL)),
        'rw_mu': jax.random.uniform(next(ks), (L, RW_IN), f32),
        'rw_w0': -1.5 + normal((L, RW_W), 0.5),
        'rw_w2': normal((L, RW_LORA_W, RW_W), 0.5 * RW_LORA_W ** -0.5),
        'rw_a0': normal((L, RW_W), 0.1),
        'rw_a2': normal((L, RW_LORA_A, RW_W), 0.5 * RW_LORA_A ** -0.5),
        'rw_g2': normal((L, RW_LORA_G, RW_W), RW_LORA_G ** -0.5),
        'rw_kk': 0.85 + normal((L, RW_W), 0.05),
        'rw_ka': gain((L, RW_W)),
        'rw_rk': normal((L, RW_HEADS, RW_DIM), 0.1),
        'rw_ln_g': gain((L, RW_W)),
        'rw_ln_b': normal((L, RW_W), 0.01),
        'dsa_q_norm': gain((L, DSA_DIM)),
        'dsa_k_norm': gain((L, DSA_DIM)),
        'dsa_kv_norm': gain((L, DSA_KV_RANK)),
        'dsa_w_uk': normal((L, DSA_KV_RANK, DSA_DIM), DSA_KV_RANK ** -0.5),
        'dsa_w_uv': normal((L, DSA_KV_RANK, DSA_DIM), DSA_KV_RANK ** -0.5),
        'sw_q_norm': gain((L, SW_DIM)),
        'sw_k_norm': gain((L, SW_DIM)),
        'sw_sinks': normal((L, SW_HEADS), 1.0),
        'w_branch': normal((L, N_BRANCH, BRANCH_W, D_MODEL), BRANCH_W ** -0.5),
        'w_out': normal((L, D_MODEL, D_MODEL), D_MODEL ** -0.5),
        'ffn_w_gate': normal((N_DENSE_LAYERS, D_MODEL, FFN_DENSE), D_MODEL ** -0.5),
        'ffn_w_up': normal((N_DENSE_LAYERS, D_MODEL, FFN_DENSE), D_MODEL ** -0.5),
        'ffn_w_down': normal((N_DENSE_LAYERS, FFN_DENSE, D_MODEL), FFN_DENSE ** -0.5),
        'moe_router': normal((N_MOE_LAYERS, D_MODEL, N_EXPERTS), D_MODEL ** -0.5),
        'moe_w_gate': normal((N_MOE_LAYERS, N_EXPERTS, D_MODEL, FFN_EXPERT), D_MODEL ** -0.5),
        'moe_w_up': normal((N_MOE_LAYERS, N_EXPERTS, D_MODEL, FFN_EXPERT), D_MODEL ** -0.5),
        'moe_w_down': normal((N_MOE_LAYERS, N_EXPERTS, FFN_EXPERT, D_MODEL), FFN_EXPERT ** -0.5),
        'ple_w_gate': normal((L, D_MODEL, D_MODEL), D_MODEL ** -0.5),
        'ple_w_proj': normal((L, PLE_DIM, D_MODEL), PLE_DIM ** -0.5),
    }


def reference(x, p, positions, w_in, mix_norm, ffn_norm, ple_norm,
              rw_mu, rw_w0, rw_w2, rw_a0, rw_a2, rw_g2, rw_kk, rw_ka, rw_rk, rw_ln_g, rw_ln_b,
              dsa_q_norm, dsa_k_norm, dsa_kv_norm, dsa_w_uk, dsa_w_uv,
              sw_q_norm, sw_k_norm, sw_sinks,
              w_branch, w_out,
              ffn_w_gate, ffn_w_up, ffn_w_down,
              moe_router, moe_w_gate, moe_w_up, moe_w_down,
              ple_w_gate, ple_w_proj):
    b, s, _ = x.shape
    for i in range(DEPTH):
        h = rms_norm(x, mix_norm[i])
        (sb_q, sb_k, sb_v, rw_cols, dsa_q, dsa_ckv, idx_q, idx_k, idx_w,
         sw_q, sw_k, sw_v, gate_cols) = split_cols(h @ w_in[i], IN_SIZES)
        y_a = stick_breaking_attention(split_heads(sb_q, SB_HEADS), split_heads(sb_k, SB_HEADS),
                                       split_heads(sb_v, SB_HEADS))
        y_b = rwkv7_time_mix(rw_cols, rw_mu[i], rw_w0[i], rw_w2[i], rw_a0[i], rw_a2[i], rw_g2[i],
                             rw_kk[i], rw_ka[i], rw_rk[i], rw_ln_g[i], rw_ln_b[i])
        y_c = dsa_attention(dsa_q, dsa_ckv, idx_q, idx_k, idx_w, positions,
                            dsa_q_norm[i], dsa_k_norm[i], dsa_kv_norm[i], dsa_w_uk[i], dsa_w_uv[i])
        y_d = sliding_window_attention(split_heads(sw_q, SW_HEADS), split_heads(sw_k, SW_KV_HEADS),
                                       split_heads(sw_v, SW_KV_HEADS), positions,
                                       sw_q_norm[i], sw_k_norm[i], sw_sinks[i])
        branches = jnp.stack([y_a, y_b, y_c, y_d], axis=2)
        gates = jax.nn.sigmoid(gate_cols.reshape(b, s, N_BRANCH, D_MODEL))
        up = jnp.einsum('bsnc,ncd->bsnd', branches, w_branch[i])
        x = x + jnp.sum(gates * up, axis=2) @ w_out[i]
        h = rms_norm(x, ffn_norm[i])
        if i % 2 == 0:
            x = x + swiglu(h, ffn_w_gate[i // 2], ffn_w_up[i // 2], ffn_w_down[i // 2])
        else:
            x = x + moe_swiglu(h, moe_router[i // 2], moe_w_gate[i // 2], moe_w_up[i // 2], moe_w_down[i // 2])
        hp = rms_norm(x, ple_norm[i])
        x = x + jax.nn.sigmoid(hp @ ple_w_gate[i]) * (p[i] @ ple_w_proj[i])
    return x
```

```python
import functools
import math

import jax
import jax.numpy as jnp
from jax import lax
from jax.experimental import pallas as pl
from jax.experimental.pallas import tpu as pltpu

F32 = jnp.float32
BF16 = jnp.bfloat16
I32 = jnp.int32

D_MODEL = 2048
ROPE_THETA = 10000.0
NORM_EPS = 1e-6
N_BRANCH = 4
BRANCH_W = 1024
SB_HEADS, SB_DIM = 8, 128
RW_HEADS, RW_DIM = 16, 64
RW_LORA_W, RW_LORA_A, RW_LORA_G = 96, 96, 64
RW_LORA = RW_LORA_W + RW_LORA_A + RW_LORA_G
RW_GN_EPS = 64e-5
DSA_HEADS, DSA_DIM, DSA_KV_RANK = 8, 128, 256
IDX_HEADS, IDX_DIM, IDX_TOPK_MAX = 16, 64, 256
SW_HEADS, SW_KV_HEADS, SW_DIM, WINDOW = 16, 2, 64, 128
N_EXPERTS, TOP_K = 8, 2

LANES = 128
VMEM_LIMIT_BYTES = 56 * 1024 * 1024

COL_GATE = 0
COL_SBQ = 8192
COL_SBK = 9216
COL_SBV = 10240
COL_DSAQ = 11264
COL_IDXQ = 12288
COL_SWQ = 13312
COL_RWR = 14336
COL_RWK = 15360
COL_RWV = 16384
COL_RWL = 17408
COL_CKV = 17664
COL_SWK = 17920
COL_SWV = 18048
COL_IDXKW = 18176
N_IN_PAD = 18432

RW_CHUNK = 64
RW_PAIRS_PER_STEP = 4


def _cparams(sem, vmem=VMEM_LIMIT_BYTES):
    return pltpu.CompilerParams(dimension_semantics=sem, vmem_limit_bytes=vmem)


def _dot(a, b):
    return jnp.dot(a, b, preferred_element_type=F32)


def _dot_nt(a, b):
    return lax.dot_general(a, b, (((1,), (1,)), ((), ())), preferred_element_type=F32)


def _split3(a):
    a1 = a.astype(BF16)
    r1 = a - a1.astype(F32)
    a2 = r1.astype(BF16)
    a3 = (r1 - a2.astype(F32)).astype(BF16)
    return a1, a2, a3


def _dot_exact_rhs(a, m_bf16):
    a1, a2, a3 = _split3(a)
    return _dot(a1, m_bf16) + _dot(a2, m_bf16) + _dot(a3, m_bf16)


def _dot_exact_lhs(m_bf16, a):
    a1, a2, a3 = _split3(a)
    return _dot(m_bf16, a1) + _dot(m_bf16, a2) + _dot(m_bf16, a3)


def _dot_f32(a, b):
    a1, a2, a3 = _split3(a)
    b1, b2, b3 = _split3(b)
    return (_dot(a1, b1) + _dot(a1, b2) + _dot(a2, b1)) + (_dot(a1, b3) + _dot(a2, b2) + _dot(a3, b1))


def _softplus(z):
    return jnp.maximum(z, 0.0) + jnp.log(1.0 + jnp.exp(-jnp.abs(z)))


def _sigmoid(z):
    return 1.0 / (1.0 + jnp.exp(-z))


def _iota(shape, dim):
    return lax.broadcasted_iota(I32, shape, dim)


def _group_matrix(n, group, value):
    r = _iota((n, n), 0) // group
    c = _iota((n, n), 1) // group
    return jnp.where(r == c, value, 0.0).astype(BF16)


def _swap_halves(u, half):
    if 2 * half == LANES:
        return pltpu.roll(u, half, 1)
    lane = _iota(u.shape, 1)
    return jnp.where(lane % (2 * half) < half, pltpu.roll(u, LANES - half, 1), pltpu.roll(u, half, 1))


def _rmsnorm_kernel(x_ref, g_ref, o_ref):
    x = x_ref[...]
    ms = jnp.mean(x * x, axis=-1, keepdims=True)
    o_ref[...] = (x * lax.rsqrt(ms + NORM_EPS) * g_ref[...]).astype(o_ref.dtype)


def rmsnorm(x, gain, out_dtype=BF16, tm=512):
    t, d = x.shape
    return pl.pallas_call(
        _rmsnorm_kernel,
        grid=(t // tm,),
        in_specs=[pl.BlockSpec((tm, d), lambda i: (i, 0)), pl.BlockSpec((1, d), lambda i: (0, 0))],
        out_specs=pl.BlockSpec((tm, d), lambda i: (i, 0)),
        out_shape=jax.ShapeDtypeStruct((t, d), out_dtype),
        compiler_params=_cparams(("parallel",)),
        name="rmsnorm",
    )(x, gain.reshape(1, d))


def _mm_kernel(*refs, has_res):
    if has_res:
        a_ref, w_ref, r_ref, o_ref = refs
    else:
        a_ref, w_ref, o_ref = refs
    acc = _dot(a_ref[...].astype(BF16), w_ref[...].astype(BF16))
    if has_res:
        acc = acc + r_ref[...]
    o_ref[...] = acc.astype(o_ref.dtype)


def matmul(a, w, *, out_dtype, residual=None, tm=1024, tn=512, name="matmul"):
    m, k = a.shape
    _, n = w.shape
    tm, tn = min(tm, m), min(tn, n)
    in_specs = [pl.BlockSpec((tm, k), lambda i, j: (i, 0)), pl.BlockSpec((k, tn), lambda i, j: (0, j))]
    args = [a, w]
    if residual is not None:
        in_specs.append(pl.BlockSpec((tm, tn), lambda i, j: (i, j)))
        args.append(residual)
    return pl.pallas_call(
        functools.partial(_mm_kernel, has_res=residual is not None),
        grid=(m // tm, n // tn),
        in_specs=in_specs,
        out_specs=pl.BlockSpec((tm, tn), lambda i, j: (i, j)),
        out_shape=jax.ShapeDtypeStruct((m, n), out_dtype),
        compiler_params=_cparams(("parallel", "parallel")),
        name=name,
    )(*args)


def _rope_table_kernel(pos_ref, f64_ref, f32_ref, ca_ref, sa_ref, cb_ref, sb_ref):
    pos = pos_ref[...]
    lane = _iota((1, LANES), 1)
    ang_a = pos * f64_ref[...]
    ang_b = pos * f32_ref[...]
    ca_ref[...] = jnp.cos(ang_a)
    sa_ref[...] = jnp.where(lane < 64, -1.0, 1.0) * jnp.sin(ang_a)
    cb_ref[...] = jnp.cos(ang_b)
    sb_ref[...] = jnp.where(lane % 64 < 32, -1.0, 1.0) * jnp.sin(ang_b)


def rope_tables(positions, tm=512):
    t = positions.size
    pos = positions.reshape(t, 1).astype(F32)
    inv64 = ROPE_THETA ** (-jnp.arange(64, dtype=F32) / 64)
    inv32 = ROPE_THETA ** (-jnp.arange(32, dtype=F32) / 32)
    f64 = jnp.tile(inv64, 2).reshape(1, LANES)
    f32 = jnp.tile(inv32, 4).reshape(1, LANES)
    row = pl.BlockSpec((tm, LANES), lambda i: (i, 0))
    vec = pl.BlockSpec((1, LANES), lambda i: (0, 0))
    return pl.pallas_call(
        _rope_table_kernel,
        grid=(t // tm,),
        in_specs=[pl.BlockSpec((tm, 1), lambda i: (i, 0)), vec, vec],
        out_specs=[row] * 4,
        out_shape=[jax.ShapeDtypeStruct((t, LANES), F32)] * 4,
        compiler_params=_cparams(("parallel",)),
        name="rope_tables",
    )(pos, f64, f32)


SB_TQ = 256
SB_G = 2


def _sb_kernel(q_ref, k_ref, v_ref, o_ref, *, tq, scale):
    qi = pl.program_id(2)
    r = _iota((tq, tq), 0)
    c = _iota((tq, tq), 1)
    later = jnp.where(r > c, 1.0, 0.0).astype(BF16)
    qs = [(q_ref[:, g * SB_DIM:(g + 1) * SB_DIM].astype(F32) * scale).astype(BF16) for g in range(SB_G)]

    def span(j, state):
        off = pl.multiple_of(j * tq, tq)
        before = (c + j * tq) < (r + qi * tq)
        new = []
        for g in range(SB_G):
            carry, acc = state[g]
            kj = k_ref[pl.ds(off, tq), g * SB_DIM:(g + 1) * SB_DIM].astype(BF16)
            vj = v_ref[pl.ds(off, tq), g * SB_DIM:(g + 1) * SB_DIM].astype(BF16)
            z = _dot_nt(qs[g], kj)
            sp = _softplus(z)
            lk = jnp.where(before, -sp, 0.0)
            cs = _dot_exact_rhs(lk, later)
            w = jnp.where(before, jnp.exp(z - sp + cs + carry), 0.0)
            acc = acc + _dot(w.astype(BF16), vj)
            carry = carry + jnp.sum(lk, axis=1, keepdims=True)
            new.append((carry, acc))
        return tuple(new)

    init = tuple((jnp.zeros((tq, 1), F32), jnp.zeros((tq, SB_DIM), F32)) for _ in range(SB_G))
    out = lax.fori_loop(0, qi + 1, lambda i, s: span(qi - i, s), init)
    for g in range(SB_G):
        o_ref[:, g * SB_DIM:(g + 1) * SB_DIM] = out[g][1].astype(o_ref.dtype)


def sb_attention(u, batch, seq):
    t = batch * seq
    tq = min(SB_TQ, seq)
    nq = seq // tq
    gw = SB_G * SB_DIM
    qc, kc, vc = COL_SBQ // gw, COL_SBK // gw, COL_SBV // gw
    return pl.pallas_call(
        functools.partial(_sb_kernel, tq=tq, scale=SB_DIM ** -0.5),
        grid=(batch, SB_HEADS // SB_G, nq),
        in_specs=[
            pl.BlockSpec((tq, gw), lambda b, h, i: (b * nq + i, qc + h)),
            pl.BlockSpec((seq, gw), lambda b, h, i: (b, kc + h)),
            pl.BlockSpec((seq, gw), lambda b, h, i: (b, vc + h)),
        ],
        out_specs=pl.BlockSpec((tq, gw), lambda b, h, i: (b * nq + i, h)),
        out_shape=jax.ShapeDtypeStruct((t, BRANCH_W), BF16),
        compiler_params=_cparams(("parallel", "parallel", "arbitrary")),
        name="sb_attention",
    )(u, u, u)


def _rw_pre_kernel(r_ref, k_ref, v_ref, l_ref, pr_ref, pk_ref, pv_ref, plr_ref,
                   mu_ref, w0_ref, a0_ref, kkg_ref, ka_ref, w2_ref, a2_ref, g2_ref,
                   ro_ref, lw_ref, ko_ref, vo_ref, kk_ref, b_ref, g_ref, *, tm, seq):
    i = pl.program_id(0)
    first = (i * tm) % seq == 0
    row0 = _iota((tm, 1), 0) == 0

    def shifted(cur_ref, prev_ref, lo, hi):
        cur = cur_ref[...].astype(F32)
        last = prev_ref[...].astype(F32)[-1:, :]
        last = jnp.where(first, 0.0, last)
        prev = jnp.where(row0, last, pltpu.roll(cur, 1, 0))
        return cur + mu_ref[:, lo:hi] * (prev - cur)

    w = BRANCH_W
    r = shifted(r_ref, pr_ref, 0, w)
    k = shifted(k_ref, pk_ref, w, 2 * w)
    v = shifted(v_ref, pv_ref, 2 * w, 3 * w)
    z = shifted(l_ref, plr_ref, 3 * w, 3 * w + RW_LORA)

    w_pre = w0_ref[...] + _dot(jnp.tanh(z).astype(BF16), w2_ref[...])
    w_log = -_softplus(-w_pre) - 0.5
    lw_ref[...] = -jnp.exp(w_log)
    a = _sigmoid(a0_ref[...] + _dot(z.astype(BF16), a2_ref[...]))
    g_ref[...] = _dot(_sigmoid(z).astype(BF16), g2_ref[...]).astype(g_ref.dtype)
    ones = _group_matrix(LANES, RW_DIM, 1.0)
    kk = k * kkg_ref[...]
    for s in range(w // LANES):
        sl = slice(s * LANES, (s + 1) * LANES)
        kks = kk[:, sl]
        ss = _dot_exact_rhs(kks * kks, ones)
        kkn = kks / jnp.maximum(jnp.sqrt(ss), 1e-12)
        kk_ref[:, sl] = kkn
        b_ref[:, sl] = kkn * a[:, sl]
    ro_ref[...] = r
    ko_ref[...] = k * (1.0 + (a - 1.0) * ka_ref[...])
    vo_ref[...] = v


def rw_prepare(u, mu, w0, a0, k_k, k_a, w2p, a2p, g2p, batch, seq, tm=256):
    t = batch * seq
    tm = min(tm, seq)
    w = BRANCH_W
    sub = 16

    def cur(width, col):
        return pl.BlockSpec((tm, width), lambda i: (i, col // width))

    def prev(width, col):
        return pl.BlockSpec((sub, width), lambda i: (jnp.maximum(i * (tm // sub) - 1, 0), col // width))

    def vec(width):
        return pl.BlockSpec((1, width), lambda i: (0, 0))

    def mat():
        return pl.BlockSpec((RW_LORA, w), lambda i: (0, 0))

    out = pl.BlockSpec((tm, w), lambda i: (i, 0))
    f = jax.ShapeDtypeStruct((t, w), F32)
    return pl.pallas_call(
        functools.partial(_rw_pre_kernel, tm=tm, seq=seq),
        grid=(t // tm,),
        in_specs=[cur(w, COL_RWR), cur(w, COL_RWK), cur(w, COL_RWV), cur(RW_LORA, COL_RWL),
                  prev(w, COL_RWR), prev(w, COL_RWK), prev(w, COL_RWV), prev(RW_LORA, COL_RWL),
                  vec(3 * w + RW_LORA), vec(w), vec(w), vec(w), vec(w), mat(), mat(), mat()],
        out_specs=[out] * 7,
        out_shape=[f, f, f, f, f, f, jax.ShapeDtypeStruct((t, w), BF16)],
        compiler_params=_cparams(("parallel",)),
        name="rw_prepare",
    )(u, u, u, u, u, u, u, u, mu, w0, a0, k_k, k_a, w2p, a2p, g2p)


def _rw_scan_kernel(r_ref, lw_ref, k_ref, v_ref, kk_ref, b_ref, g_ref, rk_ref, lng_ref, lnb_ref,
                    o_ref, st_ref, *, chunk):
    c_idx = pl.program_id(2)

    @pl.when(c_idx == 0)
    def _():
        st_ref[...] = jnp.zeros_like(st_ref)

    n = 2 * chunk
    lane = _iota((n, LANES), 1)
    row = _iota((n, LANES), 0)
    own = (lane // RW_DIM) == (row // chunk)
    rr = _iota((n, n), 0)
    cc = _iota((n, n), 1)
    same = (rr // chunk) == (cc // chunk)
    tri_incl = jnp.where(same & (rr >= cc), 1.0, 0.0).astype(BF16)
    strict = same & (rr > cc)
    incl = same & (rr >= cc)
    eye = jnp.where(rr == cc, 1.0, 0.0)
    avg = _group_matrix(LANES, RW_DIM, 1.0 / RW_DIM)
    ones = _group_matrix(LANES, RW_DIM, 1.0)

    def stack(x):
        return jnp.concatenate([x, x], axis=0)

    for p in range(RW_PAIRS_PER_STEP):
        sl = slice(p * LANES, (p + 1) * LANES)
        r, lw, k, v, kk, b = (ref[:, sl] for ref in (r_ref, lw_ref, k_ref, v_ref, kk_ref, b_ref))
        lin = _dot_exact_lhs(tri_incl, stack(lw))
        lex = lin - stack(lw)
        lend = lin[n - 1:n]
        p_in = jnp.exp(lin)
        p_inv = jnp.exp(-lin)
        p_dec = jnp.exp(lend - lin)
        zero = jnp.zeros((n, LANES), F32)
        kk_t = jnp.where(own, stack(kk) * jnp.exp(lex), zero).astype(BF16)
        r_t = jnp.where(own, stack(r) * p_in, zero).astype(BF16)
        b_t = jnp.where(own, stack(b) * p_inv, zero).astype(BF16)
        k_t = jnp.where(own, stack(k) * p_inv, zero).astype(BF16)
        b_d = jnp.where(own, stack(b) * p_dec, zero).astype(BF16)
        k_d = jnp.where(own, stack(k) * p_dec, zero).astype(BF16)
        v_f = jnp.where(own, stack(v), zero)
        v_s = v_f.astype(BF16)

        st = st_ref[p]
        st_b = st.astype(BF16)
        lhs = jnp.concatenate([kk_t, r_t], axis=0)
        a_b = _dot_nt(lhs, b_t)
        a_k = _dot_nt(lhs, k_t)
        s_t = _dot_nt(lhs, st_b)
        n_ab = jnp.where(strict, a_b[:n], 0.0)
        n_ak = jnp.where(strict, a_k[:n], 0.0)
        m_rb = jnp.where(incl, a_b[n:], 0.0)
        m_rk = jnp.where(incl, a_k[n:], 0.0)
        rhs = s_t[:n] + _dot(n_ak.astype(BF16), v_s)
        inv = eye - n_ab
        pw = n_ab
        steps = int(math.log2(chunk)) - 1
        for it in range(steps):
            pw_b = pw.astype(BF16)
            pw = _dot(pw_b, pw_b)
            inv = inv + _dot(inv.astype(BF16), pw.astype(BF16))
        u_s = -_dot(inv.astype(BF16), rhs.astype(BF16))
        u_b = u_s.astype(BF16)
        y2 = s_t[n:] + _dot(m_rb.astype(BF16), u_b) + _dot(m_rk.astype(BF16), v_s)
        y = y2[:chunk] + y2[chunk:]

        st_ref[p] = st * jnp.exp(lend) + _dot(u_s.T.astype(BF16), b_d) + _dot(v_f.T.astype(BF16), k_d)

        mean = _dot_exact_rhs(y, avg)
        d = y - mean
        var = _dot_exact_rhs(d * d, avg)
        yn = d * lax.rsqrt(var + RW_GN_EPS) * lng_ref[:, sl] + lnb_ref[:, sl]
        bonus = _dot_exact_rhs(r * k * rk_ref[:, sl], ones) * v
        o_ref[:, sl] = ((yn + bonus) * g_ref[:, sl].astype(F32)).astype(o_ref.dtype)


def rw_scan(r, lw, k, v, kk, b, g, r_k, ln_g, ln_b, batch, seq):
    t = batch * seq
    chunk = min(RW_CHUNK, seq)
    nc = seq // chunk
    gw = RW_PAIRS_PER_STEP * LANES
    ng = BRANCH_W // gw
    blk = pl.BlockSpec((chunk, gw), lambda bb, gg, c: (bb * nc + c, gg))
    vec = pl.BlockSpec((1, gw), lambda bb, gg, c: (0, gg))
    return pl.pallas_call(
        functools.partial(_rw_scan_kernel, chunk=chunk),
        grid=(batch, ng, nc),
        in_specs=[blk] * 7 + [vec] * 3,
        out_specs=blk,
        out_shape=jax.ShapeDtypeStruct((t, BRANCH_W), BF16),
        scratch_shapes=[pltpu.VMEM((RW_PAIRS_PER_STEP, LANES, LANES), F32)],
        compiler_params=_cparams(("parallel", "parallel", "arbitrary")),
        name="rw_scan",
    )(r, lw, k, v, kk, b, g, r_k, ln_g, ln_b)


def pack_in_columns(w):
    o = {}
    start = 0
    names = ("sbq", "sbk", "sbv", "rw", "dsaq", "ckv", "idxq", "idxk", "idxw", "swq", "swk", "swv", "gate")
    sizes = (1024, 1024, 1024, 3 * 1024 + RW_LORA, 1024, DSA_KV_RANK, 1024, IDX_DIM, IDX_HEADS, 1024, 128, 128,
             N_BRANCH * D_MODEL)
    for name, size in zip(names, sizes):
        o[name] = w[..., start:start + size]
        start += size
    pad = lambda n: jnp.zeros(w.shape[:-1] + (n,), w.dtype)
    out = jnp.concatenate([o["gate"], o["sbq"], o["sbk"], o["sbv"], o["dsaq"], o["idxq"], o["swq"], o["rw"],
                           o["ckv"], o["swk"], o["swv"], o["idxk"], o["idxw"], pad(48), pad(128)], axis=-1)
    assert out.shape[-1] == N_IN_PAD
    return out


def _rope(u, cos, sin, half):
    return u * cos + _swap_halves(u, half) * sin


def _dsa_pre_kernel(q_ref, c_ref, iq_ref, ikw_ref, ca_ref, sa_ref, cb_ref, sb_ref,
                    qg_ref, kg_ref, cg_ref, wuk_ref, wuv_ref,
                    qo_ref, ko_ref, vo_ref, iqo_ref, iko_ref, iwo_ref):
    ca, sa, cb, sb = ca_ref[...], sa_ref[...], cb_ref[...], sb_ref[...]
    for h in range(DSA_HEADS):
        sl = slice(h * DSA_DIM, (h + 1) * DSA_DIM)
        q = q_ref[:, sl].astype(F32)
        q = q * lax.rsqrt(jnp.mean(q * q, axis=-1, keepdims=True) + NORM_EPS) * qg_ref[...]
        qo_ref[:, sl] = (_rope(q, ca, sa, 64) * DSA_DIM ** -0.5).astype(qo_ref.dtype)
    c = c_ref[...].astype(F32)
    c = (c * lax.rsqrt(jnp.mean(c * c, axis=-1, keepdims=True) + NORM_EPS) * cg_ref[...]).astype(BF16)
    k = _dot(c, wuk_ref[...].astype(BF16))
    k = k * lax.rsqrt(jnp.mean(k * k, axis=-1, keepdims=True) + NORM_EPS) * kg_ref[...]
    ko_ref[...] = _rope(k, ca, sa, 64).astype(ko_ref.dtype)
    vo_ref[...] = _dot(c, wuv_ref[...].astype(BF16)).astype(vo_ref.dtype)
    for s in range(IDX_HEADS * IDX_DIM // LANES):
        sl = slice(s * LANES, (s + 1) * LANES)
        iqo_ref[:, sl] = _rope(iq_ref[:, sl].astype(F32), cb, sb, 32).astype(iqo_ref.dtype)
    ikw = ikw_ref[...].astype(F32)
    lane = _iota(ikw.shape, 1)
    ik = _rope(ikw, cb, sb, 32)
    iko_ref[...] = jnp.where(lane < IDX_DIM, ik, pltpu.roll(ik, IDX_DIM, 1)).astype(iko_ref.dtype)
    iw = pltpu.roll(ikw, IDX_DIM, 1) * (IDX_HEADS ** -0.5 * IDX_DIM ** -0.5)
    iwo_ref[...] = jnp.where(lane < IDX_HEADS, iw, 0.0)


def _dsa_kernel(q_ref, iq_ref, iw_ref, k_ref, v_ref, ik_ref, o_ref, key_ref, m_ref, l_ref, acc_ref,
                *, tq, top_k):
    qb = pl.program_id(1)
    nblk = qb + 1
    int_min = jnp.int32(-2 ** 31)
    r = _iota((tq, tq), 0)
    c = _iota((tq, tq), 1)
    lane = _iota((tq, LANES), 1)
    first = lane < IDX_DIM

    lhs = []
    for p in range(IDX_HEADS // 2):
        qp = iq_ref[:, p * LANES:(p + 1) * LANES]
        zero = jnp.zeros_like(qp)
        lhs.append(jnp.concatenate([jnp.where(first, qp, zero), jnp.where(first, zero, qp)], axis=0))
    iw = iw_ref[...]
    wcol = [jnp.broadcast_to(iw[:, h:h + 1], (tq, tq)) for h in range(IDX_HEADS)]

    def score_block(j, _):
        off = pl.multiple_of(j * tq, tq)
        ik = ik_ref[pl.ds(off, tq), :]
        sc = jnp.zeros((tq, tq), F32)
        for p in range(IDX_HEADS // 2):
            z = jnp.maximum(_dot_nt(lhs[p], ik), 0.0)
            sc = sc + z[:tq] * wcol[2 * p] + z[tq:] * wcol[2 * p + 1]
        sc = sc + 0.0
        bits = lax.bitcast_convert_type(sc, I32)
        skey = bits ^ ((bits >> 31) & jnp.int32(0x7FFFFFFF))
        causal = (c + j * tq) <= (r + qb * tq)
        key_ref[:, pl.ds(off, tq)] = jnp.where(causal, skey, int_min)
        return 0

    lax.fori_loop(0, nblk, score_block, 0)

    def count(pred_fn):
        def body(j, cnt):
            off = pl.multiple_of(j * tq, tq)
            return cnt + jnp.where(pred_fn(key_ref[:, pl.ds(off, tq)]), 1.0, 0.0)
        return jnp.sum(lax.fori_loop(0, nblk, body, jnp.zeros((tq, tq), F32)), axis=1, keepdims=True)

    def bit_step(i, t_u):
        cand_u = t_u | (jnp.int32(1) << (31 - i))
        cand_s = cand_u ^ int_min
        cnt = count(lambda kb: kb >= cand_s)
        return jnp.where(cnt >= top_k, cand_u, t_u)

    t_u = lax.fori_loop(0, 32, bit_step, jnp.zeros((tq, 1), I32))
    thr = t_u ^ int_min
    need = top_k - count(lambda kb: kb > thr)

    q_all = jnp.concatenate([q_ref[:, h * DSA_DIM:(h + 1) * DSA_DIM] for h in range(DSA_HEADS)], axis=0)
    earlier = jnp.where(r < c, 1.0, 0.0).astype(BF16)
    m_ref[...] = jnp.full_like(m_ref, -1e30)
    l_ref[...] = jnp.zeros_like(l_ref)
    acc_ref[...] = jnp.zeros_like(acc_ref)

    def attend(j, ties_seen):
        off = pl.multiple_of(j * tq, tq)
        kb = key_ref[:, pl.ds(off, tq)]
        causal = (c + j * tq) <= (r + qb * tq)
        tie = causal & (kb == thr)
        tie_f = jnp.where(tie, 1.0, 0.0)
        rank = ties_seen + _dot(tie_f.astype(BF16), earlier)
        sel = causal & ((kb > thr) | (tie & (rank < need)))
        kj = k_ref[pl.ds(off, tq), :]
        vj = v_ref[pl.ds(off, tq), :]
        logit = _dot_nt(q_all, kj).reshape(DSA_HEADS, tq, tq)
        sel3 = jnp.broadcast_to(sel[None], logit.shape)
        m_old = m_ref[...]
        m_new = jnp.maximum(m_old, jnp.max(jnp.where(sel3, logit, -1e30), axis=2, keepdims=True))
        pr = jnp.where(sel3, jnp.exp(logit - m_new), 0.0)
        alpha = jnp.exp(m_old - m_new)
        l_ref[...] = alpha * l_ref[...] + jnp.sum(pr, axis=2, keepdims=True)
        pv = _dot(pr.reshape(DSA_HEADS * tq, tq).astype(BF16), vj).reshape(DSA_HEADS, tq, DSA_DIM)
        acc_ref[...] = alpha * acc_ref[...] + pv
        m_ref[...] = m_new
        return ties_seen + jnp.sum(tie_f, axis=1, keepdims=True)

    lax.fori_loop(0, nblk, attend, jnp.zeros((tq, 1), F32))
    out = acc_ref[...] / l_ref[...]
    for h in range(DSA_HEADS):
        o_ref[:, h * DSA_DIM:(h + 1) * DSA_DIM] = out[h].astype(o_ref.dtype)


def dsa_attention(u, tables, q_gain, k_gain, kv_gain, w_uk, w_uv, batch, seq, tm=256, tq=128):
    t = batch * seq
    tm = min(tm, seq)
    ca, sa, cb, sb = tables
    w = BRANCH_W
    row = lambda width, col: pl.BlockSpec((tm, width), lambda i: (i, col // width))
    tab = pl.BlockSpec((tm, LANES), lambda i: (i, 0))
    vec = lambda width: pl.BlockSpec((1, width), lambda i: (0, 0))
    mat = pl.BlockSpec((DSA_KV_RANK, DSA_DIM), lambda i: (0, 0))
    o_w = pl.BlockSpec((tm, w), lambda i: (i, 0))
    o_n = pl.BlockSpec((tm, LANES), lambda i: (i, 0))
    q, k, v, iq, ik, iw = pl.pallas_call(
        _dsa_pre_kernel,
        grid=(t // tm,),
        in_specs=[row(w, COL_DSAQ), row(DSA_KV_RANK, COL_CKV), row(w, COL_IDXQ), row(LANES, COL_IDXKW),
                  tab, tab, tab, tab, vec(DSA_DIM), vec(DSA_DIM), vec(DSA_KV_RANK), mat, mat],
        out_specs=[o_w, o_n, o_n, o_w, o_n, o_n],
        out_shape=[jax.ShapeDtypeStruct((t, w), BF16), jax.ShapeDtypeStruct((t, LANES), BF16),
                   jax.ShapeDtypeStruct((t, LANES), BF16), jax.ShapeDtypeStruct((t, w), BF16),
                   jax.ShapeDtypeStruct((t, LANES), BF16), jax.ShapeDtypeStruct((t, LANES), F32)],
        compiler_params=_cparams(("parallel",)),
        name="dsa_prepare",
    )(u, u, u, u, ca, sa, cb, sb, q_gain.reshape(1, -1), k_gain.reshape(1, -1), kv_gain.reshape(1, -1), w_uk, w_uv)

    nq = seq // tq
    top_k = min(IDX_TOPK_MAX, seq // 4)
    qrow = lambda width: pl.BlockSpec((tq, width), lambda b, i: (b * nq + i, 0))
    full = pl.BlockSpec((seq, LANES), lambda b, i: (b, 0))
    return pl.pallas_call(
        functools.partial(_dsa_kernel, tq=tq, top_k=top_k),
        grid=(batch, nq),
        in_specs=[qrow(w), qrow(w), qrow(LANES), full, full, full],
        out_specs=qrow(w),
        out_shape=jax.ShapeDtypeStruct((t, w), BF16),
        scratch_shapes=[pltpu.VMEM((tq, seq), I32), pltpu.VMEM((DSA_HEADS, tq, 1), F32),
                        pltpu.VMEM((DSA_HEADS, tq, 1), F32), pltpu.VMEM((DSA_HEADS, tq, DSA_DIM), F32)],
        compiler_params=_cparams(("parallel", "arbitrary")),
        name="dsa_attention",
    )(q, iq, iw, k, v, ik)


def _sw_kernel(sink_ref, q_ref, kp_ref, kc_ref, vp_ref, vc_ref, cbp_ref, sbp_ref, cbc_ref, sbc_ref,
               qg_ref, kg_ref, o_ref, *, blk):
    n = pl.program_id(1)
    avg = _group_matrix(LANES, SW_DIM, 1.0 / SW_DIM)
    lane2 = _iota((2 * blk, LANES), 1)
    lane1 = _iota((blk, LANES), 1)

    def norm(x, gain):
        return x * lax.rsqrt(_dot_exact_rhs(x * x, avg) + NORM_EPS) * gain

    cb = jnp.concatenate([cbp_ref[...], cbc_ref[...]], axis=0)
    sb = jnp.concatenate([sbp_ref[...], sbc_ref[...]], axis=0)
    k = jnp.concatenate([kp_ref[...], kc_ref[...]], axis=0).astype(F32)
    k = _rope(norm(k, kg_ref[...]), cb, sb, 32)
    v = jnp.concatenate([vp_ref[...], vc_ref[...]], axis=0).astype(F32)
    k_sw, v_sw = pltpu.roll(k, SW_DIM, 1), pltpu.roll(v, SW_DIM, 1)
    k2 = [jnp.where(lane2 < SW_DIM, k, k_sw).astype(BF16), jnp.where(lane2 < SW_DIM, k_sw, k).astype(BF16)]
    v2 = [jnp.where(lane2 < SW_DIM, v, v_sw).astype(BF16), jnp.where(lane2 < SW_DIM, v_sw, v).astype(BF16)]

    r = _iota((2 * blk, 2 * blk), 0) % blk
    c = _iota((2 * blk, 2 * blk), 1)
    dist = r - (c - blk)
    mask = (dist >= 0) & (dist < WINDOW) & ((c >= blk) | (n > 0))
    top = _iota((2 * blk, 1), 0) < blk
    cbc, sbc = cbc_ref[...], sbc_ref[...]
    for p in range(SW_HEADS // 2):
        g = (2 * p) // (SW_HEADS // SW_KV_HEADS)
        q = q_ref[:, p * LANES:(p + 1) * LANES].astype(F32)
        q = _rope(norm(q, qg_ref[...]), cbc, sbc, 32) * SW_DIM ** -0.5
        zero = jnp.zeros_like(q)
        lhs = jnp.concatenate([jnp.where(lane1 < SW_DIM, q, zero), jnp.where(lane1 < SW_DIM, zero, q)],
                              axis=0).astype(BF16)
        logit = jnp.where(mask, _dot_nt(lhs, k2[g]), -1e30)
        sink = jnp.where(top, sink_ref[2 * p], sink_ref[2 * p + 1])
        m = jnp.maximum(jnp.max(logit, axis=1, keepdims=True), sink)
        pr = jnp.where(mask, jnp.exp(logit - m), 0.0)
        den = jnp.sum(pr, axis=1, keepdims=True) + jnp.exp(sink - m)
        o2 = _dot((pr / den).astype(BF16), v2[g])
        o_ref[:, p * LANES:(p + 1) * LANES] = jnp.where(lane1 < SW_DIM, o2[:blk], o2[blk:]).astype(o_ref.dtype)


def sw_attention(u, tables, q_gain, k_gain, sinks, batch, seq, blk=128):
    t = batch * seq
    nb = seq // blk
    _, _, cb, sb = tables
    w = BRANCH_W
    cur = lambda width, col: pl.BlockSpec((blk, width), lambda b, i: (b * nb + i, col // width))
    prev = lambda width, col: pl.BlockSpec((blk, width), lambda b, i: (b * nb + jnp.maximum(i - 1, 0), col // width))
    vec = pl.BlockSpec((1, LANES), lambda b, i: (0, 0))
    tile2 = lambda g: jnp.tile(g.reshape(1, SW_DIM), (1, 2))
    return pl.pallas_call(
        functools.partial(_sw_kernel, blk=blk),
        grid=(batch, nb),
        in_specs=[pl.BlockSpec(memory_space=pltpu.SMEM),
                  cur(w, COL_SWQ), prev(LANES, COL_SWK), cur(LANES, COL_SWK), prev(LANES, COL_SWV), cur(LANES, COL_SWV),
                  prev(LANES, 0), prev(LANES, 0), cur(LANES, 0), cur(LANES, 0), vec, vec],
        out_specs=pl.BlockSpec((blk, w), lambda b, i: (b * nb + i, 0)),
        out_shape=jax.ShapeDtypeStruct((t, w), BF16),
        compiler_params=_cparams(("parallel", "parallel")),
        name="sw_attention",
    )(sinks, u, u, u, u, u, cb, sb, cb, sb, tile2(q_gain), tile2(k_gain))


def _merge_kernel(y_ref, g_ref, w_ref, o_ref, acc_ref):
    n = pl.program_id(1)
    up = _dot(y_ref[0], w_ref[0].astype(BF16))
    contrib = _sigmoid(g_ref[...].astype(F32)) * up

    @pl.when(n == 0)
    def _():
        acc_ref[...] = contrib

    @pl.when(n > 0)
    def _():
        acc_ref[...] += contrib

    @pl.when(n == N_BRANCH - 1)
    def _():
        o_ref[...] = acc_ref[...].astype(o_ref.dtype)


def merge_branches(ys, u, w_branch, tm=1024):
    _, t, w = ys.shape
    d = w_branch.shape[-1]
    tm = min(tm, t)
    return pl.pallas_call(
        _merge_kernel,
        grid=(t // tm, N_BRANCH),
        in_specs=[pl.BlockSpec((1, tm, w), lambda i, n: (n, i, 0)),
                  pl.BlockSpec((tm, d), lambda i, n: (i, COL_GATE // d + n)),
                  pl.BlockSpec((1, w, d), lambda i, n: (n, 0, 0))],
        out_specs=pl.BlockSpec((tm, d), lambda i, n: (i, 0)),
        out_shape=jax.ShapeDtypeStruct((t, d), BF16),
        scratch_shapes=[pltpu.VMEM((tm, d), F32)],
        compiler_params=_cparams(("parallel", "arbitrary")),
        name="merge_branches",
    )(ys, u, w_branch)


FFN_TM = 1024
FFN_SUB = 512
FFN_TF = 256


def _ffn_kernel(e_ref, rows_ref, x_ref, wg_ref, wu_ref, wd_ref, rw_ref, o_ref, *, nf):
    s = pl.program_id(0)
    f = pl.program_id(1)
    rows = rows_ref[s]

    @pl.when(f == 0)
    def _():
        o_ref[...] = jnp.zeros_like(o_ref)

    wg = wg_ref[0].astype(BF16)
    wu = wu_ref[0].astype(BF16)
    wd = wd_ref[0].astype(BF16)
    for j in range(FFN_TM // FFN_SUB):
        sl = slice(j * FFN_SUB, (j + 1) * FFN_SUB)

        @pl.when(j * FFN_SUB < rows)
        def _():
            xj = x_ref[sl, :]
            g = _dot(xj, wg)
            a = (g * _sigmoid(g) * _dot(xj, wu)).astype(BF16)
            o_ref[sl, :] += _dot(a, wd)

    @pl.when(f == nf - 1)
    def _():
        o_ref[...] = o_ref[...] * rw_ref[...]


def ffn_tiles(xs, tile_expert, tile_rows, w_gate, w_up, w_down, row_w):
    r, d = xs.shape
    _, _, ff = w_gate.shape
    ns = r // FFN_TM
    nf = ff // FFN_TF

    def f_eff(s, f, rows):
        return jnp.where(rows[s] > 0, f, nf - 1)

    grid_spec = pltpu.PrefetchScalarGridSpec(
        num_scalar_prefetch=2,
        grid=(ns, nf),
        in_specs=[pl.BlockSpec((FFN_TM, d), lambda s, f, e, rows: (s, 0)),
                  pl.BlockSpec((1, d, FFN_TF), lambda s, f, e, rows: (e[s], 0, f_eff(s, f, rows))),
                  pl.BlockSpec((1, d, FFN_TF), lambda s, f, e, rows: (e[s], 0, f_eff(s, f, rows))),
                  pl.BlockSpec((1, FFN_TF, d), lambda s, f, e, rows: (e[s], f_eff(s, f, rows), 0)),
                  pl.BlockSpec((FFN_TM, 1), lambda s, f, e, rows: (s, 0))],
        out_specs=pl.BlockSpec((FFN_TM, d), lambda s, f, e, rows: (s, 0)),
    )
    return pl.pallas_call(
        functools.partial(_ffn_kernel, nf=nf),
        grid_spec=grid_spec,
        out_shape=jax.ShapeDtypeStruct((r, d), F32),
        compiler_params=_cparams(("parallel", "arbitrary")),
        name="ffn_tiles",
    )(tile_expert, tile_rows, xs, w_gate, w_up, w_down, row_w)


def _add_kernel(a_ref, b_ref, o_ref):
    o_ref[...] = a_ref[...] + b_ref[...]


def add(a, b, tm=512):
    t, d = a.shape
    spec = pl.BlockSpec((tm, d), lambda i: (i, 0))
    return pl.pallas_call(_add_kernel, grid=(t // tm,), in_specs=[spec, spec], out_specs=spec,
                          out_shape=jax.ShapeDtypeStruct((t, d), a.dtype),
                          compiler_params=_cparams(("parallel",)), name="residual_add")(a, b)


def dense_ffn(x, h, w_gate, w_up, w_down):
    t = h.shape[0]
    ns = t // FFN_TM
    ys = ffn_tiles(h, jnp.zeros((ns,), I32), jnp.full((ns,), FFN_TM, I32),
                   w_gate[None], w_up[None], w_down[None], jnp.ones((t, 1), F32))
    return add(x, ys)


def _router_kernel(x_ref, g_ref, w_ref, h_ref, r_ref):
    x = x_ref[...]
    h = x * lax.rsqrt(jnp.mean(x * x, axis=-1, keepdims=True) + NORM_EPS) * g_ref[...]
    h_ref[...] = h
    logit = _dot_f32(h, w_ref[...])
    lane = _iota(logit.shape, 1).astype(F32)
    neg = -jnp.inf
    l1 = jnp.where(lane < N_EXPERTS, logit, neg)
    m1 = jnp.max(l1, axis=1, keepdims=True)
    i1 = jnp.min(jnp.where(l1 == m1, lane, float(LANES)), axis=1, keepdims=True)
    l2 = jnp.where(lane == i1, neg, l1)
    m2 = jnp.max(l2, axis=1, keepdims=True)
    i2 = jnp.min(jnp.where(l2 == m2, lane, float(LANES)), axis=1, keepdims=True)
    e = jnp.exp(m2 - m1)
    w1 = 1.0 / (1.0 + e)
    w2 = e / (1.0 + e)
    r_ref[...] = jnp.where(lane == 0, i1, jnp.where(lane == 1, i2, jnp.where(lane == 2, w1,
                           jnp.where(lane == 3, w2, 0.0))))


def route(x, gain, router, tm=256):
    t, d = x.shape
    wp = jnp.zeros((d, LANES), F32).at[:, :N_EXPERTS].set(router)
    return pl.pallas_call(
        _router_kernel,
        grid=(t // tm,),
        in_specs=[pl.BlockSpec((tm, d), lambda i: (i, 0)), pl.BlockSpec((1, d), lambda i: (0, 0)),
                  pl.BlockSpec((d, LANES), lambda i: (0, 0))],
        out_specs=[pl.BlockSpec((tm, d), lambda i: (i, 0)), pl.BlockSpec((tm, LANES), lambda i: (i, 0))],
        out_shape=[jax.ShapeDtypeStruct((t, d), F32), jax.ShapeDtypeStruct((t, LANES), F32)],
        compiler_params=_cparams(("parallel",)),
        name="router",
    )(x, gain.reshape(1, d), wp)


def _row_copy(src_ref, row, buf, i, sem):
    return pltpu.make_async_copy(src_ref.at[pl.ds(row, 1)], buf.at[pl.ds(i, 1)], sem)


def _gather_kernel(idx_ref, src_ref, o_ref, buf, sem, *, gb):
    base = pl.program_id(0) * gb

    def issue(i, _):
        _row_copy(src_ref, idx_ref[base + i], buf, i, sem).start()
        return 0

    def wait(i, _):
        _row_copy(src_ref, 0, buf, i, sem).wait()
        return 0

    lax.fori_loop(0, gb, issue, 0)
    lax.fori_loop(0, gb, wait, 0)
    o_ref[...] = buf[...].astype(o_ref.dtype)


def gather_rows(src, idx, out_dtype, gb=256):
    r = idx.shape[0]
    d = src.shape[1]
    grid_spec = pltpu.PrefetchScalarGridSpec(
        num_scalar_prefetch=1,
        grid=(r // gb,),
        in_specs=[pl.BlockSpec(memory_space=pl.ANY)],
        out_specs=pl.BlockSpec((gb, d), lambda i, idx: (i, 0)),
        scratch_shapes=[pltpu.VMEM((gb, d), src.dtype), pltpu.SemaphoreType.DMA(())],
    )
    return pl.pallas_call(
        functools.partial(_gather_kernel, gb=gb),
        grid_spec=grid_spec,
        out_shape=jax.ShapeDtypeStruct((r, d), out_dtype),
        compiler_params=_cparams(("arbitrary",)),
        name="gather_rows",
    )(idx, src)


def _combine_kernel(d0_ref, d1_ref, x_ref, ys_ref, o_ref, b0, b1, sem0, sem1, *, tm):
    base = pl.program_id(0) * tm

    def issue(i, _):
        _row_copy(ys_ref, d0_ref[base + i], b0, i, sem0).start()
        _row_copy(ys_ref, d1_ref[base + i], b1, i, sem1).start()
        return 0

    def wait(i, _):
        _row_copy(ys_ref, 0, b0, i, sem0).wait()
        _row_copy(ys_ref, 0, b1, i, sem1).wait()
        return 0

    lax.fori_loop(0, tm, issue, 0)
    lax.fori_loop(0, tm, wait, 0)
    o_ref[...] = x_ref[...] + b0[...] + b1[...]


def combine_rows(x, ys, d0, d1, tm=256):
    t, d = x.shape
    grid_spec = pltpu.PrefetchScalarGridSpec(
        num_scalar_prefetch=2,
        grid=(t // tm,),
        in_specs=[pl.BlockSpec((tm, d), lambda i, a, b: (i, 0)), pl.BlockSpec(memory_space=pl.ANY)],
        out_specs=pl.BlockSpec((tm, d), lambda i, a, b: (i, 0)),
        scratch_shapes=[pltpu.VMEM((tm, d), F32), pltpu.VMEM((tm, d), F32),
                        pltpu.SemaphoreType.DMA(()), pltpu.SemaphoreType.DMA(())],
    )
    return pl.pallas_call(
        functools.partial(_combine_kernel, tm=tm),
        grid_spec=grid_spec,
        out_shape=jax.ShapeDtypeStruct((t, d), F32),
        compiler_params=_cparams(("arbitrary",)),
        name="combine_rows",
    )(d0, d1, x, ys)


def moe_ffn(x, gain, router, w_gate, w_up, w_down):
    t, d = x.shape
    h, rt = route(x, gain, router)
    e_flat = jnp.concatenate([rt[:, 0], rt[:, 1]]).astype(I32)
    w_flat = jnp.concatenate([rt[:, 2], rt[:, 3]])
    tok = jnp.concatenate([jnp.arange(t, dtype=I32)] * 2)
    onehot = (e_flat[:, None] == jnp.arange(N_EXPERTS, dtype=I32)[None, :]).astype(I32)
    csum = jnp.cumsum(onehot, axis=0)
    rank = jnp.take_along_axis(csum - onehot, e_flat[:, None], axis=1)[:, 0]
    counts = csum[-1]
    n_tiles = (counts + FFN_TM - 1) // FFN_TM
    tile_end = jnp.cumsum(n_tiles)
    tile_start = tile_end - n_tiles
    dest = tile_start[e_flat] * FFN_TM + rank
    ns = TOP_K * t // FFN_TM + N_EXPERTS
    s_idx = jnp.arange(ns, dtype=I32)
    used = s_idx < tile_end[-1]
    s_clip = jnp.minimum(s_idx, tile_end[-1] - 1)
    tile_expert = jnp.minimum(jnp.searchsorted(tile_end, s_clip, side="right"), N_EXPERTS - 1).astype(I32)
    tile_rows = jnp.clip(counts[tile_expert] - (s_clip - tile_start[tile_expert]) * FFN_TM, 0, FFN_TM)
    tile_rows = jnp.where(used, tile_rows, 0).astype(I32)
    src_row = jnp.zeros((ns * FFN_TM,), I32).at[dest].set(tok)
    row_w = jnp.zeros((ns * FFN_TM,), F32).at[dest].set(w_flat)

    xs = gather_rows(h, src_row, BF16)
    ys = ffn_tiles(xs, tile_expert, tile_rows, w_gate, w_up, w_down, row_w.reshape(-1, 1))
    return combine_rows(x, ys, dest[:t], dest[t:])


def _ple_kernel(h_ref, wg_ref, p_ref, wp_ref, x_ref, o_ref):
    gate = _sigmoid(_dot(h_ref[...], wg_ref[...].astype(BF16)))
    proj = _dot(p_ref[...].astype(BF16), wp_ref[...].astype(BF16))
    o_ref[...] = x_ref[...] + gate * proj


def ple(h, w_gate, p, w_proj, x, tm=1024, tn=512):
    t, d = x.shape
    pd = p.shape[1]
    tm = min(tm, t)
    return pl.pallas_call(
        _ple_kernel,
        grid=(t // tm, d // tn),
        in_specs=[pl.BlockSpec((tm, d), lambda i, j: (i, 0)), pl.BlockSpec((d, tn), lambda i, j: (0, j)),
                  pl.BlockSpec((tm, pd), lambda i, j: (i, 0)), pl.BlockSpec((pd, tn), lambda i, j: (0, j)),
                  pl.BlockSpec((tm, tn), lambda i, j: (i, j))],
        out_specs=pl.BlockSpec((tm, tn), lambda i, j: (i, j)),
        out_shape=jax.ShapeDtypeStruct((t, d), F32),
        compiler_params=_cparams(("parallel", "parallel")),
        name="ple",
    )(h, w_gate, p, w_proj, x)


def _pad_rows(w, start, total):
    return jnp.zeros((total, w.shape[1]), F32).at[start:start + w.shape[0]].set(w).astype(BF16)


def kernel(x, p, positions, w_in, mix_norm, ffn_norm, ple_norm, rw_mu, rw_w0, rw_w2, rw_a0, rw_a2, rw_g2, rw_kk,
           rw_ka, rw_rk, rw_ln_g, rw_ln_b, dsa_q_norm, dsa_k_norm, dsa_kv_norm, dsa_w_uk, dsa_w_uv, sw_q_norm,
           sw_k_norm, sw_sinks, w_branch, w_out, ffn_w_gate, ffn_w_up, ffn_w_down, moe_router, moe_w_gate, moe_w_up,
           moe_w_down, ple_w_gate, ple_w_proj):
    b, s, d = x.shape
    t = b * s
    depth = w_in.shape[0]
    xf = x.reshape(t, d)
    tables = rope_tables(positions)
    row = lambda a: a.reshape(1, -1)
    for i in range(depth):
        h = rmsnorm(xf, mix_norm[i])
        u = matmul(h, pack_in_columns(w_in[i].astype(BF16)), out_dtype=BF16, name="in_proj")
        y_a = sb_attention(u, b, s)
        rw = rw_prepare(u, row(rw_mu[i]), row(rw_w0[i]), row(rw_a0[i]), row(rw_kk[i]), row(rw_ka[i]),
                        _pad_rows(rw_w2[i], 0, RW_LORA), _pad_rows(rw_a2[i], RW_LORA_W, RW_LORA),
                        _pad_rows(rw_g2[i], RW_LORA_W + RW_LORA_A, RW_LORA), b, s)
        y_b = rw_scan(*rw, row(rw_rk[i]), row(rw_ln_g[i]), row(rw_ln_b[i]), b, s)
        y_c = dsa_attention(u, tables, dsa_q_norm[i], dsa_k_norm[i], dsa_kv_norm[i], dsa_w_uk[i], dsa_w_uv[i], b, s)
        y_d = sw_attention(u, tables, sw_q_norm[i], sw_k_norm[i], sw_sinks[i], b, s)
        merged = merge_branches(jnp.stack([y_a, y_b, y_c, y_d]), u, w_branch[i].astype(BF16))
        xf = matmul(merged, w_out[i], out_dtype=F32, residual=xf, name="out_proj")
        if i % 2 == 0:
            h = rmsnorm(xf, ffn_norm[i])
            xf = dense_ffn(xf, h, ffn_w_gate[i // 2], ffn_w_up[i // 2], ffn_w_down[i // 2])
        else:
            xf = moe_ffn(xf, ffn_norm[i], moe_router[i // 2], moe_w_gate[i // 2], moe_w_up[i // 2],
                         moe_w_down[i // 2])
        h = rmsnorm(xf, ple_norm[i])
        xf = ple(h, ple_w_gate[i], p[i].reshape(t, -1), ple_w_proj[i], xf)
    return xf.reshape(b, s, d)
```

```python
import functools
import math

import jax
import jax.numpy as jnp
from jax import lax
from jax.experimental import pallas as pl
from jax.experimental.pallas import tpu as pltpu

F32 = jnp.float32
BF16 = jnp.bfloat16
I32 = jnp.int32

D_MODEL = 2048
ROPE_THETA = 10000.0
NORM_EPS = 1e-6
N_BRANCH = 4
BRANCH_W = 1024
SB_HEADS, SB_DIM = 8, 128
RW_HEADS, RW_DIM = 16, 64
RW_LORA_W, RW_LORA_A, RW_LORA_G = 96, 96, 64
RW_LORA = RW_LORA_W + RW_LORA_A + RW_LORA_G
RW_GN_EPS = 64e-5
DSA_HEADS, DSA_DIM, DSA_KV_RANK = 8, 128, 256
IDX_HEADS, IDX_DIM, IDX_TOPK_MAX = 16, 64, 256
SW_HEADS, SW_KV_HEADS, SW_DIM, WINDOW = 16, 2, 64, 128
N_EXPERTS, TOP_K = 8, 2

LANES = 128
VMEM_LIMIT_BYTES = 56 * 1024 * 1024

COL_GATE = 0
COL_SBQ = 8192
COL_SBK = 9216
COL_SBV = 10240
COL_DSAQ = 11264
COL_IDXQ = 12288
COL_SWQ = 13312
COL_RWR = 14336
COL_RWK = 15360
COL_RWV = 16384
COL_RWL = 17408
COL_CKV = 17664
COL_SWK = 17920
COL_SWV = 18048
COL_IDXKW = 18176
N_IN_PAD = 18432

RW_CHUNK = 64


def _cparams(sem, vmem=VMEM_LIMIT_BYTES):
    return pltpu.CompilerParams(dimension_semantics=sem, vmem_limit_bytes=vmem)


def _dot(a, b):
    return jnp.dot(a, b, preferred_element_type=F32)


def _dot_nt(a, b):
    return lax.dot_general(a, b, (((1,), (1,)), ((), ())), preferred_element_type=F32)


def _split3(a):
    a1 = a.astype(BF16)
    r1 = a - a1.astype(F32)
    a2 = r1.astype(BF16)
    a3 = (r1 - a2.astype(F32)).astype(BF16)
    return a1, a2, a3


def _dot_exact_rhs(a, m_bf16):
    a1, a2, a3 = _split3(a)
    return _dot(a1, m_bf16) + _dot(a2, m_bf16) + _dot(a3, m_bf16)


def _dot_exact_lhs(m_bf16, a):
    a1, a2, a3 = _split3(a)
    return _dot(m_bf16, a1) + _dot(m_bf16, a2) + _dot(m_bf16, a3)


def _dot_f32(a, b):
    a1, a2, a3 = _split3(a)
    b1, b2, b3 = _split3(b)
    return (_dot(a1, b1) + _dot(a1, b2) + _dot(a2, b1)) + (_dot(a1, b3) + _dot(a2, b2) + _dot(a3, b1))


def _softplus(z):
    return jnp.maximum(z, 0.0) + jnp.log(1.0 + jnp.exp(-jnp.abs(z)))


def _sigmoid(z):
    return 1.0 / (1.0 + jnp.exp(-z))


def _iota(shape, dim):
    return lax.broadcasted_iota(I32, shape, dim)


def _group_matrix(n, group, value):
    r = _iota((n, n), 0) // group
    c = _iota((n, n), 1) // group
    return jnp.where(r == c, value, 0.0).astype(BF16)


def _swap_halves(u, half):
    if 2 * half == LANES:
        return pltpu.roll(u, half, 1)
    lane = _iota(u.shape, 1)
    return jnp.where(lane % (2 * half) < half, pltpu.roll(u, LANES - half, 1), pltpu.roll(u, half, 1))


def _rmsnorm_kernel(x_ref, g_ref, o_ref):
    x = x_ref[...]
    ms = jnp.mean(x * x, axis=-1, keepdims=True)
    o_ref[...] = (x * lax.rsqrt(ms + NORM_EPS) * g_ref[...]).astype(o_ref.dtype)


def rmsnorm(x, gain, out_dtype=BF16, tm=512):
    t, d = x.shape
    return pl.pallas_call(
        _rmsnorm_kernel,
        grid=(t // tm,),
        in_specs=[pl.BlockSpec((tm, d), lambda i: (i, 0)), pl.BlockSpec((1, d), lambda i: (0, 0))],
        out_specs=pl.BlockSpec((tm, d), lambda i: (i, 0)),
        out_shape=jax.ShapeDtypeStruct((t, d), out_dtype),
        compiler_params=_cparams(("parallel",)),
        name="rmsnorm",
    )(x, gain.reshape(1, d))


def _mm_kernel(*refs, has_res):
    if has_res:
        a_ref, w_ref, r_ref, o_ref = refs
    else:
        a_ref, w_ref, o_ref = refs
    acc = _dot(a_ref[...].astype(BF16), w_ref[...].astype(BF16))
    if has_res:
        acc = acc + r_ref[...]
    o_ref[...] = acc.astype(o_ref.dtype)


def matmul(a, w, *, out_dtype, residual=None, tm=1024, tn=512, name="matmul"):
    m, k = a.shape
    _, n = w.shape
    tm, tn = min(tm, m), min(tn, n)
    in_specs = [pl.BlockSpec((tm, k), lambda i, j: (i, 0)), pl.BlockSpec((k, tn), lambda i, j: (0, j))]
    args = [a, w]
    if residual is not None:
        in_specs.append(pl.BlockSpec((tm, tn), lambda i, j: (i, j)))
        args.append(residual)
    return pl.pallas_call(
        functools.partial(_mm_kernel, has_res=residual is not None),
        grid=(m // tm, n // tn),
        in_specs=in_specs,
        out_specs=pl.BlockSpec((tm, tn), lambda i, j: (i, j)),
        out_shape=jax.ShapeDtypeStruct((m, n), out_dtype),
        compiler_params=_cparams(("parallel", "parallel")),
        name=name,
    )(*args)


def _rope_table_kernel(pos_ref, f64_ref, f32_ref, ca_ref, sa_ref, cb_ref, sb_ref):
    pos = pos_ref[...]
    lane = _iota((1, LANES), 1)
    ang_a = pos * f64_ref[...]
    ang_b = pos * f32_ref[...]
    ca_ref[...] = jnp.cos(ang_a)
    sa_ref[...] = jnp.where(lane < 64, -1.0, 1.0) * jnp.sin(ang_a)
    cb_ref[...] = jnp.cos(ang_b)
    sb_ref[...] = jnp.where(lane % 64 < 32, -1.0, 1.0) * jnp.sin(ang_b)


def rope_tables(positions, tm=512):
    t = positions.size
    pos = positions.reshape(t, 1).astype(F32)
    inv64 = ROPE_THETA ** (-jnp.arange(64, dtype=F32) / 64)
    inv32 = ROPE_THETA ** (-jnp.arange(32, dtype=F32) / 32)
    f64 = jnp.tile(inv64, 2).reshape(1, LANES)
    f32 = jnp.tile(inv32, 4).reshape(1, LANES)
    row = pl.BlockSpec((tm, LANES), lambda i: (i, 0))
    vec = pl.BlockSpec((1, LANES), lambda i: (0, 0))
    return pl.pallas_call(
        _rope_table_kernel,
        grid=(t // tm,),
        in_specs=[pl.BlockSpec((tm, 1), lambda i: (i, 0)), vec, vec],
        out_specs=[row] * 4,
        out_shape=[jax.ShapeDtypeStruct((t, LANES), F32)] * 4,
        compiler_params=_cparams(("parallel",)),
        name="rope_tables",
    )(pos, f64, f32)


SB_TQ = 256
SB_G = 2


def _sb_kernel(q_ref, k_ref, v_ref, o_ref, *, tq, scale):
    qi = pl.program_id(2)
    r = _iota((tq, tq), 0)
    c = _iota((tq, tq), 1)
    later = jnp.where(r > c, 1.0, 0.0).astype(BF16)
    qs = [(q_ref[:, g * SB_DIM:(g + 1) * SB_DIM].astype(F32) * scale).astype(BF16) for g in range(SB_G)]

    def span(j, state):
        off = pl.multiple_of(j * tq, tq)
        before = (c + j * tq) < (r + qi * tq)
        new = []
        for g in range(SB_G):
            carry, acc = state[g]
            kj = k_ref[pl.ds(off, tq), g * SB_DIM:(g + 1) * SB_DIM].astype(BF16)
            vj = v_ref[pl.ds(off, tq), g * SB_DIM:(g + 1) * SB_DIM].astype(BF16)
            z = _dot_nt(qs[g], kj)
            sp = _softplus(z)
            lk = jnp.where(before, -sp, 0.0)
            cs = _dot_exact_rhs(lk, later)
            w = jnp.where(before, jnp.exp(z - sp + cs + carry), 0.0)
            acc = acc + _dot(w.astype(BF16), vj)
            carry = carry + jnp.sum(lk, axis=1, keepdims=True)
            new.append((carry, acc))
        return tuple(new)

    init = tuple((jnp.zeros((tq, 1), F32), jnp.zeros((tq, SB_DIM), F32)) for _ in range(SB_G))
    out = lax.fori_loop(0, qi + 1, lambda i, s: span(qi - i, s), init)
    for g in range(SB_G):
        o_ref[:, g * SB_DIM:(g + 1) * SB_DIM] = out[g][1].astype(o_ref.dtype)


def sb_attention(u, batch, seq):
    t = batch * seq
    tq = min(SB_TQ, seq)
    nq = seq // tq
    gw = SB_G * SB_DIM
    qc, kc, vc = COL_SBQ // gw, COL_SBK // gw, COL_SBV // gw
    return pl.pallas_call(
        functools.partial(_sb_kernel, tq=tq, scale=SB_DIM ** -0.5),
        grid=(batch, SB_HEADS // SB_G, nq),
        in_specs=[
            pl.BlockSpec((tq, gw), lambda b, h, i: (b * nq + i, qc + h)),
            pl.BlockSpec((seq, gw), lambda b, h, i: (b, kc + h)),
            pl.BlockSpec((seq, gw), lambda b, h, i: (b, vc + h)),
        ],
        out_specs=pl.BlockSpec((tq, gw), lambda b, h, i: (b * nq + i, h)),
        out_shape=jax.ShapeDtypeStruct((t, BRANCH_W), BF16),
        compiler_params=_cparams(("parallel", "parallel", "arbitrary")),
        name="sb_attention",
    )(u, u, u)


def _rw_pre_kernel(r_ref, k_ref, v_ref, l_ref, pr_ref, pk_ref, pv_ref, plr_ref,
                   mu_ref, w0_ref, a0_ref, kkg_ref, ka_ref, w2_ref, a2_ref, g2_ref,
                   ro_ref, lw_ref, ko_ref, vo_ref, kk_ref, b_ref, g_ref, *, tm, seq):
    i = pl.program_id(0)
    first = (i * tm) % seq == 0
    row0 = _iota((tm, 1), 0) == 0

    def shifted(cur_ref, prev_ref, lo, hi):
        cur = cur_ref[...].astype(F32)
        last = prev_ref[...].astype(F32)[-1:, :]
        last = jnp.where(first, 0.0, last)
        prev = jnp.where(row0, last, pltpu.roll(cur, 1, 0))
        return cur + mu_ref[:, lo:hi] * (prev - cur)

    w = BRANCH_W
    r = shifted(r_ref, pr_ref, 0, w)
    k = shifted(k_ref, pk_ref, w, 2 * w)
    v = shifted(v_ref, pv_ref, 2 * w, 3 * w)
    z = shifted(l_ref, plr_ref, 3 * w, 3 * w + RW_LORA)

    w_pre = w0_ref[...] + _dot(jnp.tanh(z).astype(BF16), w2_ref[...])
    w_log = -_softplus(-w_pre) - 0.5
    lw_ref[...] = -jnp.exp(w_log)
    a = _sigmoid(a0_ref[...] + _dot(z.astype(BF16), a2_ref[...]))
    g_ref[...] = _dot(_sigmoid(z).astype(BF16), g2_ref[...]).astype(g_ref.dtype)
    ones = _group_matrix(LANES, RW_DIM, 1.0)
    kk = k * kkg_ref[...]
    for s in range(w // LANES):
        sl = slice(s * LANES, (s + 1) * LANES)
        kks = kk[:, sl]
        ss = _dot_exact_rhs(kks * kks, ones)
        kkn = kks / jnp.maximum(jnp.sqrt(ss), 1e-12)
        kk_ref[:, sl] = kkn
        b_ref[:, sl] = kkn * a[:, sl]
    ro_ref[...] = r
    ko_ref[...] = k * (1.0 + (a - 1.0) * ka_ref[...])
    vo_ref[...] = v


def rw_prepare(u, mu, w0, a0, k_k, k_a, w2p, a2p, g2p, batch, seq, tm=256):
    t = batch * seq
    tm = min(tm, seq)
    w = BRANCH_W
    sub = 16

    def cur(width, col):
        return pl.BlockSpec((tm, width), lambda i: (i, col // width))

    def prev(width, col):
        return pl.BlockSpec((sub, width), lambda i: (jnp.maximum(i * (tm // sub) - 1, 0), col // width))

    def vec(width):
        return pl.BlockSpec((1, width), lambda i: (0, 0))

    def mat():
        return pl.BlockSpec((RW_LORA, w), lambda i: (0, 0))

    out = pl.BlockSpec((tm, w), lambda i: (i, 0))
    f = jax.ShapeDtypeStruct((t, w), F32)
    return pl.pallas_call(
        functools.partial(_rw_pre_kernel, tm=tm, seq=seq),
        grid=(t // tm,),
        in_specs=[cur(w, COL_RWR), cur(w, COL_RWK), cur(w, COL_RWV), cur(RW_LORA, COL_RWL),
                  prev(w, COL_RWR), prev(w, COL_RWK), prev(w, COL_RWV), prev(RW_LORA, COL_RWL),
                  vec(3 * w + RW_LORA), vec(w), vec(w), vec(w), vec(w), mat(), mat(), mat()],
        out_specs=[out] * 7,
        out_shape=[f, f, f, f, f, f, jax.ShapeDtypeStruct((t, w), BF16)],
        compiler_params=_cparams(("parallel",)),
        name="rw_prepare",
    )(u, u, u, u, u, u, u, u, mu, w0, a0, k_k, k_a, w2p, a2p, g2p)


def _rw_scan_kernel(r_ref, lw_ref, k_ref, v_ref, kk_ref, b_ref, g_ref, rk_ref, lng_ref, lnb_ref,
                    o_ref, st_ref, *, chunk):
    @pl.when(pl.program_id(1) == 0)
    def _():
        st_ref[...] = jnp.zeros_like(st_ref)

    npair = RW_HEADS // 2
    w = npair * LANES
    n = 2 * chunk
    lane = _iota((n, w), 1)
    row = _iota((n, w), 0)
    own = ((lane // RW_DIM) % 2) == (row // chunk)
    rr = _iota((n, n), 0)
    cc = _iota((n, n), 1)
    same = (rr // chunk) == (cc // chunk)
    tri_incl = jnp.where(same & (rr >= cc), 1.0, 0.0).astype(BF16)
    strict = same & (rr > cc)
    incl = same & (rr >= cc)
    eye = jnp.where(rr == cc, 1.0, 0.0)
    blockdiag = (_iota((LANES, LANES), 0) // RW_DIM) == (_iota((LANES, LANES), 1) // RW_DIM)
    avg = _group_matrix(LANES, RW_DIM, 1.0 / RW_DIM)
    ones = _group_matrix(LANES, RW_DIM, 1.0)
    pairs = range(npair)
    sl = [slice(p * LANES, (p + 1) * LANES) for p in pairs]

    def stack(x):
        return jnp.concatenate([x, x], axis=0)

    def per_pair_rows(x):
        return jnp.concatenate([x[:, s] for s in sl], axis=0)

    def per_pair_lanes(x):
        return jnp.concatenate([x[p * chunk:(p + 1) * chunk] for p in pairs], axis=1)

    r, lw, k, v, kk, b = (ref[...] for ref in (r_ref, lw_ref, k_ref, v_ref, kk_ref, b_ref))
    lw2 = stack(lw)
    l1 = lw2.astype(BF16)
    l2 = (lw2 - l1.astype(F32)).astype(BF16)
    lin = _dot(tri_incl, l1) + _dot(tri_incl, l2)
    lend = lin[n - 1:n]
    p_inv = jnp.exp(-lin)
    p_dec = jnp.exp(lend - lin)
    zero = jnp.zeros((n, w), F32)
    kk_t = jnp.where(own, stack(kk) * jnp.exp(lin - lw2), zero).astype(BF16)
    r_t = jnp.where(own, stack(r) * jnp.exp(lin), zero).astype(BF16)
    v_f = jnp.where(own, stack(v), zero)
    v_s = v_f.astype(BF16)
    b_t = (stack(b) * p_inv).astype(BF16)
    k_t = (stack(k) * p_inv).astype(BF16)
    b_d = (stack(b) * p_dec).astype(BF16)
    k_d = (stack(k) * p_dec).astype(BF16)
    p_end = jnp.exp(lend)

    lhs = [jnp.concatenate([kk_t[:, s], r_t[:, s]], axis=0) for s in sl]
    a_b = [_dot_nt(lhs[p], b_t[:, sl[p]]) for p in pairs]
    a_k = [_dot_nt(lhs[p], k_t[:, sl[p]]) for p in pairs]
    s_t = [_dot_nt(lhs[p], st_ref[p].astype(BF16)) for p in pairs]
    n_ab = [jnp.where(strict, a_b[p][:n], 0.0) for p in pairs]
    rhs = [s_t[p][:n] + _dot(jnp.where(strict, a_k[p][:n], 0.0).astype(BF16), v_s[:, sl[p]]) for p in pairs]
    inv = [eye - n_ab[p] for p in pairs]
    pw = n_ab
    for _ in range(int(math.log2(chunk)) - 1):
        pw_b = [pw[p].astype(BF16) for p in pairs]
        pw = [_dot(pw_b[p], pw_b[p]) for p in pairs]
        inv = [inv[p] + _dot(inv[p].astype(BF16), pw[p].astype(BF16)) for p in pairs]
    u_s = [-_dot(inv[p].astype(BF16), rhs[p].astype(BF16)) for p in pairs]
    y2 = [s_t[p][n:] + _dot(jnp.where(incl, a_b[p][n:], 0.0).astype(BF16), u_s[p].astype(BF16))
          + _dot(jnp.where(incl, a_k[p][n:], 0.0).astype(BF16), v_s[:, sl[p]]) for p in pairs]
    for p in pairs:
        upd = _dot(u_s[p].T.astype(BF16), b_d[:, sl[p]]) + _dot(v_f[:, sl[p]].T.astype(BF16), k_d[:, sl[p]])
        st_ref[p] = st_ref[p] * p_end[:, sl[p]] + jnp.where(blockdiag, upd, 0.0)

    yr = jnp.concatenate([y2[p][:chunk] + y2[p][chunk:] for p in pairs], axis=0)
    yr1, yr2, _ = _split3(yr)
    d = yr - (_dot(yr1, avg) + _dot(yr2, avg))
    dd1, dd2, _ = _split3(d * d)
    var = _dot(dd1, avg) + _dot(dd2, avg)
    yn = per_pair_lanes(d * lax.rsqrt(var + RW_GN_EPS)) * lng_ref[...] + lnb_ref[...]
    bonus = per_pair_lanes(_dot(per_pair_rows(r * k * rk_ref[...]).astype(BF16), ones)) * v
    o_ref[...] = ((yn + bonus) * g_ref[...].astype(F32)).astype(o_ref.dtype)


def rw_scan(r, lw, k, v, kk, b, g, r_k, ln_g, ln_b, batch, seq):
    t = batch * seq
    chunk = min(RW_CHUNK, seq)
    nc = seq // chunk
    w = BRANCH_W
    blk = pl.BlockSpec((chunk, w), lambda bb, c: (bb * nc + c, 0))
    vec = pl.BlockSpec((1, w), lambda bb, c: (0, 0))
    return pl.pallas_call(
        functools.partial(_rw_scan_kernel, chunk=chunk),
        grid=(batch, nc),
        in_specs=[blk] * 7 + [vec] * 3,
        out_specs=blk,
        out_shape=jax.ShapeDtypeStruct((t, w), BF16),
        scratch_shapes=[pltpu.VMEM((RW_HEADS // 2, LANES, LANES), F32)],
        compiler_params=_cparams(("parallel", "arbitrary")),
        name="rw_scan",
    )(r, lw, k, v, kk, b, g, r_k, ln_g, ln_b)


def pack_in_columns(w):
    o = {}
    start = 0
    names = ("sbq", "sbk", "sbv", "rw", "dsaq", "ckv", "idxq", "idxk", "idxw", "swq", "swk", "swv", "gate")
    sizes = (1024, 1024, 1024, 3 * 1024 + RW_LORA, 1024, DSA_KV_RANK, 1024, IDX_DIM, IDX_HEADS, 1024, 128, 128,
             N_BRANCH * D_MODEL)
    for name, size in zip(names, sizes):
        o[name] = w[..., start:start + size]
        start += size
    pad = lambda n: jnp.zeros(w.shape[:-1] + (n,), w.dtype)
    out = jnp.concatenate([o["gate"], o["sbq"], o["sbk"], o["sbv"], o["dsaq"], o["idxq"], o["swq"], o["rw"],
                           o["ckv"], o["swk"], o["swv"], o["idxk"], o["idxw"], pad(48), pad(128)], axis=-1)
    assert out.shape[-1] == N_IN_PAD
    return out


def _rope(u, cos, sin, half):
    return u * cos + _swap_halves(u, half) * sin


def _dsa_pre_kernel(q_ref, c_ref, iq_ref, ikw_ref, ca_ref, sa_ref, cb_ref, sb_ref,
                    qg_ref, kg_ref, cg_ref, wuk_ref, wuv_ref,
                    qo_ref, ko_ref, vo_ref, iqo_ref, iko_ref, iwo_ref):
    ca, sa, cb, sb = ca_ref[...], sa_ref[...], cb_ref[...], sb_ref[...]
    for h in range(DSA_HEADS):
        sl = slice(h * DSA_DIM, (h + 1) * DSA_DIM)
        q = q_ref[:, sl].astype(F32)
        q = q * lax.rsqrt(jnp.mean(q * q, axis=-1, keepdims=True) + NORM_EPS) * qg_ref[...]
        qo_ref[:, sl] = (_rope(q, ca, sa, 64) * DSA_DIM ** -0.5).astype(qo_ref.dtype)
    c = c_ref[...].astype(F32)
    c = (c * lax.rsqrt(jnp.mean(c * c, axis=-1, keepdims=True) + NORM_EPS) * cg_ref[...]).astype(BF16)
    k = _dot(c, wuk_ref[...].astype(BF16))
    k = k * lax.rsqrt(jnp.mean(k * k, axis=-1, keepdims=True) + NORM_EPS) * kg_ref[...]
    ko_ref[...] = _rope(k, ca, sa, 64).astype(ko_ref.dtype)
    vo_ref[...] = _dot(c, wuv_ref[...].astype(BF16)).astype(vo_ref.dtype)
    for s in range(IDX_HEADS * IDX_DIM // LANES):
        sl = slice(s * LANES, (s + 1) * LANES)
        iqo_ref[:, sl] = _rope(iq_ref[:, sl].astype(F32), cb, sb, 32).astype(iqo_ref.dtype)
    ikw = ikw_ref[...].astype(F32)
    lane = _iota(ikw.shape, 1)
    ik = _rope(ikw, cb, sb, 32)
    iko_ref[...] = jnp.where(lane < IDX_DIM, ik, pltpu.roll(ik, IDX_DIM, 1)).astype(iko_ref.dtype)
    iw = pltpu.roll(ikw, IDX_DIM, 1) * (IDX_HEADS ** -0.5 * IDX_DIM ** -0.5)
    iwo_ref[...] = jnp.where(lane < IDX_HEADS, iw, 0.0)


def _dsa_kernel(q_ref, iq_ref, iw_ref, k_ref, v_ref, ik_ref, o_ref, key_ref, bias_ref, vt_ref, m_ref, l_ref, acc_ref,
                *, tq, seq, top_k):
    kb = tq
    qb = pl.program_id(1)
    nblk = qb + 1
    nh = DSA_HEADS
    int_min = jnp.int32(-2 ** 31)
    kidx = _iota((kb, tq), 0)
    ridx = _iota((kb, tq), 1)

    def causal_mask(j):
        return (kidx + j * kb) <= (ridx + qb * tq)

    def block(ref, j):
        return ref[pl.ds(pl.multiple_of(j * kb, kb), kb), :]

    def fold(x, op):
        return op(x.reshape(kb // 8, 8, x.shape[1]), axis=0)

    @pl.when(qb == 0)
    def _():
        for j in range(seq // LANES):
            vt_ref[:, j * LANES:(j + 1) * LANES] = v_ref[j * LANES:(j + 1) * LANES, :].astype(F32).T.astype(BF16)

    first = _iota((tq, LANES), 1) < IDX_DIM
    lhs = []
    for p in range(IDX_HEADS // 2):
        qp = iq_ref[:, p * LANES:(p + 1) * LANES]
        zero = jnp.zeros_like(qp)
        lhs.append(jnp.concatenate([jnp.where(first, qp, zero), jnp.where(first, zero, qp)], axis=0))
    iw_t = iw_ref[...].T
    w_row = [iw_t[h:h + 1] for h in range(IDX_HEADS)]

    def score_block(j, _):
        ik = block(ik_ref, j)
        sc = jnp.zeros((kb, tq), F32)
        for p in range(IDX_HEADS // 2):
            z = jnp.maximum(_dot_nt(ik, lhs[p]), 0.0)
            sc = sc + z[:, :tq] * w_row[2 * p] + z[:, tq:] * w_row[2 * p + 1]
        sc = sc + 0.0
        bits = lax.bitcast_convert_type(sc, I32)
        skey = bits ^ ((bits >> 31) & jnp.int32(0x7FFFFFFF))
        key_ref[pl.ds(pl.multiple_of(j * kb, kb), kb), :] = jnp.where(causal_mask(j), skey, int_min)
        return 0

    lax.fori_loop(0, nblk, score_block, 0)

    def count(pred_fn):
        def body(j, cnt):
            return cnt + fold(jnp.where(pred_fn(block(key_ref, j)), 1.0, 0.0), jnp.sum)
        return jnp.sum(lax.fori_loop(0, nblk, body, jnp.zeros((8, tq), F32)), axis=0, keepdims=True)

    def bit_step(i, thr):
        cand = thr ^ (jnp.int32(1) << (31 - i))
        return jnp.where(count(lambda keys: keys >= cand) >= top_k, cand, thr)

    thr = lax.fori_loop(0, 32, bit_step, jnp.full((1, tq), int_min, I32))
    need = top_k - count(lambda keys: keys > thr)

    q_all = jnp.concatenate([q_ref[:, h * DSA_DIM:(h + 1) * DSA_DIM] for h in range(nh)], axis=0)
    lower = jnp.where(_iota((kb, kb), 0) > _iota((kb, kb), 1), 1.0, 0.0).astype(BF16)
    m_ref[...] = jnp.full_like(m_ref, -1e30)

    def logits(j, bias):
        lt = _dot_nt(block(k_ref, j), q_all)
        return jnp.concatenate([lt[:, h * tq:(h + 1) * tq] + bias for h in range(nh)], axis=1)

    def select(j, ties_seen):
        keys = block(key_ref, j)
        causal = causal_mask(j)
        tie = causal & (keys == thr)
        tie_f = jnp.where(tie, 1.0, 0.0)
        rank = ties_seen + _dot(lower, tie_f.astype(BF16))
        sel = causal & ((keys > thr) | (tie & (rank < need)))
        bias = jnp.where(sel, 0.0, -1e30)
        bias_ref[pl.ds(pl.multiple_of(j * kb, kb), kb), :] = bias
        m_ref[...] = jnp.maximum(m_ref[...], fold(logits(j, bias), jnp.max))
        return ties_seen + jnp.sum(tie_f, axis=0, keepdims=True)

    lax.fori_loop(0, nblk, select, jnp.zeros((1, tq), F32))
    m_all = jnp.broadcast_to(jnp.max(m_ref[...], axis=0, keepdims=True), m_ref.shape)
    l_ref[...] = jnp.zeros_like(l_ref)
    acc_ref[...] = jnp.zeros_like(acc_ref)

    def attend(j, _):
        lt = logits(j, block(bias_ref, j))
        pr = jnp.exp(lt.reshape(kb // 8, 8, nh * tq) - m_all[None]).reshape(kb, nh * tq)
        l_ref[...] += fold(pr, jnp.sum)
        acc_ref[...] += _dot(vt_ref[:, pl.ds(pl.multiple_of(j * kb, kb), kb)], pr.astype(BF16))
        return 0

    lax.fori_loop(0, nblk, attend, 0)
    out_t = acc_ref[...] / jnp.sum(l_ref[...], axis=0, keepdims=True)
    for h in range(nh):
        o_ref[:, h * DSA_DIM:(h + 1) * DSA_DIM] = out_t[:, h * tq:(h + 1) * tq].T.astype(o_ref.dtype)


def dsa_attention(u, tables, q_gain, k_gain, kv_gain, w_uk, w_uv, batch, seq, tm=256, tq=256):
    t = batch * seq
    tm = min(tm, seq)
    ca, sa, cb, sb = tables
    w = BRANCH_W
    row = lambda width, col: pl.BlockSpec((tm, width), lambda i: (i, col // width))
    tab = pl.BlockSpec((tm, LANES), lambda i: (i, 0))
    vec = lambda width: pl.BlockSpec((1, width), lambda i: (0, 0))
    mat = pl.BlockSpec((DSA_KV_RANK, DSA_DIM), lambda i: (0, 0))
    o_w = pl.BlockSpec((tm, w), lambda i: (i, 0))
    o_n = pl.BlockSpec((tm, LANES), lambda i: (i, 0))
    q, k, v, iq, ik, iw = pl.pallas_call(
        _dsa_pre_kernel,
        grid=(t // tm,),
        in_specs=[row(w, COL_DSAQ), row(DSA_KV_RANK, COL_CKV), row(w, COL_IDXQ), row(LANES, COL_IDXKW),
                  tab, tab, tab, tab, vec(DSA_DIM), vec(DSA_DIM), vec(DSA_KV_RANK), mat, mat],
        out_specs=[o_w, o_n, o_n, o_w, o_n, o_n],
        out_shape=[jax.ShapeDtypeStruct((t, w), BF16), jax.ShapeDtypeStruct((t, LANES), BF16),
                   jax.ShapeDtypeStruct((t, LANES), BF16), jax.ShapeDtypeStruct((t, w), BF16),
                   jax.ShapeDtypeStruct((t, LANES), BF16), jax.ShapeDtypeStruct((t, LANES), F32)],
        compiler_params=_cparams(("parallel",)),
        name="dsa_prepare",
    )(u, u, u, u, ca, sa, cb, sb, q_gain.reshape(1, -1), k_gain.reshape(1, -1), kv_gain.reshape(1, -1), w_uk, w_uv)

    nq = seq // tq
    top_k = min(IDX_TOPK_MAX, seq // 4)
    qrow = lambda width: pl.BlockSpec((tq, width), lambda b, i: (b * nq + i, 0))
    full = pl.BlockSpec((seq, LANES), lambda b, i: (b, 0))
    return pl.pallas_call(
        functools.partial(_dsa_kernel, tq=tq, seq=seq, top_k=top_k),
        grid=(batch, nq),
        in_specs=[qrow(w), qrow(w), qrow(LANES), full, full, full],
        out_specs=qrow(w),
        out_shape=jax.ShapeDtypeStruct((t, w), BF16),
        scratch_shapes=[pltpu.VMEM((seq, tq), I32), pltpu.VMEM((seq, tq), F32), pltpu.VMEM((DSA_DIM, seq), BF16),
                        pltpu.VMEM((8, DSA_HEADS * tq), F32), pltpu.VMEM((8, DSA_HEADS * tq), F32),
                        pltpu.VMEM((DSA_DIM, DSA_HEADS * tq), F32)],
        compiler_params=_cparams(("parallel", "arbitrary")),
        name="dsa_attention",
    )(q, iq, iw, k, v, ik)


def _sw_kernel(sink_ref, q_ref, kp_ref, kc_ref, vp_ref, vc_ref, cbp_ref, sbp_ref, cbc_ref, sbc_ref,
               qg_ref, kg_ref, o_ref, *, blk):
    n = pl.program_id(1)
    avg = _group_matrix(LANES, SW_DIM, 1.0 / SW_DIM)
    lane2 = _iota((2 * blk, LANES), 1)
    lane1 = _iota((blk, LANES), 1)

    def norm(x, gain):
        return x * lax.rsqrt(_dot_exact_rhs(x * x, avg) + NORM_EPS) * gain

    cb = jnp.concatenate([cbp_ref[...], cbc_ref[...]], axis=0)
    sb = jnp.concatenate([sbp_ref[...], sbc_ref[...]], axis=0)
    k = jnp.concatenate([kp_ref[...], kc_ref[...]], axis=0).astype(F32)
    k = _rope(norm(k, kg_ref[...]), cb, sb, 32)
    v = jnp.concatenate([vp_ref[...], vc_ref[...]], axis=0).astype(F32)
    k_sw, v_sw = pltpu.roll(k, SW_DIM, 1), pltpu.roll(v, SW_DIM, 1)
    k2 = [jnp.where(lane2 < SW_DIM, k, k_sw).astype(BF16), jnp.where(lane2 < SW_DIM, k_sw, k).astype(BF16)]
    v2 = [jnp.where(lane2 < SW_DIM, v, v_sw).astype(BF16), jnp.where(lane2 < SW_DIM, v_sw, v).astype(BF16)]

    r = _iota((2 * blk, 2 * blk), 0) % blk
    c = _iota((2 * blk, 2 * blk), 1)
    dist = r - (c - blk)
    mask = (dist >= 0) & (dist < WINDOW) & ((c >= blk) | (n > 0))
    top = _iota((2 * blk, 1), 0) < blk
    cbc, sbc = cbc_ref[...], sbc_ref[...]
    for p in range(SW_HEADS // 2):
        g = (2 * p) // (SW_HEADS // SW_KV_HEADS)
        q = q_ref[:, p * LANES:(p + 1) * LANES].astype(F32)
        q = _rope(norm(q, qg_ref[...]), cbc, sbc, 32) * SW_DIM ** -0.5
        zero = jnp.zeros_like(q)
        lhs = jnp.concatenate([jnp.where(lane1 < SW_DIM, q, zero), jnp.where(lane1 < SW_DIM, zero, q)],
                              axis=0).astype(BF16)
        logit = jnp.where(mask, _dot_nt(lhs, k2[g]), -1e30)
        sink = jnp.where(top, sink_ref[2 * p], sink_ref[2 * p + 1])
        m = jnp.maximum(jnp.max(logit, axis=1, keepdims=True), sink)
        pr = jnp.where(mask, jnp.exp(logit - m), 0.0)
        den = jnp.sum(pr, axis=1, keepdims=True) + jnp.exp(sink - m)
        o2 = _dot((pr / den).astype(BF16), v2[g])
        o_ref[:, p * LANES:(p + 1) * LANES] = jnp.where(lane1 < SW_DIM, o2[:blk], o2[blk:]).astype(o_ref.dtype)


def sw_attention(u, tables, q_gain, k_gain, sinks, batch, seq, blk=128):
    t = batch * seq
    nb = seq // blk
    _, _, cb, sb = tables
    w = BRANCH_W
    cur = lambda width, col: pl.BlockSpec((blk, width), lambda b, i: (b * nb + i, col // width))
    prev = lambda width, col: pl.BlockSpec((blk, width), lambda b, i: (b * nb + jnp.maximum(i - 1, 0), col // width))
    vec = pl.BlockSpec((1, LANES), lambda b, i: (0, 0))
    tile2 = lambda g: jnp.tile(g.reshape(1, SW_DIM), (1, 2))
    return pl.pallas_call(
        functools.partial(_sw_kernel, blk=blk),
        grid=(batch, nb),
        in_specs=[pl.BlockSpec(memory_space=pltpu.SMEM),
                  cur(w, COL_SWQ), prev(LANES, COL_SWK), cur(LANES, COL_SWK), prev(LANES, COL_SWV), cur(LANES, COL_SWV),
                  prev(LANES, 0), prev(LANES, 0), cur(LANES, 0), cur(LANES, 0), vec, vec],
        out_specs=pl.BlockSpec((blk, w), lambda b, i: (b * nb + i, 0)),
        out_shape=jax.ShapeDtypeStruct((t, w), BF16),
        compiler_params=_cparams(("parallel", "parallel")),
        name="sw_attention",
    )(sinks, u, u, u, u, u, cb, sb, cb, sb, tile2(q_gain), tile2(k_gain))


def _merge_kernel(y_ref, g_ref, w_ref, o_ref, acc_ref):
    n = pl.program_id(1)
    up = _dot(y_ref[0], w_ref[0].astype(BF16))
    contrib = _sigmoid(g_ref[...].astype(F32)) * up

    @pl.when(n == 0)
    def _():
        acc_ref[...] = contrib

    @pl.when(n > 0)
    def _():
        acc_ref[...] += contrib

    @pl.when(n == N_BRANCH - 1)
    def _():
        o_ref[...] = acc_ref[...].astype(o_ref.dtype)


def merge_branches(ys, u, w_branch, tm=1024):
    _, t, w = ys.shape
    d = w_branch.shape[-1]
    tm = min(tm, t)
    return pl.pallas_call(
        _merge_kernel,
        grid=(t // tm, N_BRANCH),
        in_specs=[pl.BlockSpec((1, tm, w), lambda i, n: (n, i, 0)),
                  pl.BlockSpec((tm, d), lambda i, n: (i, COL_GATE // d + n)),
                  pl.BlockSpec((1, w, d), lambda i, n: (n, 0, 0))],
        out_specs=pl.BlockSpec((tm, d), lambda i, n: (i, 0)),
        out_shape=jax.ShapeDtypeStruct((t, d), BF16),
        scratch_shapes=[pltpu.VMEM((tm, d), F32)],
        compiler_params=_cparams(("parallel", "arbitrary")),
        name="merge_branches",
    )(ys, u, w_branch)


FFN_TM = 1024
FFN_SUB = 512
FFN_TF = 256


def _ffn_kernel(e_ref, rows_ref, x_ref, wg_ref, wu_ref, wd_ref, rw_ref, o_ref, *, nf):
    s = pl.program_id(0)
    f = pl.program_id(1)
    rows = rows_ref[s]

    @pl.when(f == 0)
    def _():
        o_ref[...] = jnp.zeros_like(o_ref)

    wg = wg_ref[0].astype(BF16)
    wu = wu_ref[0].astype(BF16)
    wd = wd_ref[0].astype(BF16)
    for j in range(FFN_TM // FFN_SUB):
        sl = slice(j * FFN_SUB, (j + 1) * FFN_SUB)

        @pl.when(j * FFN_SUB < rows)
        def _():
            xj = x_ref[sl, :]
            g = _dot(xj, wg)
            a = (g * _sigmoid(g) * _dot(xj, wu)).astype(BF16)
            o_ref[sl, :] += _dot(a, wd)

    @pl.when(f == nf - 1)
    def _():
        o_ref[...] = o_ref[...] * rw_ref[...]


def ffn_tiles(xs, tile_expert, tile_rows, w_gate, w_up, w_down, row_w):
    r, d = xs.shape
    _, _, ff = w_gate.shape
    ns = r // FFN_TM
    nf = ff // FFN_TF

    def f_eff(s, f, rows):
        return jnp.where(rows[s] > 0, f, nf - 1)

    grid_spec = pltpu.PrefetchScalarGridSpec(
        num_scalar_prefetch=2,
        grid=(ns, nf),
        in_specs=[pl.BlockSpec((FFN_TM, d), lambda s, f, e, rows: (s, 0)),
                  pl.BlockSpec((1, d, FFN_TF), lambda s, f, e, rows: (e[s], 0, f_eff(s, f, rows))),
                  pl.BlockSpec((1, d, FFN_TF), lambda s, f, e, rows: (e[s], 0, f_eff(s, f, rows))),
                  pl.BlockSpec((1, FFN_TF, d), lambda s, f, e, rows: (e[s], f_eff(s, f, rows), 0)),
                  pl.BlockSpec((FFN_TM, 1), lambda s, f, e, rows: (s, 0))],
        out_specs=pl.BlockSpec((FFN_TM, d), lambda s, f, e, rows: (s, 0)),
    )
    return pl.pallas_call(
        functools.partial(_ffn_kernel, nf=nf),
        grid_spec=grid_spec,
        out_shape=jax.ShapeDtypeStruct((r, d), F32),
        compiler_params=_cparams(("parallel", "arbitrary")),
        name="ffn_tiles",
    )(tile_expert, tile_rows, xs, w_gate, w_up, w_down, row_w)


def _add_kernel(a_ref, b_ref, o_ref):
    o_ref[...] = a_ref[...] + b_ref[...]


def add(a, b, tm=512):
    t, d = a.shape
    spec = pl.BlockSpec((tm, d), lambda i: (i, 0))
    return pl.pallas_call(_add_kernel, grid=(t // tm,), in_specs=[spec, spec], out_specs=spec,
                          out_shape=jax.ShapeDtypeStruct((t, d), a.dtype),
                          compiler_params=_cparams(("parallel",)), name="residual_add")(a, b)


def dense_ffn(x, h, w_gate, w_up, w_down):
    t = h.shape[0]
    ns = t // FFN_TM
    ys = ffn_tiles(h, jnp.zeros((ns,), I32), jnp.full((ns,), FFN_TM, I32),
                   w_gate[None], w_up[None], w_down[None], jnp.ones((t, 1), F32))
    return add(x, ys)


def _router_kernel(x_ref, g_ref, w_ref, h_ref, r_ref):
    x = x_ref[...]
    h = x * lax.rsqrt(jnp.mean(x * x, axis=-1, keepdims=True) + NORM_EPS) * g_ref[...]
    h_ref[...] = h
    logit = _dot_f32(h, w_ref[...])
    lane = _iota(logit.shape, 1).astype(F32)
    neg = -jnp.inf
    l1 = jnp.where(lane < N_EXPERTS, logit, neg)
    m1 = jnp.max(l1, axis=1, keepdims=True)
    i1 = jnp.min(jnp.where(l1 == m1, lane, float(LANES)), axis=1, keepdims=True)
    l2 = jnp.where(lane == i1, neg, l1)
    m2 = jnp.max(l2, axis=1, keepdims=True)
    i2 = jnp.min(jnp.where(l2 == m2, lane, float(LANES)), axis=1, keepdims=True)
    e = jnp.exp(m2 - m1)
    w1 = 1.0 / (1.0 + e)
    w2 = e / (1.0 + e)
    r_ref[...] = jnp.where(lane == 0, i1, jnp.where(lane == 1, i2, jnp.where(lane == 2, w1,
                           jnp.where(lane == 3, w2, 0.0))))


def route(x, gain, router, tm=256):
    t, d = x.shape
    wp = jnp.zeros((d, LANES), F32).at[:, :N_EXPERTS].set(router)
    return pl.pallas_call(
        _router_kernel,
        grid=(t // tm,),
        in_specs=[pl.BlockSpec((tm, d), lambda i: (i, 0)), pl.BlockSpec((1, d), lambda i: (0, 0)),
                  pl.BlockSpec((d, LANES), lambda i: (0, 0))],
        out_specs=[pl.BlockSpec((tm, d), lambda i: (i, 0)), pl.BlockSpec((tm, LANES), lambda i: (i, 0))],
        out_shape=[jax.ShapeDtypeStruct((t, d), F32), jax.ShapeDtypeStruct((t, LANES), F32)],
        compiler_params=_cparams(("parallel",)),
        name="router",
    )(x, gain.reshape(1, d), wp)


def _row_copy(src_ref, row, buf, i, sem):
    return pltpu.make_async_copy(src_ref.at[pl.ds(row, 1)], buf.at[pl.ds(i, 1)], sem)


def _gather_kernel(idx_ref, src_ref, o_ref, buf, sem, *, gb):
    base = pl.program_id(0) * gb

    def issue(i, _):
        _row_copy(src_ref, idx_ref[base + i], buf, i, sem).start()
        return 0

    def wait(i, _):
        _row_copy(src_ref, 0, buf, i, sem).wait()
        return 0

    lax.fori_loop(0, gb, issue, 0)
    lax.fori_loop(0, gb, wait, 0)
    o_ref[...] = buf[...].astype(o_ref.dtype)


def gather_rows(src, idx, out_dtype, gb=256):
    r = idx.shape[0]
    d = src.shape[1]
    grid_spec = pltpu.PrefetchScalarGridSpec(
        num_scalar_prefetch=1,
        grid=(r // gb,),
        in_specs=[pl.BlockSpec(memory_space=pl.ANY)],
        out_specs=pl.BlockSpec((gb, d), lambda i, idx: (i, 0)),
        scratch_shapes=[pltpu.VMEM((gb, d), src.dtype), pltpu.SemaphoreType.DMA(())],
    )
    return pl.pallas_call(
        functools.partial(_gather_kernel, gb=gb),
        grid_spec=grid_spec,
        out_shape=jax.ShapeDtypeStruct((r, d), out_dtype),
        compiler_params=_cparams(("arbitrary",)),
        name="gather_rows",
    )(idx, src)


def _combine_kernel(d0_ref, d1_ref, x_ref, ys_ref, o_ref, b0, b1, sem0, sem1, *, tm):
    base = pl.program_id(0) * tm

    def issue(i, _):
        _row_copy(ys_ref, d0_ref[base + i], b0, i, sem0).start()
        _row_copy(ys_ref, d1_ref[base + i], b1, i, sem1).start()
        return 0

    def wait(i, _):
        _row_copy(ys_ref, 0, b0, i, sem0).wait()
        _row_copy(ys_ref, 0, b1, i, sem1).wait()
        return 0

    lax.fori_loop(0, tm, issue, 0)
    lax.fori_loop(0, tm, wait, 0)
    o_ref[...] = x_ref[...] + b0[...] + b1[...]


def combine_rows(x, ys, d0, d1, tm=256):
    t, d = x.shape
    grid_spec = pltpu.PrefetchScalarGridSpec(
        num_scalar_prefetch=2,
        grid=(t // tm,),
        in_specs=[pl.BlockSpec((tm, d), lambda i, a, b: (i, 0)), pl.BlockSpec(memory_space=pl.ANY)],
        out_specs=pl.BlockSpec((tm, d), lambda i, a, b: (i, 0)),
        scratch_shapes=[pltpu.VMEM((tm, d), F32), pltpu.VMEM((tm, d), F32),
                        pltpu.SemaphoreType.DMA(()), pltpu.SemaphoreType.DMA(())],
    )
    return pl.pallas_call(
        functools.partial(_combine_kernel, tm=tm),
        grid_spec=grid_spec,
        out_shape=jax.ShapeDtypeStruct((t, d), F32),
        compiler_params=_cparams(("arbitrary",)),
        name="combine_rows",
    )(d0, d1, x, ys)


def moe_ffn(x, gain, router, w_gate, w_up, w_down):
    t, d = x.shape
    h, rt = route(x, gain, router)
    e_flat = jnp.concatenate([rt[:, 0], rt[:, 1]]).astype(I32)
    w_flat = jnp.concatenate([rt[:, 2], rt[:, 3]])
    tok = jnp.concatenate([jnp.arange(t, dtype=I32)] * 2)
    onehot = (e_flat[:, None] == jnp.arange(N_EXPERTS, dtype=I32)[None, :]).astype(I32)
    csum = jnp.cumsum(onehot, axis=0)
    rank = jnp.take_along_axis(csum - onehot, e_flat[:, None], axis=1)[:, 0]
    counts = csum[-1]
    n_tiles = (counts + FFN_TM - 1) // FFN_TM
    tile_end = jnp.cumsum(n_tiles)
    tile_start = tile_end - n_tiles
    dest = tile_start[e_flat] * FFN_TM + rank
    ns = TOP_K * t // FFN_TM + N_EXPERTS
    s_idx = jnp.arange(ns, dtype=I32)
    used = s_idx < tile_end[-1]
    s_clip = jnp.minimum(s_idx, tile_end[-1] - 1)
    tile_expert = jnp.minimum(jnp.sum((s_clip[:, None] >= tile_end[None, :]).astype(I32), axis=1), N_EXPERTS - 1)
    tile_rows = jnp.clip(counts[tile_expert] - (s_clip - tile_start[tile_expert]) * FFN_TM, 0, FFN_TM)
    tile_rows = jnp.where(used, tile_rows, 0).astype(I32)
    src_row = jnp.zeros((ns * FFN_TM,), I32).at[dest].set(tok)
    row_w = jnp.zeros((ns * FFN_TM,), F32).at[dest].set(w_flat)

    xs = gather_rows(h, src_row, BF16)
    ys = ffn_tiles(xs, tile_expert, tile_rows, w_gate, w_up, w_down, row_w.reshape(-1, 1))
    return combine_rows(x, ys, dest[:t], dest[t:])


def _ple_kernel(h_ref, wg_ref, p_ref, wp_ref, x_ref, o_ref):
    gate = _sigmoid(_dot(h_ref[...], wg_ref[...].astype(BF16)))
    proj = _dot(p_ref[...].astype(BF16), wp_ref[...].astype(BF16))
    o_ref[...] = x_ref[...] + gate * proj


def ple(h, w_gate, p, w_proj, x, tm=1024, tn=512):
    t, d = x.shape
    pd = p.shape[1]
    tm = min(tm, t)
    return pl.pallas_call(
        _ple_kernel,
        grid=(t // tm, d // tn),
        in_specs=[pl.BlockSpec((tm, d), lambda i, j: (i, 0)), pl.BlockSpec((d, tn), lambda i, j: (0, j)),
                  pl.BlockSpec((tm, pd), lambda i, j: (i, 0)), pl.BlockSpec((pd, tn), lambda i, j: (0, j)),
                  pl.BlockSpec((tm, tn), lambda i, j: (i, j))],
        out_specs=pl.BlockSpec((tm, tn), lambda i, j: (i, j)),
        out_shape=jax.ShapeDtypeStruct((t, d), F32),
        compiler_params=_cparams(("parallel", "parallel")),
        name="ple",
    )(h, w_gate, p, w_proj, x)


def _pad_rows(w, start, total):
    return jnp.zeros((total, w.shape[1]), F32).at[start:start + w.shape[0]].set(w).astype(BF16)


def kernel(x, p, positions, w_in, mix_norm, ffn_norm, ple_norm, rw_mu, rw_w0, rw_w2, rw_a0, rw_a2, rw_g2, rw_kk,
           rw_ka, rw_rk, rw_ln_g, rw_ln_b, dsa_q_norm, dsa_k_norm, dsa_kv_norm, dsa_w_uk, dsa_w_uv, sw_q_norm,
           sw_k_norm, sw_sinks, w_branch, w_out, ffn_w_gate, ffn_w_up, ffn_w_down, moe_router, moe_w_gate, moe_w_up,
           moe_w_down, ple_w_gate, ple_w_proj):
    b, s, d = x.shape
    t = b * s
    depth = w_in.shape[0]
    xf = x.reshape(t, d)
    tables = rope_tables(positions)
    row = lambda a: a.reshape(1, -1)
    for i in range(depth):
        h = rmsnorm(xf, mix_norm[i])
        u = matmul(h, pack_in_columns(w_in[i].astype(BF16)), out_dtype=BF16, name="in_proj")
        y_a = sb_attention(u, b, s)
        rw = rw_prepare(u, row(rw_mu[i]), row(rw_w0[i]), row(rw_a0[i]), row(rw_kk[i]), row(rw_ka[i]),
                        _pad_rows(rw_w2[i], 0, RW_LORA), _pad_rows(rw_a2[i], RW_LORA_W, RW_LORA),
                        _pad_rows(rw_g2[i], RW_LORA_W + RW_LORA_A, RW_LORA), b, s)
        y_b = rw_scan(*rw, row(rw_rk[i]), row(rw_ln_g[i]), row(rw_ln_b[i]), b, s)
        y_c = dsa_attention(u, tables, dsa_q_norm[i], dsa_k_norm[i], dsa_kv_norm[i], dsa_w_uk[i], dsa_w_uv[i], b, s)
        y_d = sw_attention(u, tables, sw_q_norm[i], sw_k_norm[i], sw_sinks[i], b, s)
        merged = merge_branches(jnp.stack([y_a, y_b, y_c, y_d]), u, w_branch[i].astype(BF16))
        xf = matmul(merged, w_out[i], out_dtype=F32, residual=xf, name="out_proj")
        if i % 2 == 0:
            h = rmsnorm(xf, ffn_norm[i])
            xf = dense_ffn(xf, h, ffn_w_gate[i // 2], ffn_w_up[i // 2], ffn_w_down[i // 2])
        else:
            xf = moe_ffn(xf, ffn_norm[i], moe_router[i // 2], moe_w_gate[i // 2], moe_w_up[i // 2],
                         moe_w_down[i // 2])
        h = rmsnorm(xf, ple_norm[i])
        xf = ple(h, ple_w_gate[i], p[i].reshape(t, -1), ple_w_proj[i], xf)
    return xf.reshape(b, s, d)
```

```python
import functools
import math

import jax
import jax.numpy as jnp
from jax import lax
from jax.experimental import pallas as pl
from jax.experimental.pallas import tpu as pltpu

F32 = jnp.float32
BF16 = jnp.bfloat16
I32 = jnp.int32

D_MODEL = 2048
ROPE_THETA = 10000.0
NORM_EPS = 1e-6
N_BRANCH = 4
BRANCH_W = 1024
SB_HEADS, SB_DIM = 8, 128
RW_HEADS, RW_DIM = 16, 64
RW_LORA_W, RW_LORA_A, RW_LORA_G = 96, 96, 64
RW_LORA = RW_LORA_W + RW_LORA_A + RW_LORA_G
RW_GN_EPS = 64e-5
DSA_HEADS, DSA_DIM, DSA_KV_RANK = 8, 128, 256
IDX_HEADS, IDX_DIM, IDX_TOPK_MAX = 16, 64, 256
SW_HEADS, SW_KV_HEADS, SW_DIM, WINDOW = 16, 2, 64, 128
N_EXPERTS, TOP_K = 8, 2

LANES = 128
VMEM_LIMIT_BYTES = 56 * 1024 * 1024

HEAD_W = 9216
COL_SBQ = 0
COL_SBK = 1024
COL_SBV = 2048
COL_RWR = 3072
COL_RWK = 4096
COL_RWV = 5120
COL_RWL = 6144
COL_DSAQ = 6400
COL_CKV = 7424
COL_IDXQ = 7680
COL_IDXKW = 8704
SRC_SWQ = 8784
SRC_GATE = 10064
TAIL_W = 9728
COL_GATE = 0
COL_SWQ = 8192
COL_SWK = 9216
COL_SWV = 9344

RW_CHUNK = 64


def _cparams(sem, vmem=VMEM_LIMIT_BYTES):
    return pltpu.CompilerParams(dimension_semantics=sem, vmem_limit_bytes=vmem)


def _dot(a, b):
    return jnp.dot(a, b, preferred_element_type=F32)


def _dot_nt(a, b):
    return lax.dot_general(a, b, (((1,), (1,)), ((), ())), preferred_element_type=F32)


def _split3(a):
    a1 = a.astype(BF16)
    r1 = a - a1.astype(F32)
    a2 = r1.astype(BF16)
    a3 = (r1 - a2.astype(F32)).astype(BF16)
    return a1, a2, a3


def _dot_exact_rhs(a, m_bf16):
    a1, a2, a3 = _split3(a)
    return _dot(a1, m_bf16) + _dot(a2, m_bf16) + _dot(a3, m_bf16)


def _dot_exact_lhs(m_bf16, a):
    a1, a2, a3 = _split3(a)
    return _dot(m_bf16, a1) + _dot(m_bf16, a2) + _dot(m_bf16, a3)


def _dot_f32(a, b):
    a1, a2, a3 = _split3(a)
    b1, b2, b3 = _split3(b)
    return (_dot(a1, b1) + _dot(a1, b2) + _dot(a2, b1)) + (_dot(a1, b3) + _dot(a2, b2) + _dot(a3, b1))


def _softplus(z):
    return jnp.maximum(z, 0.0) + jnp.log(1.0 + jnp.exp(-jnp.abs(z)))


def _sigmoid(z):
    return 1.0 / (1.0 + jnp.exp(-z))


def _iota(shape, dim):
    return lax.broadcasted_iota(I32, shape, dim)


def _group_matrix(n, group, value):
    r = _iota((n, n), 0) // group
    c = _iota((n, n), 1) // group
    return jnp.where(r == c, value, 0.0).astype(BF16)


def _swap_halves(u, half):
    if 2 * half == LANES:
        return pltpu.roll(u, half, 1)
    lane = _iota(u.shape, 1)
    return jnp.where(lane % (2 * half) < half, pltpu.roll(u, LANES - half, 1), pltpu.roll(u, half, 1))


def _rmsnorm_kernel(x_ref, g_ref, o_ref):
    x = x_ref[...]
    ms = jnp.mean(x * x, axis=-1, keepdims=True)
    o_ref[...] = (x * lax.rsqrt(ms + NORM_EPS) * g_ref[...]).astype(o_ref.dtype)


def rmsnorm(x, gain, out_dtype=BF16, tm=512):
    t, d = x.shape
    return pl.pallas_call(
        _rmsnorm_kernel,
        grid=(t // tm,),
        in_specs=[pl.BlockSpec((tm, d), lambda i: (i, 0)), pl.BlockSpec((1, d), lambda i: (0, 0))],
        out_specs=pl.BlockSpec((tm, d), lambda i: (i, 0)),
        out_shape=jax.ShapeDtypeStruct((t, d), out_dtype),
        compiler_params=_cparams(("parallel",)),
        name="rmsnorm",
    )(x, gain.reshape(1, d))


def _mm_kernel(*refs, has_res):
    if has_res:
        a_ref, w_ref, r_ref, o_ref = refs
    else:
        a_ref, w_ref, o_ref = refs
    acc = _dot(a_ref[...].astype(BF16), w_ref[...].astype(BF16))
    if has_res:
        acc = acc + r_ref[...]
    o_ref[...] = acc.astype(o_ref.dtype)


def matmul(a, w, *, out_dtype, residual=None, n_cols=None, tm=1024, tn=512, name="matmul"):
    m, k = a.shape
    n = w.shape[1] if n_cols is None else n_cols
    tm, tn = min(tm, m), min(tn, n)
    in_specs = [pl.BlockSpec((tm, k), lambda i, j: (i, 0)), pl.BlockSpec((k, tn), lambda i, j: (0, j))]
    args = [a, w]
    if residual is not None:
        in_specs.append(pl.BlockSpec((tm, tn), lambda i, j: (i, j)))
        args.append(residual)
    return pl.pallas_call(
        functools.partial(_mm_kernel, has_res=residual is not None),
        grid=(m // tm, n // tn),
        in_specs=in_specs,
        out_specs=pl.BlockSpec((tm, tn), lambda i, j: (i, j)),
        out_shape=jax.ShapeDtypeStruct((m, n), out_dtype),
        compiler_params=_cparams(("parallel", "parallel")),
        name=name,
    )(*args)


def _rope_table_kernel(pos_ref, f64_ref, f32_ref, ca_ref, sa_ref, cb_ref, sb_ref):
    pos = pos_ref[...]
    lane = _iota((1, LANES), 1)
    ang_a = pos * f64_ref[...]
    ang_b = pos * f32_ref[...]
    ca_ref[...] = jnp.cos(ang_a)
    sa_ref[...] = jnp.where(lane < 64, -1.0, 1.0) * jnp.sin(ang_a)
    cb_ref[...] = jnp.cos(ang_b)
    sb_ref[...] = jnp.where(lane % 64 < 32, -1.0, 1.0) * jnp.sin(ang_b)


def rope_tables(positions, tm=512):
    t = positions.size
    pos = positions.reshape(t, 1).astype(F32)
    inv64 = ROPE_THETA ** (-jnp.arange(64, dtype=F32) / 64)
    inv32 = ROPE_THETA ** (-jnp.arange(32, dtype=F32) / 32)
    f64 = jnp.tile(inv64, 2).reshape(1, LANES)
    f32 = jnp.tile(inv32, 4).reshape(1, LANES)
    row = pl.BlockSpec((tm, LANES), lambda i: (i, 0))
    vec = pl.BlockSpec((1, LANES), lambda i: (0, 0))
    return pl.pallas_call(
        _rope_table_kernel,
        grid=(t // tm,),
        in_specs=[pl.BlockSpec((tm, 1), lambda i: (i, 0)), vec, vec],
        out_specs=[row] * 4,
        out_shape=[jax.ShapeDtypeStruct((t, LANES), F32)] * 4,
        compiler_params=_cparams(("parallel",)),
        name="rope_tables",
    )(pos, f64, f32)


SB_TQ = 256
SB_G = 4


def _sb_kernel(q_ref, k_ref, v_ref, o_ref, *, tq, scale):
    qi = pl.program_id(2)
    r = _iota((tq, tq), 0)
    c = _iota((tq, tq), 1)
    later = jnp.where(r > c, 1.0, 0.0).astype(BF16)
    qs = [(q_ref[:, g * SB_DIM:(g + 1) * SB_DIM].astype(F32) * scale).astype(BF16) for g in range(SB_G)]

    heads = range(SB_G)
    cols = [slice(g * SB_DIM, (g + 1) * SB_DIM) for g in heads]

    def span(j, state, diagonal):
        off = pl.multiple_of(j * tq, tq)
        carry, acc = state
        ks = [k_ref[pl.ds(off, tq), cols[g]].astype(BF16) for g in heads]
        vs = [v_ref[pl.ds(off, tq), cols[g]].astype(BF16) for g in heads]
        zs = [_dot_nt(qs[g], ks[g]) for g in heads]
        sps = [_softplus(z) for z in zs]
        lks = [jnp.where(r > c, -sp, 0.0) if diagonal else -sp for sp in sps]
        his = [lk.astype(BF16) for lk in lks]
        los = [(lks[g] - his[g].astype(F32)).astype(BF16) for g in heads]
        css = [_dot(his[g], later) + _dot(los[g], later) for g in heads]
        ws = [jnp.exp(zs[g] - sps[g] + css[g] + carry[g]) for g in heads]
        if diagonal:
            ws = [jnp.where(r > c, w, 0.0) for w in ws]
        acc = tuple(acc[g] + _dot(ws[g].astype(BF16), vs[g]) for g in heads)
        carry = tuple(carry[g] + jnp.sum(lks[g], axis=1, keepdims=True) for g in heads)
        return carry, acc

    init = (tuple(jnp.zeros((tq, 1), F32) for _ in heads), tuple(jnp.zeros((tq, SB_DIM), F32) for _ in heads))
    state = span(qi, init, True)
    _, acc = lax.fori_loop(0, qi, lambda i, s: span(qi - 1 - i, s, False), state)
    for g in heads:
        o_ref[:, cols[g]] = acc[g].astype(o_ref.dtype)


def sb_attention(u, batch, seq):
    t = batch * seq
    tq = min(SB_TQ, seq)
    nq = seq // tq
    gw = SB_G * SB_DIM
    qc, kc, vc = COL_SBQ // gw, COL_SBK // gw, COL_SBV // gw
    return pl.pallas_call(
        functools.partial(_sb_kernel, tq=tq, scale=SB_DIM ** -0.5),
        grid=(batch, SB_HEADS // SB_G, nq),
        in_specs=[
            pl.BlockSpec((tq, gw), lambda b, h, i: (b * nq + i, qc + h)),
            pl.BlockSpec((seq, gw), lambda b, h, i: (b, kc + h)),
            pl.BlockSpec((seq, gw), lambda b, h, i: (b, vc + h)),
        ],
        out_specs=pl.BlockSpec((tq, gw), lambda b, h, i: (b * nq + i, h)),
        out_shape=jax.ShapeDtypeStruct((t, BRANCH_W), BF16),
        compiler_params=_cparams(("parallel", "parallel", "arbitrary")),
        name="sb_attention",
    )(u, u, u)


def _rw_pre_kernel(r_ref, k_ref, v_ref, l_ref, pr_ref, pk_ref, pv_ref, plr_ref,
                   mu_ref, w0_ref, a0_ref, kkg_ref, ka_ref, w2_ref, a2_ref, g2_ref,
                   ro_ref, lw_ref, ko_ref, vo_ref, kk_ref, b_ref, g_ref, *, tm, seq):
    i = pl.program_id(0)
    first = (i * tm) % seq == 0
    row0 = _iota((tm, 1), 0) == 0

    def shifted(cur_ref, prev_ref, lo, hi):
        cur = cur_ref[...].astype(F32)
        last = prev_ref[...].astype(F32)[-1:, :]
        last = jnp.where(first, 0.0, last)
        prev = jnp.where(row0, last, pltpu.roll(cur, 1, 0))
        return cur + mu_ref[:, lo:hi] * (prev - cur)

    w = BRANCH_W
    r = shifted(r_ref, pr_ref, 0, w)
    k = shifted(k_ref, pk_ref, w, 2 * w)
    v = shifted(v_ref, pv_ref, 2 * w, 3 * w)
    z = shifted(l_ref, plr_ref, 3 * w, 3 * w + RW_LORA)

    w_pre = w0_ref[...] + _dot(jnp.tanh(z).astype(BF16), w2_ref[...])
    w_log = -_softplus(-w_pre) - 0.5
    lw_ref[...] = -jnp.exp(w_log)
    a = _sigmoid(a0_ref[...] + _dot(z.astype(BF16), a2_ref[...]))
    g_ref[...] = _dot(_sigmoid(z).astype(BF16), g2_ref[...]).astype(g_ref.dtype)
    ones = _group_matrix(LANES, RW_DIM, 1.0)
    kk = k * kkg_ref[...]
    for s in range(w // LANES):
        sl = slice(s * LANES, (s + 1) * LANES)
        kks = kk[:, sl]
        ss = _dot_exact_rhs(kks * kks, ones)
        kkn = kks / jnp.maximum(jnp.sqrt(ss), 1e-12)
        kk_ref[:, sl] = kkn
        b_ref[:, sl] = kkn * a[:, sl]
    ro_ref[...] = r
    ko_ref[...] = k * (1.0 + (a - 1.0) * ka_ref[...])
    vo_ref[...] = v


def rw_prepare(u, mu, w0, a0, k_k, k_a, w2p, a2p, g2p, batch, seq, tm=256):
    t = batch * seq
    tm = min(tm, seq)
    w = BRANCH_W
    sub = 16

    def cur(width, col):
        return pl.BlockSpec((tm, width), lambda i: (i, col // width))

    def prev(width, col):
        return pl.BlockSpec((sub, width), lambda i: (jnp.maximum(i * (tm // sub) - 1, 0), col // width))

    def vec(width):
        return pl.BlockSpec((1, width), lambda i: (0, 0))

    def mat():
        return pl.BlockSpec((RW_LORA, w), lambda i: (0, 0))

    out = pl.BlockSpec((tm, w), lambda i: (i, 0))
    f = jax.ShapeDtypeStruct((t, w), F32)
    return pl.pallas_call(
        functools.partial(_rw_pre_kernel, tm=tm, seq=seq),
        grid=(t // tm,),
        in_specs=[cur(w, COL_RWR), cur(w, COL_RWK), cur(w, COL_RWV), cur(RW_LORA, COL_RWL),
                  prev(w, COL_RWR), prev(w, COL_RWK), prev(w, COL_RWV), prev(RW_LORA, COL_RWL),
                  vec(3 * w + RW_LORA), vec(w), vec(w), vec(w), vec(w), mat(), mat(), mat()],
        out_specs=[out] * 7,
        out_shape=[f, f, f, f, f, f, jax.ShapeDtypeStruct((t, w), BF16)],
        compiler_params=_cparams(("parallel",)),
        name="rw_prepare",
    )(u, u, u, u, u, u, u, u, mu, w0, a0, k_k, k_a, w2p, a2p, g2p)


def _rw_scan_kernel(r_ref, lw_ref, k_ref, v_ref, kk_ref, b_ref, g_ref, rk_ref, lng_ref, lnb_ref,
                    o_ref, st_ref, *, chunk):
    @pl.when(pl.program_id(1) == 0)
    def _():
        st_ref[...] = jnp.zeros_like(st_ref)

    npair = RW_HEADS // 2
    w = npair * LANES
    n = 2 * chunk
    lane = _iota((n, w), 1)
    row = _iota((n, w), 0)
    own = ((lane // RW_DIM) % 2) == (row // chunk)
    rr = _iota((n, n), 0)
    cc = _iota((n, n), 1)
    same = (rr // chunk) == (cc // chunk)
    tri_incl = jnp.where(same & (rr >= cc), 1.0, 0.0).astype(BF16)
    strict = same & (rr > cc)
    incl = same & (rr >= cc)
    eye = jnp.where(rr == cc, 1.0, 0.0)
    blockdiag = (_iota((LANES, LANES), 0) // RW_DIM) == (_iota((LANES, LANES), 1) // RW_DIM)
    avg = _group_matrix(LANES, RW_DIM, 1.0 / RW_DIM)
    ones = _group_matrix(LANES, RW_DIM, 1.0)
    pairs = range(npair)
    sl = [slice(p * LANES, (p + 1) * LANES) for p in pairs]

    def stack(x):
        return jnp.concatenate([x, x], axis=0)

    def per_pair_rows(x):
        return jnp.concatenate([x[:, s] for s in sl], axis=0)

    def per_pair_lanes(x):
        return jnp.concatenate([x[p * chunk:(p + 1) * chunk] for p in pairs], axis=1)

    r, lw, k, v, kk, b = (ref[...] for ref in (r_ref, lw_ref, k_ref, v_ref, kk_ref, b_ref))
    lw2 = stack(lw)
    l1 = lw2.astype(BF16)
    l2 = (lw2 - l1.astype(F32)).astype(BF16)
    lin = _dot(tri_incl, l1) + _dot(tri_incl, l2)
    lend = lin[n - 1:n]
    p_inv = jnp.exp(-lin)
    p_dec = jnp.exp(lend - lin)
    zero = jnp.zeros((n, w), F32)
    kk_t = jnp.where(own, stack(kk) * jnp.exp(lin - lw2), zero).astype(BF16)
    r_t = jnp.where(own, stack(r) * jnp.exp(lin), zero).astype(BF16)
    v_f = jnp.where(own, stack(v), zero)
    v_s = v_f.astype(BF16)
    b_t = (stack(b) * p_inv).astype(BF16)
    k_t = (stack(k) * p_inv).astype(BF16)
    b_d = (stack(b) * p_dec).astype(BF16)
    k_d = (stack(k) * p_dec).astype(BF16)
    p_end = jnp.exp(lend)

    lhs = [jnp.concatenate([kk_t[:, s], r_t[:, s]], axis=0) for s in sl]
    a_b = [_dot_nt(lhs[p], b_t[:, sl[p]]) for p in pairs]
    a_k = [_dot_nt(lhs[p], k_t[:, sl[p]]) for p in pairs]
    s_t = [_dot_nt(lhs[p], st_ref[p].astype(BF16)) for p in pairs]
    n_ab = [jnp.where(strict, a_b[p][:n], 0.0) for p in pairs]
    rhs = [s_t[p][:n] + _dot(jnp.where(strict, a_k[p][:n], 0.0).astype(BF16), v_s[:, sl[p]]) for p in pairs]
    inv = [eye - n_ab[p] for p in pairs]
    pw = n_ab
    for _ in range(int(math.log2(chunk)) - 1):
        pw_b = [pw[p].astype(BF16) for p in pairs]
        pw = [_dot(pw_b[p], pw_b[p]) for p in pairs]
        inv = [inv[p] + _dot(inv[p].astype(BF16), pw[p].astype(BF16)) for p in pairs]
    u_s = [-_dot(inv[p].astype(BF16), rhs[p].astype(BF16)) for p in pairs]
    y2 = [s_t[p][n:] + _dot(jnp.where(incl, a_b[p][n:], 0.0).astype(BF16), u_s[p].astype(BF16))
          + _dot(jnp.where(incl, a_k[p][n:], 0.0).astype(BF16), v_s[:, sl[p]]) for p in pairs]
    for p in pairs:
        upd = _dot(u_s[p].T.astype(BF16), b_d[:, sl[p]]) + _dot(v_f[:, sl[p]].T.astype(BF16), k_d[:, sl[p]])
        st_ref[p] = st_ref[p] * p_end[:, sl[p]] + jnp.where(blockdiag, upd, 0.0)

    yr = jnp.concatenate([y2[p][:chunk] + y2[p][chunk:] for p in pairs], axis=0)
    yr1, yr2, _ = _split3(yr)
    d = yr - (_dot(yr1, avg) + _dot(yr2, avg))
    dd1, dd2, _ = _split3(d * d)
    var = _dot(dd1, avg) + _dot(dd2, avg)
    yn = per_pair_lanes(d * lax.rsqrt(var + RW_GN_EPS)) * lng_ref[...] + lnb_ref[...]
    bonus = per_pair_lanes(_dot(per_pair_rows(r * k * rk_ref[...]).astype(BF16), ones)) * v
    o_ref[...] = ((yn + bonus) * g_ref[...].astype(F32)).astype(o_ref.dtype)


def rw_scan(r, lw, k, v, kk, b, g, r_k, ln_g, ln_b, batch, seq):
    t = batch * seq
    chunk = min(RW_CHUNK, seq)
    nc = seq // chunk
    w = BRANCH_W
    blk = pl.BlockSpec((chunk, w), lambda bb, c: (bb * nc + c, 0))
    vec = pl.BlockSpec((1, w), lambda bb, c: (0, 0))
    return pl.pallas_call(
        functools.partial(_rw_scan_kernel, chunk=chunk),
        grid=(batch, nc),
        in_specs=[blk] * 7 + [vec] * 3,
        out_specs=blk,
        out_shape=jax.ShapeDtypeStruct((t, w), BF16),
        scratch_shapes=[pltpu.VMEM((RW_HEADS // 2, LANES, LANES), F32)],
        compiler_params=_cparams(("parallel", "arbitrary")),
        name="rw_scan",
    )(r, lw, k, v, kk, b, g, r_k, ln_g, ln_b)


def tail_columns(w):
    n = w.shape[-1]
    pad = jnp.zeros(w.shape[:-1] + (TAIL_W - (n - SRC_SWQ),), w.dtype)
    return jnp.concatenate([w[..., SRC_GATE:], w[..., SRC_SWQ:SRC_GATE], pad], axis=-1)


def _rope(u, cos, sin, half):
    return u * cos + _swap_halves(u, half) * sin


DSA_Q_BLK = 256
DSA_IQ_BLK = 512


def _dsa_pre_kernel(*refs):
    nq, niq = BRANCH_W // DSA_Q_BLK, BRANCH_W // DSA_IQ_BLK
    q_refs, refs = refs[:nq], refs[nq:]
    c_ref, refs = refs[0], refs[1:]
    iq_refs, refs = refs[:niq], refs[niq:]
    (ikw_ref, ca_ref, sa_ref, cb_ref, sb_ref, qg_ref, kg_ref, cg_ref, wuk_ref, wuv_ref,
     qo_ref, ko_ref, vo_ref, iqo_ref, iko_ref, iwo_ref) = refs
    ca, sa, cb, sb = ca_ref[...], sa_ref[...], cb_ref[...], sb_ref[...]
    for h in range(DSA_HEADS):
        sl = slice(h * DSA_DIM, (h + 1) * DSA_DIM)
        per = DSA_Q_BLK // DSA_DIM
        q = q_refs[h // per][:, (h % per) * DSA_DIM:(h % per + 1) * DSA_DIM].astype(F32)
        q = q * lax.rsqrt(jnp.mean(q * q, axis=-1, keepdims=True) + NORM_EPS) * qg_ref[...]
        qo_ref[:, sl] = (_rope(q, ca, sa, 64) * DSA_DIM ** -0.5).astype(qo_ref.dtype)
    c = c_ref[...].astype(F32)
    c = (c * lax.rsqrt(jnp.mean(c * c, axis=-1, keepdims=True) + NORM_EPS) * cg_ref[...]).astype(BF16)
    k = _dot(c, wuk_ref[...].astype(BF16))
    k = k * lax.rsqrt(jnp.mean(k * k, axis=-1, keepdims=True) + NORM_EPS) * kg_ref[...]
    ko_ref[...] = _rope(k, ca, sa, 64).astype(ko_ref.dtype)
    vo_ref[...] = _dot(c, wuv_ref[...].astype(BF16)).astype(vo_ref.dtype)
    for s in range(IDX_HEADS * IDX_DIM // LANES):
        sl = slice(s * LANES, (s + 1) * LANES)
        per = DSA_IQ_BLK // LANES
        iq = iq_refs[s // per][:, (s % per) * LANES:(s % per + 1) * LANES].astype(F32)
        iqo_ref[:, sl] = _rope(iq, cb, sb, 32).astype(iqo_ref.dtype)
    ikw = ikw_ref[...].astype(F32)
    lane = _iota(ikw.shape, 1)
    ik = _rope(ikw, cb, sb, 32)
    iko_ref[...] = jnp.where(lane < IDX_DIM, ik, pltpu.roll(ik, IDX_DIM, 1)).astype(iko_ref.dtype)
    iw = pltpu.roll(ikw, IDX_DIM, 1) * (IDX_HEADS ** -0.5 * IDX_DIM ** -0.5)
    iwo_ref[...] = jnp.where(lane < IDX_HEADS, iw, 0.0)


def _dsa_kernel(q_ref, iq_ref, iw_ref, k_ref, v_ref, ik_ref, o_ref, key_ref, bias_ref, vt_ref, m_ref, l_ref, acc_ref,
                *, tq, seq, top_k):
    kb = tq
    qb = pl.program_id(1)
    nblk = qb + 1
    nh = DSA_HEADS
    int_min = jnp.int32(-2 ** 31)
    kidx = _iota((kb, tq), 0)
    ridx = _iota((kb, tq), 1)

    def causal_mask(j):
        return (kidx + j * kb) <= (ridx + qb * tq)

    def block(ref, j):
        return ref[pl.ds(pl.multiple_of(j * kb, kb), kb), :]

    def fold(x, op):
        return op(x.reshape(kb // 8, 8, x.shape[1]), axis=0)

    @pl.when(qb == 0)
    def _():
        for j in range(seq // LANES):
            vt_ref[:, j * LANES:(j + 1) * LANES] = v_ref[j * LANES:(j + 1) * LANES, :].astype(F32).T.astype(BF16)

    first = _iota((tq, LANES), 1) < IDX_DIM
    lhs = []
    for p in range(IDX_HEADS // 2):
        qp = iq_ref[:, p * LANES:(p + 1) * LANES]
        zero = jnp.zeros_like(qp)
        lhs.append(jnp.concatenate([jnp.where(first, qp, zero), jnp.where(first, zero, qp)], axis=0))
    iw_t = iw_ref[...].T
    w_row = [iw_t[h:h + 1] for h in range(IDX_HEADS)]

    def score_block(j, _):
        ik = block(ik_ref, j)
        sc = jnp.zeros((kb, tq), F32)
        for p in range(IDX_HEADS // 2):
            z = jnp.maximum(_dot_nt(ik, lhs[p]), 0.0)
            sc = sc + z[:, :tq] * w_row[2 * p] + z[:, tq:] * w_row[2 * p + 1]
        sc = sc + 0.0
        bits = lax.bitcast_convert_type(sc, I32)
        skey = bits ^ ((bits >> 31) & jnp.int32(0x7FFFFFFF))
        key_ref[pl.ds(pl.multiple_of(j * kb, kb), kb), :] = jnp.where(causal_mask(j), skey, int_min)
        return 0

    lax.fori_loop(0, nblk, score_block, 0)

    def count(pred_fn):
        def body(j, cnt):
            return cnt + fold(jnp.where(pred_fn(block(key_ref, j)), 1.0, 0.0), jnp.sum)
        return jnp.sum(lax.fori_loop(0, nblk, body, jnp.zeros((8, tq), F32)), axis=0, keepdims=True)

    def bit_step(i, thr):
        cand = thr ^ (jnp.int32(1) << (31 - i))
        return jnp.where(count(lambda keys: keys >= cand) >= top_k, cand, thr)

    thr = lax.fori_loop(0, 32, bit_step, jnp.full((1, tq), int_min, I32))
    need = top_k - count(lambda keys: keys > thr)

    q_all = jnp.concatenate([q_ref[:, h * DSA_DIM:(h + 1) * DSA_DIM] for h in range(nh)], axis=0)
    lower = jnp.where(_iota((kb, kb), 0) > _iota((kb, kb), 1), 1.0, 0.0).astype(BF16)
    m_ref[...] = jnp.full_like(m_ref, -1e30)

    def logits(j, bias):
        lt = _dot_nt(block(k_ref, j), q_all)
        return jnp.concatenate([lt[:, h * tq:(h + 1) * tq] + bias for h in range(nh)], axis=1)

    def select(j, ties_seen):
        keys = block(key_ref, j)
        causal = causal_mask(j)
        tie = causal & (keys == thr)
        tie_f = jnp.where(tie, 1.0, 0.0)
        rank = ties_seen + _dot(lower, tie_f.astype(BF16))
        sel = causal & ((keys > thr) | (tie & (rank < need)))
        bias = jnp.where(sel, 0.0, -1e30)
        bias_ref[pl.ds(pl.multiple_of(j * kb, kb), kb), :] = bias
        m_ref[...] = jnp.maximum(m_ref[...], fold(logits(j, bias), jnp.max))
        return ties_seen + jnp.sum(tie_f, axis=0, keepdims=True)

    lax.fori_loop(0, nblk, select, jnp.zeros((1, tq), F32))
    m_all = jnp.broadcast_to(jnp.max(m_ref[...], axis=0, keepdims=True), m_ref.shape)
    l_ref[...] = jnp.zeros_like(l_ref)
    acc_ref[...] = jnp.zeros_like(acc_ref)

    def attend(j, _):
        lt = logits(j, block(bias_ref, j))
        pr = jnp.exp(lt.reshape(kb // 8, 8, nh * tq) - m_all[None]).reshape(kb, nh * tq)
        l_ref[...] += fold(pr, jnp.sum)
        acc_ref[...] += _dot(vt_ref[:, pl.ds(pl.multiple_of(j * kb, kb), kb)], pr.astype(BF16))
        return 0

    lax.fori_loop(0, nblk, attend, 0)
    out_t = acc_ref[...] / jnp.sum(l_ref[...], axis=0, keepdims=True)
    for h in range(nh):
        o_ref[:, h * DSA_DIM:(h + 1) * DSA_DIM] = out_t[:, h * tq:(h + 1) * tq].T.astype(o_ref.dtype)


def dsa_attention(u, tables, q_gain, k_gain, kv_gain, w_uk, w_uv, batch, seq, tm=256, tq=256):
    t = batch * seq
    tm = min(tm, seq)
    ca, sa, cb, sb = tables
    w = BRANCH_W
    row = lambda width, col: pl.BlockSpec((tm, width), lambda i: (i, col // width))
    tab = pl.BlockSpec((tm, LANES), lambda i: (i, 0))
    vec = lambda width: pl.BlockSpec((1, width), lambda i: (0, 0))
    mat = pl.BlockSpec((DSA_KV_RANK, DSA_DIM), lambda i: (0, 0))
    o_w = pl.BlockSpec((tm, w), lambda i: (i, 0))
    o_n = pl.BlockSpec((tm, LANES), lambda i: (i, 0))
    q, k, v, iq, ik, iw = pl.pallas_call(
        _dsa_pre_kernel,
        grid=(t // tm,),
        in_specs=[row(DSA_Q_BLK, COL_DSAQ + j * DSA_Q_BLK) for j in range(w // DSA_Q_BLK)]
        + [row(DSA_KV_RANK, COL_CKV)]
        + [row(DSA_IQ_BLK, COL_IDXQ + j * DSA_IQ_BLK) for j in range(w // DSA_IQ_BLK)]
        + [row(LANES, COL_IDXKW), tab, tab, tab, tab, vec(DSA_DIM), vec(DSA_DIM), vec(DSA_KV_RANK), mat, mat],
        out_specs=[o_w, o_n, o_n, o_w, o_n, o_n],
        out_shape=[jax.ShapeDtypeStruct((t, w), BF16), jax.ShapeDtypeStruct((t, LANES), BF16),
                   jax.ShapeDtypeStruct((t, LANES), BF16), jax.ShapeDtypeStruct((t, w), BF16),
                   jax.ShapeDtypeStruct((t, LANES), BF16), jax.ShapeDtypeStruct((t, LANES), F32)],
        compiler_params=_cparams(("parallel",)),
        name="dsa_prepare",
    )(*([u] * (w // DSA_Q_BLK + 1 + w // DSA_IQ_BLK + 1)), ca, sa, cb, sb,
      q_gain.reshape(1, -1), k_gain.reshape(1, -1), kv_gain.reshape(1, -1), w_uk, w_uv)

    nq = seq // tq
    top_k = min(IDX_TOPK_MAX, seq // 4)
    qrow = lambda width: pl.BlockSpec((tq, width), lambda b, i: (b * nq + i, 0))
    full = pl.BlockSpec((seq, LANES), lambda b, i: (b, 0))
    return pl.pallas_call(
        functools.partial(_dsa_kernel, tq=tq, seq=seq, top_k=top_k),
        grid=(batch, nq),
        in_specs=[qrow(w), qrow(w), qrow(LANES), full, full, full],
        out_specs=qrow(w),
        out_shape=jax.ShapeDtypeStruct((t, w), BF16),
        scratch_shapes=[pltpu.VMEM((seq, tq), I32), pltpu.VMEM((seq, tq), F32), pltpu.VMEM((DSA_DIM, seq), BF16),
                        pltpu.VMEM((8, DSA_HEADS * tq), F32), pltpu.VMEM((8, DSA_HEADS * tq), F32),
                        pltpu.VMEM((DSA_DIM, DSA_HEADS * tq), F32)],
        compiler_params=_cparams(("parallel", "arbitrary")),
        name="dsa_attention",
    )(q, iq, iw, k, v, ik)


def _sw_kernel(sink_ref, q_ref, kp_ref, kc_ref, vp_ref, vc_ref, cbp_ref, sbp_ref, cbc_ref, sbc_ref,
               qg_ref, kg_ref, o_ref, *, blk):
    n = pl.program_id(1)
    avg = _group_matrix(LANES, SW_DIM, 1.0 / SW_DIM)
    lane2 = _iota((2 * blk, LANES), 1)
    lane1 = _iota((blk, LANES), 1)

    def norm(x, gain):
        return x * lax.rsqrt(_dot_exact_rhs(x * x, avg) + NORM_EPS) * gain

    cb = jnp.concatenate([cbp_ref[...], cbc_ref[...]], axis=0)
    sb = jnp.concatenate([sbp_ref[...], sbc_ref[...]], axis=0)
    k = jnp.concatenate([kp_ref[...], kc_ref[...]], axis=0).astype(F32)
    k = _rope(norm(k, kg_ref[...]), cb, sb, 32)
    v = jnp.concatenate([vp_ref[...], vc_ref[...]], axis=0).astype(F32)
    k_sw, v_sw = pltpu.roll(k, SW_DIM, 1), pltpu.roll(v, SW_DIM, 1)
    k2 = [jnp.where(lane2 < SW_DIM, k, k_sw).astype(BF16), jnp.where(lane2 < SW_DIM, k_sw, k).astype(BF16)]
    v2 = [jnp.where(lane2 < SW_DIM, v, v_sw).astype(BF16), jnp.where(lane2 < SW_DIM, v_sw, v).astype(BF16)]

    r = _iota((2 * blk, 2 * blk), 0) % blk
    c = _iota((2 * blk, 2 * blk), 1)
    dist = r - (c - blk)
    mask = (dist >= 0) & (dist < WINDOW) & ((c >= blk) | (n > 0))
    top = _iota((2 * blk, 1), 0) < blk
    cbc, sbc = cbc_ref[...], sbc_ref[...]
    for p in range(SW_HEADS // 2):
        g = (2 * p) // (SW_HEADS // SW_KV_HEADS)
        q = q_ref[:, p * LANES:(p + 1) * LANES].astype(F32)
        q = _rope(norm(q, qg_ref[...]), cbc, sbc, 32) * SW_DIM ** -0.5
        zero = jnp.zeros_like(q)
        lhs = jnp.concatenate([jnp.where(lane1 < SW_DIM, q, zero), jnp.where(lane1 < SW_DIM, zero, q)],
                              axis=0).astype(BF16)
        logit = jnp.where(mask, _dot_nt(lhs, k2[g]), -1e30)
        sink = jnp.where(top, sink_ref[2 * p], sink_ref[2 * p + 1])
        m = jnp.maximum(jnp.max(logit, axis=1, keepdims=True), sink)
        pr = jnp.where(mask, jnp.exp(logit - m), 0.0)
        den = jnp.sum(pr, axis=1, keepdims=True) + jnp.exp(sink - m)
        o2 = _dot((pr / den).astype(BF16), v2[g])
        o_ref[:, p * LANES:(p + 1) * LANES] = jnp.where(lane1 < SW_DIM, o2[:blk], o2[blk:]).astype(o_ref.dtype)


def sw_attention(u, tables, q_gain, k_gain, sinks, batch, seq, blk=128):
    t = batch * seq
    nb = seq // blk
    _, _, cb, sb = tables
    w = BRANCH_W
    cur = lambda width, col: pl.BlockSpec((blk, width), lambda b, i: (b * nb + i, col // width))
    prev = lambda width, col: pl.BlockSpec((blk, width), lambda b, i: (b * nb + jnp.maximum(i - 1, 0), col // width))
    vec = pl.BlockSpec((1, LANES), lambda b, i: (0, 0))
    tile2 = lambda g: jnp.tile(g.reshape(1, SW_DIM), (1, 2))
    return pl.pallas_call(
        functools.partial(_sw_kernel, blk=blk),
        grid=(batch, nb),
        in_specs=[pl.BlockSpec(memory_space=pltpu.SMEM),
                  cur(w, COL_SWQ), prev(LANES, COL_SWK), cur(LANES, COL_SWK), prev(LANES, COL_SWV), cur(LANES, COL_SWV),
                  prev(LANES, 0), prev(LANES, 0), cur(LANES, 0), cur(LANES, 0), vec, vec],
        out_specs=pl.BlockSpec((blk, w), lambda b, i: (b * nb + i, 0)),
        out_shape=jax.ShapeDtypeStruct((t, w), BF16),
        compiler_params=_cparams(("parallel", "parallel")),
        name="sw_attention",
    )(sinks, u, u, u, u, u, cb, sb, cb, sb, tile2(q_gain), tile2(k_gain))


def _merge_kernel(y_ref, g_ref, w_ref, o_ref, acc_ref):
    n = pl.program_id(1)
    up = _dot(y_ref[0], w_ref[0].astype(BF16))
    contrib = _sigmoid(g_ref[...].astype(F32)) * up

    @pl.when(n == 0)
    def _():
        acc_ref[...] = contrib

    @pl.when(n > 0)
    def _():
        acc_ref[...] += contrib

    @pl.when(n == N_BRANCH - 1)
    def _():
        o_ref[...] = acc_ref[...].astype(o_ref.dtype)


def merge_branches(ys, u, w_branch, tm=1024):
    _, t, w = ys.shape
    d = w_branch.shape[-1]
    tm = min(tm, t)
    return pl.pallas_call(
        _merge_kernel,
        grid=(t // tm, N_BRANCH),
        in_specs=[pl.BlockSpec((1, tm, w), lambda i, n: (n, i, 0)),
                  pl.BlockSpec((tm, d), lambda i, n: (i, COL_GATE // d + n)),
                  pl.BlockSpec((1, w, d), lambda i, n: (n, 0, 0))],
        out_specs=pl.BlockSpec((tm, d), lambda i, n: (i, 0)),
        out_shape=jax.ShapeDtypeStruct((t, d), BF16),
        scratch_shapes=[pltpu.VMEM((tm, d), F32)],
        compiler_params=_cparams(("parallel", "arbitrary")),
        name="merge_branches",
    )(ys, u, w_branch)


FFN_TM = 1024
FFN_ROWS = (1024, 512, 256)
FFN_TF = 256


def _ffn_kernel(e_ref, rows_ref, x_ref, wg_ref, wu_ref, wd_ref, o_ref):
    rows = rows_ref[pl.program_id(0)]

    @pl.when(pl.program_id(1) == 0)
    def _():
        o_ref[...] = jnp.zeros_like(o_ref)

    for k, size in enumerate(FFN_ROWS):
        below = FFN_ROWS[k + 1] if k + 1 < len(FFN_ROWS) else 0

        @pl.when((rows > below) & (rows <= size))
        def _():
            x = x_ref[:size, :]
            g = _dot(x, wg_ref[0].astype(BF16))
            a = (g * _sigmoid(g) * _dot(x, wu_ref[0].astype(BF16))).astype(BF16)
            o_ref[:size, :] += _dot(a, wd_ref[0].astype(BF16))


def ffn_tiles(xs, tile_expert, tile_rows, w_gate, w_up, w_down):
    r, d = xs.shape
    _, _, ff = w_gate.shape
    ns = r // FFN_TM
    nf = ff // FFN_TF

    def f_eff(s, f, rows):
        return jnp.where(rows[s] > 0, f, nf - 1)

    grid_spec = pltpu.PrefetchScalarGridSpec(
        num_scalar_prefetch=2,
        grid=(ns, nf),
        in_specs=[pl.BlockSpec((FFN_TM, d), lambda s, f, e, rows: (s, 0)),
                  pl.BlockSpec((1, d, FFN_TF), lambda s, f, e, rows: (e[s], 0, f_eff(s, f, rows))),
                  pl.BlockSpec((1, d, FFN_TF), lambda s, f, e, rows: (e[s], 0, f_eff(s, f, rows))),
                  pl.BlockSpec((1, FFN_TF, d), lambda s, f, e, rows: (e[s], f_eff(s, f, rows), 0))],
        out_specs=pl.BlockSpec((FFN_TM, d), lambda s, f, e, rows: (s, 0)),
    )
    return pl.pallas_call(
        _ffn_kernel,
        grid_spec=grid_spec,
        out_shape=jax.ShapeDtypeStruct((r, d), F32),
        compiler_params=_cparams(("parallel", "arbitrary")),
        name="ffn_tiles",
    )(tile_expert, tile_rows, xs, w_gate, w_up, w_down)


def _add_kernel(a_ref, b_ref, o_ref):
    o_ref[...] = a_ref[...] + b_ref[...]


def add(a, b, tm=512):
    t, d = a.shape
    spec = pl.BlockSpec((tm, d), lambda i: (i, 0))
    return pl.pallas_call(_add_kernel, grid=(t // tm,), in_specs=[spec, spec], out_specs=spec,
                          out_shape=jax.ShapeDtypeStruct((t, d), a.dtype),
                          compiler_params=_cparams(("parallel",)), name="residual_add")(a, b)


def dense_ffn(x, h, w_gate, w_up, w_down):
    t = h.shape[0]
    ns = t // FFN_TM
    ys = ffn_tiles(h, jnp.zeros((ns,), I32), jnp.full((ns,), FFN_TM, I32),
                   w_gate[None], w_up[None], w_down[None])
    return add(x, ys)


def _router_kernel(x_ref, g_ref, w_ref, h_ref, r_ref):
    x = x_ref[...]
    h = x * lax.rsqrt(jnp.mean(x * x, axis=-1, keepdims=True) + NORM_EPS) * g_ref[...]
    h_ref[...] = h
    logit = _dot_f32(h, w_ref[...])
    lane = _iota(logit.shape, 1).astype(F32)
    neg = -jnp.inf
    l1 = jnp.where(lane < N_EXPERTS, logit, neg)
    m1 = jnp.max(l1, axis=1, keepdims=True)
    i1 = jnp.min(jnp.where(l1 == m1, lane, float(LANES)), axis=1, keepdims=True)
    l2 = jnp.where(lane == i1, neg, l1)
    m2 = jnp.max(l2, axis=1, keepdims=True)
    i2 = jnp.min(jnp.where(l2 == m2, lane, float(LANES)), axis=1, keepdims=True)
    e = jnp.exp(m2 - m1)
    w1 = 1.0 / (1.0 + e)
    w2 = e / (1.0 + e)
    r_ref[...] = jnp.where(lane == 0, i1, jnp.where(lane == 1, i2, jnp.where(lane == 2, w1,
                           jnp.where(lane == 3, w2, 0.0))))


def route(x, gain, router, tm=256):
    t, d = x.shape
    wp = jnp.zeros((d, LANES), F32).at[:, :N_EXPERTS].set(router)
    return pl.pallas_call(
        _router_kernel,
        grid=(t // tm,),
        in_specs=[pl.BlockSpec((tm, d), lambda i: (i, 0)), pl.BlockSpec((1, d), lambda i: (0, 0)),
                  pl.BlockSpec((d, LANES), lambda i: (0, 0))],
        out_specs=[pl.BlockSpec((tm, d), lambda i: (i, 0)), pl.BlockSpec((tm, LANES), lambda i: (i, 0))],
        out_shape=[jax.ShapeDtypeStruct((t, d), F32), jax.ShapeDtypeStruct((t, LANES), F32)],
        compiler_params=_cparams(("parallel",)),
        name="router",
    )(x, gain.reshape(1, d), wp)


def _row_copy(src_ref, row, buf, i, sem):
    return pltpu.make_async_copy(src_ref.at[pl.ds(row, 1)], buf.at[pl.ds(i, 1)], sem)


ROW_DMA_UNROLL = 8


def _row_loop(n, fn):
    def body(blk, _):
        for u in range(ROW_DMA_UNROLL):
            fn(blk * ROW_DMA_UNROLL + u)
        return 0
    lax.fori_loop(0, n // ROW_DMA_UNROLL, body, 0)


def _gather_kernel(idx_ref, src_ref, o_ref, buf, sem, *, gb):
    i = pl.program_id(0)
    n = pl.num_programs(0)

    def start_step(step):
        slot = step % 2
        _row_loop(gb, lambda r: _row_copy(src_ref, idx_ref[step * gb + r], buf.at[slot], r, sem.at[slot]).start())

    @pl.when(i == 0)
    def _():
        start_step(i)

    @pl.when(i + 1 < n)
    def _():
        start_step(i + 1)

    slot = i % 2
    _row_loop(gb, lambda r: _row_copy(src_ref, 0, buf.at[slot], r, sem.at[slot]).wait())
    o_ref[...] = buf[slot].astype(o_ref.dtype)


def gather_rows(src, idx, out_dtype, gb=256):
    r = idx.shape[0]
    d = src.shape[1]
    grid_spec = pltpu.PrefetchScalarGridSpec(
        num_scalar_prefetch=1,
        grid=(r // gb,),
        in_specs=[pl.BlockSpec(memory_space=pl.ANY)],
        out_specs=pl.BlockSpec((gb, d), lambda i, idx: (i, 0)),
        scratch_shapes=[pltpu.VMEM((2, gb, d), src.dtype), pltpu.SemaphoreType.DMA((2,))],
    )
    return pl.pallas_call(
        functools.partial(_gather_kernel, gb=gb),
        grid_spec=grid_spec,
        out_shape=jax.ShapeDtypeStruct((r, d), out_dtype),
        compiler_params=_cparams(("arbitrary",)),
        name="gather_rows",
    )(idx, src)


def _combine_kernel(d0_ref, d1_ref, x_ref, w_ref, ys_ref, o_ref, buf, sem, *, tm):
    i = pl.program_id(0)
    n = pl.num_programs(0)

    def start_step(step):
        slot = step % 2

        def one(r):
            _row_copy(ys_ref, d0_ref[step * tm + r], buf.at[slot, 0], r, sem.at[slot]).start()
            _row_copy(ys_ref, d1_ref[step * tm + r], buf.at[slot, 1], r, sem.at[slot]).start()
        _row_loop(tm, one)

    @pl.when(i == 0)
    def _():
        start_step(i)

    @pl.when(i + 1 < n)
    def _():
        start_step(i + 1)

    slot = i % 2

    def wait_one(r):
        _row_copy(ys_ref, 0, buf.at[slot, 0], r, sem.at[slot]).wait()
        _row_copy(ys_ref, 0, buf.at[slot, 1], r, sem.at[slot]).wait()
    _row_loop(tm, wait_one)
    w = w_ref[...]
    o_ref[...] = x_ref[...] + w[:, 2:3] * buf[slot, 0] + w[:, 3:4] * buf[slot, 1]


def combine_rows(x, ys, d0, d1, w, tm=128):
    t, d = x.shape
    grid_spec = pltpu.PrefetchScalarGridSpec(
        num_scalar_prefetch=2,
        grid=(t // tm,),
        in_specs=[pl.BlockSpec((tm, d), lambda i, a, b: (i, 0)), pl.BlockSpec((tm, LANES), lambda i, a, b: (i, 0)),
                  pl.BlockSpec(memory_space=pl.ANY)],
        out_specs=pl.BlockSpec((tm, d), lambda i, a, b: (i, 0)),
        scratch_shapes=[pltpu.VMEM((2, 2, tm, d), F32), pltpu.SemaphoreType.DMA((2,))],
    )
    return pl.pallas_call(
        functools.partial(_combine_kernel, tm=tm),
        grid_spec=grid_spec,
        out_shape=jax.ShapeDtypeStruct((t, d), F32),
        compiler_params=_cparams(("arbitrary",)),
        name="combine_rows",
    )(d0, d1, x, w, ys)


def moe_ffn(x, gain, router, w_gate, w_up, w_down):
    t, d = x.shape
    h, rt = route(x, gain, router)
    e_flat = jnp.concatenate([rt[:, 0], rt[:, 1]]).astype(I32)
    tok = jnp.concatenate([jnp.arange(t, dtype=I32)] * 2)
    onehot = (e_flat[:, None] == jnp.arange(N_EXPERTS, dtype=I32)[None, :]).astype(I32)
    csum = jnp.cumsum(onehot, axis=0)
    rank = jnp.take_along_axis(csum - onehot, e_flat[:, None], axis=1)[:, 0]
    counts = csum[-1]
    n_tiles = (counts + FFN_TM - 1) // FFN_TM
    tile_end = jnp.cumsum(n_tiles)
    tile_start = tile_end - n_tiles
    dest = tile_start[e_flat] * FFN_TM + rank
    ns = TOP_K * t // FFN_TM + N_EXPERTS
    s_idx = jnp.arange(ns, dtype=I32)
    used = s_idx < tile_end[-1]
    s_clip = jnp.minimum(s_idx, tile_end[-1] - 1)
    tile_expert = jnp.minimum(jnp.sum((s_clip[:, None] >= tile_end[None, :]).astype(I32), axis=1), N_EXPERTS - 1)
    tile_rows = jnp.clip(counts[tile_expert] - (s_clip - tile_start[tile_expert]) * FFN_TM, 0, FFN_TM)
    tile_rows = jnp.where(used, tile_rows, 0).astype(I32)
    src_row = jnp.zeros((ns * FFN_TM,), I32).at[dest].set(tok)

    xs = gather_rows(h, src_row, BF16)
    ys = ffn_tiles(xs, tile_expert, tile_rows, w_gate, w_up, w_down)
    return combine_rows(x, ys, dest[:t], dest[t:], rt)


def _ple_kernel(h_ref, wg_ref, p_ref, wp_ref, x_ref, o_ref):
    gate = _sigmoid(_dot(h_ref[...], wg_ref[...].astype(BF16)))
    proj = _dot(p_ref[...].astype(BF16), wp_ref[...].astype(BF16))
    o_ref[...] = x_ref[...] + gate * proj


def ple(h, w_gate, p, w_proj, x, tm=1024, tn=512):
    t, d = x.shape
    pd = p.shape[1]
    tm = min(tm, t)
    return pl.pallas_call(
        _ple_kernel,
        grid=(t // tm, d // tn),
        in_specs=[pl.BlockSpec((tm, d), lambda i, j: (i, 0)), pl.BlockSpec((d, tn), lambda i, j: (0, j)),
                  pl.BlockSpec((tm, pd), lambda i, j: (i, 0)), pl.BlockSpec((pd, tn), lambda i, j: (0, j)),
                  pl.BlockSpec((tm, tn), lambda i, j: (i, j))],
        out_specs=pl.BlockSpec((tm, tn), lambda i, j: (i, j)),
        out_shape=jax.ShapeDtypeStruct((t, d), F32),
        compiler_params=_cparams(("parallel", "parallel")),
        name="ple",
    )(h, w_gate, p, w_proj, x)


def _pad_rows(w, start, total):
    return jnp.zeros((total, w.shape[1]), F32).at[start:start + w.shape[0]].set(w).astype(BF16)


def kernel(x, p, positions, w_in, mix_norm, ffn_norm, ple_norm, rw_mu, rw_w0, rw_w2, rw_a0, rw_a2, rw_g2, rw_kk,
           rw_ka, rw_rk, rw_ln_g, rw_ln_b, dsa_q_norm, dsa_k_norm, dsa_kv_norm, dsa_w_uk, dsa_w_uv, sw_q_norm,
           sw_k_norm, sw_sinks, w_branch, w_out, ffn_w_gate, ffn_w_up, ffn_w_down, moe_router, moe_w_gate, moe_w_up,
           moe_w_down, ple_w_gate, ple_w_proj):
    b, s, d = x.shape
    t = b * s
    depth = w_in.shape[0]
    xf = x.reshape(t, d)
    tables = rope_tables(positions)
    row = lambda a: a.reshape(1, -1)
    for i in range(depth):
        h = rmsnorm(xf, mix_norm[i])
        uh = matmul(h, w_in[i], n_cols=HEAD_W, out_dtype=BF16, tm=2048, name="in_proj_head")
        ut = matmul(h, tail_columns(w_in[i]), out_dtype=BF16, tm=2048, name="in_proj_tail")
        y_a = sb_attention(uh, b, s)
        rw = rw_prepare(uh, row(rw_mu[i]), row(rw_w0[i]), row(rw_a0[i]), row(rw_kk[i]), row(rw_ka[i]),
                        _pad_rows(rw_w2[i], 0, RW_LORA), _pad_rows(rw_a2[i], RW_LORA_W, RW_LORA),
                        _pad_rows(rw_g2[i], RW_LORA_W + RW_LORA_A, RW_LORA), b, s)
        y_b = rw_scan(*rw, row(rw_rk[i]), row(rw_ln_g[i]), row(rw_ln_b[i]), b, s)
        y_c = dsa_attention(uh, tables, dsa_q_norm[i], dsa_k_norm[i], dsa_kv_norm[i], dsa_w_uk[i], dsa_w_uv[i], b, s)
        y_d = sw_attention(ut, tables, sw_q_norm[i], sw_k_norm[i], sw_sinks[i], b, s)
        merged = merge_branches(jnp.stack([y_a, y_b, y_c, y_d]), ut, w_branch[i].astype(BF16))
        xf = matmul(merged, w_out[i], out_dtype=F32, residual=xf, name="out_proj")
        if i % 2 == 0:
            h = rmsnorm(xf, ffn_norm[i])
            xf = dense_ffn(xf, h, ffn_w_gate[i // 2], ffn_w_up[i // 2], ffn_w_down[i // 2])
        else:
            xf = moe_ffn(xf, ffn_norm[i], moe_router[i // 2], moe_w_gate[i // 2], moe_w_up[i // 2],
                         moe_w_down[i // 2])
        h = rmsnorm(xf, ple_norm[i])
        xf = ple(h, ple_w_gate[i], p[i].reshape(t, -1), ple_w_proj[i], xf)
    return xf.reshape(b, s, d)
```

```python
import functools
import math

import jax
import jax.numpy as jnp
from jax import lax
from jax.experimental import pallas as pl
from jax.experimental.pallas import tpu as pltpu

F32 = jnp.float32
BF16 = jnp.bfloat16
I32 = jnp.int32

D_MODEL = 2048
ROPE_THETA = 10000.0
NORM_EPS = 1e-6
N_BRANCH = 4
BRANCH_W = 1024
SB_HEADS, SB_DIM = 8, 128
RW_HEADS, RW_DIM = 16, 64
RW_LORA_W, RW_LORA_A, RW_LORA_G = 96, 96, 64
RW_LORA = RW_LORA_W + RW_LORA_A + RW_LORA_G
RW_GN_EPS = 64e-5
DSA_HEADS, DSA_DIM, DSA_KV_RANK = 8, 128, 256
IDX_HEADS, IDX_DIM, IDX_TOPK_MAX = 16, 64, 256
SW_HEADS, SW_KV_HEADS, SW_DIM, WINDOW = 16, 2, 64, 128
N_EXPERTS, TOP_K = 8, 2

LANES = 128
VMEM_LIMIT_BYTES = 56 * 1024 * 1024

HEAD_W = 9216
COL_SBQ = 0
COL_SBK = 1024
COL_SBV = 2048
COL_RWR = 3072
COL_RWK = 4096
COL_RWV = 5120
COL_RWL = 6144
COL_DSAQ = 6400
COL_CKV = 7424
COL_IDXQ = 7680
COL_IDXKW = 8704
SRC_SWQ = 8784
SRC_GATE = 10064
TAIL_W = 9728
COL_GATE = 0
COL_SWQ = 8192
COL_SWK = 9216
COL_SWV = 9344

RW_CHUNK = 64


def _cparams(sem, vmem=VMEM_LIMIT_BYTES):
    return pltpu.CompilerParams(dimension_semantics=sem, vmem_limit_bytes=vmem)


def _dot(a, b):
    return jnp.dot(a, b, preferred_element_type=F32)


def _dot_nt(a, b):
    return lax.dot_general(a, b, (((1,), (1,)), ((), ())), preferred_element_type=F32)


def _split3(a):
    a1 = a.astype(BF16)
    r1 = a - a1.astype(F32)
    a2 = r1.astype(BF16)
    a3 = (r1 - a2.astype(F32)).astype(BF16)
    return a1, a2, a3


def _dot_exact_rhs(a, m_bf16):
    a1, a2, a3 = _split3(a)
    return _dot(a1, m_bf16) + _dot(a2, m_bf16) + _dot(a3, m_bf16)


def _dot_exact_lhs(m_bf16, a):
    a1, a2, a3 = _split3(a)
    return _dot(m_bf16, a1) + _dot(m_bf16, a2) + _dot(m_bf16, a3)


def _dot_f32(a, b):
    a1, a2, a3 = _split3(a)
    b1, b2, b3 = _split3(b)
    return (_dot(a1, b1) + _dot(a1, b2) + _dot(a2, b1)) + (_dot(a1, b3) + _dot(a2, b2) + _dot(a3, b1))


def _softplus(z):
    return jnp.maximum(z, 0.0) + jnp.log(1.0 + jnp.exp(-jnp.abs(z)))


def _sigmoid(z):
    return 1.0 / (1.0 + jnp.exp(-z))


def _iota(shape, dim):
    return lax.broadcasted_iota(I32, shape, dim)


def _group_matrix(n, group, value):
    r = _iota((n, n), 0) // group
    c = _iota((n, n), 1) // group
    return jnp.where(r == c, value, 0.0).astype(BF16)


def _swap_halves(u, half):
    if 2 * half == LANES:
        return pltpu.roll(u, half, 1)
    lane = _iota(u.shape, 1)
    return jnp.where(lane % (2 * half) < half, pltpu.roll(u, LANES - half, 1), pltpu.roll(u, half, 1))


def _rmsnorm_kernel(x_ref, g_ref, o_ref):
    x = x_ref[...]
    ms = jnp.mean(x * x, axis=-1, keepdims=True)
    o_ref[...] = (x * lax.rsqrt(ms + NORM_EPS) * g_ref[...]).astype(o_ref.dtype)


def rmsnorm(x, gain, out_dtype=BF16, tm=512):
    t, d = x.shape
    return pl.pallas_call(
        _rmsnorm_kernel,
        grid=(t // tm,),
        in_specs=[pl.BlockSpec((tm, d), lambda i: (i, 0)), pl.BlockSpec((1, d), lambda i: (0, 0))],
        out_specs=pl.BlockSpec((tm, d), lambda i: (i, 0)),
        out_shape=jax.ShapeDtypeStruct((t, d), out_dtype),
        compiler_params=_cparams(("parallel",)),
        name="rmsnorm",
    )(x, gain.reshape(1, d))


def _mm_kernel(*refs, has_res):
    if has_res:
        a_ref, w_ref, r_ref, o_ref = refs
    else:
        a_ref, w_ref, o_ref = refs
    acc = _dot(a_ref[...].astype(BF16), w_ref[...].astype(BF16))
    if has_res:
        acc = acc + r_ref[...]
    o_ref[...] = acc.astype(o_ref.dtype)


def _layer_spec(block, index_map, layer):
    return pl.BlockSpec((None,) + tuple(block), lambda *idx: (layer,) + tuple(index_map(*idx)))


def matmul(a, w, layer, *, out_dtype, residual=None, n_cols=None, tm=1024, tn=512, name="matmul"):
    m, k = a.shape
    n = w.shape[2] if n_cols is None else n_cols
    tm, tn = min(tm, m), min(tn, n)
    in_specs = [pl.BlockSpec((tm, k), lambda i, j: (i, 0)), _layer_spec((k, tn), lambda i, j: (0, j), layer)]
    args = [a, w]
    if residual is not None:
        in_specs.append(pl.BlockSpec((tm, tn), lambda i, j: (i, j)))
        args.append(residual)
    return pl.pallas_call(
        functools.partial(_mm_kernel, has_res=residual is not None),
        grid=(m // tm, n // tn),
        in_specs=in_specs,
        out_specs=pl.BlockSpec((tm, tn), lambda i, j: (i, j)),
        out_shape=jax.ShapeDtypeStruct((m, n), out_dtype),
        compiler_params=_cparams(("parallel", "parallel")),
        name=name,
    )(*args)


def _rope_table_kernel(pos_ref, f64_ref, f32_ref, ca_ref, sa_ref, cb_ref, sb_ref):
    pos = pos_ref[...]
    lane = _iota((1, LANES), 1)
    ang_a = pos * f64_ref[...]
    ang_b = pos * f32_ref[...]
    ca_ref[...] = jnp.cos(ang_a)
    sa_ref[...] = jnp.where(lane < 64, -1.0, 1.0) * jnp.sin(ang_a)
    cb_ref[...] = jnp.cos(ang_b)
    sb_ref[...] = jnp.where(lane % 64 < 32, -1.0, 1.0) * jnp.sin(ang_b)


def rope_tables(positions, tm=512):
    t = positions.size
    pos = positions.reshape(t, 1).astype(F32)
    inv64 = ROPE_THETA ** (-jnp.arange(64, dtype=F32) / 64)
    inv32 = ROPE_THETA ** (-jnp.arange(32, dtype=F32) / 32)
    f64 = jnp.tile(inv64, 2).reshape(1, LANES)
    f32 = jnp.tile(inv32, 4).reshape(1, LANES)
    row = pl.BlockSpec((tm, LANES), lambda i: (i, 0))
    vec = pl.BlockSpec((1, LANES), lambda i: (0, 0))
    return pl.pallas_call(
        _rope_table_kernel,
        grid=(t // tm,),
        in_specs=[pl.BlockSpec((tm, 1), lambda i: (i, 0)), vec, vec],
        out_specs=[row] * 4,
        out_shape=[jax.ShapeDtypeStruct((t, LANES), F32)] * 4,
        compiler_params=_cparams(("parallel",)),
        name="rope_tables",
    )(pos, f64, f32)


SB_TQ = 256
SB_G = 4


def _sb_kernel(q_ref, k_ref, v_ref, o_ref, *, tq, scale):
    qi = pl.program_id(2)
    r = _iota((tq, tq), 0)
    c = _iota((tq, tq), 1)
    later = jnp.where(r > c, 1.0, 0.0).astype(BF16)
    qs = [(q_ref[:, g * SB_DIM:(g + 1) * SB_DIM].astype(F32) * scale).astype(BF16) for g in range(SB_G)]

    heads = range(SB_G)
    cols = [slice(g * SB_DIM, (g + 1) * SB_DIM) for g in heads]

    def span(j, state, diagonal):
        off = pl.multiple_of(j * tq, tq)
        carry, acc = state
        ks = [k_ref[pl.ds(off, tq), cols[g]].astype(BF16) for g in heads]
        vs = [v_ref[pl.ds(off, tq), cols[g]].astype(BF16) for g in heads]
        zs = [_dot_nt(qs[g], ks[g]) for g in heads]
        sps = [_softplus(z) for z in zs]
        lks = [jnp.where(r > c, -sp, 0.0) if diagonal else -sp for sp in sps]
        his = [lk.astype(BF16) for lk in lks]
        los = [(lks[g] - his[g].astype(F32)).astype(BF16) for g in heads]
        css = [_dot(his[g], later) + _dot(los[g], later) for g in heads]
        ws = [jnp.exp(zs[g] - sps[g] + css[g] + carry[g]) for g in heads]
        if diagonal:
            ws = [jnp.where(r > c, w, 0.0) for w in ws]
        acc = tuple(acc[g] + _dot(ws[g].astype(BF16), vs[g]) for g in heads)
        carry = tuple(carry[g] + jnp.sum(lks[g], axis=1, keepdims=True) for g in heads)
        return carry, acc

    init = (tuple(jnp.zeros((tq, 1), F32) for _ in heads), tuple(jnp.zeros((tq, SB_DIM), F32) for _ in heads))
    state = span(qi, init, True)
    _, acc = lax.fori_loop(0, qi, lambda i, s: span(qi - 1 - i, s, False), state)
    for g in heads:
        o_ref[:, cols[g]] = acc[g].astype(o_ref.dtype)


def sb_attention(u, batch, seq):
    t = batch * seq
    tq = min(SB_TQ, seq)
    nq = seq // tq
    gw = SB_G * SB_DIM
    qc, kc, vc = COL_SBQ // gw, COL_SBK // gw, COL_SBV // gw
    return pl.pallas_call(
        functools.partial(_sb_kernel, tq=tq, scale=SB_DIM ** -0.5),
        grid=(batch, SB_HEADS // SB_G, nq),
        in_specs=[
            pl.BlockSpec((tq, gw), lambda b, h, i: (b * nq + i, qc + h)),
            pl.BlockSpec((seq, gw), lambda b, h, i: (b, kc + h)),
            pl.BlockSpec((seq, gw), lambda b, h, i: (b, vc + h)),
        ],
        out_specs=pl.BlockSpec((tq, gw), lambda b, h, i: (b * nq + i, h)),
        out_shape=jax.ShapeDtypeStruct((t, BRANCH_W), BF16),
        compiler_params=_cparams(("parallel", "parallel", "arbitrary")),
        name="sb_attention",
    )(u, u, u)


def _rw_pre_kernel(r_ref, k_ref, v_ref, l_ref, pr_ref, pk_ref, pv_ref, plr_ref,
                   mu_ref, w0_ref, a0_ref, kkg_ref, ka_ref, w2_ref, a2_ref, g2_ref,
                   ro_ref, lw_ref, ko_ref, vo_ref, kk_ref, b_ref, g_ref, *, tm, seq):
    i = pl.program_id(0)
    first = (i * tm) % seq == 0
    row0 = _iota((tm, 1), 0) == 0

    def shifted(cur_ref, prev_ref, lo, hi):
        cur = cur_ref[...].astype(F32)
        last = prev_ref[...].astype(F32)[-1:, :]
        last = jnp.where(first, 0.0, last)
        prev = jnp.where(row0, last, pltpu.roll(cur, 1, 0))
        return cur + mu_ref[:, lo:hi] * (prev - cur)

    w = BRANCH_W
    r = shifted(r_ref, pr_ref, 0, w)
    k = shifted(k_ref, pk_ref, w, 2 * w)
    v = shifted(v_ref, pv_ref, 2 * w, 3 * w)
    z = shifted(l_ref, plr_ref, 3 * w, 3 * w + RW_LORA)

    w_pre = w0_ref[...] + _dot(jnp.tanh(z).astype(BF16), w2_ref[...])
    w_log = -_softplus(-w_pre) - 0.5
    lw_ref[...] = -jnp.exp(w_log)
    a = _sigmoid(a0_ref[...] + _dot(z.astype(BF16), a2_ref[...]))
    g_ref[...] = _dot(_sigmoid(z).astype(BF16), g2_ref[...]).astype(g_ref.dtype)
    ones = _group_matrix(LANES, RW_DIM, 1.0)
    kk = k * kkg_ref[...]
    for s in range(w // LANES):
        sl = slice(s * LANES, (s + 1) * LANES)
        kks = kk[:, sl]
        ss = _dot_exact_rhs(kks * kks, ones)
        kkn = kks / jnp.maximum(jnp.sqrt(ss), 1e-12)
        kk_ref[:, sl] = kkn
        b_ref[:, sl] = kkn * a[:, sl]
    ro_ref[...] = r
    ko_ref[...] = k * (1.0 + (a - 1.0) * ka_ref[...])
    vo_ref[...] = v


def rw_prepare(u, mu, w0, a0, k_k, k_a, w2p, a2p, g2p, batch, seq, tm=256):
    t = batch * seq
    tm = min(tm, seq)
    w = BRANCH_W
    sub = 16

    def cur(width, col):
        return pl.BlockSpec((tm, width), lambda i: (i, col // width))

    def prev(width, col):
        return pl.BlockSpec((sub, width), lambda i: (jnp.maximum(i * (tm // sub) - 1, 0), col // width))

    def vec(width):
        return pl.BlockSpec((1, width), lambda i: (0, 0))

    def mat():
        return pl.BlockSpec((RW_LORA, w), lambda i: (0, 0))

    out = pl.BlockSpec((tm, w), lambda i: (i, 0))
    f = jax.ShapeDtypeStruct((t, w), F32)
    return pl.pallas_call(
        functools.partial(_rw_pre_kernel, tm=tm, seq=seq),
        grid=(t // tm,),
        in_specs=[cur(w, COL_RWR), cur(w, COL_RWK), cur(w, COL_RWV), cur(RW_LORA, COL_RWL),
                  prev(w, COL_RWR), prev(w, COL_RWK), prev(w, COL_RWV), prev(RW_LORA, COL_RWL),
                  vec(3 * w + RW_LORA), vec(w), vec(w), vec(w), vec(w), mat(), mat(), mat()],
        out_specs=[out] * 7,
        out_shape=[f, f, f, f, f, f, jax.ShapeDtypeStruct((t, w), BF16)],
        compiler_params=_cparams(("parallel",)),
        name="rw_prepare",
    )(u, u, u, u, u, u, u, u, mu, w0, a0, k_k, k_a, w2p, a2p, g2p)


def _rw_scan_kernel(r_ref, lw_ref, k_ref, v_ref, kk_ref, b_ref, g_ref, rk_ref, lng_ref, lnb_ref,
                    o_ref, st_ref, *, chunk):
    @pl.when(pl.program_id(1) == 0)
    def _():
        st_ref[...] = jnp.zeros_like(st_ref)

    npair = RW_HEADS // 2
    w = npair * LANES
    n = 2 * chunk
    lane = _iota((n, w), 1)
    row = _iota((n, w), 0)
    own = ((lane // RW_DIM) % 2) == (row // chunk)
    rr = _iota((n, n), 0)
    cc = _iota((n, n), 1)
    same = (rr // chunk) == (cc // chunk)
    tri_incl = jnp.where(same & (rr >= cc), 1.0, 0.0).astype(BF16)
    strict = same & (rr > cc)
    incl = same & (rr >= cc)
    eye = jnp.where(rr == cc, 1.0, 0.0)
    blockdiag = (_iota((LANES, LANES), 0) // RW_DIM) == (_iota((LANES, LANES), 1) // RW_DIM)
    avg = _group_matrix(LANES, RW_DIM, 1.0 / RW_DIM)
    ones = _group_matrix(LANES, RW_DIM, 1.0)
    pairs = range(npair)
    sl = [slice(p * LANES, (p + 1) * LANES) for p in pairs]

    def stack(x):
        return jnp.concatenate([x, x], axis=0)

    def per_pair_rows(x):
        return jnp.concatenate([x[:, s] for s in sl], axis=0)

    def per_pair_lanes(x):
        return jnp.concatenate([x[p * chunk:(p + 1) * chunk] for p in pairs], axis=1)

    r, lw, k, v, kk, b = (ref[...] for ref in (r_ref, lw_ref, k_ref, v_ref, kk_ref, b_ref))
    lw2 = stack(lw)
    l1 = lw2.astype(BF16)
    l2 = (lw2 - l1.astype(F32)).astype(BF16)
    lin = _dot(tri_incl, l1) + _dot(tri_incl, l2)
    lend = lin[n - 1:n]
    p_inv = jnp.exp(-lin)
    p_dec = jnp.exp(lend - lin)
    zero = jnp.zeros((n, w), F32)
    kk_t = jnp.where(own, stack(kk) * jnp.exp(lin - lw2), zero).astype(BF16)
    r_t = jnp.where(own, stack(r) * jnp.exp(lin), zero).astype(BF16)
    v_f = jnp.where(own, stack(v), zero)
    v_s = v_f.astype(BF16)
    b_t = (stack(b) * p_inv).astype(BF16)
    k_t = (stack(k) * p_inv).astype(BF16)
    b_d = (stack(b) * p_dec).astype(BF16)
    k_d = (stack(k) * p_dec).astype(BF16)
    p_end = jnp.exp(lend)

    lhs = [jnp.concatenate([kk_t[:, s], r_t[:, s]], axis=0) for s in sl]
    a_b = [_dot_nt(lhs[p], b_t[:, sl[p]]) for p in pairs]
    a_k = [_dot_nt(lhs[p], k_t[:, sl[p]]) for p in pairs]
    s_t = [_dot_nt(lhs[p], st_ref[p].astype(BF16)) for p in pairs]
    n_ab = [jnp.where(strict, a_b[p][:n], 0.0) for p in pairs]
    rhs = [s_t[p][:n] + _dot(jnp.where(strict, a_k[p][:n], 0.0).astype(BF16), v_s[:, sl[p]]) for p in pairs]
    inv = [eye - n_ab[p] for p in pairs]
    pw = n_ab
    for _ in range(int(math.log2(chunk)) - 1):
        pw_b = [pw[p].astype(BF16) for p in pairs]
        pw = [_dot(pw_b[p], pw_b[p]) for p in pairs]
        inv = [inv[p] + _dot(inv[p].astype(BF16), pw[p].astype(BF16)) for p in pairs]
    u_s = [-_dot(inv[p].astype(BF16), rhs[p].astype(BF16)) for p in pairs]
    y2 = [s_t[p][n:] + _dot(jnp.where(incl, a_b[p][n:], 0.0).astype(BF16), u_s[p].astype(BF16))
          + _dot(jnp.where(incl, a_k[p][n:], 0.0).astype(BF16), v_s[:, sl[p]]) for p in pairs]
    for p in pairs:
        upd = _dot(u_s[p].T.astype(BF16), b_d[:, sl[p]]) + _dot(v_f[:, sl[p]].T.astype(BF16), k_d[:, sl[p]])
        st_ref[p] = st_ref[p] * p_end[:, sl[p]] + jnp.where(blockdiag, upd, 0.0)

    yr = jnp.concatenate([y2[p][:chunk] + y2[p][chunk:] for p in pairs], axis=0)
    yr1, yr2, _ = _split3(yr)
    d = yr - (_dot(yr1, avg) + _dot(yr2, avg))
    dd1, dd2, _ = _split3(d * d)
    var = _dot(dd1, avg) + _dot(dd2, avg)
    yn = per_pair_lanes(d * lax.rsqrt(var + RW_GN_EPS)) * lng_ref[...] + lnb_ref[...]
    bonus = per_pair_lanes(_dot(per_pair_rows(r * k * rk_ref[...]).astype(BF16), ones)) * v
    o_ref[...] = ((yn + bonus) * g_ref[...].astype(F32)).astype(o_ref.dtype)


def rw_scan(r, lw, k, v, kk, b, g, r_k, ln_g, ln_b, batch, seq):
    t = batch * seq
    chunk = min(RW_CHUNK, seq)
    nc = seq // chunk
    w = BRANCH_W
    blk = pl.BlockSpec((chunk, w), lambda bb, c: (bb * nc + c, 0))
    vec = pl.BlockSpec((1, w), lambda bb, c: (0, 0))
    return pl.pallas_call(
        functools.partial(_rw_scan_kernel, chunk=chunk),
        grid=(batch, nc),
        in_specs=[blk] * 7 + [vec] * 3,
        out_specs=blk,
        out_shape=jax.ShapeDtypeStruct((t, w), BF16),
        scratch_shapes=[pltpu.VMEM((RW_HEADS // 2, LANES, LANES), F32)],
        compiler_params=_cparams(("parallel", "arbitrary")),
        name="rw_scan",
    )(r, lw, k, v, kk, b, g, r_k, ln_g, ln_b)


def tail_columns(w):
    n = w.shape[-1]
    pad = jnp.zeros(w.shape[:-1] + (TAIL_W - (n - SRC_SWQ),), w.dtype)
    return jnp.concatenate([w[..., SRC_GATE:], w[..., SRC_SWQ:SRC_GATE], pad], axis=-1)


def _rope(u, cos, sin, half):
    return u * cos + _swap_halves(u, half) * sin


DSA_Q_BLK = 256
DSA_IQ_BLK = 512


def _dsa_pre_kernel(*refs):
    nq, niq = BRANCH_W // DSA_Q_BLK, BRANCH_W // DSA_IQ_BLK
    q_refs, refs = refs[:nq], refs[nq:]
    c_ref, refs = refs[0], refs[1:]
    iq_refs, refs = refs[:niq], refs[niq:]
    (ikw_ref, ca_ref, sa_ref, cb_ref, sb_ref, qg_ref, kg_ref, cg_ref, wuk_ref, wuv_ref,
     qo_ref, ko_ref, vo_ref, iqo_ref, iko_ref, iwo_ref) = refs
    ca, sa, cb, sb = ca_ref[...], sa_ref[...], cb_ref[...], sb_ref[...]
    for h in range(DSA_HEADS):
        sl = slice(h * DSA_DIM, (h + 1) * DSA_DIM)
        per = DSA_Q_BLK // DSA_DIM
        q = q_refs[h // per][:, (h % per) * DSA_DIM:(h % per + 1) * DSA_DIM].astype(F32)
        q = q * lax.rsqrt(jnp.mean(q * q, axis=-1, keepdims=True) + NORM_EPS) * qg_ref[...]
        qo_ref[:, sl] = (_rope(q, ca, sa, 64) * DSA_DIM ** -0.5).astype(qo_ref.dtype)
    c = c_ref[...].astype(F32)
    c = (c * lax.rsqrt(jnp.mean(c * c, axis=-1, keepdims=True) + NORM_EPS) * cg_ref[...]).astype(BF16)
    k = _dot(c, wuk_ref[...].astype(BF16))
    k = k * lax.rsqrt(jnp.mean(k * k, axis=-1, keepdims=True) + NORM_EPS) * kg_ref[...]
    ko_ref[...] = _rope(k, ca, sa, 64).astype(ko_ref.dtype)
    vo_ref[...] = _dot(c, wuv_ref[...].astype(BF16)).astype(vo_ref.dtype)
    for s in range(IDX_HEADS * IDX_DIM // LANES):
        sl = slice(s * LANES, (s + 1) * LANES)
        per = DSA_IQ_BLK // LANES
        iq = iq_refs[s // per][:, (s % per) * LANES:(s % per + 1) * LANES].astype(F32)
        iqo_ref[:, sl] = _rope(iq, cb, sb, 32).astype(iqo_ref.dtype)
    ikw = ikw_ref[...].astype(F32)
    lane = _iota(ikw.shape, 1)
    ik = _rope(ikw, cb, sb, 32)
    iko_ref[...] = jnp.where(lane < IDX_DIM, ik, pltpu.roll(ik, IDX_DIM, 1)).astype(iko_ref.dtype)
    iw = pltpu.roll(ikw, IDX_DIM, 1) * (IDX_HEADS ** -0.5 * IDX_DIM ** -0.5)
    iwo_ref[...] = jnp.where(lane < IDX_HEADS, iw, 0.0)


def _dsa_kernel(q_ref, iq_ref, iw_ref, k_ref, v_ref, ik_ref, o_ref, key_ref, bias_ref, vt_ref, m_ref, l_ref, acc_ref,
                *, tq, seq, top_k):
    kb = tq
    qb = pl.program_id(1)
    nblk = qb + 1
    nh = DSA_HEADS
    int_min = jnp.int32(-2 ** 31)
    kidx = _iota((kb, tq), 0)
    ridx = _iota((kb, tq), 1)

    def causal_mask(j):
        return (kidx + j * kb) <= (ridx + qb * tq)

    def block(ref, j):
        return ref[pl.ds(pl.multiple_of(j * kb, kb), kb), :]

    def fold(x, op):
        return op(x.reshape(kb // 8, 8, x.shape[1]), axis=0)

    @pl.when(qb == 0)
    def _():
        for j in range(seq // LANES):
            vt_ref[:, j * LANES:(j + 1) * LANES] = v_ref[j * LANES:(j + 1) * LANES, :].astype(F32).T.astype(BF16)

    first = _iota((tq, LANES), 1) < IDX_DIM
    lhs = []
    for p in range(IDX_HEADS // 2):
        qp = iq_ref[:, p * LANES:(p + 1) * LANES]
        zero = jnp.zeros_like(qp)
        lhs.append(jnp.concatenate([jnp.where(first, qp, zero), jnp.where(first, zero, qp)], axis=0))
    iw_t = iw_ref[...].T
    w_row = [iw_t[h:h + 1] for h in range(IDX_HEADS)]

    def score_block(j, _):
        ik = block(ik_ref, j)
        sc = jnp.zeros((kb, tq), F32)
        for p in range(IDX_HEADS // 2):
            z = jnp.maximum(_dot_nt(ik, lhs[p]), 0.0)
            sc = sc + z[:, :tq] * w_row[2 * p] + z[:, tq:] * w_row[2 * p + 1]
        sc = sc + 0.0
        bits = lax.bitcast_convert_type(sc, I32)
        skey = bits ^ ((bits >> 31) & jnp.int32(0x7FFFFFFF))
        key_ref[pl.ds(pl.multiple_of(j * kb, kb), kb), :] = jnp.where(causal_mask(j), skey, int_min)
        return 0

    lax.fori_loop(0, nblk, score_block, 0)

    def count(pred_fn):
        def body(j, cnt):
            return cnt + fold(jnp.where(pred_fn(block(key_ref, j)), 1.0, 0.0), jnp.sum)
        return jnp.sum(lax.fori_loop(0, nblk, body, jnp.zeros((8, tq), F32)), axis=0, keepdims=True)

    def bit_step(i, thr):
        cand = thr ^ (jnp.int32(1) << (31 - i))
        return jnp.where(count(lambda keys: keys >= cand) >= top_k, cand, thr)

    thr = lax.fori_loop(0, 32, bit_step, jnp.full((1, tq), int_min, I32))
    need = top_k - count(lambda keys: keys > thr)

    q_all = jnp.concatenate([q_ref[:, h * DSA_DIM:(h + 1) * DSA_DIM] for h in range(nh)], axis=0)
    lower = jnp.where(_iota((kb, kb), 0) > _iota((kb, kb), 1), 1.0, 0.0).astype(BF16)
    m_ref[...] = jnp.full_like(m_ref, -1e30)

    def logits(j, bias):
        lt = _dot_nt(block(k_ref, j), q_all)
        return jnp.concatenate([lt[:, h * tq:(h + 1) * tq] + bias for h in range(nh)], axis=1)

    def select(j, ties_seen):
        keys = block(key_ref, j)
        causal = causal_mask(j)
        tie = causal & (keys == thr)
        tie_f = jnp.where(tie, 1.0, 0.0)
        rank = ties_seen + _dot(lower, tie_f.astype(BF16))
        sel = causal & ((keys > thr) | (tie & (rank < need)))
        bias = jnp.where(sel, 0.0, -1e30)
        bias_ref[pl.ds(pl.multiple_of(j * kb, kb), kb), :] = bias
        m_ref[...] = jnp.maximum(m_ref[...], fold(logits(j, bias), jnp.max))
        return ties_seen + jnp.sum(tie_f, axis=0, keepdims=True)

    lax.fori_loop(0, nblk, select, jnp.zeros((1, tq), F32))
    m_all = jnp.broadcast_to(jnp.max(m_ref[...], axis=0, keepdims=True), m_ref.shape)
    l_ref[...] = jnp.zeros_like(l_ref)
    acc_ref[...] = jnp.zeros_like(acc_ref)

    def attend(j, _):
        lt = logits(j, block(bias_ref, j))
        pr = jnp.exp(lt.reshape(kb // 8, 8, nh * tq) - m_all[None]).reshape(kb, nh * tq)
        l_ref[...] += fold(pr, jnp.sum)
        acc_ref[...] += _dot(vt_ref[:, pl.ds(pl.multiple_of(j * kb, kb), kb)], pr.astype(BF16))
        return 0

    lax.fori_loop(0, nblk, attend, 0)
    out_t = acc_ref[...] / jnp.sum(l_ref[...], axis=0, keepdims=True)
    for h in range(nh):
        o_ref[:, h * DSA_DIM:(h + 1) * DSA_DIM] = out_t[:, h * tq:(h + 1) * tq].T.astype(o_ref.dtype)


def dsa_attention(u, tables, q_gain, k_gain, kv_gain, w_uk, w_uv, batch, seq, tm=256, tq=256):
    t = batch * seq
    tm = min(tm, seq)
    ca, sa, cb, sb = tables
    w = BRANCH_W
    row = lambda width, col: pl.BlockSpec((tm, width), lambda i: (i, col // width))
    tab = pl.BlockSpec((tm, LANES), lambda i: (i, 0))
    vec = lambda width: pl.BlockSpec((1, width), lambda i: (0, 0))
    mat = pl.BlockSpec((DSA_KV_RANK, DSA_DIM), lambda i: (0, 0))
    o_w = pl.BlockSpec((tm, w), lambda i: (i, 0))
    o_n = pl.BlockSpec((tm, LANES), lambda i: (i, 0))
    q, k, v, iq, ik, iw = pl.pallas_call(
        _dsa_pre_kernel,
        grid=(t // tm,),
        in_specs=[row(DSA_Q_BLK, COL_DSAQ + j * DSA_Q_BLK) for j in range(w // DSA_Q_BLK)]
        + [row(DSA_KV_RANK, COL_CKV)]
        + [row(DSA_IQ_BLK, COL_IDXQ + j * DSA_IQ_BLK) for j in range(w // DSA_IQ_BLK)]
        + [row(LANES, COL_IDXKW), tab, tab, tab, tab, vec(DSA_DIM), vec(DSA_DIM), vec(DSA_KV_RANK), mat, mat],
        out_specs=[o_w, o_n, o_n, o_w, o_n, o_n],
        out_shape=[jax.ShapeDtypeStruct((t, w), BF16), jax.ShapeDtypeStruct((t, LANES), BF16),
                   jax.ShapeDtypeStruct((t, LANES), BF16), jax.ShapeDtypeStruct((t, w), BF16),
                   jax.ShapeDtypeStruct((t, LANES), BF16), jax.ShapeDtypeStruct((t, LANES), F32)],
        compiler_params=_cparams(("parallel",)),
        name="dsa_prepare",
    )(*([u] * (w // DSA_Q_BLK + 1 + w // DSA_IQ_BLK + 1)), ca, sa, cb, sb,
      q_gain.reshape(1, -1), k_gain.reshape(1, -1), kv_gain.reshape(1, -1), w_uk, w_uv)

    nq = seq // tq
    top_k = min(IDX_TOPK_MAX, seq // 4)
    qrow = lambda width: pl.BlockSpec((tq, width), lambda b, i: (b * nq + i, 0))
    full = pl.BlockSpec((seq, LANES), lambda b, i: (b, 0))
    return pl.pallas_call(
        functools.partial(_dsa_kernel, tq=tq, seq=seq, top_k=top_k),
        grid=(batch, nq),
        in_specs=[qrow(w), qrow(w), qrow(LANES), full, full, full],
        out_specs=qrow(w),
        out_shape=jax.ShapeDtypeStruct((t, w), BF16),
        scratch_shapes=[pltpu.VMEM((seq, tq), I32), pltpu.VMEM((seq, tq), F32), pltpu.VMEM((DSA_DIM, seq), BF16),
                        pltpu.VMEM((8, DSA_HEADS * tq), F32), pltpu.VMEM((8, DSA_HEADS * tq), F32),
                        pltpu.VMEM((DSA_DIM, DSA_HEADS * tq), F32)],
        compiler_params=_cparams(("parallel", "arbitrary")),
        name="dsa_attention",
    )(q, iq, iw, k, v, ik)


def _sw_kernel(sink_ref, q_ref, kp_ref, kc_ref, vp_ref, vc_ref, cbp_ref, sbp_ref, cbc_ref, sbc_ref,
               qg_ref, kg_ref, o_ref, *, blk):
    n = pl.program_id(1)
    avg = _group_matrix(LANES, SW_DIM, 1.0 / SW_DIM)
    lane2 = _iota((2 * blk, LANES), 1)
    lane1 = _iota((blk, LANES), 1)

    def norm(x, gain):
        return x * lax.rsqrt(_dot_exact_rhs(x * x, avg) + NORM_EPS) * gain

    cb = jnp.concatenate([cbp_ref[...], cbc_ref[...]], axis=0)
    sb = jnp.concatenate([sbp_ref[...], sbc_ref[...]], axis=0)
    k = jnp.concatenate([kp_ref[...], kc_ref[...]], axis=0).astype(F32)
    k = _rope(norm(k, kg_ref[...]), cb, sb, 32)
    v = jnp.concatenate([vp_ref[...], vc_ref[...]], axis=0).astype(F32)
    k_sw, v_sw = pltpu.roll(k, SW_DIM, 1), pltpu.roll(v, SW_DIM, 1)
    k2 = [jnp.where(lane2 < SW_DIM, k, k_sw).astype(BF16), jnp.where(lane2 < SW_DIM, k_sw, k).astype(BF16)]
    v2 = [jnp.where(lane2 < SW_DIM, v, v_sw).astype(BF16), jnp.where(lane2 < SW_DIM, v_sw, v).astype(BF16)]

    r = _iota((2 * blk, 2 * blk), 0) % blk
    c = _iota((2 * blk, 2 * blk), 1)
    dist = r - (c - blk)
    mask = (dist >= 0) & (dist < WINDOW) & ((c >= blk) | (n > 0))
    top = _iota((2 * blk, 1), 0) < blk
    cbc, sbc = cbc_ref[...], sbc_ref[...]
    for p in range(SW_HEADS // 2):
        g = (2 * p) // (SW_HEADS // SW_KV_HEADS)
        q = q_ref[:, p * LANES:(p + 1) * LANES].astype(F32)
        q = _rope(norm(q, qg_ref[...]), cbc, sbc, 32) * SW_DIM ** -0.5
        zero = jnp.zeros_like(q)
        lhs = jnp.concatenate([jnp.where(lane1 < SW_DIM, q, zero), jnp.where(lane1 < SW_DIM, zero, q)],
                              axis=0).astype(BF16)
        logit = jnp.where(mask, _dot_nt(lhs, k2[g]), -1e30)
        sink = jnp.where(top, sink_ref[2 * p], sink_ref[2 * p + 1])
        m = jnp.maximum(jnp.max(logit, axis=1, keepdims=True), sink)
        pr = jnp.where(mask, jnp.exp(logit - m), 0.0)
        den = jnp.sum(pr, axis=1, keepdims=True) + jnp.exp(sink - m)
        o2 = _dot((pr / den).astype(BF16), v2[g])
        o_ref[:, p * LANES:(p + 1) * LANES] = jnp.where(lane1 < SW_DIM, o2[:blk], o2[blk:]).astype(o_ref.dtype)


def sw_attention(u, tables, q_gain, k_gain, sinks, batch, seq, blk=128):
    t = batch * seq
    nb = seq // blk
    _, _, cb, sb = tables
    w = BRANCH_W
    cur = lambda width, col: pl.BlockSpec((blk, width), lambda b, i: (b * nb + i, col // width))
    prev = lambda width, col: pl.BlockSpec((blk, width), lambda b, i: (b * nb + jnp.maximum(i - 1, 0), col // width))
    vec = pl.BlockSpec((1, LANES), lambda b, i: (0, 0))
    tile2 = lambda g: jnp.tile(g.reshape(1, SW_DIM), (1, 2))
    return pl.pallas_call(
        functools.partial(_sw_kernel, blk=blk),
        grid=(batch, nb),
        in_specs=[pl.BlockSpec(memory_space=pltpu.SMEM),
                  cur(w, COL_SWQ), prev(LANES, COL_SWK), cur(LANES, COL_SWK), prev(LANES, COL_SWV), cur(LANES, COL_SWV),
                  prev(LANES, 0), prev(LANES, 0), cur(LANES, 0), cur(LANES, 0), vec, vec],
        out_specs=pl.BlockSpec((blk, w), lambda b, i: (b * nb + i, 0)),
        out_shape=jax.ShapeDtypeStruct((t, w), BF16),
        compiler_params=_cparams(("parallel", "parallel")),
        name="sw_attention",
    )(sinks, u, u, u, u, u, cb, sb, cb, sb, tile2(q_gain), tile2(k_gain))


def _merge_kernel(*refs):
    y_refs = refs[:N_BRANCH]
    g_ref, w_ref, o_ref, acc_ref = refs[N_BRANCH:]
    n = pl.program_id(1)
    gate = _sigmoid(g_ref[...].astype(F32))
    w = w_ref[...].astype(BF16)
    for b in range(N_BRANCH):
        @pl.when(n == b)
        def _():
            contrib = gate * _dot(y_refs[b][...], w)
            if b == 0:
                acc_ref[...] = contrib
            elif b < N_BRANCH - 1:
                acc_ref[...] += contrib
            else:
                o_ref[...] = (acc_ref[...] + contrib).astype(o_ref.dtype)


def merge_branches(ys, u, w_branch, layer, tm=512):
    t, w = ys[0].shape
    d = w_branch.shape[-1]
    tm = min(tm, t)
    y_spec = pl.BlockSpec((tm, w), lambda i, n: (i, 0))
    return pl.pallas_call(
        _merge_kernel,
        grid=(t // tm, N_BRANCH),
        in_specs=[y_spec] * N_BRANCH + [pl.BlockSpec((tm, d), lambda i, n: (i, COL_GATE // d + n)),
                                        _layer_spec((None, w, d), lambda i, n: (n, 0, 0), layer)],
        out_specs=pl.BlockSpec((tm, d), lambda i, n: (i, 0)),
        out_shape=jax.ShapeDtypeStruct((t, d), BF16),
        scratch_shapes=[pltpu.VMEM((tm, d), F32)],
        compiler_params=_cparams(("parallel", "arbitrary")),
        name="merge_branches",
    )(*ys, u, w_branch)


FFN_TM = 1024
FFN_ROWS = (1024, 512, 256)
FFN_TF = 256


def _ffn_kernel(e_ref, rows_ref, x_ref, wg_ref, wu_ref, wd_ref, o_ref):
    rows = rows_ref[pl.program_id(0)]

    @pl.when(pl.program_id(1) == 0)
    def _():
        o_ref[...] = jnp.zeros_like(o_ref)

    for k, size in enumerate(FFN_ROWS):
        below = FFN_ROWS[k + 1] if k + 1 < len(FFN_ROWS) else 0

        @pl.when((rows > below) & (rows <= size))
        def _():
            x = x_ref[:size, :]
            g = _dot(x, wg_ref[0].astype(BF16))
            a = (g * _sigmoid(g) * _dot(x, wu_ref[0].astype(BF16))).astype(BF16)
            o_ref[:size, :] += _dot(a, wd_ref[0].astype(BF16))


def ffn_tiles(xs, tile_expert, tile_rows, w_gate, w_up, w_down):
    r, d = xs.shape
    _, _, ff = w_gate.shape
    ns = r // FFN_TM
    nf = ff // FFN_TF

    def f_eff(s, f, rows):
        return jnp.where(rows[s] > 0, f, nf - 1)

    grid_spec = pltpu.PrefetchScalarGridSpec(
        num_scalar_prefetch=2,
        grid=(ns, nf),
        in_specs=[pl.BlockSpec((FFN_TM, d), lambda s, f, e, rows: (s, 0)),
                  pl.BlockSpec((1, d, FFN_TF), lambda s, f, e, rows: (e[s], 0, f_eff(s, f, rows))),
                  pl.BlockSpec((1, d, FFN_TF), lambda s, f, e, rows: (e[s], 0, f_eff(s, f, rows))),
                  pl.BlockSpec((1, FFN_TF, d), lambda s, f, e, rows: (e[s], f_eff(s, f, rows), 0))],
        out_specs=pl.BlockSpec((FFN_TM, d), lambda s, f, e, rows: (s, 0)),
    )
    return pl.pallas_call(
        _ffn_kernel,
        grid_spec=grid_spec,
        out_shape=jax.ShapeDtypeStruct((r, d), F32),
        compiler_params=_cparams(("parallel", "arbitrary")),
        name="ffn_tiles",
    )(tile_expert, tile_rows, xs, w_gate, w_up, w_down)


def _add_kernel(a_ref, b_ref, o_ref):
    o_ref[...] = a_ref[...] + b_ref[...]


def add(a, b, tm=512):
    t, d = a.shape
    spec = pl.BlockSpec((tm, d), lambda i: (i, 0))
    return pl.pallas_call(_add_kernel, grid=(t // tm,), in_specs=[spec, spec], out_specs=spec,
                          out_shape=jax.ShapeDtypeStruct((t, d), a.dtype),
                          compiler_params=_cparams(("parallel",)), name="residual_add")(a, b)


def dense_ffn(x, h, w_gate, w_up, w_down, index):
    t = h.shape[0]
    ns = t // FFN_TM
    ys = ffn_tiles(h, jnp.full((ns,), index, I32), jnp.full((ns,), FFN_TM, I32), w_gate, w_up, w_down)
    return add(x, ys)


def _router_kernel(x_ref, g_ref, w_ref, h_ref, r_ref):
    x = x_ref[...]
    h = x * lax.rsqrt(jnp.mean(x * x, axis=-1, keepdims=True) + NORM_EPS) * g_ref[...]
    h_ref[...] = h
    logit = _dot_f32(h, w_ref[...])
    lane = _iota(logit.shape, 1).astype(F32)
    neg = -jnp.inf
    l1 = jnp.where(lane < N_EXPERTS, logit, neg)
    m1 = jnp.max(l1, axis=1, keepdims=True)
    i1 = jnp.min(jnp.where(l1 == m1, lane, float(LANES)), axis=1, keepdims=True)
    l2 = jnp.where(lane == i1, neg, l1)
    m2 = jnp.max(l2, axis=1, keepdims=True)
    i2 = jnp.min(jnp.where(l2 == m2, lane, float(LANES)), axis=1, keepdims=True)
    e = jnp.exp(m2 - m1)
    w1 = 1.0 / (1.0 + e)
    w2 = e / (1.0 + e)
    r_ref[...] = jnp.where(lane == 0, i1, jnp.where(lane == 1, i2, jnp.where(lane == 2, w1,
                           jnp.where(lane == 3, w2, 0.0))))


def route(x, gain, router, tm=256):
    t, d = x.shape
    wp = jnp.zeros((d, LANES), F32).at[:, :N_EXPERTS].set(router)
    return pl.pallas_call(
        _router_kernel,
        grid=(t // tm,),
        in_specs=[pl.BlockSpec((tm, d), lambda i: (i, 0)), pl.BlockSpec((1, d), lambda i: (0, 0)),
                  pl.BlockSpec((d, LANES), lambda i: (0, 0))],
        out_specs=[pl.BlockSpec((tm, d), lambda i: (i, 0)), pl.BlockSpec((tm, LANES), lambda i: (i, 0))],
        out_shape=[jax.ShapeDtypeStruct((t, d), F32), jax.ShapeDtypeStruct((t, LANES), F32)],
        compiler_params=_cparams(("parallel",)),
        name="router",
    )(x, gain.reshape(1, d), wp)


def _row_copy(src_ref, row, buf, i, sem):
    return pltpu.make_async_copy(src_ref.at[pl.ds(row, 1)], buf.at[pl.ds(i, 1)], sem)


ROW_DMA_UNROLL = 8


def _row_loop(n, fn):
    def body(blk, _):
        for u in range(ROW_DMA_UNROLL):
            fn(blk * ROW_DMA_UNROLL + u, u)
        return 0
    lax.fori_loop(0, n // ROW_DMA_UNROLL, body, 0)


GATHER_ROWS = 256


def _gather_kernel(idx_ref, live_ref, src_ref, o_ref, buf, sem):
    gb = GATHER_ROWS
    i = pl.program_id(0)
    n = pl.num_programs(0)

    def start_step(step):
        slot = step % 2

        @pl.when(live_ref[step] > 0)
        def _():
            _row_loop(gb, lambda r, u: _row_copy(src_ref, idx_ref[step * gb + r], buf.at[slot], r,
                                                 sem.at[slot]).start(priority=u % 2))

    @pl.when(i == 0)
    def _():
        buf[...] = jnp.zeros_like(buf)
        start_step(i)

    @pl.when(i + 1 < n)
    def _():
        start_step(i + 1)

    slot = i % 2

    @pl.when(live_ref[i] > 0)
    def _():
        _row_loop(gb, lambda r, u: _row_copy(src_ref, 0, buf.at[slot], r, sem.at[slot]).wait())

    o_ref[...] = buf[slot].astype(o_ref.dtype)


def gather_rows(src, idx, live, out_dtype):
    r = idx.shape[0]
    d = src.shape[1]
    gb = GATHER_ROWS
    grid_spec = pltpu.PrefetchScalarGridSpec(
        num_scalar_prefetch=2,
        grid=(r // gb,),
        in_specs=[pl.BlockSpec(memory_space=pl.ANY)],
        out_specs=pl.BlockSpec((gb, d), lambda i, idx, live: (i, 0)),
        scratch_shapes=[pltpu.VMEM((2, gb, d), src.dtype), pltpu.SemaphoreType.DMA((2,))],
    )
    return pl.pallas_call(
        _gather_kernel,
        grid_spec=grid_spec,
        out_shape=jax.ShapeDtypeStruct((r, d), out_dtype),
        compiler_params=_cparams(("arbitrary",)),
        name="gather_rows",
    )(idx, live, src)


def _combine_kernel(d0_ref, d1_ref, x_ref, w_ref, ys_ref, o_ref, buf, sem, *, tm):
    i = pl.program_id(0)
    n = pl.num_programs(0)

    def start_step(step):
        slot = step % 2

        def one(r, u):
            _row_copy(ys_ref, d0_ref[step * tm + r], buf.at[slot, 0], r, sem.at[slot]).start(priority=0)
            _row_copy(ys_ref, d1_ref[step * tm + r], buf.at[slot, 1], r, sem.at[slot]).start(priority=1)
        _row_loop(tm, one)

    @pl.when(i == 0)
    def _():
        start_step(i)

    @pl.when(i + 1 < n)
    def _():
        start_step(i + 1)

    slot = i % 2

    def wait_one(r, u):
        _row_copy(ys_ref, 0, buf.at[slot, 0], r, sem.at[slot]).wait()
        _row_copy(ys_ref, 0, buf.at[slot, 1], r, sem.at[slot]).wait()
    _row_loop(tm, wait_one)
    w = w_ref[...]
    o_ref[...] = x_ref[...] + w[:, 2:3] * buf[slot, 0] + w[:, 3:4] * buf[slot, 1]


def combine_rows(x, ys, d0, d1, w, tm=128):
    t, d = x.shape
    grid_spec = pltpu.PrefetchScalarGridSpec(
        num_scalar_prefetch=2,
        grid=(t // tm,),
        in_specs=[pl.BlockSpec((tm, d), lambda i, a, b: (i, 0)), pl.BlockSpec((tm, LANES), lambda i, a, b: (i, 0)),
                  pl.BlockSpec(memory_space=pl.ANY)],
        out_specs=pl.BlockSpec((tm, d), lambda i, a, b: (i, 0)),
        scratch_shapes=[pltpu.VMEM((2, 2, tm, d), F32), pltpu.SemaphoreType.DMA((2,))],
    )
    return pl.pallas_call(
        functools.partial(_combine_kernel, tm=tm),
        grid_spec=grid_spec,
        out_shape=jax.ShapeDtypeStruct((t, d), F32),
        compiler_params=_cparams(("arbitrary",)),
        name="combine_rows",
    )(d0, d1, x, w, ys)


def moe_ffn(x, gain, router, w_gate, w_up, w_down, first_expert):
    t, d = x.shape
    h, rt = route(x, gain, router)
    e_flat = jnp.concatenate([rt[:, 0], rt[:, 1]]).astype(I32)
    tok = jnp.concatenate([jnp.arange(t, dtype=I32)] * 2)
    onehot = (e_flat[:, None] == jnp.arange(N_EXPERTS, dtype=I32)[None, :]).astype(I32)
    csum = jnp.cumsum(onehot, axis=0)
    rank = jnp.take_along_axis(csum - onehot, e_flat[:, None], axis=1)[:, 0]
    counts = csum[-1]
    n_tiles = (counts + FFN_TM - 1) // FFN_TM
    tile_end = jnp.cumsum(n_tiles)
    tile_start = tile_end - n_tiles
    dest = tile_start[e_flat] * FFN_TM + rank
    ns = TOP_K * t // FFN_TM + N_EXPERTS
    s_idx = jnp.arange(ns, dtype=I32)
    used = s_idx < tile_end[-1]
    s_clip = jnp.minimum(s_idx, tile_end[-1] - 1)
    tile_expert = jnp.minimum(jnp.sum((s_clip[:, None] >= tile_end[None, :]).astype(I32), axis=1), N_EXPERTS - 1)
    tile_rows = jnp.clip(counts[tile_expert] - (s_clip - tile_start[tile_expert]) * FFN_TM, 0, FFN_TM)
    tile_rows = jnp.where(used, tile_rows, 0).astype(I32)
    src_row = jnp.zeros((ns * FFN_TM,), I32).at[dest].set(tok)
    per = FFN_TM // GATHER_ROWS
    live = (jnp.arange(ns * per, dtype=I32) % per * GATHER_ROWS < jnp.repeat(tile_rows, per)).astype(I32)

    xs = gather_rows(h, src_row, live, BF16)
    ys = ffn_tiles(xs, tile_expert + first_expert, tile_rows, w_gate, w_up, w_down)
    return combine_rows(x, ys, dest[:t], dest[t:], rt)


def _ple_kernel(h_ref, wg_ref, p_ref, wp_ref, x_ref, o_ref):
    gate = _sigmoid(_dot(h_ref[...], wg_ref[...].astype(BF16)))
    proj = _dot(p_ref[...].astype(BF16), wp_ref[...].astype(BF16))
    o_ref[...] = x_ref[...] + gate * proj


def ple(h, w_gate, p, w_proj, x, layer, tm=1024, tn=512):
    t, d = x.shape
    pd = p.shape[-1]
    tm = min(tm, t)
    return pl.pallas_call(
        _ple_kernel,
        grid=(t // tm, d // tn),
        in_specs=[pl.BlockSpec((tm, d), lambda i, j: (i, 0)), _layer_spec((d, tn), lambda i, j: (0, j), layer),
                  _layer_spec((tm, pd), lambda i, j: (i, 0), layer), _layer_spec((pd, tn), lambda i, j: (0, j), layer),
                  pl.BlockSpec((tm, tn), lambda i, j: (i, j))],
        out_specs=pl.BlockSpec((tm, tn), lambda i, j: (i, j)),
        out_shape=jax.ShapeDtypeStruct((t, d), F32),
        compiler_params=_cparams(("parallel", "parallel")),
        name="ple",
    )(h, w_gate, p, w_proj, x)


def _pad_rows(w, start, total):
    return jnp.zeros((total, w.shape[1]), F32).at[start:start + w.shape[0]].set(w).astype(BF16)


def kernel(x, p, positions, w_in, mix_norm, ffn_norm, ple_norm, rw_mu, rw_w0, rw_w2, rw_a0, rw_a2, rw_g2, rw_kk,
           rw_ka, rw_rk, rw_ln_g, rw_ln_b, dsa_q_norm, dsa_k_norm, dsa_kv_norm, dsa_w_uk, dsa_w_uv, sw_q_norm,
           sw_k_norm, sw_sinks, w_branch, w_out, ffn_w_gate, ffn_w_up, ffn_w_down, moe_router, moe_w_gate, moe_w_up,
           moe_w_down, ple_w_gate, ple_w_proj):
    b, s, d = x.shape
    t = b * s
    depth = w_in.shape[0]
    xf = x.reshape(t, d)
    tables = rope_tables(positions)
    row = lambda a: a.reshape(1, -1)
    w_tail = tail_columns(w_in)
    p_rows = p.reshape(depth, t, -1)
    experts = lambda w: w.reshape((-1,) + w.shape[2:])
    for i in range(depth):
        h = rmsnorm(xf, mix_norm[i])
        uh = matmul(h, w_in, i, n_cols=HEAD_W, out_dtype=BF16, tm=2048, name="in_proj_head")
        ut = matmul(h, w_tail, i, out_dtype=BF16, tm=2048, name="in_proj_tail")
        y_a = sb_attention(uh, b, s)
        rw = rw_prepare(uh, row(rw_mu[i]), row(rw_w0[i]), row(rw_a0[i]), row(rw_kk[i]), row(rw_ka[i]),
                        _pad_rows(rw_w2[i], 0, RW_LORA), _pad_rows(rw_a2[i], RW_LORA_W, RW_LORA),
                        _pad_rows(rw_g2[i], RW_LORA_W + RW_LORA_A, RW_LORA), b, s)
        y_b = rw_scan(*rw, row(rw_rk[i]), row(rw_ln_g[i]), row(rw_ln_b[i]), b, s)
        y_c = dsa_attention(uh, tables, dsa_q_norm[i], dsa_k_norm[i], dsa_kv_norm[i], dsa_w_uk[i], dsa_w_uv[i], b, s)
        y_d = sw_attention(ut, tables, sw_q_norm[i], sw_k_norm[i], sw_sinks[i], b, s)
        merged = merge_branches([y_a, y_b, y_c, y_d], ut, w_branch, i)
        xf = matmul(merged, w_out, i, out_dtype=F32, residual=xf, name="out_proj")
        if i % 2 == 0:
            h = rmsnorm(xf, ffn_norm[i])
            xf = dense_ffn(xf, h, ffn_w_gate, ffn_w_up, ffn_w_down, i // 2)
        else:
            xf = moe_ffn(xf, ffn_norm[i], moe_router[i // 2], experts(moe_w_gate), experts(moe_w_up),
                         experts(moe_w_down), (i // 2) * N_EXPERTS)
        h = rmsnorm(xf, ple_norm[i])
        xf = ple(h, ple_w_gate, p_rows, ple_w_proj, xf, i)
    return xf.reshape(b, s, d)
```

```python
import functools
import math

import jax
import jax.numpy as jnp
from jax import lax
from jax.experimental import pallas as pl
from jax.experimental.pallas import tpu as pltpu

F32 = jnp.float32
BF16 = jnp.bfloat16
I32 = jnp.int32

D_MODEL = 2048
ROPE_THETA = 10000.0
NORM_EPS = 1e-6
N_BRANCH = 4
BRANCH_W = 1024
SB_HEADS, SB_DIM = 8, 128
RW_HEADS, RW_DIM = 16, 64
RW_LORA_W, RW_LORA_A, RW_LORA_G = 96, 96, 64
RW_LORA = RW_LORA_W + RW_LORA_A + RW_LORA_G
RW_GN_EPS = 64e-5
DSA_HEADS, DSA_DIM, DSA_KV_RANK = 8, 128, 256
IDX_HEADS, IDX_DIM, IDX_TOPK_MAX = 16, 64, 256
SW_HEADS, SW_KV_HEADS, SW_DIM, WINDOW = 16, 2, 64, 128
N_EXPERTS, TOP_K = 8, 2

LANES = 128
VMEM_LIMIT_BYTES = 56 * 1024 * 1024

HEAD_W = 9216
COL_SBQ = 0
COL_SBK = 1024
COL_SBV = 2048
COL_RWR = 3072
COL_RWK = 4096
COL_RWV = 5120
COL_RWL = 6144
COL_DSAQ = 6400
COL_CKV = 7424
COL_IDXQ = 7680
COL_IDXKW = 8704
SRC_SWQ = 8784
SRC_SWK = 9808
SRC_GATE = 10064
COL_GATE = 0
COL_SWQ = 8192
COL_SWK = 9216
COL_SWV = 9344
IN_TN = 512
IN_ALIGN = 16
HEAD_TILES = tuple(range(0, HEAD_W, IN_TN))
TAIL_TILES = (tuple(range(SRC_GATE, SRC_GATE + N_BRANCH * D_MODEL, IN_TN))
              + tuple(range(SRC_SWQ, SRC_SWK, IN_TN)) + (SRC_SWK,))

RW_CHUNK = 64


def _cparams(sem, vmem=VMEM_LIMIT_BYTES):
    return pltpu.CompilerParams(dimension_semantics=sem, vmem_limit_bytes=vmem)


def _dot(a, b):
    return jnp.dot(a, b, preferred_element_type=F32)


def _dot_nt(a, b):
    return lax.dot_general(a, b, (((1,), (1,)), ((), ())), preferred_element_type=F32)


def _split3(a):
    a1 = a.astype(BF16)
    r1 = a - a1.astype(F32)
    a2 = r1.astype(BF16)
    a3 = (r1 - a2.astype(F32)).astype(BF16)
    return a1, a2, a3


def _dot_exact_rhs(a, m_bf16):
    a1, a2, a3 = _split3(a)
    return _dot(a1, m_bf16) + _dot(a2, m_bf16) + _dot(a3, m_bf16)


def _dot_exact_lhs(m_bf16, a):
    a1, a2, a3 = _split3(a)
    return _dot(m_bf16, a1) + _dot(m_bf16, a2) + _dot(m_bf16, a3)


def _dot_f32(a, b):
    a1, a2, a3 = _split3(a)
    b1, b2, b3 = _split3(b)
    return (_dot(a1, b1) + _dot(a1, b2) + _dot(a2, b1)) + (_dot(a1, b3) + _dot(a2, b2) + _dot(a3, b1))


def _softplus(z):
    return jnp.maximum(z, 0.0) + jnp.log(1.0 + jnp.exp(-jnp.abs(z)))


def _sigmoid(z):
    return 1.0 / (1.0 + jnp.exp(-z))


def _iota(shape, dim):
    return lax.broadcasted_iota(I32, shape, dim)


def _group_matrix(n, group, value):
    r = _iota((n, n), 0) // group
    c = _iota((n, n), 1) // group
    return jnp.where(r == c, value, 0.0).astype(BF16)


def _swap_halves(u, half):
    if 2 * half == LANES:
        return pltpu.roll(u, half, 1)
    lane = _iota(u.shape, 1)
    return jnp.where(lane % (2 * half) < half, pltpu.roll(u, LANES - half, 1), pltpu.roll(u, half, 1))


def _rmsnorm_kernel(x_ref, g_ref, o_ref):
    x = x_ref[...]
    ms = jnp.mean(x * x, axis=-1, keepdims=True)
    o_ref[...] = (x * lax.rsqrt(ms + NORM_EPS) * g_ref[...]).astype(o_ref.dtype)


def rmsnorm(x, gain, out_dtype=BF16, tm=512):
    t, d = x.shape
    return pl.pallas_call(
        _rmsnorm_kernel,
        grid=(t // tm,),
        in_specs=[pl.BlockSpec((tm, d), lambda i: (i, 0)), pl.BlockSpec((1, d), lambda i: (0, 0))],
        out_specs=pl.BlockSpec((tm, d), lambda i: (i, 0)),
        out_shape=jax.ShapeDtypeStruct((t, d), out_dtype),
        compiler_params=_cparams(("parallel",)),
        name="rmsnorm",
    )(x, gain.reshape(1, d))


def _mm_kernel(*refs, has_res):
    if has_res:
        a_ref, w_ref, r_ref, o_ref = refs
    else:
        a_ref, w_ref, o_ref = refs
    acc = _dot(a_ref[...].astype(BF16), w_ref[...].astype(BF16))
    if has_res:
        acc = acc + r_ref[...]
    o_ref[...] = acc.astype(o_ref.dtype)


def _layer_spec(block, index_map, layer):
    return pl.BlockSpec((None,) + tuple(block), lambda *idx: (layer,) + tuple(index_map(*idx)))


def matmul(a, w, layer, *, out_dtype, residual=None, n_cols=None, tm=1024, tn=512, name="matmul"):
    m, k = a.shape
    n = w.shape[2] if n_cols is None else n_cols
    tm, tn = min(tm, m), min(tn, n)
    in_specs = [pl.BlockSpec((tm, k), lambda i, j: (i, 0)), _layer_spec((k, tn), lambda i, j: (0, j), layer)]
    args = [a, w]
    if residual is not None:
        in_specs.append(pl.BlockSpec((tm, tn), lambda i, j: (i, j)))
        args.append(residual)
    return pl.pallas_call(
        functools.partial(_mm_kernel, has_res=residual is not None),
        grid=(m // tm, n // tn),
        in_specs=in_specs,
        out_specs=pl.BlockSpec((tm, tn), lambda i, j: (i, j)),
        out_shape=jax.ShapeDtypeStruct((m, n), out_dtype),
        compiler_params=_cparams(("parallel", "parallel")),
        name=name,
    )(*args)


def _in_proj_kernel(off_ref, a_ref, w_ref, o_ref):
    o_ref[...] = _dot_nt(a_ref[...], w_ref[...].astype(BF16)).astype(o_ref.dtype)


def in_proj(a, w_t, layer, offsets, *, tm=2048, tn=IN_TN, name):
    m, k = a.shape
    tm = min(tm, m)
    grid_spec = pltpu.PrefetchScalarGridSpec(
        num_scalar_prefetch=1,
        grid=(m // tm, len(offsets)),
        in_specs=[pl.BlockSpec((tm, k), lambda i, j, off: (i, 0)),
                  pl.BlockSpec((None, pl.Element(tn), pl.Element(k)),
                               lambda i, j, off: (layer, off[j] * IN_ALIGN, 0))],
        out_specs=pl.BlockSpec((tm, tn), lambda i, j, off: (i, j)),
    )
    return pl.pallas_call(
        _in_proj_kernel,
        grid_spec=grid_spec,
        out_shape=jax.ShapeDtypeStruct((m, tn * len(offsets)), BF16),
        compiler_params=_cparams(("parallel", "parallel")),
        name=name,
    )(jnp.asarray([o // IN_ALIGN for o in offsets], I32), a, w_t)


def _rope_table_kernel(pos_ref, f64_ref, f32_ref, ca_ref, sa_ref, cb_ref, sb_ref):
    pos = pos_ref[...]
    lane = _iota((1, LANES), 1)
    ang_a = pos * f64_ref[...]
    ang_b = pos * f32_ref[...]
    ca_ref[...] = jnp.cos(ang_a)
    sa_ref[...] = jnp.where(lane < 64, -1.0, 1.0) * jnp.sin(ang_a)
    cb_ref[...] = jnp.cos(ang_b)
    sb_ref[...] = jnp.where(lane % 64 < 32, -1.0, 1.0) * jnp.sin(ang_b)


def rope_tables(positions, tm=512):
    t = positions.size
    pos = positions.reshape(t, 1).astype(F32)
    inv64 = ROPE_THETA ** (-jnp.arange(64, dtype=F32) / 64)
    inv32 = ROPE_THETA ** (-jnp.arange(32, dtype=F32) / 32)
    f64 = jnp.tile(inv64, 2).reshape(1, LANES)
    f32 = jnp.tile(inv32, 4).reshape(1, LANES)
    row = pl.BlockSpec((tm, LANES), lambda i: (i, 0))
    vec = pl.BlockSpec((1, LANES), lambda i: (0, 0))
    return pl.pallas_call(
        _rope_table_kernel,
        grid=(t // tm,),
        in_specs=[pl.BlockSpec((tm, 1), lambda i: (i, 0)), vec, vec],
        out_specs=[row] * 4,
        out_shape=[jax.ShapeDtypeStruct((t, LANES), F32)] * 4,
        compiler_params=_cparams(("parallel",)),
        name="rope_tables",
    )(pos, f64, f32)


SB_TQ = 256
SB_G = 4


def _sb_kernel(q_ref, k_ref, v_ref, o_ref, *, tq, scale):
    qi = pl.program_id(2)
    r = _iota((tq, tq), 0)
    c = _iota((tq, tq), 1)
    later = jnp.where(r > c, 1.0, 0.0).astype(BF16)
    qs = [(q_ref[:, g * SB_DIM:(g + 1) * SB_DIM].astype(F32) * scale).astype(BF16) for g in range(SB_G)]

    heads = range(SB_G)
    cols = [slice(g * SB_DIM, (g + 1) * SB_DIM) for g in heads]

    def span(j, state, diagonal):
        off = pl.multiple_of(j * tq, tq)
        carry, acc = state
        ks = [k_ref[pl.ds(off, tq), cols[g]].astype(BF16) for g in heads]
        vs = [v_ref[pl.ds(off, tq), cols[g]].astype(BF16) for g in heads]
        zs = [_dot_nt(qs[g], ks[g]) for g in heads]
        sps = [_softplus(z) for z in zs]
        lks = [jnp.where(r > c, -sp, 0.0) if diagonal else -sp for sp in sps]
        his = [lk.astype(BF16) for lk in lks]
        los = [(lks[g] - his[g].astype(F32)).astype(BF16) for g in heads]
        css = [_dot(his[g], later) + _dot(los[g], later) for g in heads]
        ws = [jnp.exp(zs[g] - sps[g] + css[g] + carry[g]) for g in heads]
        if diagonal:
            ws = [jnp.where(r > c, w, 0.0) for w in ws]
        acc = tuple(acc[g] + _dot(ws[g].astype(BF16), vs[g]) for g in heads)
        carry = tuple(carry[g] + jnp.sum(lks[g], axis=1, keepdims=True) for g in heads)
        return carry, acc

    init = (tuple(jnp.zeros((tq, 1), F32) for _ in heads), tuple(jnp.zeros((tq, SB_DIM), F32) for _ in heads))
    state = span(qi, init, True)
    _, acc = lax.fori_loop(0, qi, lambda i, s: span(qi - 1 - i, s, False), state)
    for g in heads:
        o_ref[:, cols[g]] = acc[g].astype(o_ref.dtype)


def sb_attention(u, batch, seq):
    t = batch * seq
    tq = min(SB_TQ, seq)
    nq = seq // tq
    gw = SB_G * SB_DIM
    qc, kc, vc = COL_SBQ // gw, COL_SBK // gw, COL_SBV // gw
    return pl.pallas_call(
        functools.partial(_sb_kernel, tq=tq, scale=SB_DIM ** -0.5),
        grid=(batch, SB_HEADS // SB_G, nq),
        in_specs=[
            pl.BlockSpec((tq, gw), lambda b, h, i: (b * nq + i, qc + h)),
            pl.BlockSpec((seq, gw), lambda b, h, i: (b, kc + h)),
            pl.BlockSpec((seq, gw), lambda b, h, i: (b, vc + h)),
        ],
        out_specs=pl.BlockSpec((tq, gw), lambda b, h, i: (b * nq + i, h)),
        out_shape=jax.ShapeDtypeStruct((t, BRANCH_W), BF16),
        compiler_params=_cparams(("parallel", "parallel", "arbitrary")),
        name="sb_attention",
    )(u, u, u)


def _rw_pre_kernel(r_ref, k_ref, v_ref, l_ref, pr_ref, pk_ref, pv_ref, plr_ref,
                   mu_ref, w0_ref, a0_ref, kkg_ref, ka_ref, w2_ref, a2_ref, g2_ref,
                   ro_ref, lw_ref, ko_ref, vo_ref, kk_ref, b_ref, g_ref, *, tm, seq):
    i = pl.program_id(0)
    first = (i * tm) % seq == 0
    row0 = _iota((tm, 1), 0) == 0

    def shifted(cur_ref, prev_ref, lo, hi):
        cur = cur_ref[...].astype(F32)
        last = prev_ref[...].astype(F32)[-1:, :]
        last = jnp.where(first, 0.0, last)
        prev = jnp.where(row0, last, pltpu.roll(cur, 1, 0))
        return cur + mu_ref[:, lo:hi] * (prev - cur)

    w = BRANCH_W
    r = shifted(r_ref, pr_ref, 0, w)
    k = shifted(k_ref, pk_ref, w, 2 * w)
    v = shifted(v_ref, pv_ref, 2 * w, 3 * w)
    z = shifted(l_ref, plr_ref, 3 * w, 3 * w + RW_LORA)

    w_pre = w0_ref[...] + _dot(jnp.tanh(z).astype(BF16), w2_ref[...])
    w_log = -_softplus(-w_pre) - 0.5
    lw_ref[...] = -jnp.exp(w_log)
    a = _sigmoid(a0_ref[...] + _dot(z.astype(BF16), a2_ref[...]))
    g_ref[...] = _dot(_sigmoid(z).astype(BF16), g2_ref[...]).astype(g_ref.dtype)
    ones = _group_matrix(LANES, RW_DIM, 1.0)
    kk = k * kkg_ref[...]
    for s in range(w // LANES):
        sl = slice(s * LANES, (s + 1) * LANES)
        kks = kk[:, sl]
        ss = _dot_exact_rhs(kks * kks, ones)
        kkn = kks / jnp.maximum(jnp.sqrt(ss), 1e-12)
        kk_ref[:, sl] = kkn
        b_ref[:, sl] = kkn * a[:, sl]
    ro_ref[...] = r
    ko_ref[...] = k * (1.0 + (a - 1.0) * ka_ref[...])
    vo_ref[...] = v


def rw_prepare(u, mu, w0, a0, k_k, k_a, w2p, a2p, g2p, batch, seq, tm=256):
    t = batch * seq
    tm = min(tm, seq)
    w = BRANCH_W
    sub = 16

    def cur(width, col):
        return pl.BlockSpec((tm, width), lambda i: (i, col // width))

    def prev(width, col):
        return pl.BlockSpec((sub, width), lambda i: (jnp.maximum(i * (tm // sub) - 1, 0), col // width))

    def vec(width):
        return pl.BlockSpec((1, width), lambda i: (0, 0))

    def mat():
        return pl.BlockSpec((RW_LORA, w), lambda i: (0, 0))

    out = pl.BlockSpec((tm, w), lambda i: (i, 0))
    f = jax.ShapeDtypeStruct((t, w), F32)
    return pl.pallas_call(
        functools.partial(_rw_pre_kernel, tm=tm, seq=seq),
        grid=(t // tm,),
        in_specs=[cur(w, COL_RWR), cur(w, COL_RWK), cur(w, COL_RWV), cur(RW_LORA, COL_RWL),
                  prev(w, COL_RWR), prev(w, COL_RWK), prev(w, COL_RWV), prev(RW_LORA, COL_RWL),
                  vec(3 * w + RW_LORA), vec(w), vec(w), vec(w), vec(w), mat(), mat(), mat()],
        out_specs=[out] * 7,
        out_shape=[f, f, f, f, f, f, jax.ShapeDtypeStruct((t, w), BF16)],
        compiler_params=_cparams(("parallel",)),
        name="rw_prepare",
    )(u, u, u, u, u, u, u, u, mu, w0, a0, k_k, k_a, w2p, a2p, g2p)


def _rw_scan_kernel(r_ref, lw_ref, k_ref, v_ref, kk_ref, b_ref, g_ref, rk_ref, lng_ref, lnb_ref,
                    o_ref, st_ref, *, chunk):
    @pl.when(pl.program_id(1) == 0)
    def _():
        st_ref[...] = jnp.zeros_like(st_ref)

    npair = RW_HEADS // 2
    w = npair * LANES
    n = 2 * chunk
    lane = _iota((n, w), 1)
    row = _iota((n, w), 0)
    own = ((lane // RW_DIM) % 2) == (row // chunk)
    rr = _iota((n, n), 0)
    cc = _iota((n, n), 1)
    same = (rr // chunk) == (cc // chunk)
    tri_incl = jnp.where(same & (rr >= cc), 1.0, 0.0).astype(BF16)
    strict = same & (rr > cc)
    incl = same & (rr >= cc)
    eye = jnp.where(rr == cc, 1.0, 0.0)
    blockdiag = (_iota((LANES, LANES), 0) // RW_DIM) == (_iota((LANES, LANES), 1) // RW_DIM)
    avg = _group_matrix(LANES, RW_DIM, 1.0 / RW_DIM)
    ones = _group_matrix(LANES, RW_DIM, 1.0)
    pairs = range(npair)
    sl = [slice(p * LANES, (p + 1) * LANES) for p in pairs]

    def stack(x):
        return jnp.concatenate([x, x], axis=0)

    def per_pair_rows(x):
        return jnp.concatenate([x[:, s] for s in sl], axis=0)

    def per_pair_lanes(x):
        return jnp.concatenate([x[p * chunk:(p + 1) * chunk] for p in pairs], axis=1)

    r, lw, k, v, kk, b = (ref[...] for ref in (r_ref, lw_ref, k_ref, v_ref, kk_ref, b_ref))
    lw2 = stack(lw)
    l1 = lw2.astype(BF16)
    l2 = (lw2 - l1.astype(F32)).astype(BF16)
    lin = _dot(tri_incl, l1) + _dot(tri_incl, l2)
    lend = lin[n - 1:n]
    p_inv = jnp.exp(-lin)
    p_dec = jnp.exp(lend - lin)
    zero = jnp.zeros((n, w), F32)
    kk_t = jnp.where(own, stack(kk) * jnp.exp(lin - lw2), zero).astype(BF16)
    r_t = jnp.where(own, stack(r) * jnp.exp(lin), zero).astype(BF16)
    v_f = jnp.where(own, stack(v), zero)
    v_s = v_f.astype(BF16)
    b_t = (stack(b) * p_inv).astype(BF16)
    k_t = (stack(k) * p_inv).astype(BF16)
    b_d = (stack(b) * p_dec).astype(BF16)
    k_d = (stack(k) * p_dec).astype(BF16)
    p_end = jnp.exp(lend)

    lhs = [jnp.concatenate([kk_t[:, s], r_t[:, s]], axis=0) for s in sl]
    a_b = [_dot_nt(lhs[p], b_t[:, sl[p]]) for p in pairs]
    a_k = [_dot_nt(lhs[p], k_t[:, sl[p]]) for p in pairs]
    s_t = [_dot_nt(lhs[p], st_ref[p].astype(BF16)) for p in pairs]
    n_ab = [jnp.where(strict, a_b[p][:n], 0.0) for p in pairs]
    rhs = [s_t[p][:n] + _dot(jnp.where(strict, a_k[p][:n], 0.0).astype(BF16), v_s[:, sl[p]]) for p in pairs]
    inv = [eye - n_ab[p] for p in pairs]
    pw = n_ab
    for _ in range(int(math.log2(chunk)) - 1):
        pw_b = [pw[p].astype(BF16) for p in pairs]
        pw = [_dot(pw_b[p], pw_b[p]) for p in pairs]
        inv = [inv[p] + _dot(inv[p].astype(BF16), pw[p].astype(BF16)) for p in pairs]
    u_s = [-_dot(inv[p].astype(BF16), rhs[p].astype(BF16)) for p in pairs]
    y2 = [s_t[p][n:] + _dot(jnp.where(incl, a_b[p][n:], 0.0).astype(BF16), u_s[p].astype(BF16))
          + _dot(jnp.where(incl, a_k[p][n:], 0.0).astype(BF16), v_s[:, sl[p]]) for p in pairs]
    for p in pairs:
        upd = _dot(u_s[p].T.astype(BF16), b_d[:, sl[p]]) + _dot(v_f[:, sl[p]].T.astype(BF16), k_d[:, sl[p]])
        st_ref[p] = st_ref[p] * p_end[:, sl[p]] + jnp.where(blockdiag, upd, 0.0)

    yr = jnp.concatenate([y2[p][:chunk] + y2[p][chunk:] for p in pairs], axis=0)
    yr1, yr2, _ = _split3(yr)
    d = yr - (_dot(yr1, avg) + _dot(yr2, avg))
    dd1, dd2, _ = _split3(d * d)
    var = _dot(dd1, avg) + _dot(dd2, avg)
    yn = per_pair_lanes(d * lax.rsqrt(var + RW_GN_EPS)) * lng_ref[...] + lnb_ref[...]
    bonus = per_pair_lanes(_dot(per_pair_rows(r * k * rk_ref[...]).astype(BF16), ones)) * v
    o_ref[...] = ((yn + bonus) * g_ref[...].astype(F32)).astype(o_ref.dtype)


def rw_scan(r, lw, k, v, kk, b, g, r_k, ln_g, ln_b, batch, seq):
    t = batch * seq
    chunk = min(RW_CHUNK, seq)
    nc = seq // chunk
    w = BRANCH_W
    blk = pl.BlockSpec((chunk, w), lambda bb, c: (bb * nc + c, 0))
    vec = pl.BlockSpec((1, w), lambda bb, c: (0, 0))
    return pl.pallas_call(
        functools.partial(_rw_scan_kernel, chunk=chunk),
        grid=(batch, nc),
        in_specs=[blk] * 7 + [vec] * 3,
        out_specs=blk,
        out_shape=jax.ShapeDtypeStruct((t, w), BF16),
        scratch_shapes=[pltpu.VMEM((RW_HEADS // 2, LANES, LANES), F32)],
        compiler_params=_cparams(("parallel", "arbitrary")),
        name="rw_scan",
    )(r, lw, k, v, kk, b, g, r_k, ln_g, ln_b)


def _rope(u, cos, sin, half):
    return u * cos + _swap_halves(u, half) * sin


DSA_Q_BLK = 256
DSA_IQ_BLK = 512


def _dsa_pre_kernel(*refs):
    nq, niq = BRANCH_W // DSA_Q_BLK, BRANCH_W // DSA_IQ_BLK
    q_refs, refs = refs[:nq], refs[nq:]
    c_ref, refs = refs[0], refs[1:]
    iq_refs, refs = refs[:niq], refs[niq:]
    (ikw_ref, ca_ref, sa_ref, cb_ref, sb_ref, qg_ref, kg_ref, cg_ref, wuk_ref, wuv_ref,
     qo_ref, ko_ref, vo_ref, iqo_ref, iko_ref, iwo_ref) = refs
    ca, sa, cb, sb = ca_ref[...], sa_ref[...], cb_ref[...], sb_ref[...]
    for h in range(DSA_HEADS):
        sl = slice(h * DSA_DIM, (h + 1) * DSA_DIM)
        per = DSA_Q_BLK // DSA_DIM
        q = q_refs[h // per][:, (h % per) * DSA_DIM:(h % per + 1) * DSA_DIM].astype(F32)
        q = q * lax.rsqrt(jnp.mean(q * q, axis=-1, keepdims=True) + NORM_EPS) * qg_ref[...]
        qo_ref[:, sl] = (_rope(q, ca, sa, 64) * DSA_DIM ** -0.5).astype(qo_ref.dtype)
    c = c_ref[...].astype(F32)
    c = (c * lax.rsqrt(jnp.mean(c * c, axis=-1, keepdims=True) + NORM_EPS) * cg_ref[...]).astype(BF16)
    k = _dot(c, wuk_ref[...].astype(BF16))
    k = k * lax.rsqrt(jnp.mean(k * k, axis=-1, keepdims=True) + NORM_EPS) * kg_ref[...]
    ko_ref[...] = _rope(k, ca, sa, 64).astype(ko_ref.dtype)
    vo_ref[...] = _dot(c, wuv_ref[...].astype(BF16)).astype(vo_ref.dtype)
    for s in range(IDX_HEADS * IDX_DIM // LANES):
        sl = slice(s * LANES, (s + 1) * LANES)
        per = DSA_IQ_BLK // LANES
        iq = iq_refs[s // per][:, (s % per) * LANES:(s % per + 1) * LANES].astype(F32)
        iqo_ref[:, sl] = _rope(iq, cb, sb, 32).astype(iqo_ref.dtype)
    ikw = ikw_ref[...].astype(F32)
    lane = _iota(ikw.shape, 1)
    ik = _rope(ikw, cb, sb, 32)
    iko_ref[...] = jnp.where(lane < IDX_DIM, ik, pltpu.roll(ik, IDX_DIM, 1)).astype(iko_ref.dtype)
    iw = pltpu.roll(ikw, IDX_DIM, 1) * (IDX_HEADS ** -0.5 * IDX_DIM ** -0.5)
    iwo_ref[...] = jnp.where(lane < IDX_HEADS, iw, 0.0)


def _dsa_kernel(q_ref, iq_ref, iw_ref, k_ref, v_ref, ik_ref, o_ref, key_ref, bias_ref, vt_ref, m_ref, l_ref, acc_ref,
                *, tq, seq, top_k):
    kb = tq
    qb = pl.program_id(1)
    nblk = qb + 1
    nh = DSA_HEADS
    int_min = jnp.int32(-2 ** 31)
    kidx = _iota((kb, tq), 0)
    ridx = _iota((kb, tq), 1)

    def causal_mask(j):
        return (kidx + j * kb) <= (ridx + qb * tq)

    def block(ref, j):
        return ref[pl.ds(pl.multiple_of(j * kb, kb), kb), :]

    def fold(x, op):
        return op(x.reshape(kb // 8, 8, x.shape[1]), axis=0)

    @pl.when(qb == 0)
    def _():
        for j in range(seq // LANES):
            vt_ref[:, j * LANES:(j + 1) * LANES] = v_ref[j * LANES:(j + 1) * LANES, :].astype(F32).T.astype(BF16)

    first = _iota((tq, LANES), 1) < IDX_DIM
    lhs = []
    for p in range(IDX_HEADS // 2):
        qp = iq_ref[:, p * LANES:(p + 1) * LANES]
        zero = jnp.zeros_like(qp)
        lhs.append(jnp.concatenate([jnp.where(first, qp, zero), jnp.where(first, zero, qp)], axis=0))
    iw_t = iw_ref[...].T
    w_row = [iw_t[h:h + 1] for h in range(IDX_HEADS)]

    def score_block(j, _):
        ik = block(ik_ref, j)
        sc = jnp.zeros((kb, tq), F32)
        for p in range(IDX_HEADS // 2):
            z = jnp.maximum(_dot_nt(ik, lhs[p]), 0.0)
            sc = sc + z[:, :tq] * w_row[2 * p] + z[:, tq:] * w_row[2 * p + 1]
        sc = sc + 0.0
        bits = lax.bitcast_convert_type(sc, I32)
        skey = bits ^ ((bits >> 31) & jnp.int32(0x7FFFFFFF))
        key_ref[pl.ds(pl.multiple_of(j * kb, kb), kb), :] = jnp.where(causal_mask(j), skey, int_min)
        return 0

    lax.fori_loop(0, nblk, score_block, 0)

    def count(pred_fn):
        def body(j, cnt):
            return cnt + fold(jnp.where(pred_fn(block(key_ref, j)), 1.0, 0.0), jnp.sum)
        return jnp.sum(lax.fori_loop(0, nblk, body, jnp.zeros((8, tq), F32)), axis=0, keepdims=True)

    def bit_step(i, thr):
        cand = thr ^ (jnp.int32(1) << (31 - i))
        return jnp.where(count(lambda keys: keys >= cand) >= top_k, cand, thr)

    thr = lax.fori_loop(0, 32, bit_step, jnp.full((1, tq), int_min, I32))
    need = top_k - count(lambda keys: keys > thr)

    q_all = jnp.concatenate([q_ref[:, h * DSA_DIM:(h + 1) * DSA_DIM] for h in range(nh)], axis=0)
    lower = jnp.where(_iota((kb, kb), 0) > _iota((kb, kb), 1), 1.0, 0.0).astype(BF16)
    m_ref[...] = jnp.full_like(m_ref, -1e30)

    def logits(j, bias):
        lt = _dot_nt(block(k_ref, j), q_all)
        return jnp.concatenate([lt[:, h * tq:(h + 1) * tq] + bias for h in range(nh)], axis=1)

    def select(j, ties_seen):
        keys = block(key_ref, j)
        causal = causal_mask(j)
        tie = causal & (keys == thr)
        tie_f = jnp.where(tie, 1.0, 0.0)
        rank = ties_seen + _dot(lower, tie_f.astype(BF16))
        sel = causal & ((keys > thr) | (tie & (rank < need)))
        bias = jnp.where(sel, 0.0, -1e30)
        bias_ref[pl.ds(pl.multiple_of(j * kb, kb), kb), :] = bias
        m_ref[...] = jnp.maximum(m_ref[...], fold(logits(j, bias), jnp.max))
        return ties_seen + jnp.sum(tie_f, axis=0, keepdims=True)

    lax.fori_loop(0, nblk, select, jnp.zeros((1, tq), F32))
    m_all = jnp.broadcast_to(jnp.max(m_ref[...], axis=0, keepdims=True), m_ref.shape)
    l_ref[...] = jnp.zeros_like(l_ref)
    acc_ref[...] = jnp.zeros_like(acc_ref)

    def attend(j, _):
        lt = logits(j, block(bias_ref, j))
        pr = jnp.exp(lt.reshape(kb // 8, 8, nh * tq) - m_all[None]).reshape(kb, nh * tq)
        l_ref[...] += fold(pr, jnp.sum)
        acc_ref[...] += _dot(vt_ref[:, pl.ds(pl.multiple_of(j * kb, kb), kb)], pr.astype(BF16))
        return 0

    lax.fori_loop(0, nblk, attend, 0)
    out_t = acc_ref[...] / jnp.sum(l_ref[...], axis=0, keepdims=True)
    for h in range(nh):
        o_ref[:, h * DSA_DIM:(h + 1) * DSA_DIM] = out_t[:, h * tq:(h + 1) * tq].T.astype(o_ref.dtype)


def dsa_attention(u, tables, q_gain, k_gain, kv_gain, w_uk, w_uv, batch, seq, tm=256, tq=256):
    t = batch * seq
    tm = min(tm, seq)
    ca, sa, cb, sb = tables
    w = BRANCH_W
    row = lambda width, col: pl.BlockSpec((tm, width), lambda i: (i, col // width))
    tab = pl.BlockSpec((tm, LANES), lambda i: (i, 0))
    vec = lambda width: pl.BlockSpec((1, width), lambda i: (0, 0))
    mat = pl.BlockSpec((DSA_KV_RANK, DSA_DIM), lambda i: (0, 0))
    o_w = pl.BlockSpec((tm, w), lambda i: (i, 0))
    o_n = pl.BlockSpec((tm, LANES), lambda i: (i, 0))
    q, k, v, iq, ik, iw = pl.pallas_call(
        _dsa_pre_kernel,
        grid=(t // tm,),
        in_specs=[row(DSA_Q_BLK, COL_DSAQ + j * DSA_Q_BLK) for j in range(w // DSA_Q_BLK)]
        + [row(DSA_KV_RANK, COL_CKV)]
        + [row(DSA_IQ_BLK, COL_IDXQ + j * DSA_IQ_BLK) for j in range(w // DSA_IQ_BLK)]
        + [row(LANES, COL_IDXKW), tab, tab, tab, tab, vec(DSA_DIM), vec(DSA_DIM), vec(DSA_KV_RANK), mat, mat],
        out_specs=[o_w, o_n, o_n, o_w, o_n, o_n],
        out_shape=[jax.ShapeDtypeStruct((t, w), BF16), jax.ShapeDtypeStruct((t, LANES), BF16),
                   jax.ShapeDtypeStruct((t, LANES), BF16), jax.ShapeDtypeStruct((t, w), BF16),
                   jax.ShapeDtypeStruct((t, LANES), BF16), jax.ShapeDtypeStruct((t, LANES), F32)],
        compiler_params=_cparams(("parallel",)),
        name="dsa_prepare",
    )(*([u] * (w // DSA_Q_BLK + 1 + w // DSA_IQ_BLK + 1)), ca, sa, cb, sb,
      q_gain.reshape(1, -1), k_gain.reshape(1, -1), kv_gain.reshape(1, -1), w_uk, w_uv)

    nq = seq // tq
    top_k = min(IDX_TOPK_MAX, seq // 4)
    qrow = lambda width: pl.BlockSpec((tq, width), lambda b, i: (b * nq + i, 0))
    full = pl.BlockSpec((seq, LANES), lambda b, i: (b, 0))
    return pl.pallas_call(
        functools.partial(_dsa_kernel, tq=tq, seq=seq, top_k=top_k),
        grid=(batch, nq),
        in_specs=[qrow(w), qrow(w), qrow(LANES), full, full, full],
        out_specs=qrow(w),
        out_shape=jax.ShapeDtypeStruct((t, w), BF16),
        scratch_shapes=[pltpu.VMEM((seq, tq), I32), pltpu.VMEM((seq, tq), F32), pltpu.VMEM((DSA_DIM, seq), BF16),
                        pltpu.VMEM((8, DSA_HEADS * tq), F32), pltpu.VMEM((8, DSA_HEADS * tq), F32),
                        pltpu.VMEM((DSA_DIM, DSA_HEADS * tq), F32)],
        compiler_params=_cparams(("parallel", "arbitrary")),
        name="dsa_attention",
    )(q, iq, iw, k, v, ik)


def _sw_kernel(sink_ref, q_ref, kp_ref, kc_ref, vp_ref, vc_ref, cbp_ref, sbp_ref, cbc_ref, sbc_ref,
               qg_ref, kg_ref, o_ref, *, blk):
    n = pl.program_id(1)
    avg = _group_matrix(LANES, SW_DIM, 1.0 / SW_DIM)
    lane2 = _iota((2 * blk, LANES), 1)
    lane1 = _iota((blk, LANES), 1)

    def norm(x, gain):
        return x * lax.rsqrt(_dot_exact_rhs(x * x, avg) + NORM_EPS) * gain

    cb = jnp.concatenate([cbp_ref[...], cbc_ref[...]], axis=0)
    sb = jnp.concatenate([sbp_ref[...], sbc_ref[...]], axis=0)
    k = jnp.concatenate([kp_ref[...], kc_ref[...]], axis=0).astype(F32)
    k = _rope(norm(k, kg_ref[...]), cb, sb, 32)
    v = jnp.concatenate([vp_ref[...], vc_ref[...]], axis=0).astype(F32)
    k_sw, v_sw = pltpu.roll(k, SW_DIM, 1), pltpu.roll(v, SW_DIM, 1)
    k2 = [jnp.where(lane2 < SW_DIM, k, k_sw).astype(BF16), jnp.where(lane2 < SW_DIM, k_sw, k).astype(BF16)]
    v2 = [jnp.where(lane2 < SW_DIM, v, v_sw).astype(BF16), jnp.where(lane2 < SW_DIM, v_sw, v).astype(BF16)]

    r = _iota((2 * blk, 2 * blk), 0) % blk
    c = _iota((2 * blk, 2 * blk), 1)
    dist = r - (c - blk)
    mask = (dist >= 0) & (dist < WINDOW) & ((c >= blk) | (n > 0))
    top = _iota((2 * blk, 1), 0) < blk
    cbc, sbc = cbc_ref[...], sbc_ref[...]
    for p in range(SW_HEADS // 2):
        g = (2 * p) // (SW_HEADS // SW_KV_HEADS)
        q = q_ref[:, p * LANES:(p + 1) * LANES].astype(F32)
        q = _rope(norm(q, qg_ref[...]), cbc, sbc, 32) * SW_DIM ** -0.5
        zero = jnp.zeros_like(q)
        lhs = jnp.concatenate([jnp.where(lane1 < SW_DIM, q, zero), jnp.where(lane1 < SW_DIM, zero, q)],
                              axis=0).astype(BF16)
        logit = jnp.where(mask, _dot_nt(lhs, k2[g]), -1e30)
        sink = jnp.where(top, sink_ref[2 * p], sink_ref[2 * p + 1])
        m = jnp.maximum(jnp.max(logit, axis=1, keepdims=True), sink)
        pr = jnp.where(mask, jnp.exp(logit - m), 0.0)
        den = jnp.sum(pr, axis=1, keepdims=True) + jnp.exp(sink - m)
        o2 = _dot((pr / den).astype(BF16), v2[g])
        o_ref[:, p * LANES:(p + 1) * LANES] = jnp.where(lane1 < SW_DIM, o2[:blk], o2[blk:]).astype(o_ref.dtype)


def sw_attention(u, tables, q_gain, k_gain, sinks, batch, seq, blk=128):
    t = batch * seq
    nb = seq // blk
    _, _, cb, sb = tables
    w = BRANCH_W
    cur = lambda width, col: pl.BlockSpec((blk, width), lambda b, i: (b * nb + i, col // width))
    prev = lambda width, col: pl.BlockSpec((blk, width), lambda b, i: (b * nb + jnp.maximum(i - 1, 0), col // width))
    vec = pl.BlockSpec((1, LANES), lambda b, i: (0, 0))
    tile2 = lambda g: jnp.tile(g.reshape(1, SW_DIM), (1, 2))
    return pl.pallas_call(
        functools.partial(_sw_kernel, blk=blk),
        grid=(batch, nb),
        in_specs=[pl.BlockSpec(memory_space=pltpu.SMEM),
                  cur(w, COL_SWQ), prev(LANES, COL_SWK), cur(LANES, COL_SWK), prev(LANES, COL_SWV), cur(LANES, COL_SWV),
                  prev(LANES, 0), prev(LANES, 0), cur(LANES, 0), cur(LANES, 0), vec, vec],
        out_specs=pl.BlockSpec((blk, w), lambda b, i: (b * nb + i, 0)),
        out_shape=jax.ShapeDtypeStruct((t, w), BF16),
        compiler_params=_cparams(("parallel", "parallel")),
        name="sw_attention",
    )(sinks, u, u, u, u, u, cb, sb, cb, sb, tile2(q_gain), tile2(k_gain))


def _merge_kernel(*refs):
    y_refs = refs[:N_BRANCH]
    g_ref, w_ref, o_ref, acc_ref = refs[N_BRANCH:]
    n = pl.program_id(1)
    gate = _sigmoid(g_ref[...].astype(F32))
    w = w_ref[...].astype(BF16)
    for b in range(N_BRANCH):
        @pl.when(n == b)
        def _():
            contrib = gate * _dot(y_refs[b][...], w)
            if b == 0:
                acc_ref[...] = contrib
            elif b < N_BRANCH - 1:
                acc_ref[...] += contrib
            else:
                o_ref[...] = (acc_ref[...] + contrib).astype(o_ref.dtype)


def merge_branches(ys, u, w_branch, layer, tm=512):
    t, w = ys[0].shape
    d = w_branch.shape[-1]
    tm = min(tm, t)
    y_spec = pl.BlockSpec((tm, w), lambda i, n: (i, 0))
    return pl.pallas_call(
        _merge_kernel,
        grid=(t // tm, N_BRANCH),
        in_specs=[y_spec] * N_BRANCH + [pl.BlockSpec((tm, d), lambda i, n: (i, COL_GATE // d + n)),
                                        _layer_spec((None, w, d), lambda i, n: (n, 0, 0), layer)],
        out_specs=pl.BlockSpec((tm, d), lambda i, n: (i, 0)),
        out_shape=jax.ShapeDtypeStruct((t, d), BF16),
        scratch_shapes=[pltpu.VMEM((tm, d), F32)],
        compiler_params=_cparams(("parallel", "arbitrary")),
        name="merge_branches",
    )(*ys, u, w_branch)


FFN_TM = 1024
FFN_ROWS = (1024, 512, 256)
FFN_TF = 256


def _ffn_kernel(e_ref, rows_ref, x_ref, wg_ref, wu_ref, wd_ref, o_ref):
    rows = rows_ref[pl.program_id(0)]

    @pl.when(pl.program_id(1) == 0)
    def _():
        o_ref[...] = jnp.zeros_like(o_ref)

    for k, size in enumerate(FFN_ROWS):
        below = FFN_ROWS[k + 1] if k + 1 < len(FFN_ROWS) else 0

        @pl.when((rows > below) & (rows <= size))
        def _():
            x = x_ref[:size, :]
            g = _dot(x, wg_ref[0].astype(BF16))
            a = (g * _sigmoid(g) * _dot(x, wu_ref[0].astype(BF16))).astype(BF16)
            o_ref[:size, :] += _dot(a, wd_ref[0].astype(BF16))


def ffn_tiles(xs, tile_expert, tile_rows, w_gate, w_up, w_down):
    r, d = xs.shape
    _, _, ff = w_gate.shape
    ns = r // FFN_TM
    nf = ff // FFN_TF

    def f_eff(s, f, rows):
        return jnp.where(rows[s] > 0, f, nf - 1)

    grid_spec = pltpu.PrefetchScalarGridSpec(
        num_scalar_prefetch=2,
        grid=(ns, nf),
        in_specs=[pl.BlockSpec((FFN_TM, d), lambda s, f, e, rows: (s, 0)),
                  pl.BlockSpec((1, d, FFN_TF), lambda s, f, e, rows: (e[s], 0, f_eff(s, f, rows))),
                  pl.BlockSpec((1, d, FFN_TF), lambda s, f, e, rows: (e[s], 0, f_eff(s, f, rows))),
                  pl.BlockSpec((1, FFN_TF, d), lambda s, f, e, rows: (e[s], f_eff(s, f, rows), 0))],
        out_specs=pl.BlockSpec((FFN_TM, d), lambda s, f, e, rows: (s, 0)),
    )
    return pl.pallas_call(
        _ffn_kernel,
        grid_spec=grid_spec,
        out_shape=jax.ShapeDtypeStruct((r, d), F32),
        compiler_params=_cparams(("parallel", "arbitrary")),
        name="ffn_tiles",
    )(tile_expert, tile_rows, xs, w_gate, w_up, w_down)


def _add_kernel(a_ref, b_ref, o_ref):
    o_ref[...] = a_ref[...] + b_ref[...]


def add(a, b, tm=512):
    t, d = a.shape
    spec = pl.BlockSpec((tm, d), lambda i: (i, 0))
    return pl.pallas_call(_add_kernel, grid=(t // tm,), in_specs=[spec, spec], out_specs=spec,
                          out_shape=jax.ShapeDtypeStruct((t, d), a.dtype),
                          compiler_params=_cparams(("parallel",)), name="residual_add")(a, b)


def dense_ffn(x, h, w_gate, w_up, w_down, index):
    t = h.shape[0]
    ns = t // FFN_TM
    ys = ffn_tiles(h, jnp.full((ns,), index, I32), jnp.full((ns,), FFN_TM, I32), w_gate, w_up, w_down)
    return add(x, ys)


def _router_kernel(x_ref, g_ref, w_ref, h_ref, r_ref):
    x = x_ref[...]
    h = x * lax.rsqrt(jnp.mean(x * x, axis=-1, keepdims=True) + NORM_EPS) * g_ref[...]
    h_ref[...] = h
    logit = _dot_f32(h, w_ref[...])
    lane = _iota(logit.shape, 1).astype(F32)
    neg = -jnp.inf
    l1 = jnp.where(lane < N_EXPERTS, logit, neg)
    m1 = jnp.max(l1, axis=1, keepdims=True)
    i1 = jnp.min(jnp.where(l1 == m1, lane, float(LANES)), axis=1, keepdims=True)
    l2 = jnp.where(lane == i1, neg, l1)
    m2 = jnp.max(l2, axis=1, keepdims=True)
    i2 = jnp.min(jnp.where(l2 == m2, lane, float(LANES)), axis=1, keepdims=True)
    e = jnp.exp(m2 - m1)
    w1 = 1.0 / (1.0 + e)
    w2 = e / (1.0 + e)
    r_ref[...] = jnp.where(lane == 0, i1, jnp.where(lane == 1, i2, jnp.where(lane == 2, w1,
                           jnp.where(lane == 3, w2, 0.0))))


def route(x, gain, router, tm=256):
    t, d = x.shape
    wp = jnp.zeros((d, LANES), F32).at[:, :N_EXPERTS].set(router)
    return pl.pallas_call(
        _router_kernel,
        grid=(t // tm,),
        in_specs=[pl.BlockSpec((tm, d), lambda i: (i, 0)), pl.BlockSpec((1, d), lambda i: (0, 0)),
                  pl.BlockSpec((d, LANES), lambda i: (0, 0))],
        out_specs=[pl.BlockSpec((tm, d), lambda i: (i, 0)), pl.BlockSpec((tm, LANES), lambda i: (i, 0))],
        out_shape=[jax.ShapeDtypeStruct((t, d), F32), jax.ShapeDtypeStruct((t, LANES), F32)],
        compiler_params=_cparams(("parallel",)),
        name="router",
    )(x, gain.reshape(1, d), wp)


def _row_copy(src_ref, row, buf, i, sem):
    return pltpu.make_async_copy(src_ref.at[pl.ds(row, 1)], buf.at[pl.ds(i, 1)], sem)


ROW_DMA_UNROLL = 8


def _row_loop(n, fn):
    def body(blk, _):
        for u in range(ROW_DMA_UNROLL):
            fn(blk * ROW_DMA_UNROLL + u, u)
        return 0
    lax.fori_loop(0, n // ROW_DMA_UNROLL, body, 0)


GATHER_ROWS = 256


def _gather_kernel(idx_ref, live_ref, src_ref, o_ref, buf, sem):
    gb = GATHER_ROWS
    i = pl.program_id(0)
    n = pl.num_programs(0)

    def start_step(step):
        slot = step % 2

        @pl.when(live_ref[step] > 0)
        def _():
            _row_loop(gb, lambda r, u: _row_copy(src_ref, idx_ref[step * gb + r], buf.at[slot], r,
                                                 sem.at[slot]).start(priority=u % 2))

    @pl.when(i == 0)
    def _():
        buf[...] = jnp.zeros_like(buf)
        start_step(i)

    @pl.when(i + 1 < n)
    def _():
        start_step(i + 1)

    slot = i % 2

    @pl.when(live_ref[i] > 0)
    def _():
        _row_loop(gb, lambda r, u: _row_copy(src_ref, 0, buf.at[slot], r, sem.at[slot]).wait())

    o_ref[...] = buf[slot].astype(o_ref.dtype)


def gather_rows(src, idx, live, out_dtype):
    r = idx.shape[0]
    d = src.shape[1]
    gb = GATHER_ROWS
    grid_spec = pltpu.PrefetchScalarGridSpec(
        num_scalar_prefetch=2,
        grid=(r // gb,),
        in_specs=[pl.BlockSpec(memory_space=pl.ANY)],
        out_specs=pl.BlockSpec((gb, d), lambda i, idx, live: (i, 0)),
        scratch_shapes=[pltpu.VMEM((2, gb, d), src.dtype), pltpu.SemaphoreType.DMA((2,))],
    )
    return pl.pallas_call(
        _gather_kernel,
        grid_spec=grid_spec,
        out_shape=jax.ShapeDtypeStruct((r, d), out_dtype),
        compiler_params=_cparams(("arbitrary",)),
        name="gather_rows",
    )(idx, live, src)


def _combine_kernel(d0_ref, d1_ref, x_ref, w_ref, ys_ref, o_ref, buf, sem, *, tm):
    i = pl.program_id(0)
    n = pl.num_programs(0)

    def start_step(step):
        slot = step % 2

        def one(r, u):
            _row_copy(ys_ref, d0_ref[step * tm + r], buf.at[slot, 0], r, sem.at[slot]).start(priority=0)
            _row_copy(ys_ref, d1_ref[step * tm + r], buf.at[slot, 1], r, sem.at[slot]).start(priority=1)
        _row_loop(tm, one)

    @pl.when(i == 0)
    def _():
        start_step(i)

    @pl.when(i + 1 < n)
    def _():
        start_step(i + 1)

    slot = i % 2

    def wait_one(r, u):
        _row_copy(ys_ref, 0, buf.at[slot, 0], r, sem.at[slot]).wait()
        _row_copy(ys_ref, 0, buf.at[slot, 1], r, sem.at[slot]).wait()
    _row_loop(tm, wait_one)
    w = w_ref[...]
    o_ref[...] = x_ref[...] + w[:, 2:3] * buf[slot, 0] + w[:, 3:4] * buf[slot, 1]


def combine_rows(x, ys, d0, d1, w, tm=128):
    t, d = x.shape
    grid_spec = pltpu.PrefetchScalarGridSpec(
        num_scalar_prefetch=2,
        grid=(t // tm,),
        in_specs=[pl.BlockSpec((tm, d), lambda i, a, b: (i, 0)), pl.BlockSpec((tm, LANES), lambda i, a, b: (i, 0)),
                  pl.BlockSpec(memory_space=pl.ANY)],
        out_specs=pl.BlockSpec((tm, d), lambda i, a, b: (i, 0)),
        scratch_shapes=[pltpu.VMEM((2, 2, tm, d), F32), pltpu.SemaphoreType.DMA((2,))],
    )
    return pl.pallas_call(
        functools.partial(_combine_kernel, tm=tm),
        grid_spec=grid_spec,
        out_shape=jax.ShapeDtypeStruct((t, d), F32),
        compiler_params=_cparams(("arbitrary",)),
        name="combine_rows",
    )(d0, d1, x, w, ys)


def moe_ffn(x, gain, router, w_gate, w_up, w_down, first_expert):
    t, d = x.shape
    h, rt = route(x, gain, router)
    e_flat = jnp.concatenate([rt[:, 0], rt[:, 1]]).astype(I32)
    tok = jnp.concatenate([jnp.arange(t, dtype=I32)] * 2)
    onehot = (e_flat[:, None] == jnp.arange(N_EXPERTS, dtype=I32)[None, :]).astype(I32)
    csum = jnp.cumsum(onehot, axis=0)
    rank = jnp.take_along_axis(csum - onehot, e_flat[:, None], axis=1)[:, 0]
    counts = csum[-1]
    n_tiles = (counts + FFN_TM - 1) // FFN_TM
    tile_end = jnp.cumsum(n_tiles)
    tile_start = tile_end - n_tiles
    dest = tile_start[e_flat] * FFN_TM + rank
    ns = TOP_K * t // FFN_TM + N_EXPERTS
    s_idx = jnp.arange(ns, dtype=I32)
    used = s_idx < tile_end[-1]
    s_clip = jnp.minimum(s_idx, tile_end[-1] - 1)
    tile_expert = jnp.minimum(jnp.sum((s_clip[:, None] >= tile_end[None, :]).astype(I32), axis=1), N_EXPERTS - 1)
    tile_rows = jnp.clip(counts[tile_expert] - (s_clip - tile_start[tile_expert]) * FFN_TM, 0, FFN_TM)
    tile_rows = jnp.where(used, tile_rows, 0).astype(I32)
    src_row = jnp.zeros((ns * FFN_TM,), I32).at[dest].set(tok)
    per = FFN_TM // GATHER_ROWS
    live = (jnp.arange(ns * per, dtype=I32) % per * GATHER_ROWS < jnp.repeat(tile_rows, per)).astype(I32)

    xs = gather_rows(h, src_row, live, BF16)
    ys = ffn_tiles(xs, tile_expert + first_expert, tile_rows, w_gate, w_up, w_down)
    return combine_rows(x, ys, dest[:t], dest[t:], rt)


def _ple_kernel(h_ref, wg_ref, p_ref, wp_ref, x_ref, o_ref):
    gate = _sigmoid(_dot(h_ref[...], wg_ref[...].astype(BF16)))
    proj = _dot(p_ref[...].astype(BF16), wp_ref[...].astype(BF16))
    o_ref[...] = x_ref[...] + gate * proj


def ple(h, w_gate, p, w_proj, x, layer, tm=1024, tn=512):
    t, d = x.shape
    pd = p.shape[-1]
    tm = min(tm, t)
    return pl.pallas_call(
        _ple_kernel,
        grid=(t // tm, d // tn),
        in_specs=[pl.BlockSpec((tm, d), lambda i, j: (i, 0)), _layer_spec((d, tn), lambda i, j: (0, j), layer),
                  _layer_spec((tm, pd), lambda i, j: (i, 0), layer), _layer_spec((pd, tn), lambda i, j: (0, j), layer),
                  pl.BlockSpec((tm, tn), lambda i, j: (i, j))],
        out_specs=pl.BlockSpec((tm, tn), lambda i, j: (i, j)),
        out_shape=jax.ShapeDtypeStruct((t, d), F32),
        compiler_params=_cparams(("parallel", "parallel")),
        name="ple",
    )(h, w_gate, p, w_proj, x)


def _pad_rows(w, start, total):
    return jnp.zeros((total, w.shape[1]), F32).at[start:start + w.shape[0]].set(w).astype(BF16)


def kernel(x, p, positions, w_in, mix_norm, ffn_norm, ple_norm, rw_mu, rw_w0, rw_w2, rw_a0, rw_a2, rw_g2, rw_kk,
           rw_ka, rw_rk, rw_ln_g, rw_ln_b, dsa_q_norm, dsa_k_norm, dsa_kv_norm, dsa_w_uk, dsa_w_uv, sw_q_norm,
           sw_k_norm, sw_sinks, w_branch, w_out, ffn_w_gate, ffn_w_up, ffn_w_down, moe_router, moe_w_gate, moe_w_up,
           moe_w_down, ple_w_gate, ple_w_proj):
    b, s, d = x.shape
    t = b * s
    depth = w_in.shape[0]
    xf = x.reshape(t, d)
    tables = rope_tables(positions)
    row = lambda a: a.reshape(1, -1)
    w_in_t = jnp.swapaxes(w_in, 1, 2)
    w_branch_b = w_branch.astype(BF16)
    p_rows = p.reshape(depth, t, -1)
    experts = lambda w: w.reshape((-1,) + w.shape[2:])
    for i in range(depth):
        h = rmsnorm(xf, mix_norm[i])
        uh = in_proj(h, w_in_t, i, HEAD_TILES, name="in_proj_head")
        ut = in_proj(h, w_in_t, i, TAIL_TILES, name="in_proj_tail")
        y_a = sb_attention(uh, b, s)
        rw = rw_prepare(uh, row(rw_mu[i]), row(rw_w0[i]), row(rw_a0[i]), row(rw_kk[i]), row(rw_ka[i]),
                        _pad_rows(rw_w2[i], 0, RW_LORA), _pad_rows(rw_a2[i], RW_LORA_W, RW_LORA),
                        _pad_rows(rw_g2[i], RW_LORA_W + RW_LORA_A, RW_LORA), b, s)
        y_b = rw_scan(*rw, row(rw_rk[i]), row(rw_ln_g[i]), row(rw_ln_b[i]), b, s)
        y_c = dsa_attention(uh, tables, dsa_q_norm[i], dsa_k_norm[i], dsa_kv_norm[i], dsa_w_uk[i], dsa_w_uv[i], b, s)
        y_d = sw_attention(ut, tables, sw_q_norm[i], sw_k_norm[i], sw_sinks[i], b, s)
        merged = merge_branches([y_a, y_b, y_c, y_d], ut, w_branch_b, i)
        xf = matmul(merged, w_out, i, out_dtype=F32, residual=xf, name="out_proj")
        if i % 2 == 0:
            h = rmsnorm(xf, ffn_norm[i])
            xf = dense_ffn(xf, h, ffn_w_gate, ffn_w_up, ffn_w_down, i // 2)
        else:
            xf = moe_ffn(xf, ffn_norm[i], moe_router[i // 2], experts(moe_w_gate), experts(moe_w_up),
                         experts(moe_w_down), (i // 2) * N_EXPERTS)
        h = rmsnorm(xf, ple_norm[i])
        xf = ple(h, ple_w_gate, p_rows, ple_w_proj, xf, i)
    return xf.reshape(b, s, d)
```

```python
import functools
import math

import jax
import jax.numpy as jnp
from jax import lax
from jax.experimental import pallas as pl
from jax.experimental.pallas import tpu as pltpu

F32 = jnp.float32
BF16 = jnp.bfloat16
I32 = jnp.int32

D_MODEL = 2048
ROPE_THETA = 10000.0
NORM_EPS = 1e-6
N_BRANCH = 4
BRANCH_W = 1024
SB_HEADS, SB_DIM = 8, 128
RW_HEADS, RW_DIM = 16, 64
RW_LORA_W, RW_LORA_A, RW_LORA_G = 96, 96, 64
RW_LORA = RW_LORA_W + RW_LORA_A + RW_LORA_G
RW_GN_EPS = 64e-5
DSA_HEADS, DSA_DIM, DSA_KV_RANK = 8, 128, 256
IDX_HEADS, IDX_DIM, IDX_TOPK_MAX = 16, 64, 256
SW_HEADS, SW_KV_HEADS, SW_DIM, WINDOW = 16, 2, 64, 128
N_EXPERTS, TOP_K = 8, 2

LANES = 128
VMEM_LIMIT_BYTES = 56 * 1024 * 1024

HEAD_W = 9216
COL_SBQ = 0
COL_SBK = 1024
COL_SBV = 2048
COL_RWR = 3072
COL_RWK = 4096
COL_RWV = 5120
COL_RWL = 6144
COL_DSAQ = 6400
COL_CKV = 7424
COL_IDXQ = 7680
COL_IDXKW = 8704
SRC_SWQ = 8784
SRC_SWK = 9808
SRC_GATE = 10064
COL_GATE = 0
COL_SWQ = 8192
COL_SWK = 9216
COL_SWV = 9344
IN_TN = 512
IN_ALIGN = 16
HEAD_TILES = tuple(range(0, HEAD_W, IN_TN))
TAIL_TILES = (tuple(range(SRC_GATE, SRC_GATE + N_BRANCH * D_MODEL, IN_TN))
              + tuple(range(SRC_SWQ, SRC_SWK, IN_TN)) + (SRC_SWK,))

RW_CHUNK = 64


def _cparams(sem, vmem=VMEM_LIMIT_BYTES):
    return pltpu.CompilerParams(dimension_semantics=sem, vmem_limit_bytes=vmem)


def _dot(a, b):
    return jnp.dot(a, b, preferred_element_type=F32)


def _dot_nt(a, b):
    return lax.dot_general(a, b, (((1,), (1,)), ((), ())), preferred_element_type=F32)


def _split3(a):
    a1 = a.astype(BF16)
    r1 = a - a1.astype(F32)
    a2 = r1.astype(BF16)
    a3 = (r1 - a2.astype(F32)).astype(BF16)
    return a1, a2, a3


def _dot_exact_rhs(a, m_bf16):
    a1, a2, a3 = _split3(a)
    return _dot(a1, m_bf16) + _dot(a2, m_bf16) + _dot(a3, m_bf16)


def _dot_exact_lhs(m_bf16, a):
    a1, a2, a3 = _split3(a)
    return _dot(m_bf16, a1) + _dot(m_bf16, a2) + _dot(m_bf16, a3)


def _dot_f32(a, b):
    a1, a2, a3 = _split3(a)
    b1, b2, b3 = _split3(b)
    return (_dot(a1, b1) + _dot(a1, b2) + _dot(a2, b1)) + (_dot(a1, b3) + _dot(a2, b2) + _dot(a3, b1))


def _softplus(z):
    return jnp.maximum(z, 0.0) + jnp.log(1.0 + jnp.exp(-jnp.abs(z)))


def _sigmoid(z):
    return 1.0 / (1.0 + jnp.exp(-z))


def _iota(shape, dim):
    return lax.broadcasted_iota(I32, shape, dim)


def _group_matrix(n, group, value):
    r = _iota((n, n), 0) // group
    c = _iota((n, n), 1) // group
    return jnp.where(r == c, value, 0.0).astype(BF16)


def _swap_halves(u, half):
    if 2 * half == LANES:
        return pltpu.roll(u, half, 1)
    lane = _iota(u.shape, 1)
    return jnp.where(lane % (2 * half) < half, pltpu.roll(u, LANES - half, 1), pltpu.roll(u, half, 1))


def _rmsnorm_kernel(x_ref, g_ref, o_ref):
    x = x_ref[...]
    ms = jnp.mean(x * x, axis=-1, keepdims=True)
    o_ref[...] = (x * lax.rsqrt(ms + NORM_EPS) * g_ref[...]).astype(o_ref.dtype)


def rmsnorm(x, gain, out_dtype=BF16, tm=512):
    t, d = x.shape
    return pl.pallas_call(
        _rmsnorm_kernel,
        grid=(t // tm,),
        in_specs=[pl.BlockSpec((tm, d), lambda i: (i, 0)), pl.BlockSpec((1, d), lambda i: (0, 0))],
        out_specs=pl.BlockSpec((tm, d), lambda i: (i, 0)),
        out_shape=jax.ShapeDtypeStruct((t, d), out_dtype),
        compiler_params=_cparams(("parallel",)),
        name="rmsnorm",
    )(x, gain.reshape(1, d))


def _mm_kernel(*refs, has_res):
    if has_res:
        a_ref, w_ref, r_ref, o_ref = refs
    else:
        a_ref, w_ref, o_ref = refs
    acc = _dot(a_ref[...].astype(BF16), w_ref[...].astype(BF16))
    if has_res:
        acc = acc + r_ref[...]
    o_ref[...] = acc.astype(o_ref.dtype)


def _layer_spec(block, index_map, layer):
    return pl.BlockSpec((None,) + tuple(block), lambda *idx: (layer,) + tuple(index_map(*idx)))


def matmul(a, w, layer, *, out_dtype, residual=None, n_cols=None, tm=1024, tn=512, name="matmul"):
    m, k = a.shape
    n = w.shape[2] if n_cols is None else n_cols
    tm, tn = min(tm, m), min(tn, n)
    in_specs = [pl.BlockSpec((tm, k), lambda i, j: (i, 0)), _layer_spec((k, tn), lambda i, j: (0, j), layer)]
    args = [a, w]
    if residual is not None:
        in_specs.append(pl.BlockSpec((tm, tn), lambda i, j: (i, j)))
        args.append(residual)
    return pl.pallas_call(
        functools.partial(_mm_kernel, has_res=residual is not None),
        grid=(m // tm, n // tn),
        in_specs=in_specs,
        out_specs=pl.BlockSpec((tm, tn), lambda i, j: (i, j)),
        out_shape=jax.ShapeDtypeStruct((m, n), out_dtype),
        compiler_params=_cparams(("parallel", "parallel")),
        name=name,
    )(*args)


def _in_proj_kernel(off_ref, a_ref, w_ref, o_ref):
    o_ref[...] = _dot_nt(a_ref[...], w_ref[...].astype(BF16)).astype(o_ref.dtype)


def in_proj(a, w_t, layer, offsets, *, tm=2048, tn=IN_TN, name):
    m, k = a.shape
    tm = min(tm, m)
    grid_spec = pltpu.PrefetchScalarGridSpec(
        num_scalar_prefetch=1,
        grid=(m // tm, len(offsets)),
        in_specs=[pl.BlockSpec((tm, k), lambda i, j, off: (i, 0)),
                  pl.BlockSpec((None, pl.Element(tn), pl.Element(k)),
                               lambda i, j, off: (layer, off[j] * IN_ALIGN, 0))],
        out_specs=pl.BlockSpec((tm, tn), lambda i, j, off: (i, j)),
    )
    return pl.pallas_call(
        _in_proj_kernel,
        grid_spec=grid_spec,
        out_shape=jax.ShapeDtypeStruct((m, tn * len(offsets)), BF16),
        compiler_params=_cparams(("parallel", "parallel")),
        name=name,
    )(jnp.asarray([o // IN_ALIGN for o in offsets], I32), a, w_t)


def _rope_table_kernel(pos_ref, f64_ref, f32_ref, ca_ref, sa_ref, cb_ref, sb_ref):
    pos = pos_ref[...]
    lane = _iota((1, LANES), 1)
    ang_a = pos * f64_ref[...]
    ang_b = pos * f32_ref[...]
    ca_ref[...] = jnp.cos(ang_a)
    sa_ref[...] = jnp.where(lane < 64, -1.0, 1.0) * jnp.sin(ang_a)
    cb_ref[...] = jnp.cos(ang_b)
    sb_ref[...] = jnp.where(lane % 64 < 32, -1.0, 1.0) * jnp.sin(ang_b)


def rope_tables(positions, tm=512):
    t = positions.size
    pos = positions.reshape(t, 1).astype(F32)
    inv64 = ROPE_THETA ** (-jnp.arange(64, dtype=F32) / 64)
    inv32 = ROPE_THETA ** (-jnp.arange(32, dtype=F32) / 32)
    f64 = jnp.tile(inv64, 2).reshape(1, LANES)
    f32 = jnp.tile(inv32, 4).reshape(1, LANES)
    row = pl.BlockSpec((tm, LANES), lambda i: (i, 0))
    vec = pl.BlockSpec((1, LANES), lambda i: (0, 0))
    return pl.pallas_call(
        _rope_table_kernel,
        grid=(t // tm,),
        in_specs=[pl.BlockSpec((tm, 1), lambda i: (i, 0)), vec, vec],
        out_specs=[row] * 4,
        out_shape=[jax.ShapeDtypeStruct((t, LANES), F32)] * 4,
        compiler_params=_cparams(("parallel",)),
        name="rope_tables",
    )(pos, f64, f32)


SB_TQ = 256
SB_G = 4


def _sb_kernel(q_ref, k_ref, v_ref, o_ref, *, tq, scale):
    qi = pl.program_id(2)
    r = _iota((tq, tq), 0)
    c = _iota((tq, tq), 1)
    later = jnp.where(r > c, 1.0, 0.0).astype(BF16)
    qs = [(q_ref[:, g * SB_DIM:(g + 1) * SB_DIM].astype(F32) * scale).astype(BF16) for g in range(SB_G)]

    heads = range(SB_G)
    cols = [slice(g * SB_DIM, (g + 1) * SB_DIM) for g in heads]

    def span(j, state, diagonal):
        off = pl.multiple_of(j * tq, tq)
        carry, acc = state
        ks = [k_ref[pl.ds(off, tq), cols[g]].astype(BF16) for g in heads]
        vs = [v_ref[pl.ds(off, tq), cols[g]].astype(BF16) for g in heads]
        zs = [_dot_nt(qs[g], ks[g]) for g in heads]
        sps = [_softplus(z) for z in zs]
        lks = [jnp.where(r > c, -sp, 0.0) if diagonal else -sp for sp in sps]
        his = [lk.astype(BF16) for lk in lks]
        los = [(lks[g] - his[g].astype(F32)).astype(BF16) for g in heads]
        css = [_dot(his[g], later) + _dot(los[g], later) for g in heads]
        ws = [jnp.exp(zs[g] - sps[g] + css[g] + carry[g]) for g in heads]
        if diagonal:
            ws = [jnp.where(r > c, w, 0.0) for w in ws]
        acc = tuple(acc[g] + _dot(ws[g].astype(BF16), vs[g]) for g in heads)
        carry = tuple(carry[g] + jnp.sum(lks[g], axis=1, keepdims=True) for g in heads)
        return carry, acc

    init = (tuple(jnp.zeros((tq, 1), F32) for _ in heads), tuple(jnp.zeros((tq, SB_DIM), F32) for _ in heads))
    state = span(qi, init, True)
    _, acc = lax.fori_loop(0, qi, lambda i, s: span(qi - 1 - i, s, False), state)
    for g in heads:
        o_ref[:, cols[g]] = acc[g].astype(o_ref.dtype)


def sb_attention(u, batch, seq):
    t = batch * seq
    tq = min(SB_TQ, seq)
    nq = seq // tq
    gw = SB_G * SB_DIM
    qc, kc, vc = COL_SBQ // gw, COL_SBK // gw, COL_SBV // gw
    return pl.pallas_call(
        functools.partial(_sb_kernel, tq=tq, scale=SB_DIM ** -0.5),
        grid=(batch, SB_HEADS // SB_G, nq),
        in_specs=[
            pl.BlockSpec((tq, gw), lambda b, h, i: (b * nq + i, qc + h)),
            pl.BlockSpec((seq, gw), lambda b, h, i: (b, kc + h)),
            pl.BlockSpec((seq, gw), lambda b, h, i: (b, vc + h)),
        ],
        out_specs=pl.BlockSpec((tq, gw), lambda b, h, i: (b * nq + i, h)),
        out_shape=jax.ShapeDtypeStruct((t, BRANCH_W), BF16),
        compiler_params=_cparams(("parallel", "parallel", "arbitrary")),
        name="sb_attention",
    )(u, u, u)


def _rw_pre_kernel(r_ref, k_ref, v_ref, l_ref, pr_ref, pk_ref, pv_ref, plr_ref,
                   mu_ref, w0_ref, a0_ref, kkg_ref, ka_ref, w2_ref, a2_ref, g2_ref,
                   ro_ref, lw_ref, ko_ref, vo_ref, kk_ref, b_ref, g_ref, *, tm, seq):
    i = pl.program_id(0)
    first = (i * tm) % seq == 0
    row0 = _iota((tm, 1), 0) == 0

    def shifted(cur_ref, prev_ref, lo, hi):
        cur = cur_ref[...].astype(F32)
        last = prev_ref[...].astype(F32)[-1:, :]
        last = jnp.where(first, 0.0, last)
        prev = jnp.where(row0, last, pltpu.roll(cur, 1, 0))
        return cur + mu_ref[:, lo:hi] * (prev - cur)

    w = BRANCH_W
    r = shifted(r_ref, pr_ref, 0, w)
    k = shifted(k_ref, pk_ref, w, 2 * w)
    v = shifted(v_ref, pv_ref, 2 * w, 3 * w)
    z = shifted(l_ref, plr_ref, 3 * w, 3 * w + RW_LORA)

    w_pre = w0_ref[...] + _dot(jnp.tanh(z).astype(BF16), w2_ref[...])
    w_log = -_softplus(-w_pre) - 0.5
    lw_ref[...] = -jnp.exp(w_log)
    a = _sigmoid(a0_ref[...] + _dot(z.astype(BF16), a2_ref[...]))
    g_ref[...] = _dot(_sigmoid(z).astype(BF16), g2_ref[...]).astype(g_ref.dtype)
    ones = _group_matrix(LANES, RW_DIM, 1.0)
    kk = k * kkg_ref[...]
    for s in range(w // LANES):
        sl = slice(s * LANES, (s + 1) * LANES)
        kks = kk[:, sl]
        ss = _dot_exact_rhs(kks * kks, ones)
        kkn = kks / jnp.maximum(jnp.sqrt(ss), 1e-12)
        kk_ref[:, sl] = kkn
        b_ref[:, sl] = kkn * a[:, sl]
    ro_ref[...] = r
    ko_ref[...] = k * (1.0 + (a - 1.0) * ka_ref[...])
    vo_ref[...] = v


def rw_prepare(u, mu, w0, a0, k_k, k_a, w2p, a2p, g2p, batch, seq, tm=256):
    t = batch * seq
    tm = min(tm, seq)
    w = BRANCH_W
    sub = 16

    def cur(width, col):
        return pl.BlockSpec((tm, width), lambda i: (i, col // width))

    def prev(width, col):
        return pl.BlockSpec((sub, width), lambda i: (jnp.maximum(i * (tm // sub) - 1, 0), col // width))

    def vec(width):
        return pl.BlockSpec((1, width), lambda i: (0, 0))

    def mat():
        return pl.BlockSpec((RW_LORA, w), lambda i: (0, 0))

    out = pl.BlockSpec((tm, w), lambda i: (i, 0))
    f = jax.ShapeDtypeStruct((t, w), F32)
    return pl.pallas_call(
        functools.partial(_rw_pre_kernel, tm=tm, seq=seq),
        grid=(t // tm,),
        in_specs=[cur(w, COL_RWR), cur(w, COL_RWK), cur(w, COL_RWV), cur(RW_LORA, COL_RWL),
                  prev(w, COL_RWR), prev(w, COL_RWK), prev(w, COL_RWV), prev(RW_LORA, COL_RWL),
                  vec(3 * w + RW_LORA), vec(w), vec(w), vec(w), vec(w), mat(), mat(), mat()],
        out_specs=[out] * 7,
        out_shape=[f, f, f, f, f, f, jax.ShapeDtypeStruct((t, w), BF16)],
        compiler_params=_cparams(("parallel",)),
        name="rw_prepare",
    )(u, u, u, u, u, u, u, u, mu, w0, a0, k_k, k_a, w2p, a2p, g2p)


def _rw_scan_kernel(r_ref, lw_ref, k_ref, v_ref, kk_ref, b_ref, g_ref, rk_ref, lng_ref, lnb_ref,
                    o_ref, st_ref, *, chunk):
    @pl.when(pl.program_id(1) == 0)
    def _():
        st_ref[...] = jnp.zeros_like(st_ref)

    npair = RW_HEADS // 2
    w = npair * LANES
    n = 2 * chunk
    lane = _iota((n, w), 1)
    row = _iota((n, w), 0)
    own = ((lane // RW_DIM) % 2) == (row // chunk)
    rr = _iota((n, n), 0)
    cc = _iota((n, n), 1)
    same = (rr // chunk) == (cc // chunk)
    tri_incl = jnp.where(same & (rr >= cc), 1.0, 0.0).astype(BF16)
    strict = same & (rr > cc)
    incl = same & (rr >= cc)
    eye = jnp.where(rr == cc, 1.0, 0.0)
    blockdiag = (_iota((LANES, LANES), 0) // RW_DIM) == (_iota((LANES, LANES), 1) // RW_DIM)
    avg = _group_matrix(LANES, RW_DIM, 1.0 / RW_DIM)
    ones = _group_matrix(LANES, RW_DIM, 1.0)
    pairs = range(npair)
    sl = [slice(p * LANES, (p + 1) * LANES) for p in pairs]

    def stack(x):
        return jnp.concatenate([x, x], axis=0)

    def per_pair_rows(x):
        return jnp.concatenate([x[:, s] for s in sl], axis=0)

    def per_pair_lanes(x):
        return jnp.concatenate([x[p * chunk:(p + 1) * chunk] for p in pairs], axis=1)

    r, lw, k, v, kk, b = (ref[...] for ref in (r_ref, lw_ref, k_ref, v_ref, kk_ref, b_ref))
    lw2 = stack(lw)
    l1 = lw2.astype(BF16)
    l2 = (lw2 - l1.astype(F32)).astype(BF16)
    lin = _dot(tri_incl, l1) + _dot(tri_incl, l2)
    lend = lin[n - 1:n]
    p_inv = jnp.exp(-lin)
    p_dec = jnp.exp(lend - lin)
    zero = jnp.zeros((n, w), F32)
    kk_t = jnp.where(own, stack(kk) * jnp.exp(lin - lw2), zero).astype(BF16)
    r_t = jnp.where(own, stack(r) * jnp.exp(lin), zero).astype(BF16)
    v_f = jnp.where(own, stack(v), zero)
    v_s = v_f.astype(BF16)
    b_t = (stack(b) * p_inv).astype(BF16)
    k_t = (stack(k) * p_inv).astype(BF16)
    b_d = (stack(b) * p_dec).astype(BF16)
    k_d = (stack(k) * p_dec).astype(BF16)
    p_end = jnp.exp(lend)

    lhs = [jnp.concatenate([kk_t[:, s], r_t[:, s]], axis=0) for s in sl]
    a_b = [_dot_nt(lhs[p], b_t[:, sl[p]]) for p in pairs]
    a_k = [_dot_nt(lhs[p], k_t[:, sl[p]]) for p in pairs]
    s_t = [_dot_nt(lhs[p], st_ref[p].astype(BF16)) for p in pairs]
    n_ab = [jnp.where(strict, a_b[p][:n], 0.0) for p in pairs]
    rhs = [s_t[p][:n] + _dot(jnp.where(strict, a_k[p][:n], 0.0).astype(BF16), v_s[:, sl[p]]) for p in pairs]
    inv = [eye - n_ab[p] for p in pairs]
    pw = n_ab
    for _ in range(int(math.log2(chunk)) - 1):
        pw_b = [pw[p].astype(BF16) for p in pairs]
        pw = [_dot(pw_b[p], pw_b[p]) for p in pairs]
        inv = [inv[p] + _dot(inv[p].astype(BF16), pw[p].astype(BF16)) for p in pairs]
    u_s = [-_dot(inv[p].astype(BF16), rhs[p].astype(BF16)) for p in pairs]
    y2 = [s_t[p][n:] + _dot(jnp.where(incl, a_b[p][n:], 0.0).astype(BF16), u_s[p].astype(BF16))
          + _dot(jnp.where(incl, a_k[p][n:], 0.0).astype(BF16), v_s[:, sl[p]]) for p in pairs]
    for p in pairs:
        upd = _dot(u_s[p].T.astype(BF16), b_d[:, sl[p]]) + _dot(v_f[:, sl[p]].T.astype(BF16), k_d[:, sl[p]])
        st_ref[p] = st_ref[p] * p_end[:, sl[p]] + jnp.where(blockdiag, upd, 0.0)

    yr = jnp.concatenate([y2[p][:chunk] + y2[p][chunk:] for p in pairs], axis=0)
    yr1, yr2, _ = _split3(yr)
    d = yr - (_dot(yr1, avg) + _dot(yr2, avg))
    dd1, dd2, _ = _split3(d * d)
    var = _dot(dd1, avg) + _dot(dd2, avg)
    yn = per_pair_lanes(d * lax.rsqrt(var + RW_GN_EPS)) * lng_ref[...] + lnb_ref[...]
    bonus = per_pair_lanes(_dot(per_pair_rows(r * k * rk_ref[...]).astype(BF16), ones)) * v
    o_ref[...] = ((yn + bonus) * g_ref[...].astype(F32)).astype(o_ref.dtype)


def rw_scan(r, lw, k, v, kk, b, g, r_k, ln_g, ln_b, batch, seq):
    t = batch * seq
    chunk = min(RW_CHUNK, seq)
    nc = seq // chunk
    w = BRANCH_W
    blk = pl.BlockSpec((chunk, w), lambda bb, c: (bb * nc + c, 0))
    vec = pl.BlockSpec((1, w), lambda bb, c: (0, 0))
    return pl.pallas_call(
        functools.partial(_rw_scan_kernel, chunk=chunk),
        grid=(batch, nc),
        in_specs=[blk] * 7 + [vec] * 3,
        out_specs=blk,
        out_shape=jax.ShapeDtypeStruct((t, w), BF16),
        scratch_shapes=[pltpu.VMEM((RW_HEADS // 2, LANES, LANES), F32)],
        compiler_params=_cparams(("parallel", "arbitrary")),
        name="rw_scan",
    )(r, lw, k, v, kk, b, g, r_k, ln_g, ln_b)


def _rope(u, cos, sin, half):
    return u * cos + _swap_halves(u, half) * sin


DSA_Q_BLK = 256
DSA_IQ_BLK = 512


def _dsa_pre_kernel(*refs):
    nq, niq = BRANCH_W // DSA_Q_BLK, BRANCH_W // DSA_IQ_BLK
    q_refs, refs = refs[:nq], refs[nq:]
    c_ref, refs = refs[0], refs[1:]
    iq_refs, refs = refs[:niq], refs[niq:]
    (ikw_ref, ca_ref, sa_ref, cb_ref, sb_ref, qg_ref, kg_ref, cg_ref, wuk_ref, wuv_ref,
     qo_ref, ko_ref, vo_ref, iqo_ref, iko_ref, iwo_ref) = refs
    ca, sa, cb, sb = ca_ref[...], sa_ref[...], cb_ref[...], sb_ref[...]
    for h in range(DSA_HEADS):
        sl = slice(h * DSA_DIM, (h + 1) * DSA_DIM)
        per = DSA_Q_BLK // DSA_DIM
        q = q_refs[h // per][:, (h % per) * DSA_DIM:(h % per + 1) * DSA_DIM].astype(F32)
        q = q * lax.rsqrt(jnp.mean(q * q, axis=-1, keepdims=True) + NORM_EPS) * qg_ref[...]
        qo_ref[:, sl] = (_rope(q, ca, sa, 64) * DSA_DIM ** -0.5).astype(qo_ref.dtype)
    c = c_ref[...].astype(F32)
    c = (c * lax.rsqrt(jnp.mean(c * c, axis=-1, keepdims=True) + NORM_EPS) * cg_ref[...]).astype(BF16)
    k = _dot(c, wuk_ref[...].astype(BF16))
    k = k * lax.rsqrt(jnp.mean(k * k, axis=-1, keepdims=True) + NORM_EPS) * kg_ref[...]
    ko_ref[...] = _rope(k, ca, sa, 64).astype(ko_ref.dtype)
    vo_ref[...] = _dot(c, wuv_ref[...].astype(BF16)).astype(vo_ref.dtype)
    for s in range(IDX_HEADS * IDX_DIM // LANES):
        sl = slice(s * LANES, (s + 1) * LANES)
        per = DSA_IQ_BLK // LANES
        iq = iq_refs[s // per][:, (s % per) * LANES:(s % per + 1) * LANES].astype(F32)
        iqo_ref[:, sl] = _rope(iq, cb, sb, 32).astype(iqo_ref.dtype)
    ikw = ikw_ref[...].astype(F32)
    lane = _iota(ikw.shape, 1)
    ik = _rope(ikw, cb, sb, 32)
    iko_ref[...] = jnp.where(lane < IDX_DIM, ik, pltpu.roll(ik, IDX_DIM, 1)).astype(iko_ref.dtype)
    iw = pltpu.roll(ikw, IDX_DIM, 1) * (IDX_HEADS ** -0.5 * IDX_DIM ** -0.5)
    iwo_ref[...] = jnp.where(lane < IDX_HEADS, iw, 0.0)


def _dsa_kernel(q_ref, iq_ref, iw_ref, k_ref, v_ref, ik_ref, o_ref, key_ref, bias_ref, vt_ref, m_ref, l_ref, acc_ref,
                *, tq, seq, top_k):
    kb = tq
    qb = pl.program_id(1)
    nblk = qb + 1
    nh = DSA_HEADS
    int_min = jnp.int32(-2 ** 31)
    kidx = _iota((kb, tq), 0)
    ridx = _iota((kb, tq), 1)

    def causal_mask(j):
        return (kidx + j * kb) <= (ridx + qb * tq)

    def block(ref, j):
        return ref[pl.ds(pl.multiple_of(j * kb, kb), kb), :]

    def fold(x, op):
        return op(x.reshape(kb // 8, 8, x.shape[1]), axis=0)

    @pl.when(qb == 0)
    def _():
        for j in range(seq // LANES):
            vt_ref[:, j * LANES:(j + 1) * LANES] = v_ref[j * LANES:(j + 1) * LANES, :].astype(F32).T.astype(BF16)

    first = _iota((tq, LANES), 1) < IDX_DIM
    lhs = []
    for p in range(IDX_HEADS // 2):
        qp = iq_ref[:, p * LANES:(p + 1) * LANES]
        zero = jnp.zeros_like(qp)
        lhs.append(jnp.concatenate([jnp.where(first, qp, zero), jnp.where(first, zero, qp)], axis=0))
    iw_t = iw_ref[...].T
    w_row = [iw_t[h:h + 1] for h in range(IDX_HEADS)]

    def score_block(j, _):
        ik = block(ik_ref, j)
        sc = jnp.zeros((kb, tq), F32)
        for p in range(IDX_HEADS // 2):
            z = jnp.maximum(_dot_nt(ik, lhs[p]), 0.0)
            sc = sc + z[:, :tq] * w_row[2 * p] + z[:, tq:] * w_row[2 * p + 1]
        sc = sc + 0.0
        bits = lax.bitcast_convert_type(sc, I32)
        skey = bits ^ ((bits >> 31) & jnp.int32(0x7FFFFFFF))
        key_ref[pl.ds(pl.multiple_of(j * kb, kb), kb), :] = jnp.where(causal_mask(j), skey, int_min)
        return 0

    lax.fori_loop(0, nblk, score_block, 0)

    def count(pred_fn):
        def body(j, cnt):
            return cnt + fold(jnp.where(pred_fn(block(key_ref, j)), 1.0, 0.0), jnp.sum)
        return jnp.sum(lax.fori_loop(0, nblk, body, jnp.zeros((8, tq), F32)), axis=0, keepdims=True)

    def bit_step(i, thr):
        cand = thr ^ (jnp.int32(1) << (31 - i))
        return jnp.where(count(lambda keys: keys >= cand) >= top_k, cand, thr)

    thr = lax.fori_loop(0, 32, bit_step, jnp.full((1, tq), int_min, I32))
    need = top_k - count(lambda keys: keys > thr)

    q_all = jnp.concatenate([q_ref[:, h * DSA_DIM:(h + 1) * DSA_DIM] for h in range(nh)], axis=0)
    lower = jnp.where(_iota((kb, kb), 0) > _iota((kb, kb), 1), 1.0, 0.0).astype(BF16)
    m_ref[...] = jnp.full_like(m_ref, -1e30)

    def logits(j, bias):
        lt = _dot_nt(block(k_ref, j), q_all)
        return jnp.concatenate([lt[:, h * tq:(h + 1) * tq] + bias for h in range(nh)], axis=1)

    def select(j, ties_seen):
        keys = block(key_ref, j)
        causal = causal_mask(j)
        tie = causal & (keys == thr)
        tie_f = jnp.where(tie, 1.0, 0.0)
        rank = ties_seen + _dot(lower, tie_f.astype(BF16))
        sel = causal & ((keys > thr) | (tie & (rank < need)))
        bias = jnp.where(sel, 0.0, -1e30)
        bias_ref[pl.ds(pl.multiple_of(j * kb, kb), kb), :] = bias
        m_ref[...] = jnp.maximum(m_ref[...], fold(logits(j, bias), jnp.max))
        return ties_seen + jnp.sum(tie_f, axis=0, keepdims=True)

    lax.fori_loop(0, nblk, select, jnp.zeros((1, tq), F32))
    m_all = jnp.broadcast_to(jnp.max(m_ref[...], axis=0, keepdims=True), m_ref.shape)
    l_ref[...] = jnp.zeros_like(l_ref)
    acc_ref[...] = jnp.zeros_like(acc_ref)

    def attend(j, _):
        lt = logits(j, block(bias_ref, j))
        pr = jnp.exp(lt.reshape(kb // 8, 8, nh * tq) - m_all[None]).reshape(kb, nh * tq)
        l_ref[...] += fold(pr, jnp.sum)
        acc_ref[...] += _dot(vt_ref[:, pl.ds(pl.multiple_of(j * kb, kb), kb)], pr.astype(BF16))
        return 0

    lax.fori_loop(0, nblk, attend, 0)
    out_t = acc_ref[...] / jnp.sum(l_ref[...], axis=0, keepdims=True)
    for h in range(nh):
        o_ref[:, h * DSA_DIM:(h + 1) * DSA_DIM] = out_t[:, h * tq:(h + 1) * tq].T.astype(o_ref.dtype)


def dsa_attention(u, tables, q_gain, k_gain, kv_gain, w_uk, w_uv, batch, seq, tm=256, tq=256):
    t = batch * seq
    tm = min(tm, seq)
    ca, sa, cb, sb = tables
    w = BRANCH_W
    row = lambda width, col: pl.BlockSpec((tm, width), lambda i: (i, col // width))
    tab = pl.BlockSpec((tm, LANES), lambda i: (i, 0))
    vec = lambda width: pl.BlockSpec((1, width), lambda i: (0, 0))
    mat = pl.BlockSpec((DSA_KV_RANK, DSA_DIM), lambda i: (0, 0))
    o_w = pl.BlockSpec((tm, w), lambda i: (i, 0))
    o_n = pl.BlockSpec((tm, LANES), lambda i: (i, 0))
    q, k, v, iq, ik, iw = pl.pallas_call(
        _dsa_pre_kernel,
        grid=(t // tm,),
        in_specs=[row(DSA_Q_BLK, COL_DSAQ + j * DSA_Q_BLK) for j in range(w // DSA_Q_BLK)]
        + [row(DSA_KV_RANK, COL_CKV)]
        + [row(DSA_IQ_BLK, COL_IDXQ + j * DSA_IQ_BLK) for j in range(w // DSA_IQ_BLK)]
        + [row(LANES, COL_IDXKW), tab, tab, tab, tab, vec(DSA_DIM), vec(DSA_DIM), vec(DSA_KV_RANK), mat, mat],
        out_specs=[o_w, o_n, o_n, o_w, o_n, o_n],
        out_shape=[jax.ShapeDtypeStruct((t, w), BF16), jax.ShapeDtypeStruct((t, LANES), BF16),
                   jax.ShapeDtypeStruct((t, LANES), BF16), jax.ShapeDtypeStruct((t, w), BF16),
                   jax.ShapeDtypeStruct((t, LANES), BF16), jax.ShapeDtypeStruct((t, LANES), F32)],
        compiler_params=_cparams(("parallel",)),
        name="dsa_prepare",
    )(*([u] * (w // DSA_Q_BLK + 1 + w // DSA_IQ_BLK + 1)), ca, sa, cb, sb,
      q_gain.reshape(1, -1), k_gain.reshape(1, -1), kv_gain.reshape(1, -1), w_uk, w_uv)

    nq = seq // tq
    top_k = min(IDX_TOPK_MAX, seq // 4)
    qrow = lambda width: pl.BlockSpec((tq, width), lambda b, i: (b * nq + i, 0))
    full = pl.BlockSpec((seq, LANES), lambda b, i: (b, 0))
    return pl.pallas_call(
        functools.partial(_dsa_kernel, tq=tq, seq=seq, top_k=top_k),
        grid=(batch, nq),
        in_specs=[qrow(w), qrow(w), qrow(LANES), full, full, full],
        out_specs=qrow(w),
        out_shape=jax.ShapeDtypeStruct((t, w), BF16),
        scratch_shapes=[pltpu.VMEM((seq, tq), I32), pltpu.VMEM((seq, tq), F32), pltpu.VMEM((DSA_DIM, seq), BF16),
                        pltpu.VMEM((8, DSA_HEADS * tq), F32), pltpu.VMEM((8, DSA_HEADS * tq), F32),
                        pltpu.VMEM((DSA_DIM, DSA_HEADS * tq), F32)],
        compiler_params=_cparams(("parallel", "arbitrary")),
        name="dsa_attention",
    )(q, iq, iw, k, v, ik)


def _sw_kernel(sink_ref, q_ref, kp_ref, kc_ref, vp_ref, vc_ref, cbp_ref, sbp_ref, cbc_ref, sbc_ref,
               qg_ref, kg_ref, o_ref, *, blk):
    n = pl.program_id(1)
    avg = _group_matrix(LANES, SW_DIM, 1.0 / SW_DIM)
    lane2 = _iota((2 * blk, LANES), 1)
    lane1 = _iota((blk, LANES), 1)

    def norm(x, gain):
        s1, s2, _ = _split3(x * x)
        return x * lax.rsqrt(_dot(s1, avg) + _dot(s2, avg) + NORM_EPS) * gain

    cb = jnp.concatenate([cbp_ref[...], cbc_ref[...]], axis=0)
    sb = jnp.concatenate([sbp_ref[...], sbc_ref[...]], axis=0)
    k = jnp.concatenate([kp_ref[...], kc_ref[...]], axis=0).astype(F32)
    k = _rope(norm(k, kg_ref[...]), cb, sb, 32)
    v = jnp.concatenate([vp_ref[...], vc_ref[...]], axis=0).astype(F32)
    k_sw, v_sw = pltpu.roll(k, SW_DIM, 1), pltpu.roll(v, SW_DIM, 1)
    k2 = [jnp.where(lane2 < SW_DIM, k, k_sw).astype(BF16), jnp.where(lane2 < SW_DIM, k_sw, k).astype(BF16)]
    v2 = [jnp.where(lane2 < SW_DIM, v, v_sw).astype(BF16), jnp.where(lane2 < SW_DIM, v_sw, v).astype(BF16)]

    r = _iota((2 * blk, 2 * blk), 0) % blk
    c = _iota((2 * blk, 2 * blk), 1)
    dist = r - (c - blk)
    mask = (dist >= 0) & (dist < WINDOW) & ((c >= blk) | (n > 0))
    top = _iota((2 * blk, 1), 0) < blk
    pairs = range(SW_HEADS // 2)
    group = [(2 * p) // (SW_HEADS // SW_KV_HEADS) for p in pairs]
    npair = len(pairs)
    q = jnp.concatenate([q_ref[:, p * LANES:(p + 1) * LANES].astype(F32) for p in pairs], axis=0)
    cbc = jnp.concatenate([cbc_ref[...]] * npair, axis=0)
    sbc = jnp.concatenate([sbc_ref[...]] * npair, axis=0)
    q = _rope(norm(q, qg_ref[...]), cbc, sbc, 32) * SW_DIM ** -0.5
    low = _iota(q.shape, 1) < SW_DIM
    q_a = jnp.where(low, q, 0.0).astype(BF16)
    q_b = jnp.where(low, 0.0, q).astype(BF16)
    rows = [slice(p * blk, (p + 1) * blk) for p in pairs]
    lhs = [jnp.concatenate([q_a[rows[p]], q_b[rows[p]]], axis=0) for p in pairs]
    logit = [jnp.where(mask, _dot_nt(lhs[p], k2[group[p]]), -1e30) for p in pairs]
    sink = [jnp.where(top, sink_ref[2 * p], sink_ref[2 * p + 1]) for p in pairs]
    m = [jnp.maximum(jnp.max(logit[p], axis=1, keepdims=True), sink[p]) for p in pairs]
    pr = [jnp.exp(logit[p] - m[p]) for p in pairs]
    den = [jnp.sum(pr[p], axis=1, keepdims=True) + jnp.exp(sink[p] - m[p]) for p in pairs]
    o2 = [_dot(pr[p].astype(BF16), v2[group[p]]) / den[p] for p in pairs]
    for p in pairs:
        o_ref[:, p * LANES:(p + 1) * LANES] = jnp.where(lane1 < SW_DIM, o2[p][:blk], o2[p][blk:]).astype(o_ref.dtype)


def sw_attention(u, tables, q_gain, k_gain, sinks, batch, seq, blk=128):
    t = batch * seq
    nb = seq // blk
    _, _, cb, sb = tables
    w = BRANCH_W
    cur = lambda width, col: pl.BlockSpec((blk, width), lambda b, i: (b * nb + i, col // width))
    prev = lambda width, col: pl.BlockSpec((blk, width), lambda b, i: (b * nb + jnp.maximum(i - 1, 0), col // width))
    vec = pl.BlockSpec((1, LANES), lambda b, i: (0, 0))
    tile2 = lambda g: jnp.tile(g.reshape(1, SW_DIM), (1, 2))
    return pl.pallas_call(
        functools.partial(_sw_kernel, blk=blk),
        grid=(batch, nb),
        in_specs=[pl.BlockSpec(memory_space=pltpu.SMEM),
                  cur(w, COL_SWQ), prev(LANES, COL_SWK), cur(LANES, COL_SWK), prev(LANES, COL_SWV), cur(LANES, COL_SWV),
                  prev(LANES, 0), prev(LANES, 0), cur(LANES, 0), cur(LANES, 0), vec, vec],
        out_specs=pl.BlockSpec((blk, w), lambda b, i: (b * nb + i, 0)),
        out_shape=jax.ShapeDtypeStruct((t, w), BF16),
        compiler_params=_cparams(("parallel", "parallel")),
        name="sw_attention",
    )(sinks, u, u, u, u, u, cb, sb, cb, sb, tile2(q_gain), tile2(k_gain))


def _merge_kernel(*refs):
    y_refs = refs[:N_BRANCH]
    g_ref, w_ref, o_ref, acc_ref = refs[N_BRANCH:]
    n = pl.program_id(2)
    gate = _sigmoid(g_ref[...].astype(F32))
    w = w_ref[...].astype(BF16)
    for b in range(N_BRANCH):
        @pl.when(n == b)
        def _():
            contrib = gate * _dot(y_refs[b][...], w)
            if b == 0:
                acc_ref[...] = contrib
            elif b < N_BRANCH - 1:
                acc_ref[...] += contrib
            else:
                o_ref[...] = (acc_ref[...] + contrib).astype(o_ref.dtype)


def merge_branches(ys, u, w_branch, layer, tm=1024, tc=1024):
    t, w = ys[0].shape
    d = w_branch.shape[-1]
    tm = min(tm, t)
    nc = d // tc
    y_spec = pl.BlockSpec((tm, w), lambda i, c, n: (i, 0))
    return pl.pallas_call(
        _merge_kernel,
        grid=(t // tm, nc, N_BRANCH),
        in_specs=[y_spec] * N_BRANCH + [pl.BlockSpec((tm, tc), lambda i, c, n: (i, COL_GATE // tc + n * nc + c)),
                                        _layer_spec((None, w, tc), lambda i, c, n: (n, 0, c), layer)],
        out_specs=pl.BlockSpec((tm, tc), lambda i, c, n: (i, c)),
        out_shape=jax.ShapeDtypeStruct((t, d), BF16),
        scratch_shapes=[pltpu.VMEM((tm, tc), F32)],
        compiler_params=_cparams(("parallel", "arbitrary", "arbitrary")),
        name="merge_branches",
    )(*ys, u, w_branch)


FFN_TM = 1024
FFN_ROWS = (1024, 768, 512, 384, 256, 128)
FFN_TF = 256


def _ffn_kernel(e_ref, rows_ref, x_ref, wg_ref, wu_ref, wd_ref, o_ref):
    rows = rows_ref[pl.program_id(0)]

    @pl.when(pl.program_id(1) == 0)
    def _():
        o_ref[...] = jnp.zeros_like(o_ref)

    for k, size in enumerate(FFN_ROWS):
        below = FFN_ROWS[k + 1] if k + 1 < len(FFN_ROWS) else 0

        @pl.when((rows > below) & (rows <= size))
        def _():
            x = x_ref[:size, :]
            g = _dot(x, wg_ref[0].astype(BF16))
            a = (g * _sigmoid(g) * _dot(x, wu_ref[0].astype(BF16))).astype(BF16)
            o_ref[:size, :] += _dot(a, wd_ref[0].astype(BF16))


def ffn_tiles(xs, tile_expert, tile_rows, w_gate, w_up, w_down):
    r, d = xs.shape
    _, _, ff = w_gate.shape
    ns = r // FFN_TM
    nf = ff // FFN_TF

    def f_eff(s, f, rows):
        return jnp.where(rows[s] > 0, f, nf - 1)

    grid_spec = pltpu.PrefetchScalarGridSpec(
        num_scalar_prefetch=2,
        grid=(ns, nf),
        in_specs=[pl.BlockSpec((FFN_TM, d), lambda s, f, e, rows: (s, 0)),
                  pl.BlockSpec((1, d, FFN_TF), lambda s, f, e, rows: (e[s], 0, f_eff(s, f, rows))),
                  pl.BlockSpec((1, d, FFN_TF), lambda s, f, e, rows: (e[s], 0, f_eff(s, f, rows))),
                  pl.BlockSpec((1, FFN_TF, d), lambda s, f, e, rows: (e[s], f_eff(s, f, rows), 0))],
        out_specs=pl.BlockSpec((FFN_TM, d), lambda s, f, e, rows: (s, 0)),
    )
    return pl.pallas_call(
        _ffn_kernel,
        grid_spec=grid_spec,
        out_shape=jax.ShapeDtypeStruct((r, d), F32),
        compiler_params=_cparams(("parallel", "arbitrary")),
        name="ffn_tiles",
    )(tile_expert, tile_rows, xs, w_gate, w_up, w_down)


def _add_kernel(a_ref, b_ref, o_ref):
    o_ref[...] = a_ref[...] + b_ref[...]


def add(a, b, tm=512):
    t, d = a.shape
    spec = pl.BlockSpec((tm, d), lambda i: (i, 0))
    return pl.pallas_call(_add_kernel, grid=(t // tm,), in_specs=[spec, spec], out_specs=spec,
                          out_shape=jax.ShapeDtypeStruct((t, d), a.dtype),
                          compiler_params=_cparams(("parallel",)), name="residual_add")(a, b)


def dense_ffn(x, h, w_gate, w_up, w_down, index):
    t = h.shape[0]
    ns = t // FFN_TM
    ys = ffn_tiles(h, jnp.full((ns,), index, I32), jnp.full((ns,), FFN_TM, I32), w_gate, w_up, w_down)
    return add(x, ys)


def _router_kernel(x_ref, g_ref, w_ref, h_ref, r_ref):
    x = x_ref[...]
    h = x * lax.rsqrt(jnp.mean(x * x, axis=-1, keepdims=True) + NORM_EPS) * g_ref[...]
    h_ref[...] = h
    logit = _dot_f32(h, w_ref[...])
    lane = _iota(logit.shape, 1).astype(F32)
    neg = -jnp.inf
    l1 = jnp.where(lane < N_EXPERTS, logit, neg)
    m1 = jnp.max(l1, axis=1, keepdims=True)
    i1 = jnp.min(jnp.where(l1 == m1, lane, float(LANES)), axis=1, keepdims=True)
    l2 = jnp.where(lane == i1, neg, l1)
    m2 = jnp.max(l2, axis=1, keepdims=True)
    i2 = jnp.min(jnp.where(l2 == m2, lane, float(LANES)), axis=1, keepdims=True)
    e = jnp.exp(m2 - m1)
    w1 = 1.0 / (1.0 + e)
    w2 = e / (1.0 + e)
    r_ref[...] = jnp.where(lane == 0, i1, jnp.where(lane == 1, i2, jnp.where(lane == 2, w1,
                           jnp.where(lane == 3, w2, 0.0))))


def route(x, gain, router, tm=256):
    t, d = x.shape
    wp = jnp.zeros((d, LANES), F32).at[:, :N_EXPERTS].set(router)
    return pl.pallas_call(
        _router_kernel,
        grid=(t // tm,),
        in_specs=[pl.BlockSpec((tm, d), lambda i: (i, 0)), pl.BlockSpec((1, d), lambda i: (0, 0)),
                  pl.BlockSpec((d, LANES), lambda i: (0, 0))],
        out_specs=[pl.BlockSpec((tm, d), lambda i: (i, 0)), pl.BlockSpec((tm, LANES), lambda i: (i, 0))],
        out_shape=[jax.ShapeDtypeStruct((t, d), F32), jax.ShapeDtypeStruct((t, LANES), F32)],
        compiler_params=_cparams(("parallel",)),
        name="router",
    )(x, gain.reshape(1, d), wp)


def _row_copy(src_ref, row, buf, i, sem):
    return pltpu.make_async_copy(src_ref.at[pl.ds(row, 1)], buf.at[pl.ds(i, 1)], sem)


ROW_DMA_UNROLL = 8


def _row_loop(n, fn):
    def body(blk, _):
        for u in range(ROW_DMA_UNROLL):
            fn(blk * ROW_DMA_UNROLL + u, u)
        return 0
    lax.fori_loop(0, n // ROW_DMA_UNROLL, body, 0)


GATHER_ROWS = 256


def _gather_kernel(idx_ref, live_ref, src_ref, o_ref, buf, sem):
    gb = GATHER_ROWS
    i = pl.program_id(0)
    n = pl.num_programs(0)

    def start_step(step):
        slot = step % 2

        @pl.when(live_ref[step] > 0)
        def _():
            _row_loop(gb, lambda r, u: _row_copy(src_ref, idx_ref[step * gb + r], buf.at[slot], r,
                                                 sem.at[slot]).start(priority=u % 2))

    @pl.when(i == 0)
    def _():
        buf[...] = jnp.zeros_like(buf)
        start_step(i)

    @pl.when(i + 1 < n)
    def _():
        start_step(i + 1)

    slot = i % 2

    @pl.when(live_ref[i] > 0)
    def _():
        _row_loop(gb, lambda r, u: _row_copy(src_ref, 0, buf.at[slot], r, sem.at[slot]).wait())

    o_ref[...] = buf[slot].astype(o_ref.dtype)


def gather_rows(src, idx, live, out_dtype):
    r = idx.shape[0]
    d = src.shape[1]
    gb = GATHER_ROWS
    grid_spec = pltpu.PrefetchScalarGridSpec(
        num_scalar_prefetch=2,
        grid=(r // gb,),
        in_specs=[pl.BlockSpec(memory_space=pl.ANY)],
        out_specs=pl.BlockSpec((gb, d), lambda i, idx, live: (i, 0)),
        scratch_shapes=[pltpu.VMEM((2, gb, d), src.dtype), pltpu.SemaphoreType.DMA((2,))],
    )
    return pl.pallas_call(
        _gather_kernel,
        grid_spec=grid_spec,
        out_shape=jax.ShapeDtypeStruct((r, d), out_dtype),
        compiler_params=_cparams(("arbitrary",)),
        name="gather_rows",
    )(idx, live, src)


def _combine_kernel(d0_ref, d1_ref, x_ref, w_ref, ys_ref, o_ref, buf, sem, *, tm):
    i = pl.program_id(0)
    n = pl.num_programs(0)

    def start_step(step):
        slot = step % 2

        def one(r, u):
            _row_copy(ys_ref, d0_ref[step * tm + r], buf.at[slot, 0], r, sem.at[slot]).start(priority=0)
            _row_copy(ys_ref, d1_ref[step * tm + r], buf.at[slot, 1], r, sem.at[slot]).start(priority=1)
        _row_loop(tm, one)

    @pl.when(i == 0)
    def _():
        start_step(i)

    @pl.when(i + 1 < n)
    def _():
        start_step(i + 1)

    slot = i % 2

    def wait_one(r, u):
        _row_copy(ys_ref, 0, buf.at[slot, 0], r, sem.at[slot]).wait()
        _row_copy(ys_ref, 0, buf.at[slot, 1], r, sem.at[slot]).wait()
    _row_loop(tm, wait_one)
    w = w_ref[...]
    o_ref[...] = x_ref[...] + w[:, 2:3] * buf[slot, 0] + w[:, 3:4] * buf[slot, 1]


def combine_rows(x, ys, d0, d1, w, tm=128):
    t, d = x.shape
    grid_spec = pltpu.PrefetchScalarGridSpec(
        num_scalar_prefetch=2,
        grid=(t // tm,),
        in_specs=[pl.BlockSpec((tm, d), lambda i, a, b: (i, 0)), pl.BlockSpec((tm, LANES), lambda i, a, b: (i, 0)),
                  pl.BlockSpec(memory_space=pl.ANY)],
        out_specs=pl.BlockSpec((tm, d), lambda i, a, b: (i, 0)),
        scratch_shapes=[pltpu.VMEM((2, 2, tm, d), F32), pltpu.SemaphoreType.DMA((2,))],
    )
    return pl.pallas_call(
        functools.partial(_combine_kernel, tm=tm),
        grid_spec=grid_spec,
        out_shape=jax.ShapeDtypeStruct((t, d), F32),
        compiler_params=_cparams(("arbitrary",)),
        name="combine_rows",
    )(d0, d1, x, w, ys)


def moe_ffn(x, gain, router, w_gate, w_up, w_down, first_expert):
    t, d = x.shape
    h, rt = route(x, gain, router)
    e_flat = jnp.concatenate([rt[:, 0], rt[:, 1]]).astype(I32)
    tok = jnp.concatenate([jnp.arange(t, dtype=I32)] * 2)
    onehot = (e_flat[:, None] == jnp.arange(N_EXPERTS, dtype=I32)[None, :]).astype(I32)
    csum = jnp.cumsum(onehot, axis=0)
    rank = jnp.take_along_axis(csum - onehot, e_flat[:, None], axis=1)[:, 0]
    counts = csum[-1]
    n_tiles = (counts + FFN_TM - 1) // FFN_TM
    tile_end = jnp.cumsum(n_tiles)
    tile_start = tile_end - n_tiles
    dest = tile_start[e_flat] * FFN_TM + rank
    ns = TOP_K * t // FFN_TM + N_EXPERTS
    s_idx = jnp.arange(ns, dtype=I32)
    used = s_idx < tile_end[-1]
    s_clip = jnp.minimum(s_idx, tile_end[-1] - 1)
    tile_expert = jnp.minimum(jnp.sum((s_clip[:, None] >= tile_end[None, :]).astype(I32), axis=1), N_EXPERTS - 1)
    tile_rows = jnp.clip(counts[tile_expert] - (s_clip - tile_start[tile_expert]) * FFN_TM, 0, FFN_TM)
    tile_rows = jnp.where(used, tile_rows, 0).astype(I32)
    src_row = jnp.zeros((ns * FFN_TM,), I32).at[dest].set(tok)
    per = FFN_TM // GATHER_ROWS
    live = (jnp.arange(ns * per, dtype=I32) % per * GATHER_ROWS < jnp.repeat(tile_rows, per)).astype(I32)

    xs = gather_rows(h, src_row, live, BF16)
    ys = ffn_tiles(xs, tile_expert + first_expert, tile_rows, w_gate, w_up, w_down)
    return combine_rows(x, ys, dest[:t], dest[t:], rt)


def _ple_kernel(h_ref, wg_ref, p_ref, wp_ref, x_ref, o_ref):
    gate = _sigmoid(_dot(h_ref[...], wg_ref[...].astype(BF16)))
    proj = _dot(p_ref[...].astype(BF16), wp_ref[...].astype(BF16))
    o_ref[...] = x_ref[...] + gate * proj


def ple(h, w_gate, p, w_proj, x, layer, tm=1024, tn=512):
    t, d = x.shape
    pd = p.shape[-1]
    tm = min(tm, t)
    return pl.pallas_call(
        _ple_kernel,
        grid=(t // tm, d // tn),
        in_specs=[pl.BlockSpec((tm, d), lambda i, j: (i, 0)), _layer_spec((d, tn), lambda i, j: (0, j), layer),
                  _layer_spec((tm, pd), lambda i, j: (i, 0), layer), _layer_spec((pd, tn), lambda i, j: (0, j), layer),
                  pl.BlockSpec((tm, tn), lambda i, j: (i, j))],
        out_specs=pl.BlockSpec((tm, tn), lambda i, j: (i, j)),
        out_shape=jax.ShapeDtypeStruct((t, d), F32),
        compiler_params=_cparams(("parallel", "parallel")),
        name="ple",
    )(h, w_gate, p, w_proj, x)


def _pad_rows(w, start, total):
    return jnp.zeros((total, w.shape[1]), F32).at[start:start + w.shape[0]].set(w).astype(BF16)


def kernel(x, p, positions, w_in, mix_norm, ffn_norm, ple_norm, rw_mu, rw_w0, rw_w2, rw_a0, rw_a2, rw_g2, rw_kk,
           rw_ka, rw_rk, rw_ln_g, rw_ln_b, dsa_q_norm, dsa_k_norm, dsa_kv_norm, dsa_w_uk, dsa_w_uv, sw_q_norm,
           sw_k_norm, sw_sinks, w_branch, w_out, ffn_w_gate, ffn_w_up, ffn_w_down, moe_router, moe_w_gate, moe_w_up,
           moe_w_down, ple_w_gate, ple_w_proj):
    b, s, d = x.shape
    t = b * s
    depth = w_in.shape[0]
    xf = x.reshape(t, d)
    tables = rope_tables(positions)
    row = lambda a: a.reshape(1, -1)
    w_in_t = jnp.swapaxes(w_in, 1, 2)
    w_branch_b = w_branch.astype(BF16)
    p_rows = p.reshape(depth, t, -1)
    experts = lambda w: w.reshape((-1,) + w.shape[2:])
    for i in range(depth):
        h = rmsnorm(xf, mix_norm[i])
        uh = in_proj(h, w_in_t, i, HEAD_TILES, name="in_proj_head")
        ut = in_proj(h, w_in_t, i, TAIL_TILES, name="in_proj_tail")
        y_a = sb_attention(uh, b, s)
        rw = rw_prepare(uh, row(rw_mu[i]), row(rw_w0[i]), row(rw_a0[i]), row(rw_kk[i]), row(rw_ka[i]),
                        _pad_rows(rw_w2[i], 0, RW_LORA), _pad_rows(rw_a2[i], RW_LORA_W, RW_LORA),
                        _pad_rows(rw_g2[i], RW_LORA_W + RW_LORA_A, RW_LORA), b, s)
        y_b = rw_scan(*rw, row(rw_rk[i]), row(rw_ln_g[i]), row(rw_ln_b[i]), b, s)
        y_c = dsa_attention(uh, tables, dsa_q_norm[i], dsa_k_norm[i], dsa_kv_norm[i], dsa_w_uk[i], dsa_w_uv[i], b, s)
        y_d = sw_attention(ut, tables, sw_q_norm[i], sw_k_norm[i], sw_sinks[i], b, s)
        merged = merge_branches([y_a, y_b, y_c, y_d], ut, w_branch_b, i)
        xf = matmul(merged, w_out, i, out_dtype=F32, residual=xf, name="out_proj")
        if i % 2 == 0:
            h = rmsnorm(xf, ffn_norm[i])
            xf = dense_ffn(xf, h, ffn_w_gate, ffn_w_up, ffn_w_down, i // 2)
        else:
            xf = moe_ffn(xf, ffn_norm[i], moe_router[i // 2], experts(moe_w_gate), experts(moe_w_up),
                         experts(moe_w_down), (i // 2) * N_EXPERTS)
        h = rmsnorm(xf, ple_norm[i])
        xf = ple(h, ple_w_gate, p_rows, ple_w_proj, xf, i)
    return xf.reshape(b, s, d)
```

```python
import functools
import math

import jax
import jax.numpy as jnp
from jax import lax
from jax.experimental import pallas as pl
from jax.experimental.pallas import tpu as pltpu

F32 = jnp.float32
BF16 = jnp.bfloat16
I32 = jnp.int32

D_MODEL = 2048
ROPE_THETA = 10000.0
NORM_EPS = 1e-6
N_BRANCH = 4
BRANCH_W = 1024
SB_HEADS, SB_DIM = 8, 128
RW_HEADS, RW_DIM = 16, 64
RW_LORA_W, RW_LORA_A, RW_LORA_G = 96, 96, 64
RW_LORA = RW_LORA_W + RW_LORA_A + RW_LORA_G
RW_GN_EPS = 64e-5
DSA_HEADS, DSA_DIM, DSA_KV_RANK = 8, 128, 256
IDX_HEADS, IDX_DIM, IDX_TOPK_MAX = 16, 64, 256
SW_HEADS, SW_KV_HEADS, SW_DIM, WINDOW = 16, 2, 64, 128
N_EXPERTS, TOP_K = 8, 2

LANES = 128
VMEM_LIMIT_BYTES = 56 * 1024 * 1024

HEAD_W = 9216
COL_SBQ = 0
COL_SBK = 1024
COL_SBV = 2048
COL_RWR = 3072
COL_RWK = 4096
COL_RWV = 5120
COL_RWL = 6144
COL_DSAQ = 6400
COL_CKV = 7424
COL_IDXQ = 7680
COL_IDXKW = 8704
SRC_SWQ = 8784
SRC_SWK = 9808
SRC_GATE = 10064
COL_GATE = 0
COL_SWQ = 8192
COL_SWK = 9216
COL_SWV = 9344
IN_TN = 512
IN_ALIGN = 16
HEAD_TILES = tuple(range(0, HEAD_W, IN_TN))
TAIL_TILES = (tuple(range(SRC_GATE, SRC_GATE + N_BRANCH * D_MODEL, IN_TN))
              + tuple(range(SRC_SWQ, SRC_SWK, IN_TN)) + (SRC_SWK,))

RW_CHUNK = 64


def _cparams(sem, vmem=VMEM_LIMIT_BYTES):
    return pltpu.CompilerParams(dimension_semantics=sem, vmem_limit_bytes=vmem)


def _dot(a, b):
    return jnp.dot(a, b, preferred_element_type=F32)


def _dot_nt(a, b):
    return lax.dot_general(a, b, (((1,), (1,)), ((), ())), preferred_element_type=F32)


def _split3(a):
    a1 = a.astype(BF16)
    r1 = a - a1.astype(F32)
    a2 = r1.astype(BF16)
    a3 = (r1 - a2.astype(F32)).astype(BF16)
    return a1, a2, a3


def _dot_exact_rhs(a, m_bf16):
    a1, a2, a3 = _split3(a)
    return _dot(a1, m_bf16) + _dot(a2, m_bf16) + _dot(a3, m_bf16)


def _dot_exact_lhs(m_bf16, a):
    a1, a2, a3 = _split3(a)
    return _dot(m_bf16, a1) + _dot(m_bf16, a2) + _dot(m_bf16, a3)


def _dot_f32(a, b):
    a1, a2, a3 = _split3(a)
    b1, b2, b3 = _split3(b)
    return (_dot(a1, b1) + _dot(a1, b2) + _dot(a2, b1)) + (_dot(a1, b3) + _dot(a2, b2) + _dot(a3, b1))


def _softplus(z):
    return jnp.maximum(z, 0.0) + jnp.log(1.0 + jnp.exp(-jnp.abs(z)))


def _sigmoid(z):
    return 1.0 / (1.0 + jnp.exp(-z))


def _iota(shape, dim):
    return lax.broadcasted_iota(I32, shape, dim)


def _group_matrix(n, group, value):
    r = _iota((n, n), 0) // group
    c = _iota((n, n), 1) // group
    return jnp.where(r == c, value, 0.0).astype(BF16)


def _swap_halves(u, half):
    if 2 * half == LANES:
        return pltpu.roll(u, half, 1)
    lane = _iota(u.shape, 1)
    return jnp.where(lane % (2 * half) < half, pltpu.roll(u, LANES - half, 1), pltpu.roll(u, half, 1))


def _rmsnorm_kernel(x_ref, g_ref, o_ref):
    x = x_ref[...]
    ms = jnp.mean(x * x, axis=-1, keepdims=True)
    o_ref[...] = (x * lax.rsqrt(ms + NORM_EPS) * g_ref[...]).astype(o_ref.dtype)


def rmsnorm(x, gain, out_dtype=BF16, tm=512):
    t, d = x.shape
    return pl.pallas_call(
        _rmsnorm_kernel,
        grid=(t // tm,),
        in_specs=[pl.BlockSpec((tm, d), lambda i: (i, 0)), pl.BlockSpec((1, d), lambda i: (0, 0))],
        out_specs=pl.BlockSpec((tm, d), lambda i: (i, 0)),
        out_shape=jax.ShapeDtypeStruct((t, d), out_dtype),
        compiler_params=_cparams(("parallel",)),
        name="rmsnorm",
    )(x, gain.reshape(1, d))


def _mm_kernel(*refs, has_res):
    if has_res:
        a_ref, w_ref, r_ref, o_ref = refs
    else:
        a_ref, w_ref, o_ref = refs
    acc = _dot(a_ref[...].astype(BF16), w_ref[...].astype(BF16))
    if has_res:
        acc = acc + r_ref[...]
    o_ref[...] = acc.astype(o_ref.dtype)


def _layer_spec(block, index_map, layer):
    return pl.BlockSpec((None,) + tuple(block), lambda *idx: (layer,) + tuple(index_map(*idx)))


def matmul(a, w, layer, *, out_dtype, residual=None, n_cols=None, tm=1024, tn=512, name="matmul"):
    m, k = a.shape
    n = w.shape[2] if n_cols is None else n_cols
    tm, tn = min(tm, m), min(tn, n)
    in_specs = [pl.BlockSpec((tm, k), lambda i, j: (i, 0)), _layer_spec((k, tn), lambda i, j: (0, j), layer)]
    args = [a, w]
    if residual is not None:
        in_specs.append(pl.BlockSpec((tm, tn), lambda i, j: (i, j)))
        args.append(residual)
    return pl.pallas_call(
        functools.partial(_mm_kernel, has_res=residual is not None),
        grid=(m // tm, n // tn),
        in_specs=in_specs,
        out_specs=pl.BlockSpec((tm, tn), lambda i, j: (i, j)),
        out_shape=jax.ShapeDtypeStruct((m, n), out_dtype),
        compiler_params=_cparams(("parallel", "parallel")),
        name=name,
    )(*args)


def _in_proj_kernel(off_ref, a_ref, w_ref, o_ref):
    o_ref[...] = _dot_nt(a_ref[...], w_ref[...].astype(BF16)).astype(o_ref.dtype)


def in_proj(a, w_t, layer, offsets, *, tm=2048, tn=IN_TN, name):
    m, k = a.shape
    tm = min(tm, m)
    grid_spec = pltpu.PrefetchScalarGridSpec(
        num_scalar_prefetch=1,
        grid=(m // tm, len(offsets)),
        in_specs=[pl.BlockSpec((tm, k), lambda i, j, off: (i, 0)),
                  pl.BlockSpec((None, pl.Element(tn), pl.Element(k)),
                               lambda i, j, off: (layer, off[j] * IN_ALIGN, 0))],
        out_specs=pl.BlockSpec((tm, tn), lambda i, j, off: (i, j)),
    )
    return pl.pallas_call(
        _in_proj_kernel,
        grid_spec=grid_spec,
        out_shape=jax.ShapeDtypeStruct((m, tn * len(offsets)), BF16),
        compiler_params=_cparams(("parallel", "parallel")),
        name=name,
    )(jnp.asarray([o // IN_ALIGN for o in offsets], I32), a, w_t)


def _rope_table_kernel(pos_ref, f64_ref, f32_ref, ca_ref, sa_ref, cb_ref, sb_ref):
    pos = pos_ref[...]
    lane = _iota((1, LANES), 1)
    ang_a = pos * f64_ref[...]
    ang_b = pos * f32_ref[...]
    ca_ref[...] = jnp.cos(ang_a)
    sa_ref[...] = jnp.where(lane < 64, -1.0, 1.0) * jnp.sin(ang_a)
    cb_ref[...] = jnp.cos(ang_b)
    sb_ref[...] = jnp.where(lane % 64 < 32, -1.0, 1.0) * jnp.sin(ang_b)


def rope_tables(positions, tm=512):
    t = positions.size
    pos = positions.reshape(t, 1).astype(F32)
    inv64 = ROPE_THETA ** (-jnp.arange(64, dtype=F32) / 64)
    inv32 = ROPE_THETA ** (-jnp.arange(32, dtype=F32) / 32)
    f64 = jnp.tile(inv64, 2).reshape(1, LANES)
    f32 = jnp.tile(inv32, 4).reshape(1, LANES)
    row = pl.BlockSpec((tm, LANES), lambda i: (i, 0))
    vec = pl.BlockSpec((1, LANES), lambda i: (0, 0))
    return pl.pallas_call(
        _rope_table_kernel,
        grid=(t // tm,),
        in_specs=[pl.BlockSpec((tm, 1), lambda i: (i, 0)), vec, vec],
        out_specs=[row] * 4,
        out_shape=[jax.ShapeDtypeStruct((t, LANES), F32)] * 4,
        compiler_params=_cparams(("parallel",)),
        name="rope_tables",
    )(pos, f64, f32)


SB_TQ = 256
SB_G = 4


def _sb_kernel(q_ref, k_ref, v_ref, o_ref, *, tq, scale):
    qi = pl.program_id(2)
    r = _iota((tq, tq), 0)
    c = _iota((tq, tq), 1)
    later = jnp.where(r > c, 1.0, 0.0).astype(BF16)
    qs = [(q_ref[:, g * SB_DIM:(g + 1) * SB_DIM].astype(F32) * scale).astype(BF16) for g in range(SB_G)]

    heads = range(SB_G)
    cols = [slice(g * SB_DIM, (g + 1) * SB_DIM) for g in heads]

    def span(j, state, diagonal):
        off = pl.multiple_of(j * tq, tq)
        carry, acc = state
        ks = [k_ref[pl.ds(off, tq), cols[g]].astype(BF16) for g in heads]
        vs = [v_ref[pl.ds(off, tq), cols[g]].astype(BF16) for g in heads]
        zs = [_dot_nt(qs[g], ks[g]) for g in heads]
        lss = [jnp.minimum(z, 0.0) - jnp.log(1.0 + jnp.exp(-jnp.abs(z))) for z in zs]
        lks = [lss[g] - zs[g] for g in heads]
        if diagonal:
            lks = [jnp.where(r > c, lk, 0.0) for lk in lks]
        his = [lk.astype(BF16) for lk in lks]
        los = [(lks[g] - his[g].astype(F32)).astype(BF16) for g in heads]
        css = [_dot(his[g], later) + _dot(los[g], later) for g in heads]
        ws = [jnp.exp(lss[g] + css[g] + carry[g]) for g in heads]
        if diagonal:
            ws = [jnp.where(r > c, w, 0.0) for w in ws]
        acc = tuple(acc[g] + _dot(ws[g].astype(BF16), vs[g]) for g in heads)
        carry = tuple(carry[g] + jnp.sum(lks[g], axis=1, keepdims=True) for g in heads)
        return carry, acc

    init = (tuple(jnp.zeros((tq, 1), F32) for _ in heads), tuple(jnp.zeros((tq, SB_DIM), F32) for _ in heads))
    state = span(qi, init, True)
    _, acc = lax.fori_loop(0, qi, lambda i, s: span(qi - 1 - i, s, False), state)
    for g in heads:
        o_ref[:, cols[g]] = acc[g].astype(o_ref.dtype)


def sb_attention(u, batch, seq):
    t = batch * seq
    tq = min(SB_TQ, seq)
    nq = seq // tq
    gw = SB_G * SB_DIM
    qc, kc, vc = COL_SBQ // gw, COL_SBK // gw, COL_SBV // gw
    return pl.pallas_call(
        functools.partial(_sb_kernel, tq=tq, scale=SB_DIM ** -0.5),
        grid=(batch, SB_HEADS // SB_G, nq),
        in_specs=[
            pl.BlockSpec((tq, gw), lambda b, h, i: (b * nq + i, qc + h)),
            pl.BlockSpec((seq, gw), lambda b, h, i: (b, kc + h)),
            pl.BlockSpec((seq, gw), lambda b, h, i: (b, vc + h)),
        ],
        out_specs=pl.BlockSpec((tq, gw), lambda b, h, i: (b * nq + i, h)),
        out_shape=jax.ShapeDtypeStruct((t, BRANCH_W), BF16),
        compiler_params=_cparams(("parallel", "parallel", "arbitrary")),
        name="sb_attention",
    )(u, u, u)


def _rw_pre_kernel(r_ref, k_ref, v_ref, l_ref, pr_ref, pk_ref, pv_ref, plr_ref,
                   mu_ref, w0_ref, a0_ref, kkg_ref, ka_ref, w2_ref, a2_ref, g2_ref,
                   ro_ref, lw_ref, ko_ref, vo_ref, kk_ref, b_ref, g_ref, *, tm, seq):
    i = pl.program_id(0)
    first = (i * tm) % seq == 0
    row0 = _iota((tm, 1), 0) == 0

    def shifted(cur_ref, prev_ref, lo, hi):
        cur = cur_ref[...].astype(F32)
        last = prev_ref[...].astype(F32)[-1:, :]
        last = jnp.where(first, 0.0, last)
        prev = jnp.where(row0, last, pltpu.roll(cur, 1, 0))
        return cur + mu_ref[:, lo:hi] * (prev - cur)

    w = BRANCH_W
    r = shifted(r_ref, pr_ref, 0, w)
    k = shifted(k_ref, pk_ref, w, 2 * w)
    v = shifted(v_ref, pv_ref, 2 * w, 3 * w)
    z = shifted(l_ref, plr_ref, 3 * w, 3 * w + RW_LORA)

    w_pre = w0_ref[...] + _dot(jnp.tanh(z).astype(BF16), w2_ref[...])
    w_log = -_softplus(-w_pre) - 0.5
    lw_ref[...] = -jnp.exp(w_log)
    a = _sigmoid(a0_ref[...] + _dot(z.astype(BF16), a2_ref[...]))
    g_ref[...] = _dot(_sigmoid(z).astype(BF16), g2_ref[...]).astype(g_ref.dtype)
    ones = _group_matrix(LANES, RW_DIM, 1.0)
    kk = k * kkg_ref[...]
    for s in range(w // LANES):
        sl = slice(s * LANES, (s + 1) * LANES)
        kks = kk[:, sl]
        ss = _dot_exact_rhs(kks * kks, ones)
        kkn = kks / jnp.maximum(jnp.sqrt(ss), 1e-12)
        kk_ref[:, sl] = kkn
        b_ref[:, sl] = kkn * a[:, sl]
    ro_ref[...] = r
    ko_ref[...] = k * (1.0 + (a - 1.0) * ka_ref[...])
    vo_ref[...] = v


def rw_prepare(u, mu, w0, a0, k_k, k_a, w2p, a2p, g2p, batch, seq, tm=256):
    t = batch * seq
    tm = min(tm, seq)
    w = BRANCH_W
    sub = 16

    def cur(width, col):
        return pl.BlockSpec((tm, width), lambda i: (i, col // width))

    def prev(width, col):
        return pl.BlockSpec((sub, width), lambda i: (jnp.maximum(i * (tm // sub) - 1, 0), col // width))

    def vec(width):
        return pl.BlockSpec((1, width), lambda i: (0, 0))

    def mat():
        return pl.BlockSpec((RW_LORA, w), lambda i: (0, 0))

    out = pl.BlockSpec((tm, w), lambda i: (i, 0))
    f = jax.ShapeDtypeStruct((t, w), F32)
    return pl.pallas_call(
        functools.partial(_rw_pre_kernel, tm=tm, seq=seq),
        grid=(t // tm,),
        in_specs=[cur(w, COL_RWR), cur(w, COL_RWK), cur(w, COL_RWV), cur(RW_LORA, COL_RWL),
                  prev(w, COL_RWR), prev(w, COL_RWK), prev(w, COL_RWV), prev(RW_LORA, COL_RWL),
                  vec(3 * w + RW_LORA), vec(w), vec(w), vec(w), vec(w), mat(), mat(), mat()],
        out_specs=[out] * 7,
        out_shape=[f, f, f, f, f, f, jax.ShapeDtypeStruct((t, w), BF16)],
        compiler_params=_cparams(("parallel",)),
        name="rw_prepare",
    )(u, u, u, u, u, u, u, u, mu, w0, a0, k_k, k_a, w2p, a2p, g2p)


def _rw_scan_kernel(r_ref, lw_ref, k_ref, v_ref, kk_ref, b_ref, g_ref, rk_ref, lng_ref, lnb_ref,
                    o_ref, st_ref, *, chunk):
    @pl.when(pl.program_id(1) == 0)
    def _():
        st_ref[...] = jnp.zeros_like(st_ref)

    npair = RW_HEADS // 2
    w = npair * LANES
    n = 2 * chunk
    lane = _iota((n, w), 1)
    row = _iota((n, w), 0)
    own = ((lane // RW_DIM) % 2) == (row // chunk)
    rr = _iota((n, n), 0)
    cc = _iota((n, n), 1)
    same = (rr // chunk) == (cc // chunk)
    tri_incl = jnp.where(same & (rr >= cc), 1.0, 0.0).astype(BF16)
    strict = same & (rr > cc)
    incl = same & (rr >= cc)
    eye = jnp.where(rr == cc, 1.0, 0.0)
    blockdiag = (_iota((LANES, LANES), 0) // RW_DIM) == (_iota((LANES, LANES), 1) // RW_DIM)
    avg = _group_matrix(LANES, RW_DIM, 1.0 / RW_DIM)
    ones = _group_matrix(LANES, RW_DIM, 1.0)
    pairs = range(npair)
    sl = [slice(p * LANES, (p + 1) * LANES) for p in pairs]

    def stack(x):
        return jnp.concatenate([x, x], axis=0)

    def per_pair_rows(x):
        return jnp.concatenate([x[:, s] for s in sl], axis=0)

    def per_pair_lanes(x):
        return jnp.concatenate([x[p * chunk:(p + 1) * chunk] for p in pairs], axis=1)

    r, lw, k, v, kk, b = (ref[...] for ref in (r_ref, lw_ref, k_ref, v_ref, kk_ref, b_ref))
    lw2 = stack(lw)
    l1 = lw2.astype(BF16)
    l2 = (lw2 - l1.astype(F32)).astype(BF16)
    lin = _dot(tri_incl, l1) + _dot(tri_incl, l2)
    lend = lin[n - 1:n]
    p_inv = jnp.exp(-lin)
    p_dec = jnp.exp(lend - lin)
    zero = jnp.zeros((n, w), F32)
    kk_t = jnp.where(own, stack(kk) * jnp.exp(lin - lw2), zero).astype(BF16)
    r_t = jnp.where(own, stack(r) * jnp.exp(lin), zero).astype(BF16)
    v_f = jnp.where(own, stack(v), zero)
    v_s = v_f.astype(BF16)
    b_t = (stack(b) * p_inv).astype(BF16)
    k_t = (stack(k) * p_inv).astype(BF16)
    b_d = (stack(b) * p_dec).astype(BF16)
    k_d = (stack(k) * p_dec).astype(BF16)
    p_end = jnp.exp(lend)

    lhs = [jnp.concatenate([kk_t[:, s], r_t[:, s]], axis=0) for s in sl]
    a_b = [_dot_nt(lhs[p], b_t[:, sl[p]]) for p in pairs]
    a_k = [_dot_nt(lhs[p], k_t[:, sl[p]]) for p in pairs]
    s_t = [_dot_nt(lhs[p], st_ref[p].astype(BF16)) for p in pairs]
    n_ab = [jnp.where(strict, a_b[p][:n], 0.0) for p in pairs]
    rhs = [s_t[p][:n] + _dot(jnp.where(strict, a_k[p][:n], 0.0).astype(BF16), v_s[:, sl[p]]) for p in pairs]
    inv = [eye - n_ab[p] for p in pairs]
    pw = n_ab
    for _ in range(int(math.log2(chunk)) - 1):
        pw_b = [pw[p].astype(BF16) for p in pairs]
        pw = [_dot(pw_b[p], pw_b[p]) for p in pairs]
        inv = [inv[p] + _dot(inv[p].astype(BF16), pw[p].astype(BF16)) for p in pairs]
    u_s = [-_dot(inv[p].astype(BF16), rhs[p].astype(BF16)) for p in pairs]
    y2 = [s_t[p][n:] + _dot(jnp.where(incl, a_b[p][n:], 0.0).astype(BF16), u_s[p].astype(BF16))
          + _dot(jnp.where(incl, a_k[p][n:], 0.0).astype(BF16), v_s[:, sl[p]]) for p in pairs]
    for p in pairs:
        upd = _dot(u_s[p].T.astype(BF16), b_d[:, sl[p]]) + _dot(v_f[:, sl[p]].T.astype(BF16), k_d[:, sl[p]])
        st_ref[p] = st_ref[p] * p_end[:, sl[p]] + jnp.where(blockdiag, upd, 0.0)

    yr = jnp.concatenate([y2[p][:chunk] + y2[p][chunk:] for p in pairs], axis=0)
    yr1, yr2, _ = _split3(yr)
    d = yr - (_dot(yr1, avg) + _dot(yr2, avg))
    dd1, dd2, _ = _split3(d * d)
    var = _dot(dd1, avg) + _dot(dd2, avg)
    yn = per_pair_lanes(d * lax.rsqrt(var + RW_GN_EPS)) * lng_ref[...] + lnb_ref[...]
    bonus = per_pair_lanes(_dot(per_pair_rows(r * k * rk_ref[...]).astype(BF16), ones)) * v
    o_ref[...] = ((yn + bonus) * g_ref[...].astype(F32)).astype(o_ref.dtype)


def rw_scan(r, lw, k, v, kk, b, g, r_k, ln_g, ln_b, batch, seq):
    t = batch * seq
    chunk = min(RW_CHUNK, seq)
    nc = seq // chunk
    w = BRANCH_W
    blk = pl.BlockSpec((chunk, w), lambda bb, c: (bb * nc + c, 0))
    vec = pl.BlockSpec((1, w), lambda bb, c: (0, 0))
    return pl.pallas_call(
        functools.partial(_rw_scan_kernel, chunk=chunk),
        grid=(batch, nc),
        in_specs=[blk] * 7 + [vec] * 3,
        out_specs=blk,
        out_shape=jax.ShapeDtypeStruct((t, w), BF16),
        scratch_shapes=[pltpu.VMEM((RW_HEADS // 2, LANES, LANES), F32)],
        compiler_params=_cparams(("parallel", "arbitrary")),
        name="rw_scan",
    )(r, lw, k, v, kk, b, g, r_k, ln_g, ln_b)


def _rope(u, cos, sin, half):
    return u * cos + _swap_halves(u, half) * sin


DSA_Q_BLK = 256
DSA_IQ_BLK = 512


def _dsa_pre_kernel(*refs):
    nq, niq = BRANCH_W // DSA_Q_BLK, BRANCH_W // DSA_IQ_BLK
    q_refs, refs = refs[:nq], refs[nq:]
    c_ref, refs = refs[0], refs[1:]
    iq_refs, refs = refs[:niq], refs[niq:]
    (ikw_ref, ca_ref, sa_ref, cb_ref, sb_ref, qg_ref, kg_ref, cg_ref, wuk_ref, wuv_ref,
     qo_ref, ko_ref, vo_ref, iqo_ref, iko_ref, iwo_ref) = refs
    ca, sa, cb, sb = ca_ref[...], sa_ref[...], cb_ref[...], sb_ref[...]
    for h in range(DSA_HEADS):
        sl = slice(h * DSA_DIM, (h + 1) * DSA_DIM)
        per = DSA_Q_BLK // DSA_DIM
        q = q_refs[h // per][:, (h % per) * DSA_DIM:(h % per + 1) * DSA_DIM].astype(F32)
        q = q * lax.rsqrt(jnp.mean(q * q, axis=-1, keepdims=True) + NORM_EPS) * qg_ref[...]
        qo_ref[:, sl] = (_rope(q, ca, sa, 64) * DSA_DIM ** -0.5).astype(qo_ref.dtype)
    c = c_ref[...].astype(F32)
    c = (c * lax.rsqrt(jnp.mean(c * c, axis=-1, keepdims=True) + NORM_EPS) * cg_ref[...]).astype(BF16)
    k = _dot(c, wuk_ref[...].astype(BF16))
    k = k * lax.rsqrt(jnp.mean(k * k, axis=-1, keepdims=True) + NORM_EPS) * kg_ref[...]
    ko_ref[...] = _rope(k, ca, sa, 64).astype(ko_ref.dtype)
    vo_ref[...] = _dot(c, wuv_ref[...].astype(BF16)).astype(vo_ref.dtype)
    for s in range(IDX_HEADS * IDX_DIM // LANES):
        sl = slice(s * LANES, (s + 1) * LANES)
        per = DSA_IQ_BLK // LANES
        iq = iq_refs[s // per][:, (s % per) * LANES:(s % per + 1) * LANES].astype(F32)
        iqo_ref[:, sl] = _rope(iq, cb, sb, 32).astype(iqo_ref.dtype)
    ikw = ikw_ref[...].astype(F32)
    lane = _iota(ikw.shape, 1)
    ik = _rope(ikw, cb, sb, 32)
    iko_ref[...] = jnp.where(lane < IDX_DIM, ik, pltpu.roll(ik, IDX_DIM, 1)).astype(iko_ref.dtype)
    iw = pltpu.roll(ikw, IDX_DIM, 1) * (IDX_HEADS ** -0.5 * IDX_DIM ** -0.5)
    iwo_ref[...] = jnp.where(lane < IDX_HEADS, iw, 0.0)


def _dsa_kernel(q_ref, iq_ref, iw_ref, k_ref, v_ref, ik_ref, o_ref, key_ref, bias_ref, vt_ref, m_ref, l_ref, acc_ref,
                *, tq, seq, top_k):
    kb = tq
    qb = pl.program_id(1)
    nblk = qb + 1
    nh = DSA_HEADS
    int_min = jnp.int32(-2 ** 31)
    kidx = _iota((kb, tq), 0)
    ridx = _iota((kb, tq), 1)

    def causal_mask(j):
        return (kidx + j * kb) <= (ridx + qb * tq)

    def block(ref, j):
        return ref[pl.ds(pl.multiple_of(j * kb, kb), kb), :]

    def fold(x, op):
        return op(x.reshape(kb // 8, 8, x.shape[1]), axis=0)

    @pl.when(qb == 0)
    def _():
        for j in range(seq // LANES):
            vt_ref[:, j * LANES:(j + 1) * LANES] = v_ref[j * LANES:(j + 1) * LANES, :].astype(F32).T.astype(BF16)

    first = _iota((tq, LANES), 1) < IDX_DIM
    lhs = []
    for p in range(IDX_HEADS // 2):
        qp = iq_ref[:, p * LANES:(p + 1) * LANES]
        zero = jnp.zeros_like(qp)
        lhs.append(jnp.concatenate([jnp.where(first, qp, zero), jnp.where(first, zero, qp)], axis=0))
    iw_t = iw_ref[...].T
    w_row = [iw_t[h:h + 1] for h in range(IDX_HEADS)]

    def score_block(j, _):
        ik = block(ik_ref, j)
        sc = jnp.zeros((kb, tq), F32)
        for p in range(IDX_HEADS // 2):
            z = jnp.maximum(_dot_nt(ik, lhs[p]), 0.0)
            sc = sc + z[:, :tq] * w_row[2 * p] + z[:, tq:] * w_row[2 * p + 1]
        sc = sc + 0.0
        bits = lax.bitcast_convert_type(sc, I32)
        skey = bits ^ ((bits >> 31) & jnp.int32(0x7FFFFFFF))
        key_ref[pl.ds(pl.multiple_of(j * kb, kb), kb), :] = jnp.where(causal_mask(j), skey, int_min)
        return 0

    lax.fori_loop(0, nblk, score_block, 0)

    def count(pred_fn):
        def body(j, cnt):
            return cnt + fold(jnp.where(pred_fn(block(key_ref, j)), 1.0, 0.0), jnp.sum)
        return jnp.sum(lax.fori_loop(0, nblk, body, jnp.zeros((8, tq), F32)), axis=0, keepdims=True)

    def bit_step(i, thr):
        cand = thr ^ (jnp.int32(1) << (31 - i))
        return jnp.where(count(lambda keys: keys >= cand) >= top_k, cand, thr)

    thr = lax.fori_loop(0, 32, bit_step, jnp.full((1, tq), int_min, I32))
    need = top_k - count(lambda keys: keys > thr)

    q_all = jnp.concatenate([q_ref[:, h * DSA_DIM:(h + 1) * DSA_DIM] for h in range(nh)], axis=0)
    lower = jnp.where(_iota((kb, kb), 0) > _iota((kb, kb), 1), 1.0, 0.0).astype(BF16)
    m_ref[...] = jnp.full_like(m_ref, -1e30)

    def logits(j, bias):
        lt = _dot_nt(block(k_ref, j), q_all)
        return jnp.concatenate([lt[:, h * tq:(h + 1) * tq] + bias for h in range(nh)], axis=1)

    def select(j, ties_seen):
        keys = block(key_ref, j)
        causal = causal_mask(j)
        tie = causal & (keys == thr)
        tie_f = jnp.where(tie, 1.0, 0.0)
        rank = ties_seen + _dot(lower, tie_f.astype(BF16))
        sel = causal & ((keys > thr) | (tie & (rank < need)))
        bias = jnp.where(sel, 0.0, -1e30)
        bias_ref[pl.ds(pl.multiple_of(j * kb, kb), kb), :] = bias
        m_ref[...] = jnp.maximum(m_ref[...], fold(logits(j, bias), jnp.max))
        return ties_seen + jnp.sum(tie_f, axis=0, keepdims=True)

    lax.fori_loop(0, nblk, select, jnp.zeros((1, tq), F32))
    m_all = jnp.broadcast_to(jnp.max(m_ref[...], axis=0, keepdims=True), m_ref.shape)
    l_ref[...] = jnp.zeros_like(l_ref)
    acc_ref[...] = jnp.zeros_like(acc_ref)

    def attend(j, _):
        lt = logits(j, block(bias_ref, j))
        pr = jnp.exp(lt.reshape(kb // 8, 8, nh * tq) - m_all[None]).reshape(kb, nh * tq)
        l_ref[...] += fold(pr, jnp.sum)
        acc_ref[...] += _dot(vt_ref[:, pl.ds(pl.multiple_of(j * kb, kb), kb)], pr.astype(BF16))
        return 0

    lax.fori_loop(0, nblk, attend, 0)
    out_t = acc_ref[...] / jnp.sum(l_ref[...], axis=0, keepdims=True)
    for h in range(nh):
        o_ref[:, h * DSA_DIM:(h + 1) * DSA_DIM] = out_t[:, h * tq:(h + 1) * tq].T.astype(o_ref.dtype)


def dsa_attention(u, tables, q_gain, k_gain, kv_gain, w_uk, w_uv, batch, seq, tm=256, tq=256):
    t = batch * seq
    tm = min(tm, seq)
    ca, sa, cb, sb = tables
    w = BRANCH_W
    row = lambda width, col: pl.BlockSpec((tm, width), lambda i: (i, col // width))
    tab = pl.BlockSpec((tm, LANES), lambda i: (i, 0))
    vec = lambda width: pl.BlockSpec((1, width), lambda i: (0, 0))
    mat = pl.BlockSpec((DSA_KV_RANK, DSA_DIM), lambda i: (0, 0))
    o_w = pl.BlockSpec((tm, w), lambda i: (i, 0))
    o_n = pl.BlockSpec((tm, LANES), lambda i: (i, 0))
    q, k, v, iq, ik, iw = pl.pallas_call(
        _dsa_pre_kernel,
        grid=(t // tm,),
        in_specs=[row(DSA_Q_BLK, COL_DSAQ + j * DSA_Q_BLK) for j in range(w // DSA_Q_BLK)]
        + [row(DSA_KV_RANK, COL_CKV)]
        + [row(DSA_IQ_BLK, COL_IDXQ + j * DSA_IQ_BLK) for j in range(w // DSA_IQ_BLK)]
        + [row(LANES, COL_IDXKW), tab, tab, tab, tab, vec(DSA_DIM), vec(DSA_DIM), vec(DSA_KV_RANK), mat, mat],
        out_specs=[o_w, o_n, o_n, o_w, o_n, o_n],
        out_shape=[jax.ShapeDtypeStruct((t, w), BF16), jax.ShapeDtypeStruct((t, LANES), BF16),
                   jax.ShapeDtypeStruct((t, LANES), BF16), jax.ShapeDtypeStruct((t, w), BF16),
                   jax.ShapeDtypeStruct((t, LANES), BF16), jax.ShapeDtypeStruct((t, LANES), F32)],
        compiler_params=_cparams(("parallel",)),
        name="dsa_prepare",
    )(*([u] * (w // DSA_Q_BLK + 1 + w // DSA_IQ_BLK + 1)), ca, sa, cb, sb,
      q_gain.reshape(1, -1), k_gain.reshape(1, -1), kv_gain.reshape(1, -1), w_uk, w_uv)

    nq = seq // tq
    top_k = min(IDX_TOPK_MAX, seq // 4)
    qrow = lambda width: pl.BlockSpec((tq, width), lambda b, i: (b * nq + i, 0))
    full = pl.BlockSpec((seq, LANES), lambda b, i: (b, 0))
    return pl.pallas_call(
        functools.partial(_dsa_kernel, tq=tq, seq=seq, top_k=top_k),
        grid=(batch, nq),
        in_specs=[qrow(w), qrow(w), qrow(LANES), full, full, full],
        out_specs=qrow(w),
        out_shape=jax.ShapeDtypeStruct((t, w), BF16),
        scratch_shapes=[pltpu.VMEM((seq, tq), I32), pltpu.VMEM((seq, tq), F32), pltpu.VMEM((DSA_DIM, seq), BF16),
                        pltpu.VMEM((8, DSA_HEADS * tq), F32), pltpu.VMEM((8, DSA_HEADS * tq), F32),
                        pltpu.VMEM((DSA_DIM, DSA_HEADS * tq), F32)],
        compiler_params=_cparams(("parallel", "arbitrary")),
        name="dsa_attention",
    )(q, iq, iw, k, v, ik)


def _sw_kernel(sink_ref, q_ref, kp_ref, kc_ref, vp_ref, vc_ref, cbp_ref, sbp_ref, cbc_ref, sbc_ref,
               qg_ref, kg_ref, o_ref, *, blk):
    n = pl.program_id(1)
    avg = _group_matrix(LANES, SW_DIM, 1.0 / SW_DIM)
    lane2 = _iota((2 * blk, LANES), 1)
    lane1 = _iota((blk, LANES), 1)

    def norm(x, gain):
        s1, s2, _ = _split3(x * x)
        return x * lax.rsqrt(_dot(s1, avg) + _dot(s2, avg) + NORM_EPS) * gain

    cb = jnp.concatenate([cbp_ref[...], cbc_ref[...]], axis=0)
    sb = jnp.concatenate([sbp_ref[...], sbc_ref[...]], axis=0)
    k = jnp.concatenate([kp_ref[...], kc_ref[...]], axis=0).astype(F32)
    k = _rope(norm(k, kg_ref[...]), cb, sb, 32)
    v = jnp.concatenate([vp_ref[...], vc_ref[...]], axis=0).astype(F32)
    k_sw, v_sw = pltpu.roll(k, SW_DIM, 1), pltpu.roll(v, SW_DIM, 1)
    k2 = [jnp.where(lane2 < SW_DIM, k, k_sw).astype(BF16), jnp.where(lane2 < SW_DIM, k_sw, k).astype(BF16)]
    v2 = [jnp.where(lane2 < SW_DIM, v, v_sw).astype(BF16), jnp.where(lane2 < SW_DIM, v_sw, v).astype(BF16)]

    r = _iota((2 * blk, 2 * blk), 0) % blk
    c = _iota((2 * blk, 2 * blk), 1)
    dist = r - (c - blk)
    mask = (dist >= 0) & (dist < WINDOW) & ((c >= blk) | (n > 0))
    top = _iota((2 * blk, 1), 0) < blk
    pairs = range(SW_HEADS // 2)
    group = [(2 * p) // (SW_HEADS // SW_KV_HEADS) for p in pairs]
    npair = len(pairs)
    q = jnp.concatenate([q_ref[:, p * LANES:(p + 1) * LANES].astype(F32) for p in pairs], axis=0)
    cbc = jnp.concatenate([cbc_ref[...]] * npair, axis=0)
    sbc = jnp.concatenate([sbc_ref[...]] * npair, axis=0)
    q = _rope(norm(q, qg_ref[...]), cbc, sbc, 32) * SW_DIM ** -0.5
    low = _iota(q.shape, 1) < SW_DIM
    q_a = jnp.where(low, q, 0.0).astype(BF16)
    q_b = jnp.where(low, 0.0, q).astype(BF16)
    rows = [slice(p * blk, (p + 1) * blk) for p in pairs]
    lhs = [jnp.concatenate([q_a[rows[p]], q_b[rows[p]]], axis=0) for p in pairs]
    logit = [jnp.where(mask, _dot_nt(lhs[p], k2[group[p]]), -1e30) for p in pairs]
    sink = [jnp.where(top, sink_ref[2 * p], sink_ref[2 * p + 1]) for p in pairs]
    m = [jnp.maximum(jnp.max(logit[p], axis=1, keepdims=True), sink[p]) for p in pairs]
    pr = [jnp.exp(logit[p] - m[p]) for p in pairs]
    den = [jnp.sum(pr[p], axis=1, keepdims=True) + jnp.exp(sink[p] - m[p]) for p in pairs]
    o2 = [_dot(pr[p].astype(BF16), v2[group[p]]) / den[p] for p in pairs]
    for p in pairs:
        o_ref[:, p * LANES:(p + 1) * LANES] = jnp.where(lane1 < SW_DIM, o2[p][:blk], o2[p][blk:]).astype(o_ref.dtype)


def sw_attention(u, tables, q_gain, k_gain, sinks, batch, seq, blk=128):
    t = batch * seq
    nb = seq // blk
    _, _, cb, sb = tables
    w = BRANCH_W
    cur = lambda width, col: pl.BlockSpec((blk, width), lambda b, i: (b * nb + i, col // width))
    prev = lambda width, col: pl.BlockSpec((blk, width), lambda b, i: (b * nb + jnp.maximum(i - 1, 0), col // width))
    vec = pl.BlockSpec((1, LANES), lambda b, i: (0, 0))
    tile2 = lambda g: jnp.tile(g.reshape(1, SW_DIM), (1, 2))
    return pl.pallas_call(
        functools.partial(_sw_kernel, blk=blk),
        grid=(batch, nb),
        in_specs=[pl.BlockSpec(memory_space=pltpu.SMEM),
                  cur(w, COL_SWQ), prev(LANES, COL_SWK), cur(LANES, COL_SWK), prev(LANES, COL_SWV), cur(LANES, COL_SWV),
                  prev(LANES, 0), prev(LANES, 0), cur(LANES, 0), cur(LANES, 0), vec, vec],
        out_specs=pl.BlockSpec((blk, w), lambda b, i: (b * nb + i, 0)),
        out_shape=jax.ShapeDtypeStruct((t, w), BF16),
        compiler_params=_cparams(("parallel", "parallel")),
        name="sw_attention",
    )(sinks, u, u, u, u, u, cb, sb, cb, sb, tile2(q_gain), tile2(k_gain))


def _merge_kernel(*refs):
    y_refs = refs[:N_BRANCH]
    g_ref, w_ref, o_ref, acc_ref = refs[N_BRANCH:]
    n = pl.program_id(2)
    for b in range(N_BRANCH):
        @pl.when(n == b)
        def _():
            contrib = _sigmoid(g_ref[...].astype(F32)) * _dot(y_refs[b][...], w_ref[...].astype(BF16))
            if b == 0:
                acc_ref[...] = contrib
            elif b < N_BRANCH - 1:
                acc_ref[...] += contrib
            else:
                o_ref[...] = (acc_ref[...] + contrib).astype(o_ref.dtype)


def merge_branches(ys, u, w_branch, layer, tm=1024, tc=1024):
    t, w = ys[0].shape
    d = w_branch.shape[-1]
    tm = min(tm, t)
    nc = d // tc
    y_spec = pl.BlockSpec((tm, w), lambda i, c, n: (i, 0))
    return pl.pallas_call(
        _merge_kernel,
        grid=(t // tm, nc, N_BRANCH),
        in_specs=[y_spec] * N_BRANCH + [pl.BlockSpec((tm, tc), lambda i, c, n: (i, COL_GATE // tc + n * nc + c)),
                                        _layer_spec((None, w, tc), lambda i, c, n: (n, 0, c), layer)],
        out_specs=pl.BlockSpec((tm, tc), lambda i, c, n: (i, c)),
        out_shape=jax.ShapeDtypeStruct((t, d), BF16),
        scratch_shapes=[pltpu.VMEM((tm, tc), F32)],
        compiler_params=_cparams(("parallel", "arbitrary", "arbitrary")),
        name="merge_branches",
    )(*ys, u, w_branch)


FFN_TM = 1024
FFN_ROWS = (1024, 768, 512, 384, 256, 128)
FFN_TF = 256


def _ffn_kernel(e_ref, rows_ref, x_ref, wg_ref, wu_ref, wd_ref, o_ref):
    rows = rows_ref[pl.program_id(0)]

    @pl.when(pl.program_id(1) == 0)
    def _():
        o_ref[...] = jnp.zeros_like(o_ref)

    for k, size in enumerate(FFN_ROWS):
        below = FFN_ROWS[k + 1] if k + 1 < len(FFN_ROWS) else 0

        @pl.when((rows > below) & (rows <= size))
        def _():
            x = x_ref[:size, :]
            g = _dot(x, wg_ref[0].astype(BF16))
            a = (g * _sigmoid(g) * _dot(x, wu_ref[0].astype(BF16))).astype(BF16)
            o_ref[:size, :] += _dot(a, wd_ref[0].astype(BF16))


def ffn_tiles(xs, tile_expert, tile_rows, w_gate, w_up, w_down):
    r, d = xs.shape
    _, _, ff = w_gate.shape
    ns = r // FFN_TM
    nf = ff // FFN_TF

    def f_eff(s, f, rows):
        return jnp.where(rows[s] > 0, f, nf - 1)

    grid_spec = pltpu.PrefetchScalarGridSpec(
        num_scalar_prefetch=2,
        grid=(ns, nf),
        in_specs=[pl.BlockSpec((FFN_TM, d), lambda s, f, e, rows: (s, 0)),
                  pl.BlockSpec((1, d, FFN_TF), lambda s, f, e, rows: (e[s], 0, f_eff(s, f, rows))),
                  pl.BlockSpec((1, d, FFN_TF), lambda s, f, e, rows: (e[s], 0, f_eff(s, f, rows))),
                  pl.BlockSpec((1, FFN_TF, d), lambda s, f, e, rows: (e[s], f_eff(s, f, rows), 0))],
        out_specs=pl.BlockSpec((FFN_TM, d), lambda s, f, e, rows: (s, 0)),
    )
    return pl.pallas_call(
        _ffn_kernel,
        grid_spec=grid_spec,
        out_shape=jax.ShapeDtypeStruct((r, d), F32),
        compiler_params=_cparams(("parallel", "arbitrary")),
        name="ffn_tiles",
    )(tile_expert, tile_rows, xs, w_gate, w_up, w_down)


def _store_with_norm(x, gain_ref, o_ref, h_ref):
    o_ref[...] = x
    ms = jnp.mean(x * x, axis=-1, keepdims=True)
    h_ref[...] = (x * lax.rsqrt(ms + NORM_EPS) * gain_ref[...]).astype(h_ref.dtype)


def _add_norm_kernel(a_ref, b_ref, g_ref, o_ref, h_ref):
    _store_with_norm(a_ref[...] + b_ref[...], g_ref, o_ref, h_ref)


def add_norm(a, b, gain, tm=512):
    t, d = a.shape
    spec = pl.BlockSpec((tm, d), lambda i: (i, 0))
    return pl.pallas_call(
        _add_norm_kernel, grid=(t // tm,),
        in_specs=[spec, spec, pl.BlockSpec((1, d), lambda i: (0, 0))], out_specs=[spec, spec],
        out_shape=[jax.ShapeDtypeStruct((t, d), a.dtype), jax.ShapeDtypeStruct((t, d), BF16)],
        compiler_params=_cparams(("parallel",)), name="residual_add_norm")(a, b, gain.reshape(1, d))


def dense_ffn(x, h, w_gate, w_up, w_down, index, next_gain):
    t = h.shape[0]
    ns = t // FFN_TM
    ys = ffn_tiles(h, jnp.full((ns,), index, I32), jnp.full((ns,), FFN_TM, I32), w_gate, w_up, w_down)
    return add_norm(x, ys, next_gain)


def _router_kernel(x_ref, g_ref, w_ref, h_ref, r_ref):
    x = x_ref[...]
    h = x * lax.rsqrt(jnp.mean(x * x, axis=-1, keepdims=True) + NORM_EPS) * g_ref[...]
    h_ref[...] = h
    logit = _dot_f32(h, w_ref[...])
    lane = _iota(logit.shape, 1).astype(F32)
    neg = -jnp.inf
    l1 = jnp.where(lane < N_EXPERTS, logit, neg)
    m1 = jnp.max(l1, axis=1, keepdims=True)
    i1 = jnp.min(jnp.where(l1 == m1, lane, float(LANES)), axis=1, keepdims=True)
    l2 = jnp.where(lane == i1, neg, l1)
    m2 = jnp.max(l2, axis=1, keepdims=True)
    i2 = jnp.min(jnp.where(l2 == m2, lane, float(LANES)), axis=1, keepdims=True)
    e = jnp.exp(m2 - m1)
    w1 = 1.0 / (1.0 + e)
    w2 = e / (1.0 + e)
    r_ref[...] = jnp.where(lane == 0, i1, jnp.where(lane == 1, i2, jnp.where(lane == 2, w1,
                           jnp.where(lane == 3, w2, 0.0))))


def route(x, gain, router, tm=256):
    t, d = x.shape
    wp = jnp.zeros((d, LANES), F32).at[:, :N_EXPERTS].set(router)
    return pl.pallas_call(
        _router_kernel,
        grid=(t // tm,),
        in_specs=[pl.BlockSpec((tm, d), lambda i: (i, 0)), pl.BlockSpec((1, d), lambda i: (0, 0)),
                  pl.BlockSpec((d, LANES), lambda i: (0, 0))],
        out_specs=[pl.BlockSpec((tm, d), lambda i: (i, 0)), pl.BlockSpec((tm, LANES), lambda i: (i, 0))],
        out_shape=[jax.ShapeDtypeStruct((t, d), F32), jax.ShapeDtypeStruct((t, LANES), F32)],
        compiler_params=_cparams(("parallel",)),
        name="router",
    )(x, gain.reshape(1, d), wp)


def _row_copy(src_ref, row, buf, i, sem):
    return pltpu.make_async_copy(src_ref.at[pl.ds(row, 1)], buf.at[pl.ds(i, 1)], sem)


ROW_DMA_UNROLL = 8


def _row_loop(n, fn):
    def body(blk, _):
        for u in range(ROW_DMA_UNROLL):
            fn(blk * ROW_DMA_UNROLL + u, u)
        return 0
    lax.fori_loop(0, n // ROW_DMA_UNROLL, body, 0)


GATHER_ROWS = 256


def _gather_kernel(idx_ref, live_ref, src_ref, o_ref, buf, sem):
    gb = GATHER_ROWS
    i = pl.program_id(0)
    n = pl.num_programs(0)

    def start_step(step):
        slot = step % 2

        @pl.when(live_ref[step] > 0)
        def _():
            _row_loop(gb, lambda r, u: _row_copy(src_ref, idx_ref[step * gb + r], buf.at[slot], r,
                                                 sem.at[slot]).start(priority=u % 2))

    @pl.when(i == 0)
    def _():
        buf[...] = jnp.zeros_like(buf)
        start_step(i)

    @pl.when(i + 1 < n)
    def _():
        start_step(i + 1)

    slot = i % 2

    @pl.when(live_ref[i] > 0)
    def _():
        _row_loop(gb, lambda r, u: _row_copy(src_ref, 0, buf.at[slot], r, sem.at[slot]).wait())

    o_ref[...] = buf[slot].astype(o_ref.dtype)


def gather_rows(src, idx, live, out_dtype):
    r = idx.shape[0]
    d = src.shape[1]
    gb = GATHER_ROWS
    grid_spec = pltpu.PrefetchScalarGridSpec(
        num_scalar_prefetch=2,
        grid=(r // gb,),
        in_specs=[pl.BlockSpec(memory_space=pl.ANY)],
        out_specs=pl.BlockSpec((gb, d), lambda i, idx, live: (i, 0)),
        scratch_shapes=[pltpu.VMEM((2, gb, d), src.dtype), pltpu.SemaphoreType.DMA((2,))],
    )
    return pl.pallas_call(
        _gather_kernel,
        grid_spec=grid_spec,
        out_shape=jax.ShapeDtypeStruct((r, d), out_dtype),
        compiler_params=_cparams(("arbitrary",)),
        name="gather_rows",
    )(idx, live, src)


def _combine_kernel(d0_ref, d1_ref, x_ref, w_ref, g_ref, ys_ref, o_ref, h_ref, buf, sem, *, tm):
    i = pl.program_id(0)
    n = pl.num_programs(0)

    def start_step(step):
        slot = step % 2

        def one(r, u):
            _row_copy(ys_ref, d0_ref[step * tm + r], buf.at[slot, 0], r, sem.at[slot]).start(priority=0)
            _row_copy(ys_ref, d1_ref[step * tm + r], buf.at[slot, 1], r, sem.at[slot]).start(priority=1)
        _row_loop(tm, one)

    @pl.when(i == 0)
    def _():
        start_step(i)

    @pl.when(i + 1 < n)
    def _():
        start_step(i + 1)

    slot = i % 2

    def wait_one(r, u):
        _row_copy(ys_ref, 0, buf.at[slot, 0], r, sem.at[slot]).wait()
        _row_copy(ys_ref, 0, buf.at[slot, 1], r, sem.at[slot]).wait()
    _row_loop(tm, wait_one)
    w = w_ref[...]
    _store_with_norm(x_ref[...] + w[:, 2:3] * buf[slot, 0] + w[:, 3:4] * buf[slot, 1], g_ref, o_ref, h_ref)


def combine_rows(x, ys, d0, d1, w, next_gain, tm=128):
    t, d = x.shape
    row = pl.BlockSpec((tm, d), lambda i, a, b: (i, 0))
    grid_spec = pltpu.PrefetchScalarGridSpec(
        num_scalar_prefetch=2,
        grid=(t // tm,),
        in_specs=[row, pl.BlockSpec((tm, LANES), lambda i, a, b: (i, 0)), pl.BlockSpec((1, d), lambda i, a, b: (0, 0)),
                  pl.BlockSpec(memory_space=pl.ANY)],
        out_specs=[row, row],
        scratch_shapes=[pltpu.VMEM((2, 2, tm, d), F32), pltpu.SemaphoreType.DMA((2,))],
    )
    return pl.pallas_call(
        functools.partial(_combine_kernel, tm=tm),
        grid_spec=grid_spec,
        out_shape=[jax.ShapeDtypeStruct((t, d), F32), jax.ShapeDtypeStruct((t, d), BF16)],
        compiler_params=_cparams(("arbitrary",)),
        name="combine_rows",
    )(d0, d1, x, w, next_gain.reshape(1, d), ys)


def moe_ffn(x, gain, router, w_gate, w_up, w_down, first_expert, next_gain):
    t, d = x.shape
    h, rt = route(x, gain, router)
    e_flat = jnp.concatenate([rt[:, 0], rt[:, 1]]).astype(I32)
    tok = jnp.concatenate([jnp.arange(t, dtype=I32)] * 2)
    onehot = (e_flat[:, None] == jnp.arange(N_EXPERTS, dtype=I32)[None, :]).astype(I32)
    csum = jnp.cumsum(onehot, axis=0)
    rank = jnp.take_along_axis(csum - onehot, e_flat[:, None], axis=1)[:, 0]
    counts = csum[-1]
    n_tiles = (counts + FFN_TM - 1) // FFN_TM
    tile_end = jnp.cumsum(n_tiles)
    tile_start = tile_end - n_tiles
    dest = tile_start[e_flat] * FFN_TM + rank
    ns = TOP_K * t // FFN_TM + N_EXPERTS
    s_idx = jnp.arange(ns, dtype=I32)
    used = s_idx < tile_end[-1]
    s_clip = jnp.minimum(s_idx, tile_end[-1] - 1)
    tile_expert = jnp.minimum(jnp.sum((s_clip[:, None] >= tile_end[None, :]).astype(I32), axis=1), N_EXPERTS - 1)
    tile_rows = jnp.clip(counts[tile_expert] - (s_clip - tile_start[tile_expert]) * FFN_TM, 0, FFN_TM)
    tile_rows = jnp.where(used, tile_rows, 0).astype(I32)
    src_row = jnp.zeros((ns * FFN_TM,), I32).at[dest].set(tok)
    per = FFN_TM // GATHER_ROWS
    live = (jnp.arange(ns * per, dtype=I32) % per * GATHER_ROWS < jnp.repeat(tile_rows, per)).astype(I32)

    xs = gather_rows(h, src_row, live, BF16)
    ys = ffn_tiles(xs, tile_expert + first_expert, tile_rows, w_gate, w_up, w_down)
    return combine_rows(x, ys, dest[:t], dest[t:], rt, next_gain)


def _ple_kernel(h_ref, wg_ref, p_ref, wp_ref, x_ref, o_ref):
    gate = _sigmoid(_dot(h_ref[...], wg_ref[...].astype(BF16)))
    proj = _dot(p_ref[...].astype(BF16), wp_ref[...].astype(BF16))
    o_ref[...] = x_ref[...] + gate * proj


def ple(h, w_gate, p, w_proj, x, layer, tm=1024, tn=512):
    t, d = x.shape
    pd = p.shape[-1]
    tm = min(tm, t)
    return pl.pallas_call(
        _ple_kernel,
        grid=(t // tm, d // tn),
        in_specs=[pl.BlockSpec((tm, d), lambda i, j: (i, 0)), _layer_spec((d, tn), lambda i, j: (0, j), layer),
                  _layer_spec((tm, pd), lambda i, j: (i, 0), layer), _layer_spec((pd, tn), lambda i, j: (0, j), layer),
                  pl.BlockSpec((tm, tn), lambda i, j: (i, j))],
        out_specs=pl.BlockSpec((tm, tn), lambda i, j: (i, j)),
        out_shape=jax.ShapeDtypeStruct((t, d), F32),
        compiler_params=_cparams(("parallel", "parallel")),
        name="ple",
    )(h, w_gate, p, w_proj, x)


def _pad_rows(w, start, total):
    return jnp.zeros((total, w.shape[1]), F32).at[start:start + w.shape[0]].set(w).astype(BF16)


def kernel(x, p, positions, w_in, mix_norm, ffn_norm, ple_norm, rw_mu, rw_w0, rw_w2, rw_a0, rw_a2, rw_g2, rw_kk,
           rw_ka, rw_rk, rw_ln_g, rw_ln_b, dsa_q_norm, dsa_k_norm, dsa_kv_norm, dsa_w_uk, dsa_w_uv, sw_q_norm,
           sw_k_norm, sw_sinks, w_branch, w_out, ffn_w_gate, ffn_w_up, ffn_w_down, moe_router, moe_w_gate, moe_w_up,
           moe_w_down, ple_w_gate, ple_w_proj):
    b, s, d = x.shape
    t = b * s
    depth = w_in.shape[0]
    xf = x.reshape(t, d)
    tables = rope_tables(positions)
    row = lambda a: a.reshape(1, -1)
    w_in_t = jnp.swapaxes(w_in, 1, 2)
    w_branch_b = w_branch.astype(BF16)
    p_rows = p.reshape(depth, t, -1)
    experts = lambda w: w.reshape((-1,) + w.shape[2:])
    for i in range(depth):
        h = rmsnorm(xf, mix_norm[i])
        uh = in_proj(h, w_in_t, i, HEAD_TILES, name="in_proj_head")
        ut = in_proj(h, w_in_t, i, TAIL_TILES, name="in_proj_tail")
        y_a = sb_attention(uh, b, s)
        rw = rw_prepare(uh, row(rw_mu[i]), row(rw_w0[i]), row(rw_a0[i]), row(rw_kk[i]), row(rw_ka[i]),
                        _pad_rows(rw_w2[i], 0, RW_LORA), _pad_rows(rw_a2[i], RW_LORA_W, RW_LORA),
                        _pad_rows(rw_g2[i], RW_LORA_W + RW_LORA_A, RW_LORA), b, s)
        y_b = rw_scan(*rw, row(rw_rk[i]), row(rw_ln_g[i]), row(rw_ln_b[i]), b, s)
        y_c = dsa_attention(uh, tables, dsa_q_norm[i], dsa_k_norm[i], dsa_kv_norm[i], dsa_w_uk[i], dsa_w_uv[i], b, s)
        y_d = sw_attention(ut, tables, sw_q_norm[i], sw_k_norm[i], sw_sinks[i], b, s)
        merged = merge_branches([y_a, y_b, y_c, y_d], ut, w_branch_b, i)
        xf = matmul(merged, w_out, i, out_dtype=F32, residual=xf, name="out_proj")
        if i % 2 == 0:
            h = rmsnorm(xf, ffn_norm[i])
            xf, h = dense_ffn(xf, h, ffn_w_gate, ffn_w_up, ffn_w_down, i // 2, ple_norm[i])
        else:
            xf, h = moe_ffn(xf, ffn_norm[i], moe_router[i // 2], experts(moe_w_gate), experts(moe_w_up),
                            experts(moe_w_down), (i // 2) * N_EXPERTS, ple_norm[i])
        xf = ple(h, ple_w_gate, p_rows, ple_w_proj, xf, i)
    return xf.reshape(b, s, d)
```

```python
import functools
import math

import jax
import jax.numpy as jnp
from jax import lax
from jax.experimental import pallas as pl
from jax.experimental.pallas import tpu as pltpu

F32 = jnp.float32
BF16 = jnp.bfloat16
I32 = jnp.int32

D_MODEL = 2048
ROPE_THETA = 10000.0
NORM_EPS = 1e-6
N_BRANCH = 4
BRANCH_W = 1024
SB_HEADS, SB_DIM = 8, 128
RW_HEADS, RW_DIM = 16, 64
RW_LORA_W, RW_LORA_A, RW_LORA_G = 96, 96, 64
RW_LORA = RW_LORA_W + RW_LORA_A + RW_LORA_G
RW_GN_EPS = 64e-5
DSA_HEADS, DSA_DIM, DSA_KV_RANK = 8, 128, 256
IDX_HEADS, IDX_DIM, IDX_TOPK_MAX = 16, 64, 256
SW_HEADS, SW_KV_HEADS, SW_DIM, WINDOW = 16, 2, 64, 128
N_EXPERTS, TOP_K = 8, 2

LANES = 128
VMEM_LIMIT_BYTES = 56 * 1024 * 1024

HEAD_W = 9216
COL_SBQ = 0
COL_SBK = 1024
COL_SBV = 2048
COL_RWR = 3072
COL_RWK = 4096
COL_RWV = 5120
COL_RWL = 6144
COL_DSAQ = 6400
COL_CKV = 7424
COL_IDXQ = 7680
COL_IDXKW = 8704
SRC_SWQ = 8784
SRC_SWK = 9808
SRC_GATE = 10064
COL_GATE = 0
COL_SWQ = 8192
COL_SWK = 9216
COL_SWV = 9344
IN_TN = 512
IN_ALIGN = 16
HEAD_TILES = tuple(range(0, HEAD_W, IN_TN))
TAIL_TILES = (tuple(range(SRC_GATE, SRC_GATE + N_BRANCH * D_MODEL, IN_TN))
              + tuple(range(SRC_SWQ, SRC_SWK, IN_TN)) + (SRC_SWK,))

RW_CHUNK = 64


def _cparams(sem, vmem=VMEM_LIMIT_BYTES):
    return pltpu.CompilerParams(dimension_semantics=sem, vmem_limit_bytes=vmem)


def _dot(a, b):
    return jnp.dot(a, b, preferred_element_type=F32)


def _dot_nt(a, b):
    return lax.dot_general(a, b, (((1,), (1,)), ((), ())), preferred_element_type=F32)


def _split3(a):
    a1 = a.astype(BF16)
    r1 = a - a1.astype(F32)
    a2 = r1.astype(BF16)
    a3 = (r1 - a2.astype(F32)).astype(BF16)
    return a1, a2, a3


def _dot_exact_rhs(a, m_bf16):
    a1, a2, a3 = _split3(a)
    return _dot(a1, m_bf16) + _dot(a2, m_bf16) + _dot(a3, m_bf16)


def _dot_exact_lhs(m_bf16, a):
    a1, a2, a3 = _split3(a)
    return _dot(m_bf16, a1) + _dot(m_bf16, a2) + _dot(m_bf16, a3)


def _dot_f32(a, b):
    a1, a2, a3 = _split3(a)
    b1, b2, b3 = _split3(b)
    return (_dot(a1, b1) + _dot(a1, b2) + _dot(a2, b1)) + (_dot(a1, b3) + _dot(a2, b2) + _dot(a3, b1))


def _softplus(z):
    return jnp.maximum(z, 0.0) + jnp.log(1.0 + jnp.exp(-jnp.abs(z)))


def _sigmoid(z):
    return 1.0 / (1.0 + jnp.exp(-z))


def _iota(shape, dim):
    return lax.broadcasted_iota(I32, shape, dim)


def _group_matrix(n, group, value):
    r = _iota((n, n), 0) // group
    c = _iota((n, n), 1) // group
    return jnp.where(r == c, value, 0.0).astype(BF16)


def _swap_halves(u, half):
    if 2 * half == LANES:
        return pltpu.roll(u, half, 1)
    lane = _iota(u.shape, 1)
    return jnp.where(lane % (2 * half) < half, pltpu.roll(u, LANES - half, 1), pltpu.roll(u, half, 1))


def _rmsnorm_kernel(x_ref, g_ref, o_ref):
    x = x_ref[...]
    ms = jnp.mean(x * x, axis=-1, keepdims=True)
    o_ref[...] = (x * lax.rsqrt(ms + NORM_EPS) * g_ref[...]).astype(o_ref.dtype)


def rmsnorm(x, gain, out_dtype=BF16, tm=512):
    t, d = x.shape
    return pl.pallas_call(
        _rmsnorm_kernel,
        grid=(t // tm,),
        in_specs=[pl.BlockSpec((tm, d), lambda i: (i, 0)), pl.BlockSpec((1, d), lambda i: (0, 0))],
        out_specs=pl.BlockSpec((tm, d), lambda i: (i, 0)),
        out_shape=jax.ShapeDtypeStruct((t, d), out_dtype),
        compiler_params=_cparams(("parallel",)),
        name="rmsnorm",
    )(x, gain.reshape(1, d))


def _mm_kernel(*refs, has_res):
    if has_res:
        a_ref, w_ref, r_ref, o_ref = refs
    else:
        a_ref, w_ref, o_ref = refs
    acc = _dot(a_ref[...].astype(BF16), w_ref[...].astype(BF16))
    if has_res:
        acc = acc + r_ref[...]
    o_ref[...] = acc.astype(o_ref.dtype)


def _layer_spec(block, index_map, layer):
    return pl.BlockSpec((None,) + tuple(block), lambda *idx: (layer,) + tuple(index_map(*idx)))


def matmul(a, w, layer, *, out_dtype, residual=None, n_cols=None, tm=1024, tn=512, name="matmul"):
    m, k = a.shape
    n = w.shape[2] if n_cols is None else n_cols
    tm, tn = min(tm, m), min(tn, n)
    in_specs = [pl.BlockSpec((tm, k), lambda i, j: (i, 0)), _layer_spec((k, tn), lambda i, j: (0, j), layer)]
    args = [a, w]
    if residual is not None:
        in_specs.append(pl.BlockSpec((tm, tn), lambda i, j: (i, j)))
        args.append(residual)
    return pl.pallas_call(
        functools.partial(_mm_kernel, has_res=residual is not None),
        grid=(m // tm, n // tn),
        in_specs=in_specs,
        out_specs=pl.BlockSpec((tm, tn), lambda i, j: (i, j)),
        out_shape=jax.ShapeDtypeStruct((m, n), out_dtype),
        compiler_params=_cparams(("parallel", "parallel")),
        name=name,
    )(*args)


def _in_proj_kernel(off_ref, a_ref, w_ref, o_ref):
    o_ref[...] = _dot_nt(a_ref[...], w_ref[...].astype(BF16)).astype(o_ref.dtype)


def in_proj(a, w_t, layer, offsets, *, tm=2048, tn=IN_TN, name):
    m, k = a.shape
    tm = min(tm, m)
    grid_spec = pltpu.PrefetchScalarGridSpec(
        num_scalar_prefetch=1,
        grid=(m // tm, len(offsets)),
        in_specs=[pl.BlockSpec((tm, k), lambda i, j, off: (i, 0)),
                  pl.BlockSpec((None, pl.Element(tn), pl.Element(k)),
                               lambda i, j, off: (layer, off[j] * IN_ALIGN, 0))],
        out_specs=pl.BlockSpec((tm, tn), lambda i, j, off: (i, j)),
    )
    return pl.pallas_call(
        _in_proj_kernel,
        grid_spec=grid_spec,
        out_shape=jax.ShapeDtypeStruct((m, tn * len(offsets)), BF16),
        compiler_params=_cparams(("parallel", "parallel")),
        name=name,
    )(jnp.asarray([o // IN_ALIGN for o in offsets], I32), a, w_t)


def _rope_table_kernel(pos_ref, f64_ref, f32_ref, ca_ref, sa_ref, cb_ref, sb_ref):
    pos = pos_ref[...]
    lane = _iota((1, LANES), 1)
    ang_a = pos * f64_ref[...]
    ang_b = pos * f32_ref[...]
    ca_ref[...] = jnp.cos(ang_a)
    sa_ref[...] = jnp.where(lane < 64, -1.0, 1.0) * jnp.sin(ang_a)
    cb_ref[...] = jnp.cos(ang_b)
    sb_ref[...] = jnp.where(lane % 64 < 32, -1.0, 1.0) * jnp.sin(ang_b)


def rope_tables(positions, tm=512):
    t = positions.size
    pos = positions.reshape(t, 1).astype(F32)
    inv64 = ROPE_THETA ** (-jnp.arange(64, dtype=F32) / 64)
    inv32 = ROPE_THETA ** (-jnp.arange(32, dtype=F32) / 32)
    f64 = jnp.tile(inv64, 2).reshape(1, LANES)
    f32 = jnp.tile(inv32, 4).reshape(1, LANES)
    row = pl.BlockSpec((tm, LANES), lambda i: (i, 0))
    vec = pl.BlockSpec((1, LANES), lambda i: (0, 0))
    return pl.pallas_call(
        _rope_table_kernel,
        grid=(t // tm,),
        in_specs=[pl.BlockSpec((tm, 1), lambda i: (i, 0)), vec, vec],
        out_specs=[row] * 4,
        out_shape=[jax.ShapeDtypeStruct((t, LANES), F32)] * 4,
        compiler_params=_cparams(("parallel",)),
        name="rope_tables",
    )(pos, f64, f32)


SB_TQ = 256
SB_G = 4


def _sb_kernel(q_ref, k_ref, v_ref, o_ref, *, tq, scale):
    qi = pl.program_id(2)
    r = _iota((tq, tq), 0)
    c = _iota((tq, tq), 1)
    later = jnp.where(r > c, 1.0, 0.0).astype(BF16)
    qs = [(q_ref[:, g * SB_DIM:(g + 1) * SB_DIM].astype(F32) * scale).astype(BF16) for g in range(SB_G)]

    heads = range(SB_G)
    cols = [slice(g * SB_DIM, (g + 1) * SB_DIM) for g in heads]

    def span(j, state, diagonal):
        off = pl.multiple_of(j * tq, tq)
        carry, acc = state
        ks = [k_ref[pl.ds(off, tq), cols[g]].astype(BF16) for g in heads]
        vs = [v_ref[pl.ds(off, tq), cols[g]].astype(BF16) for g in heads]
        zs = [_dot_nt(qs[g], ks[g]) for g in heads]
        lss = [jnp.minimum(z, 0.0) - jnp.log(1.0 + jnp.exp(-jnp.abs(z))) for z in zs]
        lks = [lss[g] - zs[g] for g in heads]
        if diagonal:
            lks = [jnp.where(r > c, lk, 0.0) for lk in lks]
        his = [lk.astype(BF16) for lk in lks]
        los = [(lks[g] - his[g].astype(F32)).astype(BF16) for g in heads]
        css = [_dot(his[g], later) + _dot(los[g], later) for g in heads]
        ws = [jnp.exp(lss[g] + css[g] + carry[g]) for g in heads]
        if diagonal:
            ws = [jnp.where(r > c, w, 0.0) for w in ws]
        acc = tuple(acc[g] + _dot(ws[g].astype(BF16), vs[g]) for g in heads)
        carry = tuple(carry[g] + jnp.sum(lks[g], axis=1, keepdims=True) for g in heads)
        return carry, acc

    init = (tuple(jnp.zeros((tq, 1), F32) for _ in heads), tuple(jnp.zeros((tq, SB_DIM), F32) for _ in heads))
    state = span(qi, init, True)
    _, acc = lax.fori_loop(0, qi, lambda i, s: span(qi - 1 - i, s, False), state)
    for g in heads:
        o_ref[:, cols[g]] = acc[g].astype(o_ref.dtype)


def sb_attention(u, batch, seq):
    t = batch * seq
    tq = min(SB_TQ, seq)
    nq = seq // tq
    gw = SB_G * SB_DIM
    qc, kc, vc = COL_SBQ // gw, COL_SBK // gw, COL_SBV // gw
    return pl.pallas_call(
        functools.partial(_sb_kernel, tq=tq, scale=SB_DIM ** -0.5),
        grid=(batch, SB_HEADS // SB_G, nq),
        in_specs=[
            pl.BlockSpec((tq, gw), lambda b, h, i: (b * nq + i, qc + h)),
            pl.BlockSpec((seq, gw), lambda b, h, i: (b, kc + h)),
            pl.BlockSpec((seq, gw), lambda b, h, i: (b, vc + h)),
        ],
        out_specs=pl.BlockSpec((tq, gw), lambda b, h, i: (b * nq + i, h)),
        out_shape=jax.ShapeDtypeStruct((t, BRANCH_W), BF16),
        compiler_params=_cparams(("parallel", "parallel", "arbitrary")),
        name="sb_attention",
    )(u, u, u)


def _rw_pre_kernel(r_ref, k_ref, v_ref, l_ref, pr_ref, pk_ref, pv_ref, plr_ref,
                   mu_ref, w0_ref, a0_ref, kkg_ref, ka_ref, w2_ref, a2_ref, g2_ref,
                   ro_ref, lw_ref, ko_ref, vo_ref, kk_ref, b_ref, g_ref, *, tm, seq):
    i = pl.program_id(0)
    first = (i * tm) % seq == 0
    row0 = _iota((tm, 1), 0) == 0

    def shifted(cur_ref, prev_ref, lo, hi):
        cur = cur_ref[...].astype(F32)
        last = prev_ref[...].astype(F32)[-1:, :]
        last = jnp.where(first, 0.0, last)
        prev = jnp.where(row0, last, pltpu.roll(cur, 1, 0))
        return cur + mu_ref[:, lo:hi] * (prev - cur)

    w = BRANCH_W
    r = shifted(r_ref, pr_ref, 0, w)
    k = shifted(k_ref, pk_ref, w, 2 * w)
    v = shifted(v_ref, pv_ref, 2 * w, 3 * w)
    z = shifted(l_ref, plr_ref, 3 * w, 3 * w + RW_LORA)

    w_pre = w0_ref[...] + _dot(jnp.tanh(z).astype(BF16), w2_ref[...])
    w_log = -_softplus(-w_pre) - 0.5
    lw_ref[...] = -jnp.exp(w_log)
    a = _sigmoid(a0_ref[...] + _dot(z.astype(BF16), a2_ref[...]))
    g_ref[...] = _dot(_sigmoid(z).astype(BF16), g2_ref[...]).astype(g_ref.dtype)
    ones = _group_matrix(LANES, RW_DIM, 1.0)
    kk = k * kkg_ref[...]
    for s in range(w // LANES):
        sl = slice(s * LANES, (s + 1) * LANES)
        kks = kk[:, sl]
        ss = _dot_exact_rhs(kks * kks, ones)
        kkn = kks / jnp.maximum(jnp.sqrt(ss), 1e-12)
        kk_ref[:, sl] = kkn
        b_ref[:, sl] = kkn * a[:, sl]
    ro_ref[...] = r
    ko_ref[...] = k * (1.0 + (a - 1.0) * ka_ref[...])
    vo_ref[...] = v


def rw_prepare(u, mu, w0, a0, k_k, k_a, w2p, a2p, g2p, batch, seq, tm=256):
    t = batch * seq
    tm = min(tm, seq)
    w = BRANCH_W
    sub = 16

    def cur(width, col):
        return pl.BlockSpec((tm, width), lambda i: (i, col // width))

    def prev(width, col):
        return pl.BlockSpec((sub, width), lambda i: (jnp.maximum(i * (tm // sub) - 1, 0), col // width))

    def vec(width):
        return pl.BlockSpec((1, width), lambda i: (0, 0))

    def mat():
        return pl.BlockSpec((RW_LORA, w), lambda i: (0, 0))

    out = pl.BlockSpec((tm, w), lambda i: (i, 0))
    f = jax.ShapeDtypeStruct((t, w), F32)
    return pl.pallas_call(
        functools.partial(_rw_pre_kernel, tm=tm, seq=seq),
        grid=(t // tm,),
        in_specs=[cur(w, COL_RWR), cur(w, COL_RWK), cur(w, COL_RWV), cur(RW_LORA, COL_RWL),
                  prev(w, COL_RWR), prev(w, COL_RWK), prev(w, COL_RWV), prev(RW_LORA, COL_RWL),
                  vec(3 * w + RW_LORA), vec(w), vec(w), vec(w), vec(w), mat(), mat(), mat()],
        out_specs=[out] * 7,
        out_shape=[f, f, f, f, f, f, jax.ShapeDtypeStruct((t, w), BF16)],
        compiler_params=_cparams(("parallel",)),
        name="rw_prepare",
    )(u, u, u, u, u, u, u, u, mu, w0, a0, k_k, k_a, w2p, a2p, g2p)


def _rw_scan_kernel(r_ref, lw_ref, k_ref, v_ref, kk_ref, b_ref, g_ref, rk_ref, lng_ref, lnb_ref,
                    o_ref, st_ref, *, chunk):
    @pl.when(pl.program_id(1) == 0)
    def _():
        st_ref[...] = jnp.zeros_like(st_ref)

    npair = RW_HEADS // 2
    w = npair * LANES
    n = 2 * chunk
    lane = _iota((n, w), 1)
    row = _iota((n, w), 0)
    own = ((lane // RW_DIM) % 2) == (row // chunk)
    rr = _iota((n, n), 0)
    cc = _iota((n, n), 1)
    same = (rr // chunk) == (cc // chunk)
    tri_incl = jnp.where(same & (rr >= cc), 1.0, 0.0).astype(BF16)
    strict = same & (rr > cc)
    incl = same & (rr >= cc)
    eye = jnp.where(rr == cc, 1.0, 0.0)
    blockdiag = (_iota((LANES, LANES), 0) // RW_DIM) == (_iota((LANES, LANES), 1) // RW_DIM)
    avg = _group_matrix(LANES, RW_DIM, 1.0 / RW_DIM)
    ones = _group_matrix(LANES, RW_DIM, 1.0)
    pairs = range(npair)
    sl = [slice(p * LANES, (p + 1) * LANES) for p in pairs]

    def stack(x):
        return jnp.concatenate([x, x], axis=0)

    def per_pair_rows(x):
        return jnp.concatenate([x[:, s] for s in sl], axis=0)

    def per_pair_lanes(x):
        return jnp.concatenate([x[p * chunk:(p + 1) * chunk] for p in pairs], axis=1)

    r, lw, k, v, kk, b = (ref[...] for ref in (r_ref, lw_ref, k_ref, v_ref, kk_ref, b_ref))
    lw2 = stack(lw)
    l1 = lw2.astype(BF16)
    l2 = (lw2 - l1.astype(F32)).astype(BF16)
    lin = _dot(tri_incl, l1) + _dot(tri_incl, l2)
    lend = lin[n - 1:n]
    p_inv = jnp.exp(-lin)
    p_dec = jnp.exp(lend - lin)
    zero = jnp.zeros((n, w), F32)
    kk_t = jnp.where(own, stack(kk) * jnp.exp(lin - lw2), zero).astype(BF16)
    r_t = jnp.where(own, stack(r) * jnp.exp(lin), zero).astype(BF16)
    v_f = jnp.where(own, stack(v), zero)
    v_s = v_f.astype(BF16)
    b_t = (stack(b) * p_inv).astype(BF16)
    k_t = (stack(k) * p_inv).astype(BF16)
    b_d = (stack(b) * p_dec).astype(BF16)
    k_d = (stack(k) * p_dec).astype(BF16)
    p_end = jnp.exp(lend)

    lhs = [jnp.concatenate([kk_t[:, s], r_t[:, s]], axis=0) for s in sl]
    a_b = [_dot_nt(lhs[p], b_t[:, sl[p]]) for p in pairs]
    a_k = [_dot_nt(lhs[p], k_t[:, sl[p]]) for p in pairs]
    s_t = [_dot_nt(lhs[p], st_ref[p].astype(BF16)) for p in pairs]
    n_ab = [jnp.where(strict, a_b[p][:n], 0.0) for p in pairs]
    rhs = [s_t[p][:n] + _dot(jnp.where(strict, a_k[p][:n], 0.0).astype(BF16), v_s[:, sl[p]]) for p in pairs]
    inv = [eye - n_ab[p] for p in pairs]
    pw = n_ab
    for _ in range(int(math.log2(chunk)) - 1):
        pw_b = [pw[p].astype(BF16) for p in pairs]
        pw = [_dot(pw_b[p], pw_b[p]) for p in pairs]
        inv = [inv[p] + _dot(inv[p].astype(BF16), pw[p].astype(BF16)) for p in pairs]
    u_s = [-_dot(inv[p].astype(BF16), rhs[p].astype(BF16)) for p in pairs]
    y2 = [s_t[p][n:] + _dot(jnp.where(incl, a_b[p][n:], 0.0).astype(BF16), u_s[p].astype(BF16))
          + _dot(jnp.where(incl, a_k[p][n:], 0.0).astype(BF16), v_s[:, sl[p]]) for p in pairs]
    for p in pairs:
        upd = _dot(u_s[p].T.astype(BF16), b_d[:, sl[p]]) + _dot(v_f[:, sl[p]].T.astype(BF16), k_d[:, sl[p]])
        st_ref[p] = st_ref[p] * p_end[:, sl[p]] + jnp.where(blockdiag, upd, 0.0)

    yr = jnp.concatenate([y2[p][:chunk] + y2[p][chunk:] for p in pairs], axis=0)
    yr1, yr2, _ = _split3(yr)
    d = yr - (_dot(yr1, avg) + _dot(yr2, avg))
    dd1, dd2, _ = _split3(d * d)
    var = _dot(dd1, avg) + _dot(dd2, avg)
    yn = per_pair_lanes(d * lax.rsqrt(var + RW_GN_EPS)) * lng_ref[...] + lnb_ref[...]
    bonus = per_pair_lanes(_dot(per_pair_rows(r * k * rk_ref[...]).astype(BF16), ones)) * v
    o_ref[...] = ((yn + bonus) * g_ref[...].astype(F32)).astype(o_ref.dtype)


def rw_scan(r, lw, k, v, kk, b, g, r_k, ln_g, ln_b, batch, seq):
    t = batch * seq
    chunk = min(RW_CHUNK, seq)
    nc = seq // chunk
    w = BRANCH_W
    blk = pl.BlockSpec((chunk, w), lambda bb, c: (bb * nc + c, 0))
    vec = pl.BlockSpec((1, w), lambda bb, c: (0, 0))
    return pl.pallas_call(
        functools.partial(_rw_scan_kernel, chunk=chunk),
        grid=(batch, nc),
        in_specs=[blk] * 7 + [vec] * 3,
        out_specs=blk,
        out_shape=jax.ShapeDtypeStruct((t, w), BF16),
        scratch_shapes=[pltpu.VMEM((RW_HEADS // 2, LANES, LANES), F32)],
        compiler_params=_cparams(("parallel", "arbitrary")),
        name="rw_scan",
    )(r, lw, k, v, kk, b, g, r_k, ln_g, ln_b)


def _rope(u, cos, sin, half):
    return u * cos + _swap_halves(u, half) * sin


DSA_Q_BLK = 256
DSA_IQ_BLK = 512


def _dsa_pre_kernel(*refs):
    nq, niq = BRANCH_W // DSA_Q_BLK, BRANCH_W // DSA_IQ_BLK
    q_refs, refs = refs[:nq], refs[nq:]
    c_ref, refs = refs[0], refs[1:]
    iq_refs, refs = refs[:niq], refs[niq:]
    (ikw_ref, ca_ref, sa_ref, cb_ref, sb_ref, qg_ref, kg_ref, cg_ref, wuk_ref, wuv_ref,
     qo_ref, ko_ref, vo_ref, iqo_ref, iko_ref, iwo_ref) = refs
    ca, sa, cb, sb = ca_ref[...], sa_ref[...], cb_ref[...], sb_ref[...]
    for h in range(DSA_HEADS):
        sl = slice(h * DSA_DIM, (h + 1) * DSA_DIM)
        per = DSA_Q_BLK // DSA_DIM
        q = q_refs[h // per][:, (h % per) * DSA_DIM:(h % per + 1) * DSA_DIM].astype(F32)
        q = q * lax.rsqrt(jnp.mean(q * q, axis=-1, keepdims=True) + NORM_EPS) * qg_ref[...]
        qo_ref[:, sl] = (_rope(q, ca, sa, 64) * DSA_DIM ** -0.5).astype(qo_ref.dtype)
    c = c_ref[...].astype(F32)
    c = (c * lax.rsqrt(jnp.mean(c * c, axis=-1, keepdims=True) + NORM_EPS) * cg_ref[...]).astype(BF16)
    k = _dot(c, wuk_ref[...].astype(BF16))
    k = k * lax.rsqrt(jnp.mean(k * k, axis=-1, keepdims=True) + NORM_EPS) * kg_ref[...]
    ko_ref[...] = _rope(k, ca, sa, 64).astype(ko_ref.dtype)
    vo_ref[...] = _dot(c, wuv_ref[...].astype(BF16)).astype(vo_ref.dtype)
    for s in range(IDX_HEADS * IDX_DIM // LANES):
        sl = slice(s * LANES, (s + 1) * LANES)
        per = DSA_IQ_BLK // LANES
        iq = iq_refs[s // per][:, (s % per) * LANES:(s % per + 1) * LANES].astype(F32)
        iqo_ref[:, sl] = _rope(iq, cb, sb, 32).astype(iqo_ref.dtype)
    ikw = ikw_ref[...].astype(F32)
    lane = _iota(ikw.shape, 1)
    ik = _rope(ikw, cb, sb, 32)
    iko_ref[...] = jnp.where(lane < IDX_DIM, ik, pltpu.roll(ik, IDX_DIM, 1)).astype(iko_ref.dtype)
    iw = pltpu.roll(ikw, IDX_DIM, 1) * (IDX_HEADS ** -0.5 * IDX_DIM ** -0.5)
    iwo_ref[...] = jnp.where(lane < IDX_HEADS, iw, 0.0)


def _dsa_kernel(q_ref, iq_ref, iw_ref, k_ref, v_ref, ik_ref, o_ref, key_ref, bias_ref, vt_ref, m_ref, l_ref, acc_ref,
                *, tq, seq, top_k):
    kb = tq
    qb = pl.program_id(1)
    nblk = qb + 1
    nh = DSA_HEADS
    int_min = jnp.int32(-2 ** 31)
    kidx = _iota((kb, tq), 0)
    ridx = _iota((kb, tq), 1)

    def causal_mask(j):
        return (kidx + j * kb) <= (ridx + qb * tq)

    def block(ref, j):
        return ref[pl.ds(pl.multiple_of(j * kb, kb), kb), :]

    def fold(x, op):
        return op(x.reshape(kb // 8, 8, x.shape[1]), axis=0)

    @pl.when(qb == 0)
    def _():
        for j in range(seq // LANES):
            vt_ref[:, j * LANES:(j + 1) * LANES] = v_ref[j * LANES:(j + 1) * LANES, :].astype(F32).T.astype(BF16)

    first = _iota((tq, LANES), 1) < IDX_DIM
    lhs = []
    for p in range(IDX_HEADS // 2):
        qp = iq_ref[:, p * LANES:(p + 1) * LANES]
        zero = jnp.zeros_like(qp)
        lhs.append(jnp.concatenate([jnp.where(first, qp, zero), jnp.where(first, zero, qp)], axis=0))
    iw_t = iw_ref[...].T
    w_row = [iw_t[h:h + 1] for h in range(IDX_HEADS)]

    def score_block(j, _):
        ik = block(ik_ref, j)
        sc = jnp.zeros((kb, tq), F32)
        for p in range(IDX_HEADS // 2):
            z = jnp.maximum(_dot_nt(ik, lhs[p]), 0.0)
            sc = sc + z[:, :tq] * w_row[2 * p] + z[:, tq:] * w_row[2 * p + 1]
        sc = sc + 0.0
        bits = lax.bitcast_convert_type(sc, I32)
        skey = bits ^ ((bits >> 31) & jnp.int32(0x7FFFFFFF))
        key_ref[pl.ds(pl.multiple_of(j * kb, kb), kb), :] = jnp.where(causal_mask(j), skey, int_min)
        return 0

    lax.fori_loop(0, nblk, score_block, 0)

    def count(pred_fn):
        def body(j, cnt):
            return cnt + fold(jnp.where(pred_fn(block(key_ref, j)), 1.0, 0.0), jnp.sum)
        return jnp.sum(lax.fori_loop(0, nblk, body, jnp.zeros((8, tq), F32)), axis=0, keepdims=True)

    def bit_step(i, thr):
        cand = thr ^ (jnp.int32(1) << (31 - i))
        return jnp.where(count(lambda keys: keys >= cand) >= top_k, cand, thr)

    thr = lax.fori_loop(0, 32, bit_step, jnp.full((1, tq), int_min, I32))
    need = top_k - count(lambda keys: keys > thr)

    q_all = jnp.concatenate([q_ref[:, h * DSA_DIM:(h + 1) * DSA_DIM] for h in range(nh)], axis=0)
    lower = jnp.where(_iota((kb, kb), 0) > _iota((kb, kb), 1), 1.0, 0.0).astype(BF16)
    m_ref[...] = jnp.full_like(m_ref, -1e30)

    def logits(j, bias):
        lt = _dot_nt(block(k_ref, j), q_all)
        return jnp.concatenate([lt[:, h * tq:(h + 1) * tq] + bias for h in range(nh)], axis=1)

    def select(j, ties_seen):
        keys = block(key_ref, j)
        causal = causal_mask(j)
        tie = causal & (keys == thr)
        tie_f = jnp.where(tie, 1.0, 0.0)
        rank = ties_seen + _dot(lower, tie_f.astype(BF16))
        sel = causal & ((keys > thr) | (tie & (rank < need)))
        bias = jnp.where(sel, 0.0, -1e30)
        bias_ref[pl.ds(pl.multiple_of(j * kb, kb), kb), :] = bias
        m_ref[...] = jnp.maximum(m_ref[...], fold(logits(j, bias), jnp.max))
        return ties_seen + jnp.sum(tie_f, axis=0, keepdims=True)

    lax.fori_loop(0, nblk, select, jnp.zeros((1, tq), F32))
    m_all = jnp.broadcast_to(jnp.max(m_ref[...], axis=0, keepdims=True), m_ref.shape)
    l_ref[...] = jnp.zeros_like(l_ref)
    acc_ref[...] = jnp.zeros_like(acc_ref)

    def attend(j, _):
        lt = logits(j, block(bias_ref, j))
        pr = jnp.exp(lt.reshape(kb // 8, 8, nh * tq) - m_all[None]).reshape(kb, nh * tq)
        l_ref[...] += fold(pr, jnp.sum)
        acc_ref[...] += _dot(vt_ref[:, pl.ds(pl.multiple_of(j * kb, kb), kb)], pr.astype(BF16))
        return 0

    lax.fori_loop(0, nblk, attend, 0)
    out_t = acc_ref[...] / jnp.sum(l_ref[...], axis=0, keepdims=True)
    for h in range(nh):
        o_ref[:, h * DSA_DIM:(h + 1) * DSA_DIM] = out_t[:, h * tq:(h + 1) * tq].T.astype(o_ref.dtype)


def dsa_attention(u, tables, q_gain, k_gain, kv_gain, w_uk, w_uv, batch, seq, tm=256, tq=256):
    t = batch * seq
    tm = min(tm, seq)
    ca, sa, cb, sb = tables
    w = BRANCH_W
    row = lambda width, col: pl.BlockSpec((tm, width), lambda i: (i, col // width))
    tab = pl.BlockSpec((tm, LANES), lambda i: (i, 0))
    vec = lambda width: pl.BlockSpec((1, width), lambda i: (0, 0))
    mat = pl.BlockSpec((DSA_KV_RANK, DSA_DIM), lambda i: (0, 0))
    o_w = pl.BlockSpec((tm, w), lambda i: (i, 0))
    o_n = pl.BlockSpec((tm, LANES), lambda i: (i, 0))
    q, k, v, iq, ik, iw = pl.pallas_call(
        _dsa_pre_kernel,
        grid=(t // tm,),
        in_specs=[row(DSA_Q_BLK, COL_DSAQ + j * DSA_Q_BLK) for j in range(w // DSA_Q_BLK)]
        + [row(DSA_KV_RANK, COL_CKV)]
        + [row(DSA_IQ_BLK, COL_IDXQ + j * DSA_IQ_BLK) for j in range(w // DSA_IQ_BLK)]
        + [row(LANES, COL_IDXKW), tab, tab, tab, tab, vec(DSA_DIM), vec(DSA_DIM), vec(DSA_KV_RANK), mat, mat],
        out_specs=[o_w, o_n, o_n, o_w, o_n, o_n],
        out_shape=[jax.ShapeDtypeStruct((t, w), BF16), jax.ShapeDtypeStruct((t, LANES), BF16),
                   jax.ShapeDtypeStruct((t, LANES), BF16), jax.ShapeDtypeStruct((t, w), BF16),
                   jax.ShapeDtypeStruct((t, LANES), BF16), jax.ShapeDtypeStruct((t, LANES), F32)],
        compiler_params=_cparams(("parallel",)),
        name="dsa_prepare",
    )(*([u] * (w // DSA_Q_BLK + 1 + w // DSA_IQ_BLK + 1)), ca, sa, cb, sb,
      q_gain.reshape(1, -1), k_gain.reshape(1, -1), kv_gain.reshape(1, -1), w_uk, w_uv)

    nq = seq // tq
    top_k = min(IDX_TOPK_MAX, seq // 4)
    qrow = lambda width: pl.BlockSpec((tq, width), lambda b, i: (b * nq + i, 0))
    full = pl.BlockSpec((seq, LANES), lambda b, i: (b, 0))
    return pl.pallas_call(
        functools.partial(_dsa_kernel, tq=tq, seq=seq, top_k=top_k),
        grid=(batch, nq),
        in_specs=[qrow(w), qrow(w), qrow(LANES), full, full, full],
        out_specs=qrow(w),
        out_shape=jax.ShapeDtypeStruct((t, w), BF16),
        scratch_shapes=[pltpu.VMEM((seq, tq), I32), pltpu.VMEM((seq, tq), F32), pltpu.VMEM((DSA_DIM, seq), BF16),
                        pltpu.VMEM((8, DSA_HEADS * tq), F32), pltpu.VMEM((8, DSA_HEADS * tq), F32),
                        pltpu.VMEM((DSA_DIM, DSA_HEADS * tq), F32)],
        compiler_params=_cparams(("parallel", "arbitrary")),
        name="dsa_attention",
    )(q, iq, iw, k, v, ik)


def _sw_kernel(sink_ref, q_ref, kp_ref, kc_ref, vp_ref, vc_ref, cbp_ref, sbp_ref, cbc_ref, sbc_ref,
               qg_ref, kg_ref, o_ref, *, blk):
    n = pl.program_id(1)
    avg = _group_matrix(LANES, SW_DIM, 1.0 / SW_DIM)
    lane2 = _iota((2 * blk, LANES), 1)
    lane1 = _iota((blk, LANES), 1)

    def norm(x, gain):
        s1, s2, _ = _split3(x * x)
        return x * lax.rsqrt(_dot(s1, avg) + _dot(s2, avg) + NORM_EPS) * gain

    cb = jnp.concatenate([cbp_ref[...], cbc_ref[...]], axis=0)
    sb = jnp.concatenate([sbp_ref[...], sbc_ref[...]], axis=0)
    k = jnp.concatenate([kp_ref[...], kc_ref[...]], axis=0).astype(F32)
    k = _rope(norm(k, kg_ref[...]), cb, sb, 32)
    v = jnp.concatenate([vp_ref[...], vc_ref[...]], axis=0).astype(F32)
    k_sw, v_sw = pltpu.roll(k, SW_DIM, 1), pltpu.roll(v, SW_DIM, 1)
    k2 = [jnp.where(lane2 < SW_DIM, k, k_sw).astype(BF16), jnp.where(lane2 < SW_DIM, k_sw, k).astype(BF16)]
    v2 = [jnp.where(lane2 < SW_DIM, v, v_sw).astype(BF16), jnp.where(lane2 < SW_DIM, v_sw, v).astype(BF16)]

    r = _iota((2 * blk, 2 * blk), 0) % blk
    c = _iota((2 * blk, 2 * blk), 1)
    dist = r - (c - blk)
    mask = (dist >= 0) & (dist < WINDOW) & ((c >= blk) | (n > 0))
    top = _iota((2 * blk, 1), 0) < blk
    pairs = range(SW_HEADS // 2)
    group = [(2 * p) // (SW_HEADS // SW_KV_HEADS) for p in pairs]
    npair = len(pairs)
    q = jnp.concatenate([q_ref[:, p * LANES:(p + 1) * LANES].astype(F32) for p in pairs], axis=0)
    cbc = jnp.concatenate([cbc_ref[...]] * npair, axis=0)
    sbc = jnp.concatenate([sbc_ref[...]] * npair, axis=0)
    q = _rope(norm(q, qg_ref[...]), cbc, sbc, 32) * SW_DIM ** -0.5
    low = _iota(q.shape, 1) < SW_DIM
    q_a = jnp.where(low, q, 0.0).astype(BF16)
    q_b = jnp.where(low, 0.0, q).astype(BF16)
    rows = [slice(p * blk, (p + 1) * blk) for p in pairs]
    lhs = [jnp.concatenate([q_a[rows[p]], q_b[rows[p]]], axis=0) for p in pairs]
    logit = [jnp.where(mask, _dot_nt(lhs[p], k2[group[p]]), -1e30) for p in pairs]
    sink = [jnp.where(top, sink_ref[2 * p], sink_ref[2 * p + 1]) for p in pairs]
    m = [jnp.maximum(jnp.max(logit[p], axis=1, keepdims=True), sink[p]) for p in pairs]
    pr = [jnp.exp(logit[p] - m[p]) for p in pairs]
    den = [jnp.sum(pr[p], axis=1, keepdims=True) + jnp.exp(sink[p] - m[p]) for p in pairs]
    o2 = [_dot(pr[p].astype(BF16), v2[group[p]]) / den[p] for p in pairs]
    for p in pairs:
        o_ref[:, p * LANES:(p + 1) * LANES] = jnp.where(lane1 < SW_DIM, o2[p][:blk], o2[p][blk:]).astype(o_ref.dtype)


def sw_attention(u, tables, q_gain, k_gain, sinks, batch, seq, blk=128):
    t = batch * seq
    nb = seq // blk
    _, _, cb, sb = tables
    w = BRANCH_W
    cur = lambda width, col: pl.BlockSpec((blk, width), lambda b, i: (b * nb + i, col // width))
    prev = lambda width, col: pl.BlockSpec((blk, width), lambda b, i: (b * nb + jnp.maximum(i - 1, 0), col // width))
    vec = pl.BlockSpec((1, LANES), lambda b, i: (0, 0))
    tile2 = lambda g: jnp.tile(g.reshape(1, SW_DIM), (1, 2))
    return pl.pallas_call(
        functools.partial(_sw_kernel, blk=blk),
        grid=(batch, nb),
        in_specs=[pl.BlockSpec(memory_space=pltpu.SMEM),
                  cur(w, COL_SWQ), prev(LANES, COL_SWK), cur(LANES, COL_SWK), prev(LANES, COL_SWV), cur(LANES, COL_SWV),
                  prev(LANES, 0), prev(LANES, 0), cur(LANES, 0), cur(LANES, 0), vec, vec],
        out_specs=pl.BlockSpec((blk, w), lambda b, i: (b * nb + i, 0)),
        out_shape=jax.ShapeDtypeStruct((t, w), BF16),
        compiler_params=_cparams(("parallel", "parallel")),
        name="sw_attention",
    )(sinks, u, u, u, u, u, cb, sb, cb, sb, tile2(q_gain), tile2(k_gain))


def _merge_kernel(*refs):
    y_refs = refs[:N_BRANCH]
    g_ref, w_ref, o_ref, acc_ref = refs[N_BRANCH:]
    n = pl.program_id(2)
    for b in range(N_BRANCH):
        @pl.when(n == b)
        def _():
            contrib = _sigmoid(g_ref[...].astype(F32)) * _dot(y_refs[b][...], w_ref[...].astype(BF16))
            if b == 0:
                acc_ref[...] = contrib
            elif b < N_BRANCH - 1:
                acc_ref[...] += contrib
            else:
                o_ref[...] = (acc_ref[...] + contrib).astype(o_ref.dtype)


def merge_branches(ys, u, w_branch, layer, tm=1024, tc=1024):
    t, w = ys[0].shape
    d = w_branch.shape[-1]
    tm = min(tm, t)
    nc = d // tc
    y_spec = pl.BlockSpec((tm, w), lambda i, c, n: (i, 0))
    return pl.pallas_call(
        _merge_kernel,
        grid=(t // tm, nc, N_BRANCH),
        in_specs=[y_spec] * N_BRANCH + [pl.BlockSpec((tm, tc), lambda i, c, n: (i, COL_GATE // tc + n * nc + c)),
                                        _layer_spec((None, w, tc), lambda i, c, n: (n, 0, c), layer)],
        out_specs=pl.BlockSpec((tm, tc), lambda i, c, n: (i, c)),
        out_shape=jax.ShapeDtypeStruct((t, d), BF16),
        scratch_shapes=[pltpu.VMEM((tm, tc), F32)],
        compiler_params=_cparams(("parallel", "arbitrary", "arbitrary")),
        name="merge_branches",
    )(*ys, u, w_branch)


FFN_TM = 1024
FFN_ROWS = (1024, 768, 512, 384, 256, 128)
FFN_TF = 512
FFN_VMEM_BYTES = 60 * 1024 * 1024


def _ffn_kernel(e_ref, rows_ref, x_ref, wg_ref, wu_ref, wd_ref, o_ref):
    rows = rows_ref[pl.program_id(0)]

    @pl.when(pl.program_id(1) == 0)
    def _():
        o_ref[...] = jnp.zeros_like(o_ref)

    for k, size in enumerate(FFN_ROWS):
        below = FFN_ROWS[k + 1] if k + 1 < len(FFN_ROWS) else 0

        @pl.when((rows > below) & (rows <= size))
        def _():
            x = x_ref[:size, :]
            g = _dot(x, wg_ref[0].astype(BF16))
            a = (g * _sigmoid(g) * _dot(x, wu_ref[0].astype(BF16))).astype(BF16)
            o_ref[:size, :] += _dot(a, wd_ref[0].astype(BF16))


def ffn_tiles(xs, tile_expert, tile_rows, w_gate, w_up, w_down):
    r, d = xs.shape
    _, _, ff = w_gate.shape
    ns = r // FFN_TM
    nf = ff // FFN_TF

    def f_eff(s, f, rows):
        return jnp.where(rows[s] > 0, f, nf - 1)

    grid_spec = pltpu.PrefetchScalarGridSpec(
        num_scalar_prefetch=2,
        grid=(ns, nf),
        in_specs=[pl.BlockSpec((FFN_TM, d), lambda s, f, e, rows: (s, 0)),
                  pl.BlockSpec((1, d, FFN_TF), lambda s, f, e, rows: (e[s], 0, f_eff(s, f, rows))),
                  pl.BlockSpec((1, d, FFN_TF), lambda s, f, e, rows: (e[s], 0, f_eff(s, f, rows))),
                  pl.BlockSpec((1, FFN_TF, d), lambda s, f, e, rows: (e[s], f_eff(s, f, rows), 0))],
        out_specs=pl.BlockSpec((FFN_TM, d), lambda s, f, e, rows: (s, 0)),
    )
    return pl.pallas_call(
        _ffn_kernel,
        grid_spec=grid_spec,
        out_shape=jax.ShapeDtypeStruct((r, d), F32),
        compiler_params=_cparams(("parallel", "arbitrary"), FFN_VMEM_BYTES),
        name="ffn_tiles",
    )(tile_expert, tile_rows, xs, w_gate, w_up, w_down)


def _store_with_norm(x, gain_ref, o_ref, h_ref):
    o_ref[...] = x
    ms = jnp.mean(x * x, axis=-1, keepdims=True)
    h_ref[...] = (x * lax.rsqrt(ms + NORM_EPS) * gain_ref[...]).astype(h_ref.dtype)


def _add_norm_kernel(a_ref, b_ref, g_ref, o_ref, h_ref):
    _store_with_norm(a_ref[...] + b_ref[...], g_ref, o_ref, h_ref)


def add_norm(a, b, gain, tm=512):
    t, d = a.shape
    spec = pl.BlockSpec((tm, d), lambda i: (i, 0))
    return pl.pallas_call(
        _add_norm_kernel, grid=(t // tm,),
        in_specs=[spec, spec, pl.BlockSpec((1, d), lambda i: (0, 0))], out_specs=[spec, spec],
        out_shape=[jax.ShapeDtypeStruct((t, d), a.dtype), jax.ShapeDtypeStruct((t, d), BF16)],
        compiler_params=_cparams(("parallel",)), name="residual_add_norm")(a, b, gain.reshape(1, d))


def dense_ffn(x, h, w_gate, w_up, w_down, index, next_gain):
    t = h.shape[0]
    ns = t // FFN_TM
    ys = ffn_tiles(h, jnp.full((ns,), index, I32), jnp.full((ns,), FFN_TM, I32), w_gate, w_up, w_down)
    return add_norm(x, ys, next_gain)


def _router_kernel(x_ref, g_ref, w_ref, h_ref, r_ref):
    x = x_ref[...]
    h = x * lax.rsqrt(jnp.mean(x * x, axis=-1, keepdims=True) + NORM_EPS) * g_ref[...]
    h_ref[...] = h
    logit = _dot_f32(h, w_ref[...])
    lane = _iota(logit.shape, 1).astype(F32)
    neg = -jnp.inf
    l1 = jnp.where(lane < N_EXPERTS, logit, neg)
    m1 = jnp.max(l1, axis=1, keepdims=True)
    i1 = jnp.min(jnp.where(l1 == m1, lane, float(LANES)), axis=1, keepdims=True)
    l2 = jnp.where(lane == i1, neg, l1)
    m2 = jnp.max(l2, axis=1, keepdims=True)
    i2 = jnp.min(jnp.where(l2 == m2, lane, float(LANES)), axis=1, keepdims=True)
    e = jnp.exp(m2 - m1)
    w1 = 1.0 / (1.0 + e)
    w2 = e / (1.0 + e)
    r_ref[...] = jnp.where(lane == 0, i1, jnp.where(lane == 1, i2, jnp.where(lane == 2, w1,
                           jnp.where(lane == 3, w2, 0.0))))


def route(x, gain, router, tm=256):
    t, d = x.shape
    wp = jnp.zeros((d, LANES), F32).at[:, :N_EXPERTS].set(router)
    return pl.pallas_call(
        _router_kernel,
        grid=(t // tm,),
        in_specs=[pl.BlockSpec((tm, d), lambda i: (i, 0)), pl.BlockSpec((1, d), lambda i: (0, 0)),
                  pl.BlockSpec((d, LANES), lambda i: (0, 0))],
        out_specs=[pl.BlockSpec((tm, d), lambda i: (i, 0)), pl.BlockSpec((tm, LANES), lambda i: (i, 0))],
        out_shape=[jax.ShapeDtypeStruct((t, d), F32), jax.ShapeDtypeStruct((t, LANES), F32)],
        compiler_params=_cparams(("parallel",)),
        name="router",
    )(x, gain.reshape(1, d), wp)


def _row_copy(src_ref, row, buf, i, sem):
    return pltpu.make_async_copy(src_ref.at[pl.ds(row, 1)], buf.at[pl.ds(i, 1)], sem)


ROW_DMA_UNROLL = 8


def _row_loop(n, fn):
    def body(blk, _):
        for u in range(ROW_DMA_UNROLL):
            fn(blk * ROW_DMA_UNROLL + u, u)
        return 0
    lax.fori_loop(0, n // ROW_DMA_UNROLL, body, 0)


GATHER_ROWS = 256


def _gather_kernel(idx_ref, live_ref, src_ref, o_ref, buf, sem):
    gb = GATHER_ROWS
    i = pl.program_id(0)
    n = pl.num_programs(0)

    def start_step(step):
        slot = step % 2

        @pl.when(live_ref[step] > 0)
        def _():
            _row_loop(gb, lambda r, u: _row_copy(src_ref, idx_ref[step * gb + r], buf.at[slot], r,
                                                 sem.at[slot]).start(priority=u % 2))

    @pl.when(i == 0)
    def _():
        buf[...] = jnp.zeros_like(buf)
        start_step(i)

    @pl.when(i + 1 < n)
    def _():
        start_step(i + 1)

    slot = i % 2

    @pl.when(live_ref[i] > 0)
    def _():
        _row_loop(gb, lambda r, u: _row_copy(src_ref, 0, buf.at[slot], r, sem.at[slot]).wait())

    o_ref[...] = buf[slot].astype(o_ref.dtype)


def gather_rows(src, idx, live, out_dtype):
    r = idx.shape[0]
    d = src.shape[1]
    gb = GATHER_ROWS
    grid_spec = pltpu.PrefetchScalarGridSpec(
        num_scalar_prefetch=2,
        grid=(r // gb,),
        in_specs=[pl.BlockSpec(memory_space=pl.ANY)],
        out_specs=pl.BlockSpec((gb, d), lambda i, idx, live: (i, 0)),
        scratch_shapes=[pltpu.VMEM((2, gb, d), src.dtype), pltpu.SemaphoreType.DMA((2,))],
    )
    return pl.pallas_call(
        _gather_kernel,
        grid_spec=grid_spec,
        out_shape=jax.ShapeDtypeStruct((r, d), out_dtype),
        compiler_params=_cparams(("arbitrary",)),
        name="gather_rows",
    )(idx, live, src)


def _combine_kernel(d0_ref, d1_ref, x_ref, w_ref, g_ref, ys_ref, o_ref, h_ref, buf, sem, *, tm):
    i = pl.program_id(0)
    n = pl.num_programs(0)

    def start_step(step):
        slot = step % 2

        def one(r, u):
            _row_copy(ys_ref, d0_ref[step * tm + r], buf.at[slot, 0], r, sem.at[slot]).start(priority=0)
            _row_copy(ys_ref, d1_ref[step * tm + r], buf.at[slot, 1], r, sem.at[slot]).start(priority=1)
        _row_loop(tm, one)

    @pl.when(i == 0)
    def _():
        start_step(i)

    @pl.when(i + 1 < n)
    def _():
        start_step(i + 1)

    slot = i % 2

    def wait_one(r, u):
        _row_copy(ys_ref, 0, buf.at[slot, 0], r, sem.at[slot]).wait()
        _row_copy(ys_ref, 0, buf.at[slot, 1], r, sem.at[slot]).wait()
    _row_loop(tm, wait_one)
    w = w_ref[...]
    _store_with_norm(x_ref[...] + w[:, 2:3] * buf[slot, 0] + w[:, 3:4] * buf[slot, 1], g_ref, o_ref, h_ref)


def combine_rows(x, ys, d0, d1, w, next_gain, tm=128):
    t, d = x.shape
    row = pl.BlockSpec((tm, d), lambda i, a, b: (i, 0))
    grid_spec = pltpu.PrefetchScalarGridSpec(
        num_scalar_prefetch=2,
        grid=(t // tm,),
        in_specs=[row, pl.BlockSpec((tm, LANES), lambda i, a, b: (i, 0)), pl.BlockSpec((1, d), lambda i, a, b: (0, 0)),
                  pl.BlockSpec(memory_space=pl.ANY)],
        out_specs=[row, row],
        scratch_shapes=[pltpu.VMEM((2, 2, tm, d), F32), pltpu.SemaphoreType.DMA((2,))],
    )
    return pl.pallas_call(
        functools.partial(_combine_kernel, tm=tm),
        grid_spec=grid_spec,
        out_shape=[jax.ShapeDtypeStruct((t, d), F32), jax.ShapeDtypeStruct((t, d), BF16)],
        compiler_params=_cparams(("arbitrary",)),
        name="combine_rows",
    )(d0, d1, x, w, next_gain.reshape(1, d), ys)


def moe_ffn(x, gain, router, w_gate, w_up, w_down, first_expert, next_gain):
    t, d = x.shape
    h, rt = route(x, gain, router)
    e_flat = jnp.concatenate([rt[:, 0], rt[:, 1]]).astype(I32)
    tok = jnp.concatenate([jnp.arange(t, dtype=I32)] * 2)
    onehot = (e_flat[:, None] == jnp.arange(N_EXPERTS, dtype=I32)[None, :]).astype(I32)
    csum = jnp.cumsum(onehot, axis=0)
    rank = jnp.take_along_axis(csum - onehot, e_flat[:, None], axis=1)[:, 0]
    counts = csum[-1]
    n_tiles = (counts + FFN_TM - 1) // FFN_TM
    tile_end = jnp.cumsum(n_tiles)
    tile_start = tile_end - n_tiles
    dest = tile_start[e_flat] * FFN_TM + rank
    ns = TOP_K * t // FFN_TM + N_EXPERTS
    s_idx = jnp.arange(ns, dtype=I32)
    used = s_idx < tile_end[-1]
    s_clip = jnp.minimum(s_idx, tile_end[-1] - 1)
    tile_expert = jnp.minimum(jnp.sum((s_clip[:, None] >= tile_end[None, :]).astype(I32), axis=1), N_EXPERTS - 1)
    tile_rows = jnp.clip(counts[tile_expert] - (s_clip - tile_start[tile_expert]) * FFN_TM, 0, FFN_TM)
    tile_rows = jnp.where(used, tile_rows, 0).astype(I32)
    src_row = jnp.zeros((ns * FFN_TM,), I32).at[dest].set(tok)
    per = FFN_TM // GATHER_ROWS
    live = (jnp.arange(ns * per, dtype=I32) % per * GATHER_ROWS < jnp.repeat(tile_rows, per)).astype(I32)

    xs = gather_rows(h, src_row, live, BF16)
    ys = ffn_tiles(xs, tile_expert + first_expert, tile_rows, w_gate, w_up, w_down)
    return combine_rows(x, ys, dest[:t], dest[t:], rt, next_gain)


def _ple_kernel(h_ref, wg_ref, p_ref, wp_ref, x_ref, o_ref):
    gate = _sigmoid(_dot(h_ref[...], wg_ref[...].astype(BF16)))
    proj = _dot(p_ref[...].astype(BF16), wp_ref[...].astype(BF16))
    o_ref[...] = x_ref[...] + gate * proj


def ple(h, w_gate, p, w_proj, x, layer, tm=1024, tn=512):
    t, d = x.shape
    pd = p.shape[-1]
    tm = min(tm, t)
    return pl.pallas_call(
        _ple_kernel,
        grid=(t // tm, d // tn),
        in_specs=[pl.BlockSpec((tm, d), lambda i, j: (i, 0)), _layer_spec((d, tn), lambda i, j: (0, j), layer),
                  _layer_spec((tm, pd), lambda i, j: (i, 0), layer), _layer_spec((pd, tn), lambda i, j: (0, j), layer),
                  pl.BlockSpec((tm, tn), lambda i, j: (i, j))],
        out_specs=pl.BlockSpec((tm, tn), lambda i, j: (i, j)),
        out_shape=jax.ShapeDtypeStruct((t, d), F32),
        compiler_params=_cparams(("parallel", "parallel")),
        name="ple",
    )(h, w_gate, p, w_proj, x)


def _pad_rows(w, start, total):
    return jnp.zeros((total, w.shape[1]), F32).at[start:start + w.shape[0]].set(w).astype(BF16)


def kernel(x, p, positions, w_in, mix_norm, ffn_norm, ple_norm, rw_mu, rw_w0, rw_w2, rw_a0, rw_a2, rw_g2, rw_kk,
           rw_ka, rw_rk, rw_ln_g, rw_ln_b, dsa_q_norm, dsa_k_norm, dsa_kv_norm, dsa_w_uk, dsa_w_uv, sw_q_norm,
           sw_k_norm, sw_sinks, w_branch, w_out, ffn_w_gate, ffn_w_up, ffn_w_down, moe_router, moe_w_gate, moe_w_up,
           moe_w_down, ple_w_gate, ple_w_proj):
    b, s, d = x.shape
    t = b * s
    depth = w_in.shape[0]
    xf = x.reshape(t, d)
    tables = rope_tables(positions)
    row = lambda a: a.reshape(1, -1)
    w_in_t = jnp.swapaxes(w_in, 1, 2)
    w_branch_b = w_branch.astype(BF16)
    p_rows = p.reshape(depth, t, -1)
    experts = lambda w: w.reshape((-1,) + w.shape[2:])
    for i in range(depth):
        h = rmsnorm(xf, mix_norm[i])
        uh = in_proj(h, w_in_t, i, HEAD_TILES, name="in_proj_head")
        ut = in_proj(h, w_in_t, i, TAIL_TILES, name="in_proj_tail")
        y_a = sb_attention(uh, b, s)
        rw = rw_prepare(uh, row(rw_mu[i]), row(rw_w0[i]), row(rw_a0[i]), row(rw_kk[i]), row(rw_ka[i]),
                        _pad_rows(rw_w2[i], 0, RW_LORA), _pad_rows(rw_a2[i], RW_LORA_W, RW_LORA),
                        _pad_rows(rw_g2[i], RW_LORA_W + RW_LORA_A, RW_LORA), b, s)
        y_b = rw_scan(*rw, row(rw_rk[i]), row(rw_ln_g[i]), row(rw_ln_b[i]), b, s)
        y_c = dsa_attention(uh, tables, dsa_q_norm[i], dsa_k_norm[i], dsa_kv_norm[i], dsa_w_uk[i], dsa_w_uv[i], b, s)
        y_d = sw_attention(ut, tables, sw_q_norm[i], sw_k_norm[i], sw_sinks[i], b, s)
        merged = merge_branches([y_a, y_b, y_c, y_d], ut, w_branch_b, i)
        xf = matmul(merged, w_out, i, out_dtype=F32, residual=xf, name="out_proj")
        if i % 2 == 0:
            h = rmsnorm(xf, ffn_norm[i])
            xf, h = dense_ffn(xf, h, ffn_w_gate, ffn_w_up, ffn_w_down, i // 2, ple_norm[i])
        else:
            xf, h = moe_ffn(xf, ffn_norm[i], moe_router[i // 2], experts(moe_w_gate), experts(moe_w_up),
                            experts(moe_w_down), (i // 2) * N_EXPERTS, ple_norm[i])
        xf = ple(h, ple_w_gate, p_rows, ple_w_proj, xf, i)
    return xf.reshape(b, s, d)
```

```python
import functools
import math

import jax
import jax.numpy as jnp
from jax import lax
from jax.experimental import pallas as pl
from jax.experimental.pallas import tpu as pltpu

F32 = jnp.float32
BF16 = jnp.bfloat16
I32 = jnp.int32

D_MODEL = 2048
ROPE_THETA = 10000.0
NORM_EPS = 1e-6
N_BRANCH = 4
BRANCH_W = 1024
SB_HEADS, SB_DIM = 8, 128
RW_HEADS, RW_DIM = 16, 64
RW_LORA_W, RW_LORA_A, RW_LORA_G = 96, 96, 64
RW_LORA = RW_LORA_W + RW_LORA_A + RW_LORA_G
RW_GN_EPS = 64e-5
DSA_HEADS, DSA_DIM, DSA_KV_RANK = 8, 128, 256
IDX_HEADS, IDX_DIM, IDX_TOPK_MAX = 16, 64, 256
SW_HEADS, SW_KV_HEADS, SW_DIM, WINDOW = 16, 2, 64, 128
N_EXPERTS, TOP_K = 8, 2

LANES = 128
VMEM_LIMIT_BYTES = 56 * 1024 * 1024

HEAD_W = 9216
COL_SBQ = 0
COL_SBK = 1024
COL_SBV = 2048
COL_RWR = 3072
COL_RWK = 4096
COL_RWV = 5120
COL_RWL = 6144
COL_DSAQ = 6400
COL_CKV = 7424
COL_IDXQ = 7680
COL_IDXKW = 8704
SRC_SWQ = 8784
SRC_SWK = 9808
SRC_GATE = 10064
COL_GATE = 0
COL_SWQ = 8192
COL_SWK = 9216
COL_SWV = 9344
IN_TN = 512
IN_ALIGN = 16
HEAD_TILES = tuple(range(0, HEAD_W, IN_TN))
TAIL_TILES = (tuple(range(SRC_GATE, SRC_GATE + N_BRANCH * D_MODEL, IN_TN))
              + tuple(range(SRC_SWQ, SRC_SWK, IN_TN)) + (SRC_SWK,))

RW_CHUNK = 64


def _cparams(sem, vmem=VMEM_LIMIT_BYTES):
    return pltpu.CompilerParams(dimension_semantics=sem, vmem_limit_bytes=vmem)


def _dot(a, b):
    return jnp.dot(a, b, preferred_element_type=F32)


def _dot_nt(a, b):
    return lax.dot_general(a, b, (((1,), (1,)), ((), ())), preferred_element_type=F32)


def _split3(a):
    a1 = a.astype(BF16)
    r1 = a - a1.astype(F32)
    a2 = r1.astype(BF16)
    a3 = (r1 - a2.astype(F32)).astype(BF16)
    return a1, a2, a3


def _dot_exact_rhs(a, m_bf16):
    a1, a2, a3 = _split3(a)
    return _dot(a1, m_bf16) + _dot(a2, m_bf16) + _dot(a3, m_bf16)


def _dot_exact_lhs(m_bf16, a):
    a1, a2, a3 = _split3(a)
    return _dot(m_bf16, a1) + _dot(m_bf16, a2) + _dot(m_bf16, a3)


def _dot_f32(a, b):
    a1, a2, _ = _split3(a)
    b1, b2, _ = _split3(b)
    return _dot(a1, b1) + (_dot(a1, b2) + _dot(a2, b1))


def _softplus(z):
    return jnp.maximum(z, 0.0) + jnp.log(1.0 + jnp.exp(-jnp.abs(z)))


def _sigmoid(z):
    return 1.0 / (1.0 + jnp.exp(-z))


def _iota(shape, dim):
    return lax.broadcasted_iota(I32, shape, dim)


def _group_matrix(n, group, value):
    r = _iota((n, n), 0) // group
    c = _iota((n, n), 1) // group
    return jnp.where(r == c, value, 0.0).astype(BF16)


def _swap_halves(u, half):
    if 2 * half == LANES:
        return pltpu.roll(u, half, 1)
    lane = _iota(u.shape, 1)
    return jnp.where(lane % (2 * half) < half, pltpu.roll(u, LANES - half, 1), pltpu.roll(u, half, 1))


def _rmsnorm_kernel(x_ref, g_ref, o_ref):
    x = x_ref[...]
    ms = jnp.mean(x * x, axis=-1, keepdims=True)
    o_ref[...] = (x * lax.rsqrt(ms + NORM_EPS) * g_ref[...]).astype(o_ref.dtype)


def rmsnorm(x, gain, out_dtype=BF16, tm=512):
    t, d = x.shape
    return pl.pallas_call(
        _rmsnorm_kernel,
        grid=(t // tm,),
        in_specs=[pl.BlockSpec((tm, d), lambda i: (i, 0)), pl.BlockSpec((1, d), lambda i: (0, 0))],
        out_specs=pl.BlockSpec((tm, d), lambda i: (i, 0)),
        out_shape=jax.ShapeDtypeStruct((t, d), out_dtype),
        compiler_params=_cparams(("parallel",)),
        name="rmsnorm",
    )(x, gain.reshape(1, d))


def _mm_kernel(*refs, has_res):
    if has_res:
        a_ref, w_ref, r_ref, o_ref = refs
    else:
        a_ref, w_ref, o_ref = refs
    acc = _dot(a_ref[...].astype(BF16), w_ref[...].astype(BF16))
    if has_res:
        acc = acc + r_ref[...]
    o_ref[...] = acc.astype(o_ref.dtype)


def _layer_spec(block, index_map, layer):
    return pl.BlockSpec((None,) + tuple(block), lambda *idx: (layer,) + tuple(index_map(*idx)))


def matmul(a, w, layer, *, out_dtype, residual=None, n_cols=None, tm=1024, tn=512, name="matmul"):
    m, k = a.shape
    n = w.shape[2] if n_cols is None else n_cols
    tm, tn = min(tm, m), min(tn, n)
    in_specs = [pl.BlockSpec((tm, k), lambda i, j: (i, 0)), _layer_spec((k, tn), lambda i, j: (0, j), layer)]
    args = [a, w]
    if residual is not None:
        in_specs.append(pl.BlockSpec((tm, tn), lambda i, j: (i, j)))
        args.append(residual)
    return pl.pallas_call(
        functools.partial(_mm_kernel, has_res=residual is not None),
        grid=(m // tm, n // tn),
        in_specs=in_specs,
        out_specs=pl.BlockSpec((tm, tn), lambda i, j: (i, j)),
        out_shape=jax.ShapeDtypeStruct((m, n), out_dtype),
        compiler_params=_cparams(("parallel", "parallel")),
        name=name,
    )(*args)


def _in_proj_kernel(off_ref, a_ref, w_ref, o_ref):
    o_ref[...] = _dot_nt(a_ref[...], w_ref[...].astype(BF16)).astype(o_ref.dtype)


def in_proj(a, w_t, layer, offsets, *, tm=2048, tn=IN_TN, name):
    m, k = a.shape
    tm = min(tm, m)
    grid_spec = pltpu.PrefetchScalarGridSpec(
        num_scalar_prefetch=1,
        grid=(m // tm, len(offsets)),
        in_specs=[pl.BlockSpec((tm, k), lambda i, j, off: (i, 0)),
                  pl.BlockSpec((None, pl.Element(tn), pl.Element(k)),
                               lambda i, j, off: (layer, off[j] * IN_ALIGN, 0))],
        out_specs=pl.BlockSpec((tm, tn), lambda i, j, off: (i, j)),
    )
    return pl.pallas_call(
        _in_proj_kernel,
        grid_spec=grid_spec,
        out_shape=jax.ShapeDtypeStruct((m, tn * len(offsets)), BF16),
        compiler_params=_cparams(("parallel", "parallel")),
        name=name,
    )(jnp.asarray([o // IN_ALIGN for o in offsets], I32), a, w_t)


def _rope_table_kernel(pos_ref, f64_ref, f32_ref, ca_ref, sa_ref, cb_ref, sb_ref):
    pos = pos_ref[...]
    lane = _iota((1, LANES), 1)
    ang_a = pos * f64_ref[...]
    ang_b = pos * f32_ref[...]
    ca_ref[...] = jnp.cos(ang_a)
    sa_ref[...] = jnp.where(lane < 64, -1.0, 1.0) * jnp.sin(ang_a)
    cb_ref[...] = jnp.cos(ang_b)
    sb_ref[...] = jnp.where(lane % 64 < 32, -1.0, 1.0) * jnp.sin(ang_b)


def rope_tables(positions, tm=512):
    t = positions.size
    pos = positions.reshape(t, 1).astype(F32)
    inv64 = ROPE_THETA ** (-jnp.arange(64, dtype=F32) / 64)
    inv32 = ROPE_THETA ** (-jnp.arange(32, dtype=F32) / 32)
    f64 = jnp.tile(inv64, 2).reshape(1, LANES)
    f32 = jnp.tile(inv32, 4).reshape(1, LANES)
    row = pl.BlockSpec((tm, LANES), lambda i: (i, 0))
    vec = pl.BlockSpec((1, LANES), lambda i: (0, 0))
    return pl.pallas_call(
        _rope_table_kernel,
        grid=(t // tm,),
        in_specs=[pl.BlockSpec((tm, 1), lambda i: (i, 0)), vec, vec],
        out_specs=[row] * 4,
        out_shape=[jax.ShapeDtypeStruct((t, LANES), F32)] * 4,
        compiler_params=_cparams(("parallel",)),
        name="rope_tables",
    )(pos, f64, f32)


SB_TQ = 256
SB_G = 4


def _sb_kernel(q_ref, k_ref, v_ref, o_ref, *, tq, scale):
    qi = pl.program_id(2)
    r = _iota((tq, tq), 0)
    c = _iota((tq, tq), 1)
    later = jnp.where(r > c, 1.0, 0.0).astype(BF16)
    qs = [(q_ref[:, g * SB_DIM:(g + 1) * SB_DIM].astype(F32) * scale).astype(BF16) for g in range(SB_G)]

    heads = range(SB_G)
    cols = [slice(g * SB_DIM, (g + 1) * SB_DIM) for g in heads]

    def span(j, state, diagonal):
        off = pl.multiple_of(j * tq, tq)
        carry, acc = state
        ks = [k_ref[pl.ds(off, tq), cols[g]].astype(BF16) for g in heads]
        vs = [v_ref[pl.ds(off, tq), cols[g]].astype(BF16) for g in heads]
        zs = [_dot_nt(qs[g], ks[g]) for g in heads]
        lss = [jnp.minimum(z, 0.0) - jnp.log(1.0 + jnp.exp(-jnp.abs(z))) for z in zs]
        lks = [lss[g] - zs[g] for g in heads]
        if diagonal:
            lks = [jnp.where(r > c, lk, 0.0) for lk in lks]
        his = [lk.astype(BF16) for lk in lks]
        los = [(lks[g] - his[g].astype(F32)).astype(BF16) for g in heads]
        css = [_dot(his[g], later) + _dot(los[g], later) for g in heads]
        ws = [jnp.exp(lss[g] + css[g] + carry[g]) for g in heads]
        if diagonal:
            ws = [jnp.where(r > c, w, 0.0) for w in ws]
        acc = tuple(acc[g] + _dot(ws[g].astype(BF16), vs[g]) for g in heads)
        carry = tuple(carry[g] + jnp.sum(lks[g], axis=1, keepdims=True) for g in heads)
        return carry, acc

    init = (tuple(jnp.zeros((tq, 1), F32) for _ in heads), tuple(jnp.zeros((tq, SB_DIM), F32) for _ in heads))
    state = span(qi, init, True)
    _, acc = lax.fori_loop(0, qi, lambda i, s: span(qi - 1 - i, s, False), state)
    for g in heads:
        o_ref[:, cols[g]] = acc[g].astype(o_ref.dtype)


def sb_attention(u, batch, seq):
    t = batch * seq
    tq = min(SB_TQ, seq)
    nq = seq // tq
    gw = SB_G * SB_DIM
    qc, kc, vc = COL_SBQ // gw, COL_SBK // gw, COL_SBV // gw
    return pl.pallas_call(
        functools.partial(_sb_kernel, tq=tq, scale=SB_DIM ** -0.5),
        grid=(batch, SB_HEADS // SB_G, nq),
        in_specs=[
            pl.BlockSpec((tq, gw), lambda b, h, i: (b * nq + i, qc + h)),
            pl.BlockSpec((seq, gw), lambda b, h, i: (b, kc + h)),
            pl.BlockSpec((seq, gw), lambda b, h, i: (b, vc + h)),
        ],
        out_specs=pl.BlockSpec((tq, gw), lambda b, h, i: (b * nq + i, h)),
        out_shape=jax.ShapeDtypeStruct((t, BRANCH_W), BF16),
        compiler_params=_cparams(("parallel", "parallel", "arbitrary")),
        name="sb_attention",
    )(u, u, u)


def _rw_pre_kernel(r_ref, k_ref, v_ref, l_ref, pr_ref, pk_ref, pv_ref, plr_ref,
                   mu_ref, w0_ref, a0_ref, kkg_ref, ka_ref, w2_ref, a2_ref, g2_ref,
                   ro_ref, lw_ref, ko_ref, vo_ref, kk_ref, b_ref, g_ref, *, tm, seq):
    i = pl.program_id(0)
    first = (i * tm) % seq == 0
    row0 = _iota((tm, 1), 0) == 0

    def shifted(cur_ref, prev_ref, lo, hi):
        cur = cur_ref[...].astype(F32)
        last = prev_ref[...].astype(F32)[-1:, :]
        last = jnp.where(first, 0.0, last)
        prev = jnp.where(row0, last, pltpu.roll(cur, 1, 0))
        return cur + mu_ref[:, lo:hi] * (prev - cur)

    w = BRANCH_W
    r = shifted(r_ref, pr_ref, 0, w)
    k = shifted(k_ref, pk_ref, w, 2 * w)
    v = shifted(v_ref, pv_ref, 2 * w, 3 * w)
    z = shifted(l_ref, plr_ref, 3 * w, 3 * w + RW_LORA)

    w_pre = w0_ref[...] + _dot(jnp.tanh(z).astype(BF16), w2_ref[...])
    w_log = -_softplus(-w_pre) - 0.5
    lw_ref[...] = -jnp.exp(w_log)
    a = _sigmoid(a0_ref[...] + _dot(z.astype(BF16), a2_ref[...]))
    g_ref[...] = _dot(_sigmoid(z).astype(BF16), g2_ref[...]).astype(g_ref.dtype)
    ones = _group_matrix(LANES, RW_DIM, 1.0)
    kk = k * kkg_ref[...]
    for s in range(w // LANES):
        sl = slice(s * LANES, (s + 1) * LANES)
        kks = kk[:, sl]
        ss = _dot_exact_rhs(kks * kks, ones)
        kkn = kks / jnp.maximum(jnp.sqrt(ss), 1e-12)
        kk_ref[:, sl] = kkn
        b_ref[:, sl] = kkn * a[:, sl]
    ro_ref[...] = r
    ko_ref[...] = k * (1.0 + (a - 1.0) * ka_ref[...])
    vo_ref[...] = v


def rw_prepare(u, mu, w0, a0, k_k, k_a, w2p, a2p, g2p, batch, seq, tm=256):
    t = batch * seq
    tm = min(tm, seq)
    w = BRANCH_W
    sub = 16

    def cur(width, col):
        return pl.BlockSpec((tm, width), lambda i: (i, col // width))

    def prev(width, col):
        return pl.BlockSpec((sub, width), lambda i: (jnp.maximum(i * (tm // sub) - 1, 0), col // width))

    def vec(width):
        return pl.BlockSpec((1, width), lambda i: (0, 0))

    def mat():
        return pl.BlockSpec((RW_LORA, w), lambda i: (0, 0))

    out = pl.BlockSpec((tm, w), lambda i: (i, 0))
    f = jax.ShapeDtypeStruct((t, w), F32)
    return pl.pallas_call(
        functools.partial(_rw_pre_kernel, tm=tm, seq=seq),
        grid=(t // tm,),
        in_specs=[cur(w, COL_RWR), cur(w, COL_RWK), cur(w, COL_RWV), cur(RW_LORA, COL_RWL),
                  prev(w, COL_RWR), prev(w, COL_RWK), prev(w, COL_RWV), prev(RW_LORA, COL_RWL),
                  vec(3 * w + RW_LORA), vec(w), vec(w), vec(w), vec(w), mat(), mat(), mat()],
        out_specs=[out] * 7,
        out_shape=[f, f, f, f, f, f, jax.ShapeDtypeStruct((t, w), BF16)],
        compiler_params=_cparams(("parallel",)),
        name="rw_prepare",
    )(u, u, u, u, u, u, u, u, mu, w0, a0, k_k, k_a, w2p, a2p, g2p)


def _rw_scan_kernel(r_ref, lw_ref, k_ref, v_ref, kk_ref, b_ref, g_ref, rk_ref, lng_ref, lnb_ref,
                    o_ref, st_ref, *, chunk):
    @pl.when(pl.program_id(1) == 0)
    def _():
        st_ref[...] = jnp.zeros_like(st_ref)

    npair = RW_HEADS // 2
    w = npair * LANES
    n = 2 * chunk
    lane = _iota((n, w), 1)
    row = _iota((n, w), 0)
    own = ((lane // RW_DIM) % 2) == (row // chunk)
    rr = _iota((n, n), 0)
    cc = _iota((n, n), 1)
    same = (rr // chunk) == (cc // chunk)
    tri_incl = jnp.where(same & (rr >= cc), 1.0, 0.0).astype(BF16)
    strict = same & (rr > cc)
    incl = same & (rr >= cc)
    eye = jnp.where(rr == cc, 1.0, 0.0)
    blockdiag = (_iota((LANES, LANES), 0) // RW_DIM) == (_iota((LANES, LANES), 1) // RW_DIM)
    avg = _group_matrix(LANES, RW_DIM, 1.0 / RW_DIM)
    ones = _group_matrix(LANES, RW_DIM, 1.0)
    pairs = range(npair)
    sl = [slice(p * LANES, (p + 1) * LANES) for p in pairs]

    def stack(x):
        return jnp.concatenate([x, x], axis=0)

    def per_pair_rows(x):
        return jnp.concatenate([x[:, s] for s in sl], axis=0)

    def per_pair_lanes(x):
        return jnp.concatenate([x[p * chunk:(p + 1) * chunk] for p in pairs], axis=1)

    r, lw, k, v, kk, b = (ref[...] for ref in (r_ref, lw_ref, k_ref, v_ref, kk_ref, b_ref))
    lw2 = stack(lw)
    l1 = lw2.astype(BF16)
    l2 = (lw2 - l1.astype(F32)).astype(BF16)
    lin = _dot(tri_incl, l1) + _dot(tri_incl, l2)
    lend = lin[n - 1:n]
    p_inv = jnp.exp(-lin)
    p_dec = jnp.exp(lend - lin)
    zero = jnp.zeros((n, w), F32)
    kk_t = jnp.where(own, stack(kk) * jnp.exp(lin - lw2), zero).astype(BF16)
    r_t = jnp.where(own, stack(r) * jnp.exp(lin), zero).astype(BF16)
    v_f = jnp.where(own, stack(v), zero)
    v_s = v_f.astype(BF16)
    b_t = (stack(b) * p_inv).astype(BF16)
    k_t = (stack(k) * p_inv).astype(BF16)
    b_d = (stack(b) * p_dec).astype(BF16)
    k_d = (stack(k) * p_dec).astype(BF16)
    p_end = jnp.exp(lend)

    lhs = [jnp.concatenate([kk_t[:, s], r_t[:, s]], axis=0) for s in sl]
    a_b = [_dot_nt(lhs[p], b_t[:, sl[p]]) for p in pairs]
    a_k = [_dot_nt(lhs[p], k_t[:, sl[p]]) for p in pairs]
    s_t = [_dot_nt(lhs[p], st_ref[p].astype(BF16)) for p in pairs]
    n_ab = [jnp.where(strict, a_b[p][:n], 0.0) for p in pairs]
    rhs = [s_t[p][:n] + _dot(jnp.where(strict, a_k[p][:n], 0.0).astype(BF16), v_s[:, sl[p]]) for p in pairs]
    inv = [eye - n_ab[p] for p in pairs]
    pw = n_ab
    for _ in range(int(math.log2(chunk)) - 1):
        pw_b = [pw[p].astype(BF16) for p in pairs]
        pw = [_dot(pw_b[p], pw_b[p]) for p in pairs]
        inv = [inv[p] + _dot(inv[p].astype(BF16), pw[p].astype(BF16)) for p in pairs]
    u_s = [-_dot(inv[p].astype(BF16), rhs[p].astype(BF16)) for p in pairs]
    y2 = [s_t[p][n:] + _dot(jnp.where(incl, a_b[p][n:], 0.0).astype(BF16), u_s[p].astype(BF16))
          + _dot(jnp.where(incl, a_k[p][n:], 0.0).astype(BF16), v_s[:, sl[p]]) for p in pairs]
    for p in pairs:
        upd = _dot(u_s[p].T.astype(BF16), b_d[:, sl[p]]) + _dot(v_f[:, sl[p]].T.astype(BF16), k_d[:, sl[p]])
        st_ref[p] = st_ref[p] * p_end[:, sl[p]] + jnp.where(blockdiag, upd, 0.0)

    yr = jnp.concatenate([y2[p][:chunk] + y2[p][chunk:] for p in pairs], axis=0)
    yr1, yr2, _ = _split3(yr)
    d = yr - (_dot(yr1, avg) + _dot(yr2, avg))
    dd1, dd2, _ = _split3(d * d)
    var = _dot(dd1, avg) + _dot(dd2, avg)
    yn = per_pair_lanes(d * lax.rsqrt(var + RW_GN_EPS)) * lng_ref[...] + lnb_ref[...]
    bonus = per_pair_lanes(_dot(per_pair_rows(r * k * rk_ref[...]).astype(BF16), ones)) * v
    o_ref[...] = ((yn + bonus) * g_ref[...].astype(F32)).astype(o_ref.dtype)


def rw_scan(r, lw, k, v, kk, b, g, r_k, ln_g, ln_b, batch, seq):
    t = batch * seq
    chunk = min(RW_CHUNK, seq)
    nc = seq // chunk
    w = BRANCH_W
    blk = pl.BlockSpec((chunk, w), lambda bb, c: (bb * nc + c, 0))
    vec = pl.BlockSpec((1, w), lambda bb, c: (0, 0))
    return pl.pallas_call(
        functools.partial(_rw_scan_kernel, chunk=chunk),
        grid=(batch, nc),
        in_specs=[blk] * 7 + [vec] * 3,
        out_specs=blk,
        out_shape=jax.ShapeDtypeStruct((t, w), BF16),
        scratch_shapes=[pltpu.VMEM((RW_HEADS // 2, LANES, LANES), F32)],
        compiler_params=_cparams(("parallel", "arbitrary")),
        name="rw_scan",
    )(r, lw, k, v, kk, b, g, r_k, ln_g, ln_b)


def _rope(u, cos, sin, half):
    return u * cos + _swap_halves(u, half) * sin


DSA_Q_BLK = 256
DSA_IQ_BLK = 512


def _dsa_pre_kernel(*refs):
    nq, niq = BRANCH_W // DSA_Q_BLK, BRANCH_W // DSA_IQ_BLK
    q_refs, refs = refs[:nq], refs[nq:]
    c_ref, refs = refs[0], refs[1:]
    iq_refs, refs = refs[:niq], refs[niq:]
    (ikw_ref, ca_ref, sa_ref, cb_ref, sb_ref, qg_ref, kg_ref, cg_ref, wuk_ref, wuv_ref,
     qo_ref, ko_ref, vo_ref, iqo_ref, iko_ref, iwo_ref) = refs
    ca, sa, cb, sb = ca_ref[...], sa_ref[...], cb_ref[...], sb_ref[...]
    for h in range(DSA_HEADS):
        sl = slice(h * DSA_DIM, (h + 1) * DSA_DIM)
        per = DSA_Q_BLK // DSA_DIM
        q = q_refs[h // per][:, (h % per) * DSA_DIM:(h % per + 1) * DSA_DIM].astype(F32)
        q = q * lax.rsqrt(jnp.mean(q * q, axis=-1, keepdims=True) + NORM_EPS) * qg_ref[...]
        qo_ref[:, sl] = (_rope(q, ca, sa, 64) * DSA_DIM ** -0.5).astype(qo_ref.dtype)
    c = c_ref[...].astype(F32)
    c = (c * lax.rsqrt(jnp.mean(c * c, axis=-1, keepdims=True) + NORM_EPS) * cg_ref[...]).astype(BF16)
    k = _dot(c, wuk_ref[...].astype(BF16))
    k = k * lax.rsqrt(jnp.mean(k * k, axis=-1, keepdims=True) + NORM_EPS) * kg_ref[...]
    ko_ref[...] = _rope(k, ca, sa, 64).astype(ko_ref.dtype)
    vo_ref[...] = _dot(c, wuv_ref[...].astype(BF16)).astype(vo_ref.dtype)
    for s in range(IDX_HEADS * IDX_DIM // LANES):
        sl = slice(s * LANES, (s + 1) * LANES)
        per = DSA_IQ_BLK // LANES
        iq = iq_refs[s // per][:, (s % per) * LANES:(s % per + 1) * LANES].astype(F32)
        iqo_ref[:, sl] = _rope(iq, cb, sb, 32).astype(iqo_ref.dtype)
    ikw = ikw_ref[...].astype(F32)
    lane = _iota(ikw.shape, 1)
    ik = _rope(ikw, cb, sb, 32)
    iko_ref[...] = jnp.where(lane < IDX_DIM, ik, pltpu.roll(ik, IDX_DIM, 1)).astype(iko_ref.dtype)
    iw = pltpu.roll(ikw, IDX_DIM, 1) * (IDX_HEADS ** -0.5 * IDX_DIM ** -0.5)
    iwo_ref[...] = jnp.where(lane < IDX_HEADS, iw, 0.0)


def _dsa_kernel(q_ref, iq_ref, iw_ref, k_ref, v_ref, ik_ref, o_ref, key_ref, bias_ref, vt_ref, m_ref, l_ref, acc_ref,
                *, tq, seq, top_k):
    kb = tq
    qb = pl.program_id(1)
    nblk = qb + 1
    nh = DSA_HEADS
    int_min = jnp.int32(-2 ** 31)
    kidx = _iota((kb, tq), 0)
    ridx = _iota((kb, tq), 1)

    def causal_mask(j):
        return (kidx + j * kb) <= (ridx + qb * tq)

    def block(ref, j):
        return ref[pl.ds(pl.multiple_of(j * kb, kb), kb), :]

    def fold(x, op):
        return op(x.reshape(kb // 8, 8, x.shape[1]), axis=0)

    @pl.when(qb == 0)
    def _():
        for j in range(seq // LANES):
            vt_ref[:, j * LANES:(j + 1) * LANES] = v_ref[j * LANES:(j + 1) * LANES, :].astype(F32).T.astype(BF16)

    first = _iota((tq, LANES), 1) < IDX_DIM
    lhs = []
    for p in range(IDX_HEADS // 2):
        qp = iq_ref[:, p * LANES:(p + 1) * LANES]
        zero = jnp.zeros_like(qp)
        lhs.append(jnp.concatenate([jnp.where(first, qp, zero), jnp.where(first, zero, qp)], axis=0))
    iw_t = iw_ref[...].T
    w_row = [iw_t[h:h + 1] for h in range(IDX_HEADS)]

    def score_block(j, _):
        ik = block(ik_ref, j)
        sc = jnp.zeros((kb, tq), F32)
        for p in range(IDX_HEADS // 2):
            z = jnp.maximum(_dot_nt(ik, lhs[p]), 0.0)
            sc = sc + z[:, :tq] * w_row[2 * p] + z[:, tq:] * w_row[2 * p + 1]
        sc = sc + 0.0
        bits = lax.bitcast_convert_type(sc, I32)
        skey = bits ^ ((bits >> 31) & jnp.int32(0x7FFFFFFF))
        key_ref[pl.ds(pl.multiple_of(j * kb, kb), kb), :] = jnp.where(causal_mask(j), skey, int_min)
        return 0

    lax.fori_loop(0, nblk, score_block, 0)

    def count(pred_fn):
        def body(j, cnt):
            return cnt + fold(jnp.where(pred_fn(block(key_ref, j)), 1.0, 0.0), jnp.sum)
        return jnp.sum(lax.fori_loop(0, nblk, body, jnp.zeros((8, tq), F32)), axis=0, keepdims=True)

    def bit_step(i, thr):
        cand = thr ^ (jnp.int32(1) << (31 - i))
        return jnp.where(count(lambda keys: keys >= cand) >= top_k, cand, thr)

    thr = lax.fori_loop(0, 32, bit_step, jnp.full((1, tq), int_min, I32))
    need = top_k - count(lambda keys: keys > thr)

    q_all = jnp.concatenate([q_ref[:, h * DSA_DIM:(h + 1) * DSA_DIM] for h in range(nh)], axis=0)
    lower = jnp.where(_iota((kb, kb), 0) > _iota((kb, kb), 1), 1.0, 0.0).astype(BF16)
    m_ref[...] = jnp.full_like(m_ref, -1e30)

    def logits(j, bias):
        lt = _dot_nt(block(k_ref, j), q_all)
        return jnp.concatenate([lt[:, h * tq:(h + 1) * tq] + bias for h in range(nh)], axis=1)

    def select(j, ties_seen):
        keys = block(key_ref, j)
        causal = causal_mask(j)
        tie = causal & (keys == thr)
        tie_f = jnp.where(tie, 1.0, 0.0)
        rank = ties_seen + _dot(lower, tie_f.astype(BF16))
        sel = causal & ((keys > thr) | (tie & (rank < need)))
        bias = jnp.where(sel, 0.0, -1e30)
        bias_ref[pl.ds(pl.multiple_of(j * kb, kb), kb), :] = bias
        m_ref[...] = jnp.maximum(m_ref[...], fold(logits(j, bias), jnp.max))
        return ties_seen + jnp.sum(tie_f, axis=0, keepdims=True)

    lax.fori_loop(0, nblk, select, jnp.zeros((1, tq), F32))
    m_all = jnp.broadcast_to(jnp.max(m_ref[...], axis=0, keepdims=True), m_ref.shape)
    l_ref[...] = jnp.zeros_like(l_ref)
    acc_ref[...] = jnp.zeros_like(acc_ref)

    def attend(j, _):
        lt = logits(j, block(bias_ref, j))
        pr = jnp.exp(lt.reshape(kb // 8, 8, nh * tq) - m_all[None]).reshape(kb, nh * tq)
        l_ref[...] += fold(pr, jnp.sum)
        acc_ref[...] += _dot(vt_ref[:, pl.ds(pl.multiple_of(j * kb, kb), kb)], pr.astype(BF16))
        return 0

    lax.fori_loop(0, nblk, attend, 0)
    out_t = acc_ref[...] / jnp.sum(l_ref[...], axis=0, keepdims=True)
    for h in range(nh):
        o_ref[:, h * DSA_DIM:(h + 1) * DSA_DIM] = out_t[:, h * tq:(h + 1) * tq].T.astype(o_ref.dtype)


def dsa_attention(u, tables, q_gain, k_gain, kv_gain, w_uk, w_uv, batch, seq, tm=256, tq=256):
    t = batch * seq
    tm = min(tm, seq)
    ca, sa, cb, sb = tables
    w = BRANCH_W
    row = lambda width, col: pl.BlockSpec((tm, width), lambda i: (i, col // width))
    tab = pl.BlockSpec((tm, LANES), lambda i: (i, 0))
    vec = lambda width: pl.BlockSpec((1, width), lambda i: (0, 0))
    mat = pl.BlockSpec((DSA_KV_RANK, DSA_DIM), lambda i: (0, 0))
    o_w = pl.BlockSpec((tm, w), lambda i: (i, 0))
    o_n = pl.BlockSpec((tm, LANES), lambda i: (i, 0))
    q, k, v, iq, ik, iw = pl.pallas_call(
        _dsa_pre_kernel,
        grid=(t // tm,),
        in_specs=[row(DSA_Q_BLK, COL_DSAQ + j * DSA_Q_BLK) for j in range(w // DSA_Q_BLK)]
        + [row(DSA_KV_RANK, COL_CKV)]
        + [row(DSA_IQ_BLK, COL_IDXQ + j * DSA_IQ_BLK) for j in range(w // DSA_IQ_BLK)]
        + [row(LANES, COL_IDXKW), tab, tab, tab, tab, vec(DSA_DIM), vec(DSA_DIM), vec(DSA_KV_RANK), mat, mat],
        out_specs=[o_w, o_n, o_n, o_w, o_n, o_n],
        out_shape=[jax.ShapeDtypeStruct((t, w), BF16), jax.ShapeDtypeStruct((t, LANES), BF16),
                   jax.ShapeDtypeStruct((t, LANES), BF16), jax.ShapeDtypeStruct((t, w), BF16),
                   jax.ShapeDtypeStruct((t, LANES), BF16), jax.ShapeDtypeStruct((t, LANES), F32)],
        compiler_params=_cparams(("parallel",)),
        name="dsa_prepare",
    )(*([u] * (w // DSA_Q_BLK + 1 + w // DSA_IQ_BLK + 1)), ca, sa, cb, sb,
      q_gain.reshape(1, -1), k_gain.reshape(1, -1), kv_gain.reshape(1, -1), w_uk, w_uv)

    nq = seq // tq
    top_k = min(IDX_TOPK_MAX, seq // 4)
    qrow = lambda width: pl.BlockSpec((tq, width), lambda b, i: (b * nq + i, 0))
    full = pl.BlockSpec((seq, LANES), lambda b, i: (b, 0))
    return pl.pallas_call(
        functools.partial(_dsa_kernel, tq=tq, seq=seq, top_k=top_k),
        grid=(batch, nq),
        in_specs=[qrow(w), qrow(w), qrow(LANES), full, full, full],
        out_specs=qrow(w),
        out_shape=jax.ShapeDtypeStruct((t, w), BF16),
        scratch_shapes=[pltpu.VMEM((seq, tq), I32), pltpu.VMEM((seq, tq), F32), pltpu.VMEM((DSA_DIM, seq), BF16),
                        pltpu.VMEM((8, DSA_HEADS * tq), F32), pltpu.VMEM((8, DSA_HEADS * tq), F32),
                        pltpu.VMEM((DSA_DIM, DSA_HEADS * tq), F32)],
        compiler_params=_cparams(("parallel", "arbitrary")),
        name="dsa_attention",
    )(q, iq, iw, k, v, ik)


def _sw_kernel(sink_ref, q_ref, kp_ref, kc_ref, vp_ref, vc_ref, cbp_ref, sbp_ref, cbc_ref, sbc_ref,
               qg_ref, kg_ref, o_ref, *, blk):
    n = pl.program_id(1)
    avg = _group_matrix(LANES, SW_DIM, 1.0 / SW_DIM)
    lane2 = _iota((2 * blk, LANES), 1)
    lane1 = _iota((blk, LANES), 1)

    def norm(x, gain):
        s1, s2, _ = _split3(x * x)
        return x * lax.rsqrt(_dot(s1, avg) + _dot(s2, avg) + NORM_EPS) * gain

    cb = jnp.concatenate([cbp_ref[...], cbc_ref[...]], axis=0)
    sb = jnp.concatenate([sbp_ref[...], sbc_ref[...]], axis=0)
    k = jnp.concatenate([kp_ref[...], kc_ref[...]], axis=0).astype(F32)
    k = _rope(norm(k, kg_ref[...]), cb, sb, 32)
    v = jnp.concatenate([vp_ref[...], vc_ref[...]], axis=0).astype(F32)
    k_sw, v_sw = pltpu.roll(k, SW_DIM, 1), pltpu.roll(v, SW_DIM, 1)
    k2 = [jnp.where(lane2 < SW_DIM, k, k_sw).astype(BF16), jnp.where(lane2 < SW_DIM, k_sw, k).astype(BF16)]
    v2 = [jnp.where(lane2 < SW_DIM, v, v_sw).astype(BF16), jnp.where(lane2 < SW_DIM, v_sw, v).astype(BF16)]

    r = _iota((2 * blk, 2 * blk), 0) % blk
    c = _iota((2 * blk, 2 * blk), 1)
    dist = r - (c - blk)
    mask = (dist >= 0) & (dist < WINDOW) & ((c >= blk) | (n > 0))
    top = _iota((2 * blk, 1), 0) < blk
    pairs = range(SW_HEADS // 2)
    group = [(2 * p) // (SW_HEADS // SW_KV_HEADS) for p in pairs]
    npair = len(pairs)
    q = jnp.concatenate([q_ref[:, p * LANES:(p + 1) * LANES].astype(F32) for p in pairs], axis=0)
    cbc = jnp.concatenate([cbc_ref[...]] * npair, axis=0)
    sbc = jnp.concatenate([sbc_ref[...]] * npair, axis=0)
    q = _rope(norm(q, qg_ref[...]), cbc, sbc, 32) * SW_DIM ** -0.5
    low = _iota(q.shape, 1) < SW_DIM
    q_a = jnp.where(low, q, 0.0).astype(BF16)
    q_b = jnp.where(low, 0.0, q).astype(BF16)
    rows = [slice(p * blk, (p + 1) * blk) for p in pairs]
    lhs = [jnp.concatenate([q_a[rows[p]], q_b[rows[p]]], axis=0) for p in pairs]
    logit = [jnp.where(mask, _dot_nt(lhs[p], k2[group[p]]), -1e30) for p in pairs]
    sink = [jnp.where(top, sink_ref[2 * p], sink_ref[2 * p + 1]) for p in pairs]
    m = [jnp.maximum(jnp.max(logit[p], axis=1, keepdims=True), sink[p]) for p in pairs]
    pr = [jnp.exp(logit[p] - m[p]) for p in pairs]
    den = [jnp.sum(pr[p], axis=1, keepdims=True) + jnp.exp(sink[p] - m[p]) for p in pairs]
    o2 = [_dot(pr[p].astype(BF16), v2[group[p]]) / den[p] for p in pairs]
    for p in pairs:
        o_ref[:, p * LANES:(p + 1) * LANES] = jnp.where(lane1 < SW_DIM, o2[p][:blk], o2[p][blk:]).astype(o_ref.dtype)


def sw_attention(u, tables, q_gain, k_gain, sinks, batch, seq, blk=128):
    t = batch * seq
    nb = seq // blk
    _, _, cb, sb = tables
    w = BRANCH_W
    cur = lambda width, col: pl.BlockSpec((blk, width), lambda b, i: (b * nb + i, col // width))
    prev = lambda width, col: pl.BlockSpec((blk, width), lambda b, i: (b * nb + jnp.maximum(i - 1, 0), col // width))
    vec = pl.BlockSpec((1, LANES), lambda b, i: (0, 0))
    tile2 = lambda g: jnp.tile(g.reshape(1, SW_DIM), (1, 2))
    return pl.pallas_call(
        functools.partial(_sw_kernel, blk=blk),
        grid=(batch, nb),
        in_specs=[pl.BlockSpec(memory_space=pltpu.SMEM),
                  cur(w, COL_SWQ), prev(LANES, COL_SWK), cur(LANES, COL_SWK), prev(LANES, COL_SWV), cur(LANES, COL_SWV),
                  prev(LANES, 0), prev(LANES, 0), cur(LANES, 0), cur(LANES, 0), vec, vec],
        out_specs=pl.BlockSpec((blk, w), lambda b, i: (b * nb + i, 0)),
        out_shape=jax.ShapeDtypeStruct((t, w), BF16),
        compiler_params=_cparams(("parallel", "parallel")),
        name="sw_attention",
    )(sinks, u, u, u, u, u, cb, sb, cb, sb, tile2(q_gain), tile2(k_gain))


def _merge_kernel(*refs):
    y_refs = refs[:N_BRANCH]
    g_ref, w_ref, o_ref, acc_ref = refs[N_BRANCH:]
    n = pl.program_id(2)
    for b in range(N_BRANCH):
        @pl.when(n == b)
        def _():
            contrib = _sigmoid(g_ref[...].astype(F32)) * _dot(y_refs[b][...], w_ref[...].astype(BF16))
            if b == 0:
                acc_ref[...] = contrib
            elif b < N_BRANCH - 1:
                acc_ref[...] += contrib
            else:
                o_ref[...] = (acc_ref[...] + contrib).astype(o_ref.dtype)


def merge_branches(ys, u, w_branch, layer, tm=1024, tc=1024):
    t, w = ys[0].shape
    d = w_branch.shape[-1]
    tm = min(tm, t)
    nc = d // tc
    y_spec = pl.BlockSpec((tm, w), lambda i, c, n: (i, 0))
    return pl.pallas_call(
        _merge_kernel,
        grid=(t // tm, nc, N_BRANCH),
        in_specs=[y_spec] * N_BRANCH + [pl.BlockSpec((tm, tc), lambda i, c, n: (i, COL_GATE // tc + n * nc + c)),
                                        _layer_spec((None, w, tc), lambda i, c, n: (n, 0, c), layer)],
        out_specs=pl.BlockSpec((tm, tc), lambda i, c, n: (i, c)),
        out_shape=jax.ShapeDtypeStruct((t, d), BF16),
        scratch_shapes=[pltpu.VMEM((tm, tc), F32)],
        compiler_params=_cparams(("parallel", "arbitrary", "arbitrary")),
        name="merge_branches",
    )(*ys, u, w_branch)


FFN_TM = 1024
FFN_ROWS = (1024, 768, 512, 384, 256, 128)
FFN_TF = 512
FFN_VMEM_BYTES = 60 * 1024 * 1024


def _ffn_kernel(e_ref, rows_ref, x_ref, wg_ref, wu_ref, wd_ref, o_ref):
    rows = rows_ref[pl.program_id(0)]

    @pl.when(pl.program_id(1) == 0)
    def _():
        o_ref[...] = jnp.zeros_like(o_ref)

    for k, size in enumerate(FFN_ROWS):
        below = FFN_ROWS[k + 1] if k + 1 < len(FFN_ROWS) else 0

        @pl.when((rows > below) & (rows <= size))
        def _():
            x = x_ref[:size, :]
            g = _dot(x, wg_ref[0].astype(BF16))
            a = (g * _sigmoid(g) * _dot(x, wu_ref[0].astype(BF16))).astype(BF16)
            o_ref[:size, :] += _dot(a, wd_ref[0].astype(BF16))


def ffn_tiles(xs, tile_expert, tile_rows, w_gate, w_up, w_down):
    r, d = xs.shape
    _, _, ff = w_gate.shape
    ns = r // FFN_TM
    nf = ff // FFN_TF

    def f_eff(s, f, rows):
        return jnp.where(rows[s] > 0, f, nf - 1)

    grid_spec = pltpu.PrefetchScalarGridSpec(
        num_scalar_prefetch=2,
        grid=(ns, nf),
        in_specs=[pl.BlockSpec((FFN_TM, d), lambda s, f, e, rows: (s, 0)),
                  pl.BlockSpec((1, d, FFN_TF), lambda s, f, e, rows: (e[s], 0, f_eff(s, f, rows))),
                  pl.BlockSpec((1, d, FFN_TF), lambda s, f, e, rows: (e[s], 0, f_eff(s, f, rows))),
                  pl.BlockSpec((1, FFN_TF, d), lambda s, f, e, rows: (e[s], f_eff(s, f, rows), 0))],
        out_specs=pl.BlockSpec((FFN_TM, d), lambda s, f, e, rows: (s, 0)),
    )
    return pl.pallas_call(
        _ffn_kernel,
        grid_spec=grid_spec,
        out_shape=jax.ShapeDtypeStruct((r, d), F32),
        compiler_params=_cparams(("parallel", "arbitrary"), FFN_VMEM_BYTES),
        name="ffn_tiles",
    )(tile_expert, tile_rows, xs, w_gate, w_up, w_down)


def _store_with_norm(x, gain_ref, o_ref, h_ref):
    o_ref[...] = x
    ms = jnp.mean(x * x, axis=-1, keepdims=True)
    h_ref[...] = (x * lax.rsqrt(ms + NORM_EPS) * gain_ref[...]).astype(h_ref.dtype)


def _add_norm_kernel(a_ref, b_ref, g_ref, o_ref, h_ref):
    _store_with_norm(a_ref[...] + b_ref[...], g_ref, o_ref, h_ref)


def add_norm(a, b, gain, tm=512):
    t, d = a.shape
    spec = pl.BlockSpec((tm, d), lambda i: (i, 0))
    return pl.pallas_call(
        _add_norm_kernel, grid=(t // tm,),
        in_specs=[spec, spec, pl.BlockSpec((1, d), lambda i: (0, 0))], out_specs=[spec, spec],
        out_shape=[jax.ShapeDtypeStruct((t, d), a.dtype), jax.ShapeDtypeStruct((t, d), BF16)],
        compiler_params=_cparams(("parallel",)), name="residual_add_norm")(a, b, gain.reshape(1, d))


def dense_ffn(x, h, w_gate, w_up, w_down, index, next_gain):
    t = h.shape[0]
    ns = t // FFN_TM
    ys = ffn_tiles(h, jnp.full((ns,), index, I32), jnp.full((ns,), FFN_TM, I32), w_gate, w_up, w_down)
    return add_norm(x, ys, next_gain)


def _router_kernel(x_ref, g_ref, w_ref, h_ref, r_ref):
    x = x_ref[...]
    h = x * lax.rsqrt(jnp.mean(x * x, axis=-1, keepdims=True) + NORM_EPS) * g_ref[...]
    h_ref[...] = h
    logit = _dot_f32(h, w_ref[...])
    lane = _iota(logit.shape, 1).astype(F32)
    neg = -jnp.inf
    l1 = jnp.where(lane < N_EXPERTS, logit, neg)
    m1 = jnp.max(l1, axis=1, keepdims=True)
    i1 = jnp.min(jnp.where(l1 == m1, lane, float(LANES)), axis=1, keepdims=True)
    l2 = jnp.where(lane == i1, neg, l1)
    m2 = jnp.max(l2, axis=1, keepdims=True)
    i2 = jnp.min(jnp.where(l2 == m2, lane, float(LANES)), axis=1, keepdims=True)
    e = jnp.exp(m2 - m1)
    w1 = 1.0 / (1.0 + e)
    w2 = e / (1.0 + e)
    r_ref[...] = jnp.where(lane == 0, i1, jnp.where(lane == 1, i2, jnp.where(lane == 2, w1,
                           jnp.where(lane == 3, w2, 0.0))))


def route(x, gain, router, tm=256):
    t, d = x.shape
    wp = jnp.zeros((d, LANES), F32).at[:, :N_EXPERTS].set(router)
    return pl.pallas_call(
        _router_kernel,
        grid=(t // tm,),
        in_specs=[pl.BlockSpec((tm, d), lambda i: (i, 0)), pl.BlockSpec((1, d), lambda i: (0, 0)),
                  pl.BlockSpec((d, LANES), lambda i: (0, 0))],
        out_specs=[pl.BlockSpec((tm, d), lambda i: (i, 0)), pl.BlockSpec((tm, LANES), lambda i: (i, 0))],
        out_shape=[jax.ShapeDtypeStruct((t, d), F32), jax.ShapeDtypeStruct((t, LANES), F32)],
        compiler_params=_cparams(("parallel",)),
        name="router",
    )(x, gain.reshape(1, d), wp)


def _row_copy(src_ref, row, buf, i, sem):
    return pltpu.make_async_copy(src_ref.at[pl.ds(row, 1)], buf.at[pl.ds(i, 1)], sem)


ROW_DMA_UNROLL = 8


def _row_loop(n, fn):
    def body(blk, _):
        for u in range(ROW_DMA_UNROLL):
            fn(blk * ROW_DMA_UNROLL + u, u)
        return 0
    lax.fori_loop(0, n // ROW_DMA_UNROLL, body, 0)


GATHER_ROWS = 256


def _gather_kernel(idx_ref, live_ref, src_ref, o_ref, buf, sem):
    gb = GATHER_ROWS
    i = pl.program_id(0)
    n = pl.num_programs(0)

    def start_step(step):
        slot = step % 2

        @pl.when(live_ref[step] > 0)
        def _():
            _row_loop(gb, lambda r, u: _row_copy(src_ref, idx_ref[step * gb + r], buf.at[slot], r,
                                                 sem.at[slot]).start(priority=u % 2))

    @pl.when(i == 0)
    def _():
        buf[...] = jnp.zeros_like(buf)
        start_step(i)

    @pl.when(i + 1 < n)
    def _():
        start_step(i + 1)

    slot = i % 2

    @pl.when(live_ref[i] > 0)
    def _():
        _row_loop(gb, lambda r, u: _row_copy(src_ref, 0, buf.at[slot], r, sem.at[slot]).wait())

    o_ref[...] = buf[slot].astype(o_ref.dtype)


def gather_rows(src, idx, live, out_dtype):
    r = idx.shape[0]
    d = src.shape[1]
    gb = GATHER_ROWS
    grid_spec = pltpu.PrefetchScalarGridSpec(
        num_scalar_prefetch=2,
        grid=(r // gb,),
        in_specs=[pl.BlockSpec(memory_space=pl.ANY)],
        out_specs=pl.BlockSpec((gb, d), lambda i, idx, live: (i, 0)),
        scratch_shapes=[pltpu.VMEM((2, gb, d), src.dtype), pltpu.SemaphoreType.DMA((2,))],
    )
    return pl.pallas_call(
        _gather_kernel,
        grid_spec=grid_spec,
        out_shape=jax.ShapeDtypeStruct((r, d), out_dtype),
        compiler_params=_cparams(("arbitrary",)),
        name="gather_rows",
    )(idx, live, src)


def _combine_kernel(d0_ref, d1_ref, x_ref, w_ref, g_ref, ys_ref, o_ref, h_ref, buf, sem, *, tm):
    i = pl.program_id(0)
    n = pl.num_programs(0)

    def start_step(step):
        slot = step % 2

        def one(r, u):
            _row_copy(ys_ref, d0_ref[step * tm + r], buf.at[slot, 0], r, sem.at[slot]).start(priority=0)
            _row_copy(ys_ref, d1_ref[step * tm + r], buf.at[slot, 1], r, sem.at[slot]).start(priority=1)
        _row_loop(tm, one)

    @pl.when(i == 0)
    def _():
        start_step(i)

    @pl.when(i + 1 < n)
    def _():
        start_step(i + 1)

    slot = i % 2

    def wait_one(r, u):
        _row_copy(ys_ref, 0, buf.at[slot, 0], r, sem.at[slot]).wait()
        _row_copy(ys_ref, 0, buf.at[slot, 1], r, sem.at[slot]).wait()
    _row_loop(tm, wait_one)
    w = w_ref[...]
    _store_with_norm(x_ref[...] + w[:, 2:3] * buf[slot, 0] + w[:, 3:4] * buf[slot, 1], g_ref, o_ref, h_ref)


def combine_rows(x, ys, d0, d1, w, next_gain, tm=128):
    t, d = x.shape
    row = pl.BlockSpec((tm, d), lambda i, a, b: (i, 0))
    grid_spec = pltpu.PrefetchScalarGridSpec(
        num_scalar_prefetch=2,
        grid=(t // tm,),
        in_specs=[row, pl.BlockSpec((tm, LANES), lambda i, a, b: (i, 0)), pl.BlockSpec((1, d), lambda i, a, b: (0, 0)),
                  pl.BlockSpec(memory_space=pl.ANY)],
        out_specs=[row, row],
        scratch_shapes=[pltpu.VMEM((2, 2, tm, d), F32), pltpu.SemaphoreType.DMA((2,))],
    )
    return pl.pallas_call(
        functools.partial(_combine_kernel, tm=tm),
        grid_spec=grid_spec,
        out_shape=[jax.ShapeDtypeStruct((t, d), F32), jax.ShapeDtypeStruct((t, d), BF16)],
        compiler_params=_cparams(("arbitrary",)),
        name="combine_rows",
    )(d0, d1, x, w, next_gain.reshape(1, d), ys)


def moe_ffn(x, gain, router, w_gate, w_up, w_down, first_expert, next_gain):
    t, d = x.shape
    h, rt = route(x, gain, router)
    e_flat = jnp.concatenate([rt[:, 0], rt[:, 1]]).astype(I32)
    tok = jnp.concatenate([jnp.arange(t, dtype=I32)] * 2)
    onehot = (e_flat[:, None] == jnp.arange(N_EXPERTS, dtype=I32)[None, :]).astype(I32)
    csum = jnp.cumsum(onehot, axis=0)
    rank = jnp.take_along_axis(csum - onehot, e_flat[:, None], axis=1)[:, 0]
    counts = csum[-1]
    n_tiles = (counts + FFN_TM - 1) // FFN_TM
    tile_end = jnp.cumsum(n_tiles)
    tile_start = tile_end - n_tiles
    dest = tile_start[e_flat] * FFN_TM + rank
    ns = TOP_K * t // FFN_TM + N_EXPERTS
    s_idx = jnp.arange(ns, dtype=I32)
    used = s_idx < tile_end[-1]
    s_clip = jnp.minimum(s_idx, tile_end[-1] - 1)
    tile_expert = jnp.minimum(jnp.sum((s_clip[:, None] >= tile_end[None, :]).astype(I32), axis=1), N_EXPERTS - 1)
    tile_rows = jnp.clip(counts[tile_expert] - (s_clip - tile_start[tile_expert]) * FFN_TM, 0, FFN_TM)
    tile_rows = jnp.where(used, tile_rows, 0).astype(I32)
    src_row = jnp.zeros((ns * FFN_TM,), I32).at[dest].set(tok)
    per = FFN_TM // GATHER_ROWS
    live = (jnp.arange(ns * per, dtype=I32) % per * GATHER_ROWS < jnp.repeat(tile_rows, per)).astype(I32)

    xs = gather_rows(h, src_row, live, BF16)
    ys = ffn_tiles(xs, tile_expert + first_expert, tile_rows, w_gate, w_up, w_down)
    return combine_rows(x, ys, dest[:t], dest[t:], rt, next_gain)


def _ple_kernel(h_ref, wg_ref, p_ref, wp_ref, x_ref, o_ref):
    gate = _sigmoid(_dot(h_ref[...], wg_ref[...].astype(BF16)))
    proj = _dot(p_ref[...].astype(BF16), wp_ref[...].astype(BF16))
    o_ref[...] = x_ref[...] + gate * proj


def ple(h, w_gate, p, w_proj, x, layer, tm=2048, tn=512):
    t, d = x.shape
    pd = p.shape[-1]
    tm = min(tm, t)
    return pl.pallas_call(
        _ple_kernel,
        grid=(t // tm, d // tn),
        in_specs=[pl.BlockSpec((tm, d), lambda i, j: (i, 0)), _layer_spec((d, tn), lambda i, j: (0, j), layer),
                  _layer_spec((tm, pd), lambda i, j: (i, 0), layer), _layer_spec((pd, tn), lambda i, j: (0, j), layer),
                  pl.BlockSpec((tm, tn), lambda i, j: (i, j))],
        out_specs=pl.BlockSpec((tm, tn), lambda i, j: (i, j)),
        out_shape=jax.ShapeDtypeStruct((t, d), F32),
        compiler_params=_cparams(("parallel", "parallel")),
        name="ple",
    )(h, w_gate, p, w_proj, x)


def _pad_rows(w, start, total):
    return jnp.zeros((total, w.shape[1]), F32).at[start:start + w.shape[0]].set(w).astype(BF16)


def kernel(x, p, positions, w_in, mix_norm, ffn_norm, ple_norm, rw_mu, rw_w0, rw_w2, rw_a0, rw_a2, rw_g2, rw_kk,
           rw_ka, rw_rk, rw_ln_g, rw_ln_b, dsa_q_norm, dsa_k_norm, dsa_kv_norm, dsa_w_uk, dsa_w_uv, sw_q_norm,
           sw_k_norm, sw_sinks, w_branch, w_out, ffn_w_gate, ffn_w_up, ffn_w_down, moe_router, moe_w_gate, moe_w_up,
           moe_w_down, ple_w_gate, ple_w_proj):
    b, s, d = x.shape
    t = b * s
    depth = w_in.shape[0]
    xf = x.reshape(t, d)
    tables = rope_tables(positions)
    row = lambda a: a.reshape(1, -1)
    w_in_t = jnp.swapaxes(w_in, 1, 2)
    p_rows = p.reshape(depth, t, -1)
    experts = lambda w: w.reshape((-1,) + w.shape[2:])
    for i in range(depth):
        h = rmsnorm(xf, mix_norm[i])
        uh = in_proj(h, w_in_t, i, HEAD_TILES, name="in_proj_head")
        ut = in_proj(h, w_in_t, i, TAIL_TILES, name="in_proj_tail")
        y_a = sb_attention(uh, b, s)
        rw = rw_prepare(uh, row(rw_mu[i]), row(rw_w0[i]), row(rw_a0[i]), row(rw_kk[i]), row(rw_ka[i]),
                        _pad_rows(rw_w2[i], 0, RW_LORA), _pad_rows(rw_a2[i], RW_LORA_W, RW_LORA),
                        _pad_rows(rw_g2[i], RW_LORA_W + RW_LORA_A, RW_LORA), b, s)
        y_b = rw_scan(*rw, row(rw_rk[i]), row(rw_ln_g[i]), row(rw_ln_b[i]), b, s)
        y_c = dsa_attention(uh, tables, dsa_q_norm[i], dsa_k_norm[i], dsa_kv_norm[i], dsa_w_uk[i], dsa_w_uv[i], b, s)
        y_d = sw_attention(ut, tables, sw_q_norm[i], sw_k_norm[i], sw_sinks[i], b, s)
        merged = merge_branches([y_a, y_b, y_c, y_d], ut, w_branch, i)
        xf = matmul(merged, w_out, i, out_dtype=F32, residual=xf, tm=2048, name="out_proj")
        if i % 2 == 0:
            h = rmsnorm(xf, ffn_norm[i])
            xf, h = dense_ffn(xf, h, ffn_w_gate, ffn_w_up, ffn_w_down, i // 2, ple_norm[i])
        else:
            xf, h = moe_ffn(xf, ffn_norm[i], moe_router[i // 2], experts(moe_w_gate), experts(moe_w_up),
                            experts(moe_w_down), (i // 2) * N_EXPERTS, ple_norm[i])
        xf = ple(h, ple_w_gate, p_rows, ple_w_proj, xf, i)
    return xf.reshape(b, s, d)
```

```python
import functools
import math

import jax
import jax.numpy as jnp
from jax import lax
from jax.experimental import pallas as pl
from jax.experimental.pallas import tpu as pltpu

F32 = jnp.float32
BF16 = jnp.bfloat16
I32 = jnp.int32

D_MODEL = 2048
ROPE_THETA = 10000.0
NORM_EPS = 1e-6
N_BRANCH = 4
BRANCH_W = 1024
SB_HEADS, SB_DIM = 8, 128
RW_HEADS, RW_DIM = 16, 64
RW_LORA_W, RW_LORA_A, RW_LORA_G = 96, 96, 64
RW_LORA = RW_LORA_W + RW_LORA_A + RW_LORA_G
RW_GN_EPS = 64e-5
DSA_HEADS, DSA_DIM, DSA_KV_RANK = 8, 128, 256
IDX_HEADS, IDX_DIM, IDX_TOPK_MAX = 16, 64, 256
SW_HEADS, SW_KV_HEADS, SW_DIM, WINDOW = 16, 2, 64, 128
N_EXPERTS, TOP_K = 8, 2

LANES = 128
VMEM_LIMIT_BYTES = 56 * 1024 * 1024

HEAD_W = 9216
COL_SBQ = 0
COL_SBK = 1024
COL_SBV = 2048
COL_RWR = 3072
COL_RWK = 4096
COL_RWV = 5120
COL_RWL = 6144
COL_DSAQ = 6400
COL_CKV = 7424
COL_IDXQ = 7680
COL_IDXKW = 8704
SRC_SWQ = 8784
SRC_SWK = 9808
SRC_GATE = 10064
COL_GATE = 0
COL_SWQ = 8192
COL_SWK = 9216
COL_SWV = 9344
IN_TN = 512
IN_ALIGN = 16
HEAD_TILES = tuple(range(0, HEAD_W, IN_TN))
TAIL_TILES = (tuple(range(SRC_GATE, SRC_GATE + N_BRANCH * D_MODEL, IN_TN))
              + tuple(range(SRC_SWQ, SRC_SWK, IN_TN)) + (SRC_SWK,))

RW_CHUNK = 64


def _cparams(sem, vmem=VMEM_LIMIT_BYTES):
    return pltpu.CompilerParams(dimension_semantics=sem, vmem_limit_bytes=vmem)


def _dot(a, b):
    return jnp.dot(a, b, preferred_element_type=F32)


def _dot_nt(a, b):
    return lax.dot_general(a, b, (((1,), (1,)), ((), ())), preferred_element_type=F32)


def _split3(a):
    a1 = a.astype(BF16)
    r1 = a - a1.astype(F32)
    a2 = r1.astype(BF16)
    a3 = (r1 - a2.astype(F32)).astype(BF16)
    return a1, a2, a3


def _dot_exact_rhs(a, m_bf16):
    a1, a2, a3 = _split3(a)
    return _dot(a1, m_bf16) + _dot(a2, m_bf16) + _dot(a3, m_bf16)


def _dot_f32(a, b):
    a1, a2, _ = _split3(a)
    b1, b2, _ = _split3(b)
    return _dot(a1, b1) + (_dot(a1, b2) + _dot(a2, b1))


def _softplus(z):
    return jnp.maximum(z, 0.0) + jnp.log(1.0 + jnp.exp(-jnp.abs(z)))


def _sigmoid(z):
    return 1.0 / (1.0 + jnp.exp(-z))


def _iota(shape, dim):
    return lax.broadcasted_iota(I32, shape, dim)


def _group_matrix(n, group, value):
    r = _iota((n, n), 0) // group
    c = _iota((n, n), 1) // group
    return jnp.where(r == c, value, 0.0).astype(BF16)


def _swap_halves(u, half):
    if 2 * half == LANES:
        return pltpu.roll(u, half, 1)
    lane = _iota(u.shape, 1)
    return jnp.where(lane % (2 * half) < half, pltpu.roll(u, LANES - half, 1), pltpu.roll(u, half, 1))


def _rmsnorm_kernel(x_ref, g_ref, o_ref):
    x = x_ref[...]
    ms = jnp.mean(x * x, axis=-1, keepdims=True)
    o_ref[...] = (x * lax.rsqrt(ms + NORM_EPS) * g_ref[...]).astype(o_ref.dtype)


def rmsnorm(x, gain, out_dtype=BF16, tm=512):
    t, d = x.shape
    return pl.pallas_call(
        _rmsnorm_kernel,
        grid=(t // tm,),
        in_specs=[pl.BlockSpec((tm, d), lambda i: (i, 0)), pl.BlockSpec((1, d), lambda i: (0, 0))],
        out_specs=pl.BlockSpec((tm, d), lambda i: (i, 0)),
        out_shape=jax.ShapeDtypeStruct((t, d), out_dtype),
        compiler_params=_cparams(("parallel",)),
        name="rmsnorm",
    )(x, gain.reshape(1, d))


def _mm_kernel(*refs, has_res):
    if has_res:
        a_ref, w_ref, r_ref, o_ref = refs
    else:
        a_ref, w_ref, o_ref = refs
    acc = _dot(a_ref[...].astype(BF16), w_ref[...].astype(BF16))
    if has_res:
        acc = acc + r_ref[...]
    o_ref[...] = acc.astype(o_ref.dtype)


def _layer_spec(block, index_map, layer):
    return pl.BlockSpec((None,) + tuple(block), lambda *idx: (layer,) + tuple(index_map(*idx)))


def matmul(a, w, layer, *, out_dtype, residual=None, tm=1024, tn=512, name="matmul"):
    m, k = a.shape
    n = w.shape[2]
    tm, tn = min(tm, m), min(tn, n)
    in_specs = [pl.BlockSpec((tm, k), lambda i, j: (i, 0)), _layer_spec((k, tn), lambda i, j: (0, j), layer)]
    args = [a, w]
    if residual is not None:
        in_specs.append(pl.BlockSpec((tm, tn), lambda i, j: (i, j)))
        args.append(residual)
    return pl.pallas_call(
        functools.partial(_mm_kernel, has_res=residual is not None),
        grid=(m // tm, n // tn),
        in_specs=in_specs,
        out_specs=pl.BlockSpec((tm, tn), lambda i, j: (i, j)),
        out_shape=jax.ShapeDtypeStruct((m, n), out_dtype),
        compiler_params=_cparams(("parallel", "parallel")),
        name=name,
    )(*args)


def _in_proj_kernel(off_ref, a_ref, w_ref, o_ref):
    o_ref[...] = _dot_nt(a_ref[...], w_ref[...].astype(BF16)).astype(o_ref.dtype)


def in_proj(a, w_t, layer, offsets, *, tm=2048, tn=IN_TN, name):
    m, k = a.shape
    tm = min(tm, m)
    grid_spec = pltpu.PrefetchScalarGridSpec(
        num_scalar_prefetch=1,
        grid=(m // tm, len(offsets)),
        in_specs=[pl.BlockSpec((tm, k), lambda i, j, off: (i, 0)),
                  pl.BlockSpec((None, pl.Element(tn), pl.Element(k)),
                               lambda i, j, off: (layer, off[j] * IN_ALIGN, 0))],
        out_specs=pl.BlockSpec((tm, tn), lambda i, j, off: (i, j)),
    )
    return pl.pallas_call(
        _in_proj_kernel,
        grid_spec=grid_spec,
        out_shape=jax.ShapeDtypeStruct((m, tn * len(offsets)), BF16),
        compiler_params=_cparams(("parallel", "parallel")),
        name=name,
    )(jnp.asarray([o // IN_ALIGN for o in offsets], I32), a, w_t)


def _rope_table_kernel(pos_ref, f64_ref, f32_ref, ca_ref, sa_ref, cb_ref, sb_ref):
    pos = pos_ref[...]
    lane = _iota((1, LANES), 1)
    ang_a = pos * f64_ref[...]
    ang_b = pos * f32_ref[...]
    ca_ref[...] = jnp.cos(ang_a)
    sa_ref[...] = jnp.where(lane < 64, -1.0, 1.0) * jnp.sin(ang_a)
    cb_ref[...] = jnp.cos(ang_b)
    sb_ref[...] = jnp.where(lane % 64 < 32, -1.0, 1.0) * jnp.sin(ang_b)


def rope_tables(positions, tm=512):
    t = positions.size
    pos = positions.reshape(t, 1).astype(F32)
    inv64 = ROPE_THETA ** (-jnp.arange(64, dtype=F32) / 64)
    inv32 = ROPE_THETA ** (-jnp.arange(32, dtype=F32) / 32)
    f64 = jnp.tile(inv64, 2).reshape(1, LANES)
    f32 = jnp.tile(inv32, 4).reshape(1, LANES)
    row = pl.BlockSpec((tm, LANES), lambda i: (i, 0))
    vec = pl.BlockSpec((1, LANES), lambda i: (0, 0))
    return pl.pallas_call(
        _rope_table_kernel,
        grid=(t // tm,),
        in_specs=[pl.BlockSpec((tm, 1), lambda i: (i, 0)), vec, vec],
        out_specs=[row] * 4,
        out_shape=[jax.ShapeDtypeStruct((t, LANES), F32)] * 4,
        compiler_params=_cparams(("parallel",)),
        name="rope_tables",
    )(pos, f64, f32)


SB_TQ = 256
SB_G = 8


def _sb_kernel(q_ref, k_ref, v_ref, o_ref, *, tq, scale):
    qi = pl.program_id(2)
    r = _iota((tq, tq), 0)
    c = _iota((tq, tq), 1)
    later = jnp.where(r > c, 1.0, 0.0).astype(BF16)
    qs = [(q_ref[:, g * SB_DIM:(g + 1) * SB_DIM].astype(F32) * scale).astype(BF16) for g in range(SB_G)]

    heads = range(SB_G)
    cols = [slice(g * SB_DIM, (g + 1) * SB_DIM) for g in heads]

    def span(j, state, diagonal):
        off = pl.multiple_of(j * tq, tq)
        carry, acc = state
        ks = [k_ref[pl.ds(off, tq), cols[g]].astype(BF16) for g in heads]
        vs = [v_ref[pl.ds(off, tq), cols[g]].astype(BF16) for g in heads]
        zs = [_dot_nt(qs[g], ks[g]) for g in heads]
        lss = [jnp.minimum(z, 0.0) - jnp.log(1.0 + jnp.exp(-jnp.abs(z))) for z in zs]
        lks = [lss[g] - zs[g] for g in heads]
        if diagonal:
            lks = [jnp.where(r > c, lk, 0.0) for lk in lks]
        his = [lk.astype(BF16) for lk in lks]
        los = [(lks[g] - his[g].astype(F32)).astype(BF16) for g in heads]
        css = [_dot(his[g], later) + _dot(los[g], later) for g in heads]
        ws = [jnp.exp(lss[g] + css[g] + carry[g]) for g in heads]
        if diagonal:
            ws = [jnp.where(r > c, w, 0.0) for w in ws]
        acc = tuple(acc[g] + _dot(ws[g].astype(BF16), vs[g]) for g in heads)
        carry = tuple(carry[g] + jnp.sum(lks[g], axis=1, keepdims=True) for g in heads)
        return carry, acc

    init = (tuple(jnp.zeros((tq, 1), F32) for _ in heads), tuple(jnp.zeros((tq, SB_DIM), F32) for _ in heads))
    state = span(qi, init, True)
    _, acc = lax.fori_loop(0, qi, lambda i, s: span(qi - 1 - i, s, False), state)
    for g in heads:
        o_ref[:, cols[g]] = acc[g].astype(o_ref.dtype)


def sb_attention(u, batch, seq):
    t = batch * seq
    tq = min(SB_TQ, seq)
    nq = seq // tq
    gw = SB_G * SB_DIM
    qc, kc, vc = COL_SBQ // gw, COL_SBK // gw, COL_SBV // gw
    return pl.pallas_call(
        functools.partial(_sb_kernel, tq=tq, scale=SB_DIM ** -0.5),
        grid=(batch, SB_HEADS // SB_G, nq),
        in_specs=[
            pl.BlockSpec((tq, gw), lambda b, h, i: (b * nq + i, qc + h)),
            pl.BlockSpec((seq, gw), lambda b, h, i: (b, kc + h)),
            pl.BlockSpec((seq, gw), lambda b, h, i: (b, vc + h)),
        ],
        out_specs=pl.BlockSpec((tq, gw), lambda b, h, i: (b * nq + i, h)),
        out_shape=jax.ShapeDtypeStruct((t, BRANCH_W), BF16),
        compiler_params=_cparams(("parallel", "parallel", "arbitrary")),
        name="sb_attention",
    )(u, u, u)


def _rw_pre_kernel(r_ref, k_ref, v_ref, l_ref, pr_ref, pk_ref, pv_ref, plr_ref,
                   mu_ref, w0_ref, a0_ref, kkg_ref, ka_ref, w2_ref, a2_ref, g2_ref,
                   ro_ref, lw_ref, ko_ref, vo_ref, kk_ref, b_ref, g_ref, *, tm, seq):
    i = pl.program_id(0)
    first = (i * tm) % seq == 0
    row0 = _iota((tm, 1), 0) == 0

    def shifted(cur_ref, prev_ref, lo, hi):
        cur = cur_ref[...].astype(F32)
        last = prev_ref[...].astype(F32)[-1:, :]
        last = jnp.where(first, 0.0, last)
        prev = jnp.where(row0, last, pltpu.roll(cur, 1, 0))
        return cur + mu_ref[:, lo:hi] * (prev - cur)

    w = BRANCH_W
    r = shifted(r_ref, pr_ref, 0, w)
    k = shifted(k_ref, pk_ref, w, 2 * w)
    v = shifted(v_ref, pv_ref, 2 * w, 3 * w)
    z = shifted(l_ref, plr_ref, 3 * w, 3 * w + RW_LORA)

    w_pre = w0_ref[...] + _dot(jnp.tanh(z).astype(BF16), w2_ref[...])
    w_log = -_softplus(-w_pre) - 0.5
    lw_ref[...] = -jnp.exp(w_log)
    a = _sigmoid(a0_ref[...] + _dot(z.astype(BF16), a2_ref[...]))
    g_ref[...] = _dot(_sigmoid(z).astype(BF16), g2_ref[...]).astype(g_ref.dtype)
    ones = _group_matrix(LANES, RW_DIM, 1.0)
    kk = k * kkg_ref[...]
    for s in range(w // LANES):
        sl = slice(s * LANES, (s + 1) * LANES)
        kks = kk[:, sl]
        ss = _dot_exact_rhs(kks * kks, ones)
        kkn = kks / jnp.maximum(jnp.sqrt(ss), 1e-12)
        kk_ref[:, sl] = kkn
        b_ref[:, sl] = kkn * a[:, sl]
    ro_ref[...] = r
    ko_ref[...] = k * (1.0 + (a - 1.0) * ka_ref[...])
    vo_ref[...] = v


def rw_prepare(u, mu, w0, a0, k_k, k_a, w2p, a2p, g2p, batch, seq, tm=256):
    t = batch * seq
    tm = min(tm, seq)
    w = BRANCH_W
    sub = 16

    def cur(width, col):
        return pl.BlockSpec((tm, width), lambda i: (i, col // width))

    def prev(width, col):
        return pl.BlockSpec((sub, width), lambda i: (jnp.maximum(i * (tm // sub) - 1, 0), col // width))

    def vec(width):
        return pl.BlockSpec((1, width), lambda i: (0, 0))

    def mat():
        return pl.BlockSpec((RW_LORA, w), lambda i: (0, 0))

    out = pl.BlockSpec((tm, w), lambda i: (i, 0))
    f = jax.ShapeDtypeStruct((t, w), F32)
    return pl.pallas_call(
        functools.partial(_rw_pre_kernel, tm=tm, seq=seq),
        grid=(t // tm,),
        in_specs=[cur(w, COL_RWR), cur(w, COL_RWK), cur(w, COL_RWV), cur(RW_LORA, COL_RWL),
                  prev(w, COL_RWR), prev(w, COL_RWK), prev(w, COL_RWV), prev(RW_LORA, COL_RWL),
                  vec(3 * w + RW_LORA), vec(w), vec(w), vec(w), vec(w), mat(), mat(), mat()],
        out_specs=[out] * 7,
        out_shape=[f, f, f, f, f, f, jax.ShapeDtypeStruct((t, w), BF16)],
        compiler_params=_cparams(("parallel",)),
        name="rw_prepare",
    )(u, u, u, u, u, u, u, u, mu, w0, a0, k_k, k_a, w2p, a2p, g2p)


def _rw_scan_kernel(r_ref, lw_ref, k_ref, v_ref, kk_ref, b_ref, g_ref, rk_ref, lng_ref, lnb_ref,
                    o_ref, st_ref, *, chunk):
    @pl.when(pl.program_id(1) == 0)
    def _():
        st_ref[...] = jnp.zeros_like(st_ref)

    npair = RW_HEADS // 2
    w = npair * LANES
    n = 2 * chunk
    lane = _iota((n, w), 1)
    row = _iota((n, w), 0)
    own = ((lane // RW_DIM) % 2) == (row // chunk)
    rr = _iota((n, n), 0)
    cc = _iota((n, n), 1)
    same = (rr // chunk) == (cc // chunk)
    tri_incl = jnp.where(same & (rr >= cc), 1.0, 0.0).astype(BF16)
    strict = same & (rr > cc)
    incl = same & (rr >= cc)
    eye = jnp.where(rr == cc, 1.0, 0.0)
    blockdiag = (_iota((LANES, LANES), 0) // RW_DIM) == (_iota((LANES, LANES), 1) // RW_DIM)
    avg = _group_matrix(LANES, RW_DIM, 1.0 / RW_DIM)
    ones = _group_matrix(LANES, RW_DIM, 1.0)
    pairs = range(npair)
    sl = [slice(p * LANES, (p + 1) * LANES) for p in pairs]

    def stack(x):
        return jnp.concatenate([x, x], axis=0)

    def per_pair_rows(x):
        return jnp.concatenate([x[:, s] for s in sl], axis=0)

    def per_pair_lanes(x):
        return jnp.concatenate([x[p * chunk:(p + 1) * chunk] for p in pairs], axis=1)

    r, lw, k, v, kk, b = (ref[...] for ref in (r_ref, lw_ref, k_ref, v_ref, kk_ref, b_ref))
    lw2 = stack(lw)
    l1 = lw2.astype(BF16)
    l2 = (lw2 - l1.astype(F32)).astype(BF16)
    lin = _dot(tri_incl, l1) + _dot(tri_incl, l2)
    lend = lin[n - 1:n]
    p_inv = jnp.exp(-lin)
    p_dec = jnp.exp(lend - lin)
    zero = jnp.zeros((n, w), F32)
    kk_t = jnp.where(own, stack(kk) * jnp.exp(lin - lw2), zero).astype(BF16)
    r_t = jnp.where(own, stack(r) * jnp.exp(lin), zero).astype(BF16)
    v_f = jnp.where(own, stack(v), zero)
    v_s = v_f.astype(BF16)
    b_t = (stack(b) * p_inv).astype(BF16)
    k_t = (stack(k) * p_inv).astype(BF16)
    b_d = (stack(b) * p_dec).astype(BF16)
    k_d = (stack(k) * p_dec).astype(BF16)
    p_end = jnp.exp(lend)

    lhs = [jnp.concatenate([kk_t[:, s], r_t[:, s]], axis=0) for s in sl]
    a_b = [_dot_nt(lhs[p], b_t[:, sl[p]]) for p in pairs]
    a_k = [_dot_nt(lhs[p], k_t[:, sl[p]]) for p in pairs]
    s_t = [_dot_nt(lhs[p], st_ref[p].astype(BF16)) for p in pairs]
    n_ab = [jnp.where(strict, a_b[p][:n], 0.0) for p in pairs]
    rhs = [s_t[p][:n] + _dot(jnp.where(strict, a_k[p][:n], 0.0).astype(BF16), v_s[:, sl[p]]) for p in pairs]
    inv = [eye - n_ab[p] for p in pairs]
    pw = n_ab
    for _ in range(int(math.log2(chunk)) - 1):
        pw_b = [pw[p].astype(BF16) for p in pairs]
        pw = [_dot(pw_b[p], pw_b[p]) for p in pairs]
        inv = [inv[p] + _dot(inv[p].astype(BF16), pw[p].astype(BF16)) for p in pairs]
    u_s = [-_dot(inv[p].astype(BF16), rhs[p].astype(BF16)) for p in pairs]
    y2 = [s_t[p][n:] + _dot(jnp.where(incl, a_b[p][n:], 0.0).astype(BF16), u_s[p].astype(BF16))
          + _dot(jnp.where(incl, a_k[p][n:], 0.0).astype(BF16), v_s[:, sl[p]]) for p in pairs]
    for p in pairs:
        upd = _dot(u_s[p].T.astype(BF16), b_d[:, sl[p]]) + _dot(v_f[:, sl[p]].T.astype(BF16), k_d[:, sl[p]])
        st_ref[p] = st_ref[p] * p_end[:, sl[p]] + jnp.where(blockdiag, upd, 0.0)

    yr = jnp.concatenate([y2[p][:chunk] + y2[p][chunk:] for p in pairs], axis=0)
    yr1, yr2, _ = _split3(yr)
    d = yr - (_dot(yr1, avg) + _dot(yr2, avg))
    dd1, dd2, _ = _split3(d * d)
    var = _dot(dd1, avg) + _dot(dd2, avg)
    yn = per_pair_lanes(d * lax.rsqrt(var + RW_GN_EPS)) * lng_ref[...] + lnb_ref[...]
    bonus = per_pair_lanes(_dot(per_pair_rows(r * k * rk_ref[...]).astype(BF16), ones)) * v
    o_ref[...] = ((yn + bonus) * g_ref[...].astype(F32)).astype(o_ref.dtype)


def rw_scan(r, lw, k, v, kk, b, g, r_k, ln_g, ln_b, batch, seq):
    t = batch * seq
    chunk = min(RW_CHUNK, seq)
    nc = seq // chunk
    w = BRANCH_W
    blk = pl.BlockSpec((chunk, w), lambda bb, c: (bb * nc + c, 0))
    vec = pl.BlockSpec((1, w), lambda bb, c: (0, 0))
    return pl.pallas_call(
        functools.partial(_rw_scan_kernel, chunk=chunk),
        grid=(batch, nc),
        in_specs=[blk] * 7 + [vec] * 3,
        out_specs=blk,
        out_shape=jax.ShapeDtypeStruct((t, w), BF16),
        scratch_shapes=[pltpu.VMEM((RW_HEADS // 2, LANES, LANES), F32)],
        compiler_params=_cparams(("parallel", "arbitrary")),
        name="rw_scan",
    )(r, lw, k, v, kk, b, g, r_k, ln_g, ln_b)


def _rope(u, cos, sin, half):
    return u * cos + _swap_halves(u, half) * sin


DSA_Q_BLK = 256
DSA_IQ_BLK = 512


def _dsa_pre_kernel(*refs):
    nq, niq = BRANCH_W // DSA_Q_BLK, BRANCH_W // DSA_IQ_BLK
    q_refs, refs = refs[:nq], refs[nq:]
    c_ref, refs = refs[0], refs[1:]
    iq_refs, refs = refs[:niq], refs[niq:]
    (ikw_ref, ca_ref, sa_ref, cb_ref, sb_ref, qg_ref, kg_ref, cg_ref, wuk_ref, wuv_ref,
     qo_ref, ko_ref, vo_ref, iqo_ref, iko_ref, iwo_ref) = refs
    ca, sa, cb, sb = ca_ref[...], sa_ref[...], cb_ref[...], sb_ref[...]
    for h in range(DSA_HEADS):
        sl = slice(h * DSA_DIM, (h + 1) * DSA_DIM)
        per = DSA_Q_BLK // DSA_DIM
        q = q_refs[h // per][:, (h % per) * DSA_DIM:(h % per + 1) * DSA_DIM].astype(F32)
        q = q * lax.rsqrt(jnp.mean(q * q, axis=-1, keepdims=True) + NORM_EPS) * qg_ref[...]
        qo_ref[:, sl] = (_rope(q, ca, sa, 64) * DSA_DIM ** -0.5).astype(qo_ref.dtype)
    c = c_ref[...].astype(F32)
    c = (c * lax.rsqrt(jnp.mean(c * c, axis=-1, keepdims=True) + NORM_EPS) * cg_ref[...]).astype(BF16)
    k = _dot(c, wuk_ref[...].astype(BF16))
    k = k * lax.rsqrt(jnp.mean(k * k, axis=-1, keepdims=True) + NORM_EPS) * kg_ref[...]
    ko_ref[...] = _rope(k, ca, sa, 64).astype(ko_ref.dtype)
    vo_ref[...] = _dot(c, wuv_ref[...].astype(BF16)).astype(vo_ref.dtype)
    for s in range(IDX_HEADS * IDX_DIM // LANES):
        sl = slice(s * LANES, (s + 1) * LANES)
        per = DSA_IQ_BLK // LANES
        iq = iq_refs[s // per][:, (s % per) * LANES:(s % per + 1) * LANES].astype(F32)
        iqo_ref[:, sl] = _rope(iq, cb, sb, 32).astype(iqo_ref.dtype)
    ikw = ikw_ref[...].astype(F32)
    lane = _iota(ikw.shape, 1)
    ik = _rope(ikw, cb, sb, 32)
    iko_ref[...] = jnp.where(lane < IDX_DIM, ik, pltpu.roll(ik, IDX_DIM, 1)).astype(iko_ref.dtype)
    iw = pltpu.roll(ikw, IDX_DIM, 1) * (IDX_HEADS ** -0.5 * IDX_DIM ** -0.5)
    iwo_ref[...] = jnp.where(lane < IDX_HEADS, iw, 0.0)


def _dsa_kernel(q_ref, iq_ref, iw_ref, k_ref, v_ref, ik_ref, o_ref, key_ref, bias_ref, vt_ref, m_ref, l_ref, acc_ref,
                *, tq, seq, top_k):
    kb = tq
    qb = pl.program_id(1)
    nblk = qb + 1
    nh = DSA_HEADS
    int_min = jnp.int32(-2 ** 31)
    kidx = _iota((kb, tq), 0)
    ridx = _iota((kb, tq), 1)

    def causal_mask(j):
        return (kidx + j * kb) <= (ridx + qb * tq)

    def block(ref, j):
        return ref[pl.ds(pl.multiple_of(j * kb, kb), kb), :]

    def fold(x, op):
        return op(x.reshape(kb // 8, 8, x.shape[1]), axis=0)

    @pl.when(qb == 0)
    def _():
        for j in range(seq // LANES):
            vt_ref[:, j * LANES:(j + 1) * LANES] = v_ref[j * LANES:(j + 1) * LANES, :].astype(F32).T.astype(BF16)

    first = _iota((tq, LANES), 1) < IDX_DIM
    lhs = []
    for p in range(IDX_HEADS // 2):
        qp = iq_ref[:, p * LANES:(p + 1) * LANES]
        zero = jnp.zeros_like(qp)
        lhs.append(jnp.concatenate([jnp.where(first, qp, zero), jnp.where(first, zero, qp)], axis=0))
    iw_t = iw_ref[...].T
    w_row = [iw_t[h:h + 1] for h in range(IDX_HEADS)]

    def score_block(j, _):
        ik = block(ik_ref, j)
        sc = jnp.zeros((kb, tq), F32)
        for p in range(IDX_HEADS // 2):
            z = jnp.maximum(_dot_nt(ik, lhs[p]), 0.0)
            sc = sc + z[:, :tq] * w_row[2 * p] + z[:, tq:] * w_row[2 * p + 1]
        sc = sc + 0.0
        bits = lax.bitcast_convert_type(sc, I32)
        skey = bits ^ ((bits >> 31) & jnp.int32(0x7FFFFFFF))
        key_ref[pl.ds(pl.multiple_of(j * kb, kb), kb), :] = jnp.where(causal_mask(j), skey, int_min)
        return 0

    lax.fori_loop(0, nblk, score_block, 0)

    def count(pred_fn):
        def body(j, cnt):
            return cnt + fold(jnp.where(pred_fn(block(key_ref, j)), 1.0, 0.0), jnp.sum)
        return jnp.sum(lax.fori_loop(0, nblk, body, jnp.zeros((8, tq), F32)), axis=0, keepdims=True)

    def bit_step(i, thr):
        cand = thr ^ (jnp.int32(1) << (31 - i))
        return jnp.where(count(lambda keys: keys >= cand) >= top_k, cand, thr)

    thr = lax.fori_loop(0, 32, bit_step, jnp.full((1, tq), int_min, I32))
    need = top_k - count(lambda keys: keys > thr)

    q_all = jnp.concatenate([q_ref[:, h * DSA_DIM:(h + 1) * DSA_DIM] for h in range(nh)], axis=0)
    lower = jnp.where(_iota((kb, kb), 0) > _iota((kb, kb), 1), 1.0, 0.0).astype(BF16)
    m_ref[...] = jnp.full_like(m_ref, -1e30)

    def logits(j, bias):
        lt = _dot_nt(block(k_ref, j), q_all)
        return jnp.concatenate([lt[:, h * tq:(h + 1) * tq] + bias for h in range(nh)], axis=1)

    def select(j, ties_seen):
        keys = block(key_ref, j)
        causal = causal_mask(j)
        tie = causal & (keys == thr)
        tie_f = jnp.where(tie, 1.0, 0.0)
        rank = ties_seen + _dot(lower, tie_f.astype(BF16))
        sel = causal & ((keys > thr) | (tie & (rank < need)))
        bias = jnp.where(sel, 0.0, -1e30)
        bias_ref[pl.ds(pl.multiple_of(j * kb, kb), kb), :] = bias
        m_ref[...] = jnp.maximum(m_ref[...], fold(logits(j, bias), jnp.max))
        return ties_seen + jnp.sum(tie_f, axis=0, keepdims=True)

    lax.fori_loop(0, nblk, select, jnp.zeros((1, tq), F32))
    m_all = jnp.broadcast_to(jnp.max(m_ref[...], axis=0, keepdims=True), m_ref.shape)
    l_ref[...] = jnp.zeros_like(l_ref)
    acc_ref[...] = jnp.zeros_like(acc_ref)

    def attend(j, _):
        lt = logits(j, block(bias_ref, j))
        pr = jnp.exp(lt.reshape(kb // 8, 8, nh * tq) - m_all[None]).reshape(kb, nh * tq)
        l_ref[...] += fold(pr, jnp.sum)
        acc_ref[...] += _dot(vt_ref[:, pl.ds(pl.multiple_of(j * kb, kb), kb)], pr.astype(BF16))
        return 0

    lax.fori_loop(0, nblk, attend, 0)
    out_t = acc_ref[...] / jnp.sum(l_ref[...], axis=0, keepdims=True)
    for h in range(nh):
        o_ref[:, h * DSA_DIM:(h + 1) * DSA_DIM] = out_t[:, h * tq:(h + 1) * tq].T.astype(o_ref.dtype)


def dsa_attention(u, tables, q_gain, k_gain, kv_gain, w_uk, w_uv, batch, seq, tm=256, tq=256):
    t = batch * seq
    tm = min(tm, seq)
    ca, sa, cb, sb = tables
    w = BRANCH_W
    row = lambda width, col: pl.BlockSpec((tm, width), lambda i: (i, col // width))
    tab = pl.BlockSpec((tm, LANES), lambda i: (i, 0))
    vec = lambda width: pl.BlockSpec((1, width), lambda i: (0, 0))
    mat = pl.BlockSpec((DSA_KV_RANK, DSA_DIM), lambda i: (0, 0))
    o_w = pl.BlockSpec((tm, w), lambda i: (i, 0))
    o_n = pl.BlockSpec((tm, LANES), lambda i: (i, 0))
    q, k, v, iq, ik, iw = pl.pallas_call(
        _dsa_pre_kernel,
        grid=(t // tm,),
        in_specs=[row(DSA_Q_BLK, COL_DSAQ + j * DSA_Q_BLK) for j in range(w // DSA_Q_BLK)]
        + [row(DSA_KV_RANK, COL_CKV)]
        + [row(DSA_IQ_BLK, COL_IDXQ + j * DSA_IQ_BLK) for j in range(w // DSA_IQ_BLK)]
        + [row(LANES, COL_IDXKW), tab, tab, tab, tab, vec(DSA_DIM), vec(DSA_DIM), vec(DSA_KV_RANK), mat, mat],
        out_specs=[o_w, o_n, o_n, o_w, o_n, o_n],
        out_shape=[jax.ShapeDtypeStruct((t, w), BF16), jax.ShapeDtypeStruct((t, LANES), BF16),
                   jax.ShapeDtypeStruct((t, LANES), BF16), jax.ShapeDtypeStruct((t, w), BF16),
                   jax.ShapeDtypeStruct((t, LANES), BF16), jax.ShapeDtypeStruct((t, LANES), F32)],
        compiler_params=_cparams(("parallel",)),
        name="dsa_prepare",
    )(*([u] * (w // DSA_Q_BLK + 1 + w // DSA_IQ_BLK + 1)), ca, sa, cb, sb,
      q_gain.reshape(1, -1), k_gain.reshape(1, -1), kv_gain.reshape(1, -1), w_uk, w_uv)

    nq = seq // tq
    top_k = min(IDX_TOPK_MAX, seq // 4)
    qrow = lambda width: pl.BlockSpec((tq, width), lambda b, i: (b * nq + i, 0))
    full = pl.BlockSpec((seq, LANES), lambda b, i: (b, 0))
    return pl.pallas_call(
        functools.partial(_dsa_kernel, tq=tq, seq=seq, top_k=top_k),
        grid=(batch, nq),
        in_specs=[qrow(w), qrow(w), qrow(LANES), full, full, full],
        out_specs=qrow(w),
        out_shape=jax.ShapeDtypeStruct((t, w), BF16),
        scratch_shapes=[pltpu.VMEM((seq, tq), I32), pltpu.VMEM((seq, tq), F32), pltpu.VMEM((DSA_DIM, seq), BF16),
                        pltpu.VMEM((8, DSA_HEADS * tq), F32), pltpu.VMEM((8, DSA_HEADS * tq), F32),
                        pltpu.VMEM((DSA_DIM, DSA_HEADS * tq), F32)],
        compiler_params=_cparams(("parallel", "arbitrary")),
        name="dsa_attention",
    )(q, iq, iw, k, v, ik)


def _sw_kernel(sink_ref, q_ref, kp_ref, kc_ref, vp_ref, vc_ref, cbp_ref, sbp_ref, cbc_ref, sbc_ref,
               qg_ref, kg_ref, o_ref, *, blk):
    n = pl.program_id(1)
    avg = _group_matrix(LANES, SW_DIM, 1.0 / SW_DIM)
    lane2 = _iota((2 * blk, LANES), 1)
    lane1 = _iota((blk, LANES), 1)

    def norm(x, gain):
        s1, s2, _ = _split3(x * x)
        return x * lax.rsqrt(_dot(s1, avg) + _dot(s2, avg) + NORM_EPS) * gain

    cb = jnp.concatenate([cbp_ref[...], cbc_ref[...]], axis=0)
    sb = jnp.concatenate([sbp_ref[...], sbc_ref[...]], axis=0)
    k = jnp.concatenate([kp_ref[...], kc_ref[...]], axis=0).astype(F32)
    k = _rope(norm(k, kg_ref[...]), cb, sb, 32)
    v = jnp.concatenate([vp_ref[...], vc_ref[...]], axis=0).astype(F32)
    k_sw, v_sw = pltpu.roll(k, SW_DIM, 1), pltpu.roll(v, SW_DIM, 1)
    k2 = [jnp.where(lane2 < SW_DIM, k, k_sw).astype(BF16), jnp.where(lane2 < SW_DIM, k_sw, k).astype(BF16)]
    v2 = [jnp.where(lane2 < SW_DIM, v, v_sw).astype(BF16), jnp.where(lane2 < SW_DIM, v_sw, v).astype(BF16)]

    r = _iota((2 * blk, 2 * blk), 0) % blk
    c = _iota((2 * blk, 2 * blk), 1)
    dist = r - (c - blk)
    mask = (dist >= 0) & (dist < WINDOW) & ((c >= blk) | (n > 0))
    top = _iota((2 * blk, 1), 0) < blk
    pairs = range(SW_HEADS // 2)
    group = [(2 * p) // (SW_HEADS // SW_KV_HEADS) for p in pairs]
    npair = len(pairs)
    q = jnp.concatenate([q_ref[:, p * LANES:(p + 1) * LANES].astype(F32) for p in pairs], axis=0)
    cbc = jnp.concatenate([cbc_ref[...]] * npair, axis=0)
    sbc = jnp.concatenate([sbc_ref[...]] * npair, axis=0)
    q = _rope(norm(q, qg_ref[...]), cbc, sbc, 32) * SW_DIM ** -0.5
    low = _iota(q.shape, 1) < SW_DIM
    q_a = jnp.where(low, q, 0.0).astype(BF16)
    q_b = jnp.where(low, 0.0, q).astype(BF16)
    rows = [slice(p * blk, (p + 1) * blk) for p in pairs]
    lhs = [jnp.concatenate([q_a[rows[p]], q_b[rows[p]]], axis=0) for p in pairs]
    logit = [jnp.where(mask, _dot_nt(lhs[p], k2[group[p]]), -1e30) for p in pairs]
    sink = [jnp.where(top, sink_ref[2 * p], sink_ref[2 * p + 1]) for p in pairs]
    m = [jnp.maximum(jnp.max(logit[p], axis=1, keepdims=True), sink[p]) for p in pairs]
    pr = [jnp.exp(logit[p] - m[p]) for p in pairs]
    den = [jnp.sum(pr[p], axis=1, keepdims=True) + jnp.exp(sink[p] - m[p]) for p in pairs]
    o2 = [_dot(pr[p].astype(BF16), v2[group[p]]) / den[p] for p in pairs]
    for p in pairs:
        o_ref[:, p * LANES:(p + 1) * LANES] = jnp.where(lane1 < SW_DIM, o2[p][:blk], o2[p][blk:]).astype(o_ref.dtype)


def sw_attention(u, tables, q_gain, k_gain, sinks, batch, seq, blk=128):
    t = batch * seq
    nb = seq // blk
    _, _, cb, sb = tables
    w = BRANCH_W
    cur = lambda width, col: pl.BlockSpec((blk, width), lambda b, i: (b * nb + i, col // width))
    prev = lambda width, col: pl.BlockSpec((blk, width), lambda b, i: (b * nb + jnp.maximum(i - 1, 0), col // width))
    vec = pl.BlockSpec((1, LANES), lambda b, i: (0, 0))
    tile2 = lambda g: jnp.tile(g.reshape(1, SW_DIM), (1, 2))
    return pl.pallas_call(
        functools.partial(_sw_kernel, blk=blk),
        grid=(batch, nb),
        in_specs=[pl.BlockSpec(memory_space=pltpu.SMEM),
                  cur(w, COL_SWQ), prev(LANES, COL_SWK), cur(LANES, COL_SWK), prev(LANES, COL_SWV), cur(LANES, COL_SWV),
                  prev(LANES, 0), prev(LANES, 0), cur(LANES, 0), cur(LANES, 0), vec, vec],
        out_specs=pl.BlockSpec((blk, w), lambda b, i: (b * nb + i, 0)),
        out_shape=jax.ShapeDtypeStruct((t, w), BF16),
        compiler_params=_cparams(("parallel", "parallel")),
        name="sw_attention",
    )(sinks, u, u, u, u, u, cb, sb, cb, sb, tile2(q_gain), tile2(k_gain))


def _merge_kernel(*refs):
    y_refs = refs[:N_BRANCH]
    g_ref, w_ref, o_ref, acc_ref = refs[N_BRANCH:]
    n = pl.program_id(2)
    for b in range(N_BRANCH):
        @pl.when(n == b)
        def _():
            contrib = _sigmoid(g_ref[...].astype(F32)) * _dot(y_refs[b][...], w_ref[...].astype(BF16))
            if b == 0:
                acc_ref[...] = contrib
            elif b < N_BRANCH - 1:
                acc_ref[...] += contrib
            else:
                o_ref[...] = (acc_ref[...] + contrib).astype(o_ref.dtype)


def merge_branches(ys, u, w_branch, layer, tm=1024, tc=1024):
    t, w = ys[0].shape
    d = w_branch.shape[-1]
    tm = min(tm, t)
    nc = d // tc
    y_spec = pl.BlockSpec((tm, w), lambda i, c, n: (i, 0))
    return pl.pallas_call(
        _merge_kernel,
        grid=(t // tm, nc, N_BRANCH),
        in_specs=[y_spec] * N_BRANCH + [pl.BlockSpec((tm, tc), lambda i, c, n: (i, COL_GATE // tc + n * nc + c)),
                                        _layer_spec((None, w, tc), lambda i, c, n: (n, 0, c), layer)],
        out_specs=pl.BlockSpec((tm, tc), lambda i, c, n: (i, c)),
        out_shape=jax.ShapeDtypeStruct((t, d), BF16),
        scratch_shapes=[pltpu.VMEM((tm, tc), F32)],
        compiler_params=_cparams(("parallel", "arbitrary", "arbitrary")),
        name="merge_branches",
    )(*ys, u, w_branch)


FFN_TM = 1024
FFN_ROWS = (1024, 768, 512, 384, 256, 128)
FFN_TF = 512
FFN_VMEM_BYTES = 60 * 1024 * 1024


def _ffn_kernel(e_ref, rows_ref, x_ref, wg_ref, wu_ref, wd_ref, o_ref):
    rows = rows_ref[pl.program_id(0)]

    @pl.when(pl.program_id(1) == 0)
    def _():
        o_ref[...] = jnp.zeros_like(o_ref)

    for k, size in enumerate(FFN_ROWS):
        below = FFN_ROWS[k + 1] if k + 1 < len(FFN_ROWS) else 0

        @pl.when((rows > below) & (rows <= size))
        def _():
            x = x_ref[:size, :]
            g = _dot(x, wg_ref[0].astype(BF16))
            a = (g * _sigmoid(g) * _dot(x, wu_ref[0].astype(BF16))).astype(BF16)
            o_ref[:size, :] += _dot(a, wd_ref[0].astype(BF16))


def ffn_tiles(xs, tile_expert, tile_rows, w_gate, w_up, w_down):
    r, d = xs.shape
    _, _, ff = w_gate.shape
    ns = r // FFN_TM
    nf = ff // FFN_TF

    def f_eff(s, f, rows):
        return jnp.where(rows[s] > 0, f, nf - 1)

    grid_spec = pltpu.PrefetchScalarGridSpec(
        num_scalar_prefetch=2,
        grid=(ns, nf),
        in_specs=[pl.BlockSpec((FFN_TM, d), lambda s, f, e, rows: (s, 0)),
                  pl.BlockSpec((1, d, FFN_TF), lambda s, f, e, rows: (e[s], 0, f_eff(s, f, rows))),
                  pl.BlockSpec((1, d, FFN_TF), lambda s, f, e, rows: (e[s], 0, f_eff(s, f, rows))),
                  pl.BlockSpec((1, FFN_TF, d), lambda s, f, e, rows: (e[s], f_eff(s, f, rows), 0))],
        out_specs=pl.BlockSpec((FFN_TM, d), lambda s, f, e, rows: (s, 0)),
    )
    return pl.pallas_call(
        _ffn_kernel,
        grid_spec=grid_spec,
        out_shape=jax.ShapeDtypeStruct((r, d), F32),
        compiler_params=_cparams(("parallel", "arbitrary"), FFN_VMEM_BYTES),
        name="ffn_tiles",
    )(tile_expert, tile_rows, xs, w_gate, w_up, w_down)


def _store_with_norm(x, gain_ref, o_ref, h_ref):
    o_ref[...] = x
    ms = jnp.mean(x * x, axis=-1, keepdims=True)
    h_ref[...] = (x * lax.rsqrt(ms + NORM_EPS) * gain_ref[...]).astype(h_ref.dtype)


def _add_norm_kernel(a_ref, b_ref, g_ref, o_ref, h_ref):
    _store_with_norm(a_ref[...] + b_ref[...], g_ref, o_ref, h_ref)


def add_norm(a, b, gain, tm=512):
    t, d = a.shape
    spec = pl.BlockSpec((tm, d), lambda i: (i, 0))
    return pl.pallas_call(
        _add_norm_kernel, grid=(t // tm,),
        in_specs=[spec, spec, pl.BlockSpec((1, d), lambda i: (0, 0))], out_specs=[spec, spec],
        out_shape=[jax.ShapeDtypeStruct((t, d), a.dtype), jax.ShapeDtypeStruct((t, d), BF16)],
        compiler_params=_cparams(("parallel",)), name="residual_add_norm")(a, b, gain.reshape(1, d))


def dense_ffn(x, h, w_gate, w_up, w_down, index, next_gain):
    t = h.shape[0]
    ns = t // FFN_TM
    ys = ffn_tiles(h, jnp.full((ns,), index, I32), jnp.full((ns,), FFN_TM, I32), w_gate, w_up, w_down)
    return add_norm(x, ys, next_gain)


def _router_kernel(x_ref, g_ref, w_ref, h_ref, r_ref):
    x = x_ref[...]
    h = x * lax.rsqrt(jnp.mean(x * x, axis=-1, keepdims=True) + NORM_EPS) * g_ref[...]
    h_ref[...] = h
    logit = _dot_f32(h, w_ref[...])
    lane = _iota(logit.shape, 1).astype(F32)
    neg = -jnp.inf
    l1 = jnp.where(lane < N_EXPERTS, logit, neg)
    m1 = jnp.max(l1, axis=1, keepdims=True)
    i1 = jnp.min(jnp.where(l1 == m1, lane, float(LANES)), axis=1, keepdims=True)
    l2 = jnp.where(lane == i1, neg, l1)
    m2 = jnp.max(l2, axis=1, keepdims=True)
    i2 = jnp.min(jnp.where(l2 == m2, lane, float(LANES)), axis=1, keepdims=True)
    e = jnp.exp(m2 - m1)
    w1 = 1.0 / (1.0 + e)
    w2 = e / (1.0 + e)
    r_ref[...] = jnp.where(lane == 0, i1, jnp.where(lane == 1, i2, jnp.where(lane == 2, w1,
                           jnp.where(lane == 3, w2, 0.0))))


def route(x, gain, router, tm=256):
    t, d = x.shape
    wp = jnp.zeros((d, LANES), F32).at[:, :N_EXPERTS].set(router)
    return pl.pallas_call(
        _router_kernel,
        grid=(t // tm,),
        in_specs=[pl.BlockSpec((tm, d), lambda i: (i, 0)), pl.BlockSpec((1, d), lambda i: (0, 0)),
                  pl.BlockSpec((d, LANES), lambda i: (0, 0))],
        out_specs=[pl.BlockSpec((tm, d), lambda i: (i, 0)), pl.BlockSpec((tm, LANES), lambda i: (i, 0))],
        out_shape=[jax.ShapeDtypeStruct((t, d), F32), jax.ShapeDtypeStruct((t, LANES), F32)],
        compiler_params=_cparams(("parallel",)),
        name="router",
    )(x, gain.reshape(1, d), wp)


def _row_copy(src_ref, row, buf, i, sem):
    return pltpu.make_async_copy(src_ref.at[pl.ds(row, 1)], buf.at[pl.ds(i, 1)], sem)


ROW_DMA_UNROLL = 8


def _row_loop(n, fn):
    def body(blk, _):
        for u in range(ROW_DMA_UNROLL):
            fn(blk * ROW_DMA_UNROLL + u, u)
        return 0
    lax.fori_loop(0, n // ROW_DMA_UNROLL, body, 0)


GATHER_ROWS = 256


def _gather_kernel(idx_ref, live_ref, src_ref, o_ref, buf, sem):
    gb = GATHER_ROWS
    i = pl.program_id(0)
    n = pl.num_programs(0)

    def start_step(step):
        slot = step % 2

        @pl.when(live_ref[step] > 0)
        def _():
            _row_loop(gb, lambda r, u: _row_copy(src_ref, idx_ref[step * gb + r], buf.at[slot], r,
                                                 sem.at[slot]).start(priority=u % 2))

    @pl.when(i == 0)
    def _():
        buf[...] = jnp.zeros_like(buf)
        start_step(i)

    @pl.when(i + 1 < n)
    def _():
        start_step(i + 1)

    slot = i % 2

    @pl.when(live_ref[i] > 0)
    def _():
        _row_loop(gb, lambda r, u: _row_copy(src_ref, 0, buf.at[slot], r, sem.at[slot]).wait())

    o_ref[...] = buf[slot].astype(o_ref.dtype)


def gather_rows(src, idx, live, out_dtype):
    r = idx.shape[0]
    d = src.shape[1]
    gb = GATHER_ROWS
    grid_spec = pltpu.PrefetchScalarGridSpec(
        num_scalar_prefetch=2,
        grid=(r // gb,),
        in_specs=[pl.BlockSpec(memory_space=pl.ANY)],
        out_specs=pl.BlockSpec((gb, d), lambda i, idx, live: (i, 0)),
        scratch_shapes=[pltpu.VMEM((2, gb, d), src.dtype), pltpu.SemaphoreType.DMA((2,))],
    )
    return pl.pallas_call(
        _gather_kernel,
        grid_spec=grid_spec,
        out_shape=jax.ShapeDtypeStruct((r, d), out_dtype),
        compiler_params=_cparams(("arbitrary",)),
        name="gather_rows",
    )(idx, live, src)


def _combine_kernel(d0_ref, d1_ref, x_ref, w_ref, g_ref, ys_ref, o_ref, h_ref, buf, sem, *, tm):
    i = pl.program_id(0)
    n = pl.num_programs(0)

    def start_step(step):
        slot = step % 2

        def one(r, u):
            _row_copy(ys_ref, d0_ref[step * tm + r], buf.at[slot, 0], r, sem.at[slot]).start(priority=0)
            _row_copy(ys_ref, d1_ref[step * tm + r], buf.at[slot, 1], r, sem.at[slot]).start(priority=1)
        _row_loop(tm, one)

    @pl.when(i == 0)
    def _():
        start_step(i)

    @pl.when(i + 1 < n)
    def _():
        start_step(i + 1)

    slot = i % 2

    def wait_one(r, u):
        _row_copy(ys_ref, 0, buf.at[slot, 0], r, sem.at[slot]).wait()
        _row_copy(ys_ref, 0, buf.at[slot, 1], r, sem.at[slot]).wait()
    _row_loop(tm, wait_one)
    w = w_ref[...]
    _store_with_norm(x_ref[...] + w[:, 2:3] * buf[slot, 0] + w[:, 3:4] * buf[slot, 1], g_ref, o_ref, h_ref)


def combine_rows(x, ys, d0, d1, w, next_gain, tm=128):
    t, d = x.shape
    row = pl.BlockSpec((tm, d), lambda i, a, b: (i, 0))
    grid_spec = pltpu.PrefetchScalarGridSpec(
        num_scalar_prefetch=2,
        grid=(t // tm,),
        in_specs=[row, pl.BlockSpec((tm, LANES), lambda i, a, b: (i, 0)), pl.BlockSpec((1, d), lambda i, a, b: (0, 0)),
                  pl.BlockSpec(memory_space=pl.ANY)],
        out_specs=[row, row],
        scratch_shapes=[pltpu.VMEM((2, 2, tm, d), F32), pltpu.SemaphoreType.DMA((2,))],
    )
    return pl.pallas_call(
        functools.partial(_combine_kernel, tm=tm),
        grid_spec=grid_spec,
        out_shape=[jax.ShapeDtypeStruct((t, d), F32), jax.ShapeDtypeStruct((t, d), BF16)],
        compiler_params=_cparams(("arbitrary",)),
        name="combine_rows",
    )(d0, d1, x, w, next_gain.reshape(1, d), ys)


def moe_ffn(x, gain, router, w_gate, w_up, w_down, first_expert, next_gain):
    t, d = x.shape
    h, rt = route(x, gain, router)
    e_flat = jnp.concatenate([rt[:, 0], rt[:, 1]]).astype(I32)
    tok = jnp.concatenate([jnp.arange(t, dtype=I32)] * 2)
    onehot = (e_flat[:, None] == jnp.arange(N_EXPERTS, dtype=I32)[None, :]).astype(I32)
    csum = jnp.cumsum(onehot, axis=0)
    rank = jnp.take_along_axis(csum - onehot, e_flat[:, None], axis=1)[:, 0]
    counts = csum[-1]
    n_tiles = (counts + FFN_TM - 1) // FFN_TM
    tile_end = jnp.cumsum(n_tiles)
    tile_start = tile_end - n_tiles
    dest = tile_start[e_flat] * FFN_TM + rank
    ns = TOP_K * t // FFN_TM + N_EXPERTS
    s_idx = jnp.arange(ns, dtype=I32)
    used = s_idx < tile_end[-1]
    s_clip = jnp.minimum(s_idx, tile_end[-1] - 1)
    tile_expert = jnp.minimum(jnp.sum((s_clip[:, None] >= tile_end[None, :]).astype(I32), axis=1), N_EXPERTS - 1)
    tile_rows = jnp.clip(counts[tile_expert] - (s_clip - tile_start[tile_expert]) * FFN_TM, 0, FFN_TM)
    tile_rows = jnp.where(used, tile_rows, 0).astype(I32)
    src_row = jnp.zeros((ns * FFN_TM,), I32).at[dest].set(tok)
    per = FFN_TM // GATHER_ROWS
    live = (jnp.arange(ns * per, dtype=I32) % per * GATHER_ROWS < jnp.repeat(tile_rows, per)).astype(I32)

    xs = gather_rows(h, src_row, live, BF16)
    ys = ffn_tiles(xs, tile_expert + first_expert, tile_rows, w_gate, w_up, w_down)
    return combine_rows(x, ys, dest[:t], dest[t:], rt, next_gain)


def _ple_kernel(h_ref, wg_ref, p_ref, wp_ref, x_ref, o_ref):
    gate = _sigmoid(_dot(h_ref[...], wg_ref[...].astype(BF16)))
    proj = _dot(p_ref[...].astype(BF16), wp_ref[...].astype(BF16))
    o_ref[...] = x_ref[...] + gate * proj


def ple(h, w_gate, p, w_proj, x, layer, tm=2048, tn=512):
    t, d = x.shape
    pd = p.shape[-1]
    tm = min(tm, t)
    return pl.pallas_call(
        _ple_kernel,
        grid=(t // tm, d // tn),
        in_specs=[pl.BlockSpec((tm, d), lambda i, j: (i, 0)), _layer_spec((d, tn), lambda i, j: (0, j), layer),
                  _layer_spec((tm, pd), lambda i, j: (i, 0), layer), _layer_spec((pd, tn), lambda i, j: (0, j), layer),
                  pl.BlockSpec((tm, tn), lambda i, j: (i, j))],
        out_specs=pl.BlockSpec((tm, tn), lambda i, j: (i, j)),
        out_shape=jax.ShapeDtypeStruct((t, d), F32),
        compiler_params=_cparams(("parallel", "parallel")),
        name="ple",
    )(h, w_gate, p, w_proj, x)


def _pad_rows(w, start, total):
    return jnp.zeros((total, w.shape[1]), F32).at[start:start + w.shape[0]].set(w).astype(BF16)


def kernel(x, p, positions, w_in, mix_norm, ffn_norm, ple_norm, rw_mu, rw_w0, rw_w2, rw_a0, rw_a2, rw_g2, rw_kk,
           rw_ka, rw_rk, rw_ln_g, rw_ln_b, dsa_q_norm, dsa_k_norm, dsa_kv_norm, dsa_w_uk, dsa_w_uv, sw_q_norm,
           sw_k_norm, sw_sinks, w_branch, w_out, ffn_w_gate, ffn_w_up, ffn_w_down, moe_router, moe_w_gate, moe_w_up,
           moe_w_down, ple_w_gate, ple_w_proj):
    b, s, d = x.shape
    t = b * s
    depth = w_in.shape[0]
    xf = x.reshape(t, d)
    tables = rope_tables(positions)
    row = lambda a: a.reshape(1, -1)
    w_in_t = jnp.swapaxes(w_in, 1, 2)
    p_rows = p.reshape(depth, t, -1)
    experts = lambda w: w.reshape((-1,) + w.shape[2:])
    for i in range(depth):
        h = rmsnorm(xf, mix_norm[i])
        uh = in_proj(h, w_in_t, i, HEAD_TILES, name="in_proj_head")
        ut = in_proj(h, w_in_t, i, TAIL_TILES, name="in_proj_tail")
        y_a = sb_attention(uh, b, s)
        rw = rw_prepare(uh, row(rw_mu[i]), row(rw_w0[i]), row(rw_a0[i]), row(rw_kk[i]), row(rw_ka[i]),
                        _pad_rows(rw_w2[i], 0, RW_LORA), _pad_rows(rw_a2[i], RW_LORA_W, RW_LORA),
                        _pad_rows(rw_g2[i], RW_LORA_W + RW_LORA_A, RW_LORA), b, s)
        y_b = rw_scan(*rw, row(rw_rk[i]), row(rw_ln_g[i]), row(rw_ln_b[i]), b, s)
        y_c = dsa_attention(uh, tables, dsa_q_norm[i], dsa_k_norm[i], dsa_kv_norm[i], dsa_w_uk[i], dsa_w_uv[i], b, s)
        y_d = sw_attention(ut, tables, sw_q_norm[i], sw_k_norm[i], sw_sinks[i], b, s)
        merged = merge_branches([y_a, y_b, y_c, y_d], ut, w_branch, i)
        xf = matmul(merged, w_out, i, out_dtype=F32, residual=xf, tm=2048, name="out_proj")
        if i % 2 == 0:
            h = rmsnorm(xf, ffn_norm[i])
            xf, h = dense_ffn(xf, h, ffn_w_gate, ffn_w_up, ffn_w_down, i // 2, ple_norm[i])
        else:
            xf, h = moe_ffn(xf, ffn_norm[i], moe_router[i // 2], experts(moe_w_gate), experts(moe_w_up),
                            experts(moe_w_down), (i // 2) * N_EXPERTS, ple_norm[i])
        xf = ple(h, ple_w_gate, p_rows, ple_w_proj, xf, i)
    return xf.reshape(b, s, d)
```

```python
import functools
import math

import jax
import jax.numpy as jnp
from jax import lax
from jax.experimental import pallas as pl
from jax.experimental.pallas import tpu as pltpu

F32 = jnp.float32
BF16 = jnp.bfloat16
I32 = jnp.int32

D_MODEL = 2048
ROPE_THETA = 10000.0
NORM_EPS = 1e-6
N_BRANCH = 4
BRANCH_W = 1024
SB_HEADS, SB_DIM = 8, 128
RW_HEADS, RW_DIM = 16, 64
RW_LORA_W, RW_LORA_A, RW_LORA_G = 96, 96, 64
RW_LORA = RW_LORA_W + RW_LORA_A + RW_LORA_G
RW_GN_EPS = 64e-5
DSA_HEADS, DSA_DIM, DSA_KV_RANK = 8, 128, 256
IDX_HEADS, IDX_DIM, IDX_TOPK_MAX = 16, 64, 256
SW_HEADS, SW_KV_HEADS, SW_DIM, WINDOW = 16, 2, 64, 128
N_EXPERTS, TOP_K = 8, 2

LANES = 128
VMEM_LIMIT_BYTES = 56 * 1024 * 1024

HEAD_W = 9216
COL_SBQ = 0
COL_SBK = 1024
COL_SBV = 2048
COL_RWR = 3072
COL_RWK = 4096
COL_RWV = 5120
COL_RWL = 6144
COL_DSAQ = 6400
COL_CKV = 7424
COL_IDXQ = 7680
COL_IDXKW = 8704
SRC_SWQ = 8784
SRC_SWK = 9808
SRC_GATE = 10064
COL_GATE = 0
COL_SWQ = 8192
COL_SWK = 9216
COL_SWV = 9344
IN_TN = 512
IN_ALIGN = 16
HEAD_TILES = tuple(range(0, HEAD_W, IN_TN))
TAIL_TILES = (tuple(range(SRC_GATE, SRC_GATE + N_BRANCH * D_MODEL, IN_TN))
              + tuple(range(SRC_SWQ, SRC_SWK, IN_TN)) + (SRC_SWK,))

RW_CHUNK = 64


def _cparams(sem, vmem=VMEM_LIMIT_BYTES):
    return pltpu.CompilerParams(dimension_semantics=sem, vmem_limit_bytes=vmem)


def _dot(a, b):
    return jnp.dot(a, b, preferred_element_type=F32)


def _dot_nt(a, b):
    return lax.dot_general(a, b, (((1,), (1,)), ((), ())), preferred_element_type=F32)


def _split3(a):
    a1 = a.astype(BF16)
    r1 = a - a1.astype(F32)
    a2 = r1.astype(BF16)
    a3 = (r1 - a2.astype(F32)).astype(BF16)
    return a1, a2, a3


def _dot_exact_rhs(a, m_bf16):
    a1, a2, a3 = _split3(a)
    return _dot(a1, m_bf16) + _dot(a2, m_bf16) + _dot(a3, m_bf16)


def _dot_f32(a, b):
    a1, a2, _ = _split3(a)
    b1, b2, _ = _split3(b)
    return _dot(a1, b1) + (_dot(a1, b2) + _dot(a2, b1))


def _softplus(z):
    return jnp.maximum(z, 0.0) + jnp.log(1.0 + jnp.exp(-jnp.abs(z)))


def _sigmoid(z):
    return 1.0 / (1.0 + jnp.exp(-z))


def _iota(shape, dim):
    return lax.broadcasted_iota(I32, shape, dim)


def _group_matrix(n, group, value):
    r = _iota((n, n), 0) // group
    c = _iota((n, n), 1) // group
    return jnp.where(r == c, value, 0.0).astype(BF16)


def _swap_halves(u, half):
    if 2 * half == LANES:
        return pltpu.roll(u, half, 1)
    lane = _iota(u.shape, 1)
    return jnp.where(lane % (2 * half) < half, pltpu.roll(u, LANES - half, 1), pltpu.roll(u, half, 1))


def _rmsnorm_kernel(x_ref, g_ref, o_ref):
    x = x_ref[...]
    ms = jnp.mean(x * x, axis=-1, keepdims=True)
    o_ref[...] = (x * lax.rsqrt(ms + NORM_EPS) * g_ref[...]).astype(o_ref.dtype)


def rmsnorm(x, gain, out_dtype=BF16, tm=512):
    t, d = x.shape
    return pl.pallas_call(
        _rmsnorm_kernel,
        grid=(t // tm,),
        in_specs=[pl.BlockSpec((tm, d), lambda i: (i, 0)), pl.BlockSpec((1, d), lambda i: (0, 0))],
        out_specs=pl.BlockSpec((tm, d), lambda i: (i, 0)),
        out_shape=jax.ShapeDtypeStruct((t, d), out_dtype),
        compiler_params=_cparams(("parallel",)),
        name="rmsnorm",
    )(x, gain.reshape(1, d))


def _mm_kernel(*refs, has_res):
    if has_res:
        a_ref, w_ref, r_ref, o_ref = refs
    else:
        a_ref, w_ref, o_ref = refs
    acc = _dot(a_ref[...].astype(BF16), w_ref[...].astype(BF16))
    if has_res:
        acc = acc + r_ref[...]
    o_ref[...] = acc.astype(o_ref.dtype)


def _layer_spec(block, index_map, layer):
    return pl.BlockSpec((None,) + tuple(block), lambda *idx: (layer,) + tuple(index_map(*idx)))


def matmul(a, w, layer, *, out_dtype, residual=None, tm=1024, tn=512, name="matmul"):
    m, k = a.shape
    n = w.shape[2]
    tm, tn = min(tm, m), min(tn, n)
    in_specs = [pl.BlockSpec((tm, k), lambda i, j: (i, 0)), _layer_spec((k, tn), lambda i, j: (0, j), layer)]
    args = [a, w]
    if residual is not None:
        in_specs.append(pl.BlockSpec((tm, tn), lambda i, j: (i, j)))
        args.append(residual)
    return pl.pallas_call(
        functools.partial(_mm_kernel, has_res=residual is not None),
        grid=(m // tm, n // tn),
        in_specs=in_specs,
        out_specs=pl.BlockSpec((tm, tn), lambda i, j: (i, j)),
        out_shape=jax.ShapeDtypeStruct((m, n), out_dtype),
        compiler_params=_cparams(("parallel", "parallel")),
        name=name,
    )(*args)


def _in_proj_kernel(off_ref, a_ref, w_ref, o_ref):
    o_ref[...] = _dot_nt(a_ref[...], w_ref[...].astype(BF16)).astype(o_ref.dtype)


def in_proj(a, w_t, layer, offsets, *, tm=2048, tn=IN_TN, name):
    m, k = a.shape
    tm = min(tm, m)
    grid_spec = pltpu.PrefetchScalarGridSpec(
        num_scalar_prefetch=1,
        grid=(m // tm, len(offsets)),
        in_specs=[pl.BlockSpec((tm, k), lambda i, j, off: (i, 0)),
                  pl.BlockSpec((None, pl.Element(tn), pl.Element(k)),
                               lambda i, j, off: (layer, off[j] * IN_ALIGN, 0))],
        out_specs=pl.BlockSpec((tm, tn), lambda i, j, off: (i, j)),
    )
    return pl.pallas_call(
        _in_proj_kernel,
        grid_spec=grid_spec,
        out_shape=jax.ShapeDtypeStruct((m, tn * len(offsets)), BF16),
        compiler_params=_cparams(("parallel", "parallel")),
        name=name,
    )(jnp.asarray([o // IN_ALIGN for o in offsets], I32), a, w_t)


def _rope_table_kernel(pos_ref, f64_ref, f32_ref, ca_ref, sa_ref, cb_ref, sb_ref):
    pos = pos_ref[...]
    lane = _iota((1, LANES), 1)
    ang_a = pos * f64_ref[...]
    ang_b = pos * f32_ref[...]
    ca_ref[...] = jnp.cos(ang_a)
    sa_ref[...] = jnp.where(lane < 64, -1.0, 1.0) * jnp.sin(ang_a)
    cb_ref[...] = jnp.cos(ang_b)
    sb_ref[...] = jnp.where(lane % 64 < 32, -1.0, 1.0) * jnp.sin(ang_b)


def rope_tables(positions, tm=512):
    t = positions.size
    pos = positions.reshape(t, 1).astype(F32)
    inv64 = ROPE_THETA ** (-jnp.arange(64, dtype=F32) / 64)
    inv32 = ROPE_THETA ** (-jnp.arange(32, dtype=F32) / 32)
    f64 = jnp.tile(inv64, 2).reshape(1, LANES)
    f32 = jnp.tile(inv32, 4).reshape(1, LANES)
    row = pl.BlockSpec((tm, LANES), lambda i: (i, 0))
    vec = pl.BlockSpec((1, LANES), lambda i: (0, 0))
    return pl.pallas_call(
        _rope_table_kernel,
        grid=(t // tm,),
        in_specs=[pl.BlockSpec((tm, 1), lambda i: (i, 0)), vec, vec],
        out_specs=[row] * 4,
        out_shape=[jax.ShapeDtypeStruct((t, LANES), F32)] * 4,
        compiler_params=_cparams(("parallel",)),
        name="rope_tables",
    )(pos, f64, f32)


SB_TQ = 256
SB_G = 8


def _sb_kernel(q_ref, k_ref, v_ref, o_ref, *, tq, scale):
    qi = pl.program_id(2)
    r = _iota((tq, tq), 0)
    c = _iota((tq, tq), 1)
    later = jnp.where(r > c, 1.0, 0.0).astype(BF16)
    qs = [(q_ref[:, g * SB_DIM:(g + 1) * SB_DIM].astype(F32) * scale).astype(BF16) for g in range(SB_G)]

    heads = range(SB_G)
    cols = [slice(g * SB_DIM, (g + 1) * SB_DIM) for g in heads]

    def span(j, state, diagonal):
        off = pl.multiple_of(j * tq, tq)
        carry, acc = state
        ks = [k_ref[pl.ds(off, tq), cols[g]].astype(BF16) for g in heads]
        vs = [v_ref[pl.ds(off, tq), cols[g]].astype(BF16) for g in heads]
        zs = [_dot_nt(qs[g], ks[g]) for g in heads]
        lss = [jnp.minimum(z, 0.0) - jnp.log(1.0 + jnp.exp(-jnp.abs(z))) for z in zs]
        lks = [lss[g] - zs[g] for g in heads]
        if diagonal:
            lks = [jnp.where(r > c, lk, 0.0) for lk in lks]
        his = [lk.astype(BF16) for lk in lks]
        los = [(lks[g] - his[g].astype(F32)).astype(BF16) for g in heads]
        css = [_dot(his[g], later) + _dot(los[g], later) for g in heads]
        ws = [jnp.exp(lss[g] + css[g] + carry[g]) for g in heads]
        if diagonal:
            ws = [jnp.where(r > c, w, 0.0) for w in ws]
        acc = tuple(acc[g] + _dot(ws[g].astype(BF16), vs[g]) for g in heads)
        carry = tuple(carry[g] + jnp.sum(lks[g], axis=1, keepdims=True) for g in heads)
        return carry, acc

    init = (tuple(jnp.zeros((tq, 1), F32) for _ in heads), tuple(jnp.zeros((tq, SB_DIM), F32) for _ in heads))
    state = span(qi, init, True)
    _, acc = lax.fori_loop(0, qi, lambda i, s: span(qi - 1 - i, s, False), state)
    for g in heads:
        o_ref[:, cols[g]] = acc[g].astype(o_ref.dtype)


def sb_attention(u, batch, seq):
    t = batch * seq
    tq = min(SB_TQ, seq)
    nq = seq // tq
    gw = SB_G * SB_DIM
    qc, kc, vc = COL_SBQ // gw, COL_SBK // gw, COL_SBV // gw
    return pl.pallas_call(
        functools.partial(_sb_kernel, tq=tq, scale=SB_DIM ** -0.5),
        grid=(batch, SB_HEADS // SB_G, nq),
        in_specs=[
            pl.BlockSpec((tq, gw), lambda b, h, i: (b * nq + i, qc + h)),
            pl.BlockSpec((seq, gw), lambda b, h, i: (b, kc + h)),
            pl.BlockSpec((seq, gw), lambda b, h, i: (b, vc + h)),
        ],
        out_specs=pl.BlockSpec((tq, gw), lambda b, h, i: (b * nq + i, h)),
        out_shape=jax.ShapeDtypeStruct((t, BRANCH_W), BF16),
        compiler_params=_cparams(("parallel", "parallel", "arbitrary")),
        name="sb_attention",
    )(u, u, u)


def _rw_pre_kernel(r_ref, k_ref, v_ref, l_ref, pr_ref, pk_ref, pv_ref, plr_ref,
                   mu_ref, w0_ref, a0_ref, kkg_ref, ka_ref, w2_ref, a2_ref, g2_ref,
                   ro_ref, lw_ref, ko_ref, vo_ref, kk_ref, b_ref, g_ref, *, tm, seq):
    i = pl.program_id(0)
    first = (i * tm) % seq == 0
    row0 = _iota((tm, 1), 0) == 0

    def shifted(cur_ref, prev_ref, lo, hi):
        cur = cur_ref[...].astype(F32)
        last = prev_ref[...].astype(F32)[-1:, :]
        last = jnp.where(first, 0.0, last)
        prev = jnp.where(row0, last, pltpu.roll(cur, 1, 0))
        return cur + mu_ref[:, lo:hi] * (prev - cur)

    w = BRANCH_W
    r = shifted(r_ref, pr_ref, 0, w)
    k = shifted(k_ref, pk_ref, w, 2 * w)
    v = shifted(v_ref, pv_ref, 2 * w, 3 * w)
    z = shifted(l_ref, plr_ref, 3 * w, 3 * w + RW_LORA)

    w_pre = w0_ref[...] + _dot(jnp.tanh(z).astype(BF16), w2_ref[...])
    w_log = -_softplus(-w_pre) - 0.5
    lw_ref[...] = -jnp.exp(w_log)
    a = _sigmoid(a0_ref[...] + _dot(z.astype(BF16), a2_ref[...]))
    g_ref[...] = _dot(_sigmoid(z).astype(BF16), g2_ref[...]).astype(g_ref.dtype)
    ones = _group_matrix(LANES, RW_DIM, 1.0)
    kk = k * kkg_ref[...]
    for s in range(w // LANES):
        sl = slice(s * LANES, (s + 1) * LANES)
        kks = kk[:, sl]
        ss = _dot_exact_rhs(kks * kks, ones)
        kkn = kks / jnp.maximum(jnp.sqrt(ss), 1e-12)
        kk_ref[:, sl] = kkn
        b_ref[:, sl] = kkn * a[:, sl]
    ro_ref[...] = r
    ko_ref[...] = k * (1.0 + (a - 1.0) * ka_ref[...])
    vo_ref[...] = v


def rw_prepare(u, mu, w0, a0, k_k, k_a, w2p, a2p, g2p, batch, seq, tm=256):
    t = batch * seq
    tm = min(tm, seq)
    w = BRANCH_W
    sub = 16

    def cur(width, col):
        return pl.BlockSpec((tm, width), lambda i: (i, col // width))

    def prev(width, col):
        return pl.BlockSpec((sub, width), lambda i: (jnp.maximum(i * (tm // sub) - 1, 0), col // width))

    def vec(width):
        return pl.BlockSpec((1, width), lambda i: (0, 0))

    def mat():
        return pl.BlockSpec((RW_LORA, w), lambda i: (0, 0))

    out = pl.BlockSpec((tm, w), lambda i: (i, 0))
    f = jax.ShapeDtypeStruct((t, w), F32)
    return pl.pallas_call(
        functools.partial(_rw_pre_kernel, tm=tm, seq=seq),
        grid=(t // tm,),
        in_specs=[cur(w, COL_RWR), cur(w, COL_RWK), cur(w, COL_RWV), cur(RW_LORA, COL_RWL),
                  prev(w, COL_RWR), prev(w, COL_RWK), prev(w, COL_RWV), prev(RW_LORA, COL_RWL),
                  vec(3 * w + RW_LORA), vec(w), vec(w), vec(w), vec(w), mat(), mat(), mat()],
        out_specs=[out] * 7,
        out_shape=[f, f, f, f, f, f, jax.ShapeDtypeStruct((t, w), BF16)],
        compiler_params=_cparams(("parallel",)),
        name="rw_prepare",
    )(u, u, u, u, u, u, u, u, mu, w0, a0, k_k, k_a, w2p, a2p, g2p)


def _rw_scan_kernel(r_ref, lw_ref, k_ref, v_ref, kk_ref, b_ref, g_ref, rk_ref, lng_ref, lnb_ref,
                    o_ref, st_ref, *, chunk):
    @pl.when(pl.program_id(1) == 0)
    def _():
        st_ref[...] = jnp.zeros_like(st_ref)

    npair = RW_HEADS // 2
    w = npair * LANES
    n = 2 * chunk
    lane = _iota((n, w), 1)
    row = _iota((n, w), 0)
    own = ((lane // RW_DIM) % 2) == (row // chunk)
    rr = _iota((n, n), 0)
    cc = _iota((n, n), 1)
    same = (rr // chunk) == (cc // chunk)
    tri_incl = jnp.where(same & (rr >= cc), 1.0, 0.0).astype(BF16)
    strict = same & (rr > cc)
    incl = same & (rr >= cc)
    eye = jnp.where(rr == cc, 1.0, 0.0)
    blockdiag = (_iota((LANES, LANES), 0) // RW_DIM) == (_iota((LANES, LANES), 1) // RW_DIM)
    avg = _group_matrix(LANES, RW_DIM, 1.0 / RW_DIM)
    ones = _group_matrix(LANES, RW_DIM, 1.0)
    pairs = range(npair)
    sl = [slice(p * LANES, (p + 1) * LANES) for p in pairs]

    def stack(x):
        return jnp.concatenate([x, x], axis=0)

    def per_pair_rows(x):
        return jnp.concatenate([x[:, s] for s in sl], axis=0)

    def per_pair_lanes(x):
        return jnp.concatenate([x[p * chunk:(p + 1) * chunk] for p in pairs], axis=1)

    r, lw, k, v, kk, b = (ref[...] for ref in (r_ref, lw_ref, k_ref, v_ref, kk_ref, b_ref))
    lw2 = stack(lw)
    l1 = lw2.astype(BF16)
    l2 = (lw2 - l1.astype(F32)).astype(BF16)
    lin = _dot(tri_incl, l1) + _dot(tri_incl, l2)
    lend = lin[n - 1:n]
    p_inv = jnp.exp(-lin)
    p_dec = jnp.exp(lend - lin)
    zero = jnp.zeros((n, w), F32)
    kk_t = jnp.where(own, stack(kk) * jnp.exp(lin - lw2), zero).astype(BF16)
    r_t = jnp.where(own, stack(r) * jnp.exp(lin), zero).astype(BF16)
    v_f = jnp.where(own, stack(v), zero)
    v_s = v_f.astype(BF16)
    b_t = (stack(b) * p_inv).astype(BF16)
    k_t = (stack(k) * p_inv).astype(BF16)
    b_d = (stack(b) * p_dec).astype(BF16)
    k_d = (stack(k) * p_dec).astype(BF16)
    p_end = jnp.exp(lend)

    lhs = [jnp.concatenate([kk_t[:, s], r_t[:, s]], axis=0) for s in sl]
    a_b = [_dot_nt(lhs[p], b_t[:, sl[p]]) for p in pairs]
    a_k = [_dot_nt(lhs[p], k_t[:, sl[p]]) for p in pairs]
    s_t = [_dot_nt(lhs[p], st_ref[p].astype(BF16)) for p in pairs]
    n_ab = [jnp.where(strict, a_b[p][:n], 0.0) for p in pairs]
    rhs = [s_t[p][:n] + _dot(jnp.where(strict, a_k[p][:n], 0.0).astype(BF16), v_s[:, sl[p]]) for p in pairs]
    inv = [eye - n_ab[p] for p in pairs]
    pw = n_ab
    for _ in range(int(math.log2(chunk)) - 1):
        pw_b = [pw[p].astype(BF16) for p in pairs]
        pw = [_dot(pw_b[p], pw_b[p]) for p in pairs]
        inv = [inv[p] + _dot(inv[p].astype(BF16), pw[p].astype(BF16)) for p in pairs]
    u_s = [-_dot(inv[p].astype(BF16), rhs[p].astype(BF16)) for p in pairs]
    y2 = [s_t[p][n:] + _dot(jnp.where(incl, a_b[p][n:], 0.0).astype(BF16), u_s[p].astype(BF16))
          + _dot(jnp.where(incl, a_k[p][n:], 0.0).astype(BF16), v_s[:, sl[p]]) for p in pairs]
    for p in pairs:
        upd = _dot(u_s[p].T.astype(BF16), b_d[:, sl[p]]) + _dot(v_f[:, sl[p]].T.astype(BF16), k_d[:, sl[p]])
        st_ref[p] = st_ref[p] * p_end[:, sl[p]] + jnp.where(blockdiag, upd, 0.0)

    yr = jnp.concatenate([y2[p][:chunk] + y2[p][chunk:] for p in pairs], axis=0)
    yr1, yr2, _ = _split3(yr)
    d = yr - (_dot(yr1, avg) + _dot(yr2, avg))
    dd1, dd2, _ = _split3(d * d)
    var = _dot(dd1, avg) + _dot(dd2, avg)
    yn = per_pair_lanes(d * lax.rsqrt(var + RW_GN_EPS)) * lng_ref[...] + lnb_ref[...]
    bonus = per_pair_lanes(_dot(per_pair_rows(r * k * rk_ref[...]).astype(BF16), ones)) * v
    o_ref[...] = ((yn + bonus) * g_ref[...].astype(F32)).astype(o_ref.dtype)


def rw_scan(r, lw, k, v, kk, b, g, r_k, ln_g, ln_b, batch, seq):
    t = batch * seq
    chunk = min(RW_CHUNK, seq)
    nc = seq // chunk
    w = BRANCH_W
    blk = pl.BlockSpec((chunk, w), lambda bb, c: (bb * nc + c, 0))
    vec = pl.BlockSpec((1, w), lambda bb, c: (0, 0))
    return pl.pallas_call(
        functools.partial(_rw_scan_kernel, chunk=chunk),
        grid=(batch, nc),
        in_specs=[blk] * 7 + [vec] * 3,
        out_specs=blk,
        out_shape=jax.ShapeDtypeStruct((t, w), BF16),
        scratch_shapes=[pltpu.VMEM((RW_HEADS // 2, LANES, LANES), F32)],
        compiler_params=_cparams(("parallel", "arbitrary")),
        name="rw_scan",
    )(r, lw, k, v, kk, b, g, r_k, ln_g, ln_b)


def _rope(u, cos, sin, half):
    return u * cos + _swap_halves(u, half) * sin


DSA_Q_BLK = 256
DSA_IQ_BLK = 512


def _dsa_pre_kernel(*refs):
    nq, niq = BRANCH_W // DSA_Q_BLK, BRANCH_W // DSA_IQ_BLK
    q_refs, refs = refs[:nq], refs[nq:]
    c_ref, refs = refs[0], refs[1:]
    iq_refs, refs = refs[:niq], refs[niq:]
    (ikw_ref, ca_ref, sa_ref, cb_ref, sb_ref, qg_ref, kg_ref, cg_ref, wuk_ref, wuv_ref,
     qo_ref, ko_ref, vo_ref, iqo_ref, iko_ref, iwo_ref) = refs
    ca, sa, cb, sb = ca_ref[...], sa_ref[...], cb_ref[...], sb_ref[...]
    for h in range(DSA_HEADS):
        sl = slice(h * DSA_DIM, (h + 1) * DSA_DIM)
        per = DSA_Q_BLK // DSA_DIM
        q = q_refs[h // per][:, (h % per) * DSA_DIM:(h % per + 1) * DSA_DIM].astype(F32)
        q = q * lax.rsqrt(jnp.mean(q * q, axis=-1, keepdims=True) + NORM_EPS) * qg_ref[...]
        qo_ref[:, sl] = (_rope(q, ca, sa, 64) * DSA_DIM ** -0.5).astype(qo_ref.dtype)
    c = c_ref[...].astype(F32)
    c = (c * lax.rsqrt(jnp.mean(c * c, axis=-1, keepdims=True) + NORM_EPS) * cg_ref[...]).astype(BF16)
    k = _dot(c, wuk_ref[...].astype(BF16))
    k = k * lax.rsqrt(jnp.mean(k * k, axis=-1, keepdims=True) + NORM_EPS) * kg_ref[...]
    ko_ref[...] = _rope(k, ca, sa, 64).astype(ko_ref.dtype)
    vo_ref[...] = _dot(c, wuv_ref[...].astype(BF16)).astype(vo_ref.dtype)
    for s in range(IDX_HEADS * IDX_DIM // LANES):
        sl = slice(s * LANES, (s + 1) * LANES)
        per = DSA_IQ_BLK // LANES
        iq = iq_refs[s // per][:, (s % per) * LANES:(s % per + 1) * LANES].astype(F32)
        iqo_ref[:, sl] = _rope(iq, cb, sb, 32).astype(iqo_ref.dtype)
    ikw = ikw_ref[...].astype(F32)
    lane = _iota(ikw.shape, 1)
    ik = _rope(ikw, cb, sb, 32)
    iko_ref[...] = jnp.where(lane < IDX_DIM, ik, pltpu.roll(ik, IDX_DIM, 1)).astype(iko_ref.dtype)
    iw = pltpu.roll(ikw, IDX_DIM, 1) * (IDX_HEADS ** -0.5 * IDX_DIM ** -0.5)
    iwo_ref[...] = jnp.where(lane < IDX_HEADS, iw, 0.0)


def _dsa_kernel(q_ref, iq_ref, iw_ref, k_ref, v_ref, ik_ref, o_ref, key_ref, bias_ref, vt_ref, m_ref, l_ref, acc_ref,
                *, tq, seq, top_k):
    kb = tq
    qb = pl.program_id(1)
    nblk = qb + 1
    nh = DSA_HEADS
    int_min = jnp.int32(-2 ** 31)
    kidx = _iota((kb, tq), 0)
    ridx = _iota((kb, tq), 1)

    def causal_mask(j):
        return (kidx + j * kb) <= (ridx + qb * tq)

    def block(ref, j):
        return ref[pl.ds(pl.multiple_of(j * kb, kb), kb), :]

    def fold(x, op):
        return op(x.reshape(kb // 8, 8, x.shape[1]), axis=0)

    @pl.when(qb == 0)
    def _():
        for j in range(seq // LANES):
            vt_ref[:, j * LANES:(j + 1) * LANES] = v_ref[j * LANES:(j + 1) * LANES, :].astype(F32).T.astype(BF16)

    first = _iota((tq, LANES), 1) < IDX_DIM
    lhs = []
    for p in range(IDX_HEADS // 2):
        qp = iq_ref[:, p * LANES:(p + 1) * LANES]
        zero = jnp.zeros_like(qp)
        lhs.append(jnp.concatenate([jnp.where(first, qp, zero), jnp.where(first, zero, qp)], axis=0))
    iw_t = iw_ref[...].T
    w_row = [iw_t[h:h + 1] for h in range(IDX_HEADS)]

    def score_block(j, _):
        ik = block(ik_ref, j)
        sc = jnp.zeros((kb, tq), F32)
        for p in range(IDX_HEADS // 2):
            z = jnp.maximum(_dot_nt(ik, lhs[p]), 0.0)
            sc = sc + z[:, :tq] * w_row[2 * p] + z[:, tq:] * w_row[2 * p + 1]
        sc = sc + 0.0
        bits = lax.bitcast_convert_type(sc, I32)
        skey = bits ^ ((bits >> 31) & jnp.int32(0x7FFFFFFF))
        key_ref[pl.ds(pl.multiple_of(j * kb, kb), kb), :] = jnp.where(causal_mask(j), skey, int_min)
        return 0

    lax.fori_loop(0, nblk, score_block, 0)

    def count(pred_fn):
        def body(j, cnt):
            return cnt + fold(jnp.where(pred_fn(block(key_ref, j)), 1.0, 0.0), jnp.sum)
        return jnp.sum(lax.fori_loop(0, nblk, body, jnp.zeros((8, tq), F32)), axis=0, keepdims=True)

    def bit_step(i, thr):
        cand = thr ^ (jnp.int32(1) << (31 - i))
        return jnp.where(count(lambda keys: keys >= cand) >= top_k, cand, thr)

    thr = lax.fori_loop(0, 32, bit_step, jnp.full((1, tq), int_min, I32))
    need = top_k - count(lambda keys: keys > thr)

    q_all = jnp.concatenate([q_ref[:, h * DSA_DIM:(h + 1) * DSA_DIM] for h in range(nh)], axis=0)
    lower = jnp.where(_iota((kb, kb), 0) > _iota((kb, kb), 1), 1.0, 0.0).astype(BF16)
    m_ref[...] = jnp.full_like(m_ref, -1e30)

    def logits(j, bias):
        lt = _dot_nt(block(k_ref, j), q_all)
        return jnp.concatenate([lt[:, h * tq:(h + 1) * tq] + bias for h in range(nh)], axis=1)

    def select(j, ties_seen):
        keys = block(key_ref, j)
        causal = causal_mask(j)
        tie = causal & (keys == thr)
        tie_f = jnp.where(tie, 1.0, 0.0)
        rank = ties_seen + _dot(lower, tie_f.astype(BF16))
        sel = causal & ((keys > thr) | (tie & (rank < need)))
        bias = jnp.where(sel, 0.0, -1e30)
        bias_ref[pl.ds(pl.multiple_of(j * kb, kb), kb), :] = bias
        m_ref[...] = jnp.maximum(m_ref[...], fold(logits(j, bias), jnp.max))
        return ties_seen + jnp.sum(tie_f, axis=0, keepdims=True)

    lax.fori_loop(0, nblk, select, jnp.zeros((1, tq), F32))
    m_all = jnp.broadcast_to(jnp.max(m_ref[...], axis=0, keepdims=True), m_ref.shape)
    l_ref[...] = jnp.zeros_like(l_ref)
    acc_ref[...] = jnp.zeros_like(acc_ref)

    def attend(j, _):
        lt = logits(j, block(bias_ref, j))
        pr = jnp.exp(lt.reshape(kb // 8, 8, nh * tq) - m_all[None]).reshape(kb, nh * tq)
        l_ref[...] += fold(pr, jnp.sum)
        acc_ref[...] += _dot(vt_ref[:, pl.ds(pl.multiple_of(j * kb, kb), kb)], pr.astype(BF16))
        return 0

    lax.fori_loop(0, nblk, attend, 0)
    out_t = acc_ref[...] / jnp.sum(l_ref[...], axis=0, keepdims=True)
    for h in range(nh):
        o_ref[:, h * DSA_DIM:(h + 1) * DSA_DIM] = out_t[:, h * tq:(h + 1) * tq].T.astype(o_ref.dtype)


def dsa_attention(u, tables, q_gain, k_gain, kv_gain, w_uk, w_uv, batch, seq, tm=256, tq=256):
    t = batch * seq
    tm = min(tm, seq)
    ca, sa, cb, sb = tables
    w = BRANCH_W
    row = lambda width, col: pl.BlockSpec((tm, width), lambda i: (i, col // width))
    tab = pl.BlockSpec((tm, LANES), lambda i: (i, 0))
    vec = lambda width: pl.BlockSpec((1, width), lambda i: (0, 0))
    mat = pl.BlockSpec((DSA_KV_RANK, DSA_DIM), lambda i: (0, 0))
    o_w = pl.BlockSpec((tm, w), lambda i: (i, 0))
    o_n = pl.BlockSpec((tm, LANES), lambda i: (i, 0))
    q, k, v, iq, ik, iw = pl.pallas_call(
        _dsa_pre_kernel,
        grid=(t // tm,),
        in_specs=[row(DSA_Q_BLK, COL_DSAQ + j * DSA_Q_BLK) for j in range(w // DSA_Q_BLK)]
        + [row(DSA_KV_RANK, COL_CKV)]
        + [row(DSA_IQ_BLK, COL_IDXQ + j * DSA_IQ_BLK) for j in range(w // DSA_IQ_BLK)]
        + [row(LANES, COL_IDXKW), tab, tab, tab, tab, vec(DSA_DIM), vec(DSA_DIM), vec(DSA_KV_RANK), mat, mat],
        out_specs=[o_w, o_n, o_n, o_w, o_n, o_n],
        out_shape=[jax.ShapeDtypeStruct((t, w), BF16), jax.ShapeDtypeStruct((t, LANES), BF16),
                   jax.ShapeDtypeStruct((t, LANES), BF16), jax.ShapeDtypeStruct((t, w), BF16),
                   jax.ShapeDtypeStruct((t, LANES), BF16), jax.ShapeDtypeStruct((t, LANES), F32)],
        compiler_params=_cparams(("parallel",)),
        name="dsa_prepare",
    )(*([u] * (w // DSA_Q_BLK + 1 + w // DSA_IQ_BLK + 1)), ca, sa, cb, sb,
      q_gain.reshape(1, -1), k_gain.reshape(1, -1), kv_gain.reshape(1, -1), w_uk, w_uv)

    nq = seq // tq
    top_k = min(IDX_TOPK_MAX, seq // 4)
    qrow = lambda width: pl.BlockSpec((tq, width), lambda b, i: (b * nq + i, 0))
    full = pl.BlockSpec((seq, LANES), lambda b, i: (b, 0))
    return pl.pallas_call(
        functools.partial(_dsa_kernel, tq=tq, seq=seq, top_k=top_k),
        grid=(batch, nq),
        in_specs=[qrow(w), qrow(w), qrow(LANES), full, full, full],
        out_specs=qrow(w),
        out_shape=jax.ShapeDtypeStruct((t, w), BF16),
        scratch_shapes=[pltpu.VMEM((seq, tq), I32), pltpu.VMEM((seq, tq), F32), pltpu.VMEM((DSA_DIM, seq), BF16),
                        pltpu.VMEM((8, DSA_HEADS * tq), F32), pltpu.VMEM((8, DSA_HEADS * tq), F32),
                        pltpu.VMEM((DSA_DIM, DSA_HEADS * tq), F32)],
        compiler_params=_cparams(("parallel", "arbitrary")),
        name="dsa_attention",
    )(q, iq, iw, k, v, ik)


def _sw_kernel(sink_ref, q_ref, kp_ref, kc_ref, vp_ref, vc_ref, cbp_ref, sbp_ref, cbc_ref, sbc_ref,
               qg_ref, kg_ref, o_ref, *, blk):
    n = pl.program_id(1)
    avg = _group_matrix(LANES, SW_DIM, 1.0 / SW_DIM)
    lane2 = _iota((2 * blk, LANES), 1)
    lane1 = _iota((blk, LANES), 1)

    def norm(x, gain):
        s1, s2, _ = _split3(x * x)
        return x * lax.rsqrt(_dot(s1, avg) + _dot(s2, avg) + NORM_EPS) * gain

    cb = jnp.concatenate([cbp_ref[...], cbc_ref[...]], axis=0)
    sb = jnp.concatenate([sbp_ref[...], sbc_ref[...]], axis=0)
    k = jnp.concatenate([kp_ref[...], kc_ref[...]], axis=0).astype(F32)
    k = _rope(norm(k, kg_ref[...]), cb, sb, 32)
    v = jnp.concatenate([vp_ref[...], vc_ref[...]], axis=0).astype(F32)
    k_sw, v_sw = pltpu.roll(k, SW_DIM, 1), pltpu.roll(v, SW_DIM, 1)
    k2 = [jnp.where(lane2 < SW_DIM, k, k_sw).astype(BF16), jnp.where(lane2 < SW_DIM, k_sw, k).astype(BF16)]
    v2 = [jnp.where(lane2 < SW_DIM, v, v_sw).astype(BF16), jnp.where(lane2 < SW_DIM, v_sw, v).astype(BF16)]

    r = _iota((2 * blk, 2 * blk), 0) % blk
    c = _iota((2 * blk, 2 * blk), 1)
    dist = r - (c - blk)
    mask = (dist >= 0) & (dist < WINDOW) & ((c >= blk) | (n > 0))
    top = _iota((2 * blk, 1), 0) < blk
    pairs = range(SW_HEADS // 2)
    group = [(2 * p) // (SW_HEADS // SW_KV_HEADS) for p in pairs]
    npair = len(pairs)
    q = jnp.concatenate([q_ref[:, p * LANES:(p + 1) * LANES].astype(F32) for p in pairs], axis=0)
    cbc = jnp.concatenate([cbc_ref[...]] * npair, axis=0)
    sbc = jnp.concatenate([sbc_ref[...]] * npair, axis=0)
    q = _rope(norm(q, qg_ref[...]), cbc, sbc, 32) * SW_DIM ** -0.5
    low = _iota(q.shape, 1) < SW_DIM
    q_a = jnp.where(low, q, 0.0).astype(BF16)
    q_b = jnp.where(low, 0.0, q).astype(BF16)
    rows = [slice(p * blk, (p + 1) * blk) for p in pairs]
    lhs = [jnp.concatenate([q_a[rows[p]], q_b[rows[p]]], axis=0) for p in pairs]
    logit = [jnp.where(mask, _dot_nt(lhs[p], k2[group[p]]), -1e30) for p in pairs]
    sink = [jnp.where(top, sink_ref[2 * p], sink_ref[2 * p + 1]) for p in pairs]
    m = [jnp.maximum(jnp.max(logit[p], axis=1, keepdims=True), sink[p]) for p in pairs]
    pr = [jnp.exp(logit[p] - m[p]) for p in pairs]
    den = [jnp.sum(pr[p], axis=1, keepdims=True) + jnp.exp(sink[p] - m[p]) for p in pairs]
    o2 = [_dot(pr[p].astype(BF16), v2[group[p]]) / den[p] for p in pairs]
    for p in pairs:
        o_ref[:, p * LANES:(p + 1) * LANES] = jnp.where(lane1 < SW_DIM, o2[p][:blk], o2[p][blk:]).astype(o_ref.dtype)


def sw_attention(u, tables, q_gain, k_gain, sinks, batch, seq, blk=128):
    t = batch * seq
    nb = seq // blk
    _, _, cb, sb = tables
    w = BRANCH_W
    cur = lambda width, col: pl.BlockSpec((blk, width), lambda b, i: (b * nb + i, col // width))
    prev = lambda width, col: pl.BlockSpec((blk, width), lambda b, i: (b * nb + jnp.maximum(i - 1, 0), col // width))
    vec = pl.BlockSpec((1, LANES), lambda b, i: (0, 0))
    tile2 = lambda g: jnp.tile(g.reshape(1, SW_DIM), (1, 2))
    return pl.pallas_call(
        functools.partial(_sw_kernel, blk=blk),
        grid=(batch, nb),
        in_specs=[pl.BlockSpec(memory_space=pltpu.SMEM),
                  cur(w, COL_SWQ), prev(LANES, COL_SWK), cur(LANES, COL_SWK), prev(LANES, COL_SWV), cur(LANES, COL_SWV),
                  prev(LANES, 0), prev(LANES, 0), cur(LANES, 0), cur(LANES, 0), vec, vec],
        out_specs=pl.BlockSpec((blk, w), lambda b, i: (b * nb + i, 0)),
        out_shape=jax.ShapeDtypeStruct((t, w), BF16),
        compiler_params=_cparams(("parallel", "parallel")),
        name="sw_attention",
    )(sinks, u, u, u, u, u, cb, sb, cb, sb, tile2(q_gain), tile2(k_gain))


def _merge_kernel(*refs):
    y_refs = refs[:N_BRANCH]
    g_ref, w_ref, o_ref, acc_ref = refs[N_BRANCH:]
    n = pl.program_id(2)
    for b in range(N_BRANCH):
        @pl.when(n == b)
        def _():
            contrib = _sigmoid(g_ref[...].astype(F32)) * _dot(y_refs[b][...], w_ref[...].astype(BF16))
            if b == 0:
                acc_ref[...] = contrib
            elif b < N_BRANCH - 1:
                acc_ref[...] += contrib
            else:
                o_ref[...] = (acc_ref[...] + contrib).astype(o_ref.dtype)


def merge_branches(ys, u, w_branch, layer, tm=1024, tc=1024):
    t, w = ys[0].shape
    d = w_branch.shape[-1]
    tm = min(tm, t)
    nc = d // tc
    y_spec = pl.BlockSpec((tm, w), lambda i, c, n: (i, 0))
    return pl.pallas_call(
        _merge_kernel,
        grid=(t // tm, nc, N_BRANCH),
        in_specs=[y_spec] * N_BRANCH + [pl.BlockSpec((tm, tc), lambda i, c, n: (i, COL_GATE // tc + n * nc + c)),
                                        _layer_spec((None, w, tc), lambda i, c, n: (n, 0, c), layer)],
        out_specs=pl.BlockSpec((tm, tc), lambda i, c, n: (i, c)),
        out_shape=jax.ShapeDtypeStruct((t, d), BF16),
        scratch_shapes=[pltpu.VMEM((tm, tc), F32)],
        compiler_params=_cparams(("parallel", "arbitrary", "arbitrary")),
        name="merge_branches",
    )(*ys, u, w_branch)


FFN_TM = 1024
MOE_TM = 1152
FFN_ROWS = (1152, 1024, 768, 512, 384, 256, 128)
FFN_TF = 512
FFN_VMEM_BYTES = 60 * 1024 * 1024


def _ffn_kernel(e_ref, rows_ref, x_ref, wg_ref, wu_ref, wd_ref, o_ref):
    rows = rows_ref[pl.program_id(0)]

    @pl.when(pl.program_id(1) == 0)
    def _():
        o_ref[...] = jnp.zeros_like(o_ref)

    sizes = [s for s in FFN_ROWS if s <= x_ref.shape[0]]
    for k, size in enumerate(sizes):
        below = sizes[k + 1] if k + 1 < len(sizes) else 0

        @pl.when((rows > below) & (rows <= size))
        def _():
            x = x_ref[:size, :]
            g = _dot(x, wg_ref[0].astype(BF16))
            a = (g * _sigmoid(g) * _dot(x, wu_ref[0].astype(BF16))).astype(BF16)
            o_ref[:size, :] += _dot(a, wd_ref[0].astype(BF16))


def ffn_tiles(xs, tile_expert, tile_rows, w_gate, w_up, w_down, tm):
    r, d = xs.shape
    _, _, ff = w_gate.shape
    ns = r // tm
    nf = ff // FFN_TF

    def f_eff(s, f, rows):
        return jnp.where(rows[s] > 0, f, nf - 1)

    grid_spec = pltpu.PrefetchScalarGridSpec(
        num_scalar_prefetch=2,
        grid=(ns, nf),
        in_specs=[pl.BlockSpec((tm, d), lambda s, f, e, rows: (s, 0)),
                  pl.BlockSpec((1, d, FFN_TF), lambda s, f, e, rows: (e[s], 0, f_eff(s, f, rows))),
                  pl.BlockSpec((1, d, FFN_TF), lambda s, f, e, rows: (e[s], 0, f_eff(s, f, rows))),
                  pl.BlockSpec((1, FFN_TF, d), lambda s, f, e, rows: (e[s], f_eff(s, f, rows), 0))],
        out_specs=pl.BlockSpec((tm, d), lambda s, f, e, rows: (s, 0)),
    )
    return pl.pallas_call(
        _ffn_kernel,
        grid_spec=grid_spec,
        out_shape=jax.ShapeDtypeStruct((r, d), F32),
        compiler_params=_cparams(("parallel", "arbitrary"), FFN_VMEM_BYTES),
        name="ffn_tiles",
    )(tile_expert, tile_rows, xs, w_gate, w_up, w_down)


def _store_with_norm(x, gain_ref, o_ref, h_ref):
    o_ref[...] = x
    ms = jnp.mean(x * x, axis=-1, keepdims=True)
    h_ref[...] = (x * lax.rsqrt(ms + NORM_EPS) * gain_ref[...]).astype(h_ref.dtype)


def _add_norm_kernel(a_ref, b_ref, g_ref, o_ref, h_ref):
    _store_with_norm(a_ref[...] + b_ref[...], g_ref, o_ref, h_ref)


def add_norm(a, b, gain, tm=512):
    t, d = a.shape
    spec = pl.BlockSpec((tm, d), lambda i: (i, 0))
    return pl.pallas_call(
        _add_norm_kernel, grid=(t // tm,),
        in_specs=[spec, spec, pl.BlockSpec((1, d), lambda i: (0, 0))], out_specs=[spec, spec],
        out_shape=[jax.ShapeDtypeStruct((t, d), a.dtype), jax.ShapeDtypeStruct((t, d), BF16)],
        compiler_params=_cparams(("parallel",)), name="residual_add_norm")(a, b, gain.reshape(1, d))


def dense_ffn(x, h, w_gate, w_up, w_down, index, next_gain):
    t = h.shape[0]
    ns = t // FFN_TM
    ys = ffn_tiles(h, jnp.full((ns,), index, I32), jnp.full((ns,), FFN_TM, I32), w_gate, w_up, w_down, FFN_TM)
    return add_norm(x, ys, next_gain)


def _router_kernel(x_ref, g_ref, w_ref, h_ref, r_ref):
    x = x_ref[...]
    h = x * lax.rsqrt(jnp.mean(x * x, axis=-1, keepdims=True) + NORM_EPS) * g_ref[...]
    h_ref[...] = h
    logit = _dot_f32(h, w_ref[...])
    lane = _iota(logit.shape, 1).astype(F32)
    neg = -jnp.inf
    l1 = jnp.where(lane < N_EXPERTS, logit, neg)
    m1 = jnp.max(l1, axis=1, keepdims=True)
    i1 = jnp.min(jnp.where(l1 == m1, lane, float(LANES)), axis=1, keepdims=True)
    l2 = jnp.where(lane == i1, neg, l1)
    m2 = jnp.max(l2, axis=1, keepdims=True)
    i2 = jnp.min(jnp.where(l2 == m2, lane, float(LANES)), axis=1, keepdims=True)
    e = jnp.exp(m2 - m1)
    w1 = 1.0 / (1.0 + e)
    w2 = e / (1.0 + e)
    r_ref[...] = jnp.where(lane == 0, i1, jnp.where(lane == 1, i2, jnp.where(lane == 2, w1,
                           jnp.where(lane == 3, w2, 0.0))))


def route(x, gain, router, tm=256):
    t, d = x.shape
    wp = jnp.zeros((d, LANES), F32).at[:, :N_EXPERTS].set(router)
    return pl.pallas_call(
        _router_kernel,
        grid=(t // tm,),
        in_specs=[pl.BlockSpec((tm, d), lambda i: (i, 0)), pl.BlockSpec((1, d), lambda i: (0, 0)),
                  pl.BlockSpec((d, LANES), lambda i: (0, 0))],
        out_specs=[pl.BlockSpec((tm, d), lambda i: (i, 0)), pl.BlockSpec((tm, LANES), lambda i: (i, 0))],
        out_shape=[jax.ShapeDtypeStruct((t, d), F32), jax.ShapeDtypeStruct((t, LANES), F32)],
        compiler_params=_cparams(("parallel",)),
        name="router",
    )(x, gain.reshape(1, d), wp)


def _row_copy(src_ref, row, buf, i, sem):
    return pltpu.make_async_copy(src_ref.at[pl.ds(row, 1)], buf.at[pl.ds(i, 1)], sem)


ROW_DMA_UNROLL = 8


def _row_loop(n, fn):
    def body(blk, _):
        for u in range(ROW_DMA_UNROLL):
            fn(blk * ROW_DMA_UNROLL + u, u)
        return 0
    lax.fori_loop(0, n // ROW_DMA_UNROLL, body, 0)


GATHER_ROWS = 384


def _gather_kernel(idx_ref, live_ref, src_ref, o_ref, buf, sem):
    gb = GATHER_ROWS
    i = pl.program_id(0)
    n = pl.num_programs(0)

    def start_step(step):
        slot = step % 2

        @pl.when(live_ref[step] > 0)
        def _():
            _row_loop(gb, lambda r, u: _row_copy(src_ref, idx_ref[step * gb + r], buf.at[slot], r,
                                                 sem.at[slot]).start(priority=u % 2))

    @pl.when(i == 0)
    def _():
        buf[...] = jnp.zeros_like(buf)
        start_step(i)

    @pl.when(i + 1 < n)
    def _():
        start_step(i + 1)

    slot = i % 2

    @pl.when(live_ref[i] > 0)
    def _():
        _row_loop(gb, lambda r, u: _row_copy(src_ref, 0, buf.at[slot], r, sem.at[slot]).wait())

    o_ref[...] = buf[slot].astype(o_ref.dtype)


def gather_rows(src, idx, live, out_dtype):
    r = idx.shape[0]
    d = src.shape[1]
    gb = GATHER_ROWS
    grid_spec = pltpu.PrefetchScalarGridSpec(
        num_scalar_prefetch=2,
        grid=(r // gb,),
        in_specs=[pl.BlockSpec(memory_space=pl.ANY)],
        out_specs=pl.BlockSpec((gb, d), lambda i, idx, live: (i, 0)),
        scratch_shapes=[pltpu.VMEM((2, gb, d), src.dtype), pltpu.SemaphoreType.DMA((2,))],
    )
    return pl.pallas_call(
        _gather_kernel,
        grid_spec=grid_spec,
        out_shape=jax.ShapeDtypeStruct((r, d), out_dtype),
        compiler_params=_cparams(("arbitrary",)),
        name="gather_rows",
    )(idx, live, src)


def _combine_kernel(d0_ref, d1_ref, x_ref, w_ref, g_ref, ys_ref, o_ref, h_ref, buf, sem, *, tm):
    i = pl.program_id(0)
    n = pl.num_programs(0)

    def start_step(step):
        slot = step % 2

        def one(r, u):
            _row_copy(ys_ref, d0_ref[step * tm + r], buf.at[slot, 0], r, sem.at[slot]).start(priority=0)
            _row_copy(ys_ref, d1_ref[step * tm + r], buf.at[slot, 1], r, sem.at[slot]).start(priority=1)
        _row_loop(tm, one)

    @pl.when(i == 0)
    def _():
        start_step(i)

    @pl.when(i + 1 < n)
    def _():
        start_step(i + 1)

    slot = i % 2

    def wait_one(r, u):
        _row_copy(ys_ref, 0, buf.at[slot, 0], r, sem.at[slot]).wait()
        _row_copy(ys_ref, 0, buf.at[slot, 1], r, sem.at[slot]).wait()
    _row_loop(tm, wait_one)
    w = w_ref[...]
    _store_with_norm(x_ref[...] + w[:, 2:3] * buf[slot, 0] + w[:, 3:4] * buf[slot, 1], g_ref, o_ref, h_ref)


def combine_rows(x, ys, d0, d1, w, next_gain, tm=128):
    t, d = x.shape
    row = pl.BlockSpec((tm, d), lambda i, a, b: (i, 0))
    grid_spec = pltpu.PrefetchScalarGridSpec(
        num_scalar_prefetch=2,
        grid=(t // tm,),
        in_specs=[row, pl.BlockSpec((tm, LANES), lambda i, a, b: (i, 0)), pl.BlockSpec((1, d), lambda i, a, b: (0, 0)),
                  pl.BlockSpec(memory_space=pl.ANY)],
        out_specs=[row, row],
        scratch_shapes=[pltpu.VMEM((2, 2, tm, d), F32), pltpu.SemaphoreType.DMA((2,))],
    )
    return pl.pallas_call(
        functools.partial(_combine_kernel, tm=tm),
        grid_spec=grid_spec,
        out_shape=[jax.ShapeDtypeStruct((t, d), F32), jax.ShapeDtypeStruct((t, d), BF16)],
        compiler_params=_cparams(("arbitrary",)),
        name="combine_rows",
    )(d0, d1, x, w, next_gain.reshape(1, d), ys)


def moe_ffn(x, gain, router, w_gate, w_up, w_down, first_expert, next_gain):
    t, d = x.shape
    h, rt = route(x, gain, router)
    e_flat = jnp.concatenate([rt[:, 0], rt[:, 1]]).astype(I32)
    tok = jnp.concatenate([jnp.arange(t, dtype=I32)] * 2)
    onehot = (e_flat[:, None] == jnp.arange(N_EXPERTS, dtype=I32)[None, :]).astype(I32)
    csum = jnp.cumsum(onehot, axis=0)
    rank = jnp.take_along_axis(csum - onehot, e_flat[:, None], axis=1)[:, 0]
    counts = csum[-1]
    tm = MOE_TM
    n_tiles = (counts + tm - 1) // tm
    tile_end = jnp.cumsum(n_tiles)
    tile_start = tile_end - n_tiles
    dest = tile_start[e_flat] * tm + rank
    ns = -(-TOP_K * t // tm) + N_EXPERTS
    s_idx = jnp.arange(ns, dtype=I32)
    used = s_idx < tile_end[-1]
    s_clip = jnp.minimum(s_idx, tile_end[-1] - 1)
    tile_expert = jnp.minimum(jnp.sum((s_clip[:, None] >= tile_end[None, :]).astype(I32), axis=1), N_EXPERTS - 1)
    tile_rows = jnp.clip(counts[tile_expert] - (s_clip - tile_start[tile_expert]) * tm, 0, tm)
    tile_rows = jnp.where(used, tile_rows, 0).astype(I32)
    src_row = jnp.zeros((ns * tm,), I32).at[dest].set(tok)
    per = tm // GATHER_ROWS
    live = (jnp.arange(ns * per, dtype=I32) % per * GATHER_ROWS < jnp.repeat(tile_rows, per)).astype(I32)

    xs = gather_rows(h, src_row, live, BF16)
    ys = ffn_tiles(xs, tile_expert + first_expert, tile_rows, w_gate, w_up, w_down, tm)
    return combine_rows(x, ys, dest[:t], dest[t:], rt, next_gain)


def _ple_kernel(h_ref, wg_ref, p_ref, wp_ref, x_ref, o_ref):
    gate = _sigmoid(_dot(h_ref[...], wg_ref[...].astype(BF16)))
    proj = _dot(p_ref[...].astype(BF16), wp_ref[...].astype(BF16))
    o_ref[...] = x_ref[...] + gate * proj


def ple(h, w_gate, p, w_proj, x, layer, tm=2048, tn=512):
    t, d = x.shape
    pd = p.shape[-1]
    tm = min(tm, t)
    return pl.pallas_call(
        _ple_kernel,
        grid=(t // tm, d // tn),
        in_specs=[pl.BlockSpec((tm, d), lambda i, j: (i, 0)), _layer_spec((d, tn), lambda i, j: (0, j), layer),
                  _layer_spec((tm, pd), lambda i, j: (i, 0), layer), _layer_spec((pd, tn), lambda i, j: (0, j), layer),
                  pl.BlockSpec((tm, tn), lambda i, j: (i, j))],
        out_specs=pl.BlockSpec((tm, tn), lambda i, j: (i, j)),
        out_shape=jax.ShapeDtypeStruct((t, d), F32),
        compiler_params=_cparams(("parallel", "parallel")),
        name="ple",
    )(h, w_gate, p, w_proj, x)


def _pad_rows(w, start, total):
    return jnp.zeros((total, w.shape[1]), F32).at[start:start + w.shape[0]].set(w).astype(BF16)


def kernel(x, p, positions, w_in, mix_norm, ffn_norm, ple_norm, rw_mu, rw_w0, rw_w2, rw_a0, rw_a2, rw_g2, rw_kk,
           rw_ka, rw_rk, rw_ln_g, rw_ln_b, dsa_q_norm, dsa_k_norm, dsa_kv_norm, dsa_w_uk, dsa_w_uv, sw_q_norm,
           sw_k_norm, sw_sinks, w_branch, w_out, ffn_w_gate, ffn_w_up, ffn_w_down, moe_router, moe_w_gate, moe_w_up,
           moe_w_down, ple_w_gate, ple_w_proj):
    b, s, d = x.shape
    t = b * s
    depth = w_in.shape[0]
    xf = x.reshape(t, d)
    tables = rope_tables(positions)
    row = lambda a: a.reshape(1, -1)
    w_in_t = jnp.swapaxes(w_in, 1, 2)
    p_rows = p.reshape(depth, t, -1)
    experts = lambda w: w.reshape((-1,) + w.shape[2:])
    for i in range(depth):
        h = rmsnorm(xf, mix_norm[i])
        uh = in_proj(h, w_in_t, i, HEAD_TILES, name="in_proj_head")
        ut = in_proj(h, w_in_t, i, TAIL_TILES, name="in_proj_tail")
        y_a = sb_attention(uh, b, s)
        rw = rw_prepare(uh, row(rw_mu[i]), row(rw_w0[i]), row(rw_a0[i]), row(rw_kk[i]), row(rw_ka[i]),
                        _pad_rows(rw_w2[i], 0, RW_LORA), _pad_rows(rw_a2[i], RW_LORA_W, RW_LORA),
                        _pad_rows(rw_g2[i], RW_LORA_W + RW_LORA_A, RW_LORA), b, s)
        y_b = rw_scan(*rw, row(rw_rk[i]), row(rw_ln_g[i]), row(rw_ln_b[i]), b, s)
        y_c = dsa_attention(uh, tables, dsa_q_norm[i], dsa_k_norm[i], dsa_kv_norm[i], dsa_w_uk[i], dsa_w_uv[i], b, s)
        y_d = sw_attention(ut, tables, sw_q_norm[i], sw_k_norm[i], sw_sinks[i], b, s)
        merged = merge_branches([y_a, y_b, y_c, y_d], ut, w_branch, i)
        xf = matmul(merged, w_out, i, out_dtype=F32, residual=xf, tm=2048, name="out_proj")
        if i % 2 == 0:
            h = rmsnorm(xf, ffn_norm[i])
            xf, h = dense_ffn(xf, h, ffn_w_gate, ffn_w_up, ffn_w_down, i // 2, ple_norm[i])
        else:
            xf, h = moe_ffn(xf, ffn_norm[i], moe_router[i // 2], experts(moe_w_gate), experts(moe_w_up),
                            experts(moe_w_down), (i // 2) * N_EXPERTS, ple_norm[i])
        xf = ple(h, ple_w_gate, p_rows, ple_w_proj, xf, i)
    return xf.reshape(b, s, d)
```
